```python
import jax, jax.numpy as jnp
from jax import lax
import numpy as np

D_MODEL = 2048
BATCH = 8
SEQ = 2048
DEPTH = 1

MIX_WIDTH = D_MODEL
HGRN_WIDTH = MIX_WIDTH // 2
HGRN_DK = 128
HGRN_HEADS = HGRN_WIDTH // HGRN_DK
HGRN_DV = HGRN_WIDTH // HGRN_HEADS
HGRN_CHUNK = 64
MLA_WIDTH = MIX_WIDTH - HGRN_WIDTH
MLA_V_DIM = 128
MLA_HEADS = MLA_WIDTH // MLA_V_DIM
MLA_NOPE = 128
MLA_ROPE = 64
MLA_QK = MLA_NOPE + MLA_ROPE
Q_LORA = D_MODEL // 4
KV_LORA = D_MODEL // 8
ATTN_BLOCK = 128
ROPE_BASE = 10000.0
IN_COLS = 4 * HGRN_WIDTH + Q_LORA + KV_LORA + MLA_ROPE
N_GROUPS = 4
EXPERTS_PER_GROUP = 8
N_EXPERTS = N_GROUPS * EXPERTS_PER_GROUP
TOP_K_IN_GROUP = 2
D_EXPERT = D_MODEL // 4
MOE_BLOCK = 256
EPS = 1e-6

kernel_name = "hymba_hgrn2_mla_hmoe_adaln"


def _rmsnorm(x, g):
    xf = x.astype(jnp.float32)
    y = xf * lax.rsqrt(jnp.mean(xf * xf, axis=-1, keepdims=True) + EPS)
    return (y * g.astype(jnp.float32)).astype(x.dtype)


def _modulate(h, shift, scale):
    return h * (1 + scale[:, None, :]) + shift[:, None, :]


def _rope(x, positions):
    inv_freq = ROPE_BASE ** (-jnp.arange(0, MLA_ROPE, 2, dtype=jnp.float32) / MLA_ROPE)
    ang = positions.astype(jnp.float32)[..., None] * inv_freq
    cos = jnp.cos(ang)[:, :, None, :]
    sin = jnp.sin(ang)[:, :, None, :]
    xf = x.astype(jnp.float32)
    x1, x2 = xf[..., : MLA_ROPE // 2], xf[..., MLA_ROPE // 2:]
    out = jnp.concatenate([x1 * cos - x2 * sin, x2 * cos + x1 * sin], axis=-1)
    return out.astype(x.dtype)


def _hgrn2_chunked(q, f_logit, v, lb):
    B, S = q.shape[0], q.shape[1]
    C = HGRN_CHUNK
    N = S // C
    f = lb + (1.0 - lb) * jax.nn.sigmoid(f_logit.astype(jnp.float32))
    log_f = jnp.log(f)
    k = 1.0 - f
    qf = q.astype(jnp.float32) * HGRN_DK ** -0.5
    vf = v.astype(jnp.float32)

    def chunks(t):
        return t.reshape(B, N, C, t.shape[2], t.shape[3]).transpose(0, 3, 1, 2, 4)

    qc, kc, vc, lfc = chunks(qf), chunks(k), chunks(vf), chunks(log_f)
    b = jnp.cumsum(lfc, axis=3)
    b_last = b[..., C - 1:C, :]
    b_mid = b[..., C // 2 - 1:C // 2, :]
    q_in = qc * jnp.exp(b - b_mid)
    k_in = kc * jnp.exp(b_mid - b)
    A = jnp.einsum('bhnqd,bhnkd->bhnqk', q_in, k_in)
    causal = jnp.tril(jnp.ones((C, C), dtype=bool))
    A = jnp.where(causal, A, 0.0)
    o_intra = jnp.einsum('bhnqk,bhnkv->bhnqv', A, vc)
    U = jnp.einsum('bhnkd,bhnkv->bhndv', kc * jnp.exp(b_last - b), vc)
    decay = jnp.exp(b_last[..., 0, :])

    def step(state, xs):
        d, u = xs
        return d[..., None] * state + u, state

    s0 = jnp.zeros((B, q.shape[2], HGRN_DK, v.shape[3]), jnp.float32)
    _, s_prev = lax.scan(step, s0, (jnp.moveaxis(decay, 2, 0), jnp.moveaxis(U, 2, 0)))
    s_prev = jnp.moveaxis(s_prev, 0, 2)
    o_inter = jnp.einsum('bhnqd,bhndv->bhnqv', qc * jnp.exp(b), s_prev)
    o = o_intra + o_inter
    return o.transpose(0, 2, 3, 1, 4).reshape(B, S, q.shape[2], v.shape[3])


def _causal_block_attention(q, k, v):
    B, H, S, dqk = q.shape
    dv = v.shape[-1]
    nb = S // ATTN_BLOCK
    scale = dqk ** -0.5
    qb = q.reshape(B, H, nb, ATTN_BLOCK, dqk).transpose(2, 0, 1, 3, 4)
    k_pos = jnp.arange(S)
    neg = jnp.finfo(jnp.float32).min

    def one_block(args):
        qblk, blk = args
        s = jnp.einsum('bhqd,bhkd->bhqk', qblk, k).astype(jnp.float32) * scale
        q_pos = blk * ATTN_BLOCK + jnp.arange(ATTN_BLOCK)
        s = jnp.where(k_pos[None, :] <= q_pos[:, None], s, neg)
        p = jax.nn.softmax(s, axis=-1)
        return jnp.einsum('bhqk,bhkv->bhqv', p.astype(v.dtype), v)

    out = lax.map(one_block, (qb, jnp.arange(nb)))
    return out.transpose(1, 0, 3, 2, 4).reshape(B, S, H, dv)


def _hier_moe(h, w_group, b_group, w_router, b_router, w_gate, w_up, w_down):
    B, S, D = h.shape
    T = B * S
    xt = h.reshape(T, D)
    g_logits = (xt @ w_group).astype(jnp.float32) + b_group.astype(jnp.float32)
    g_prob = jax.nn.softmax(g_logits, axis=-1)
    g_sel = jnp.argmax(g_logits, axis=-1)
    p_group = jnp.take_along_axis(g_prob, g_sel[:, None], axis=-1)[:, 0]
    e_logits = ((xt @ w_router).astype(jnp.float32) + b_router.astype(jnp.float32))
    e_logits = e_logits.reshape(T, N_GROUPS, EXPERTS_PER_GROUP)
    e_in = jnp.take_along_axis(e_logits, g_sel[:, None, None], axis=1)[:, 0]
    top_val, top_idx = lax.top_k(e_in, TOP_K_IN_GROUP)
    weights = p_group[:, None] * jax.nn.softmax(top_val, axis=-1)
    expert_ids = g_sel[:, None] * EXPERTS_PER_GROUP + top_idx

    A = T * TOP_K_IN_GROUP
    e_flat = expert_ids.reshape(A).astype(jnp.int32)
    w_flat = weights.reshape(A)
    tok = jnp.arange(A, dtype=jnp.int32) // TOP_K_IN_GROUP
    counts = jax.ops.segment_sum(jnp.ones((A,), jnp.int32), e_flat, num_segments=N_EXPERTS)
    padded = ((counts + MOE_BLOCK - 1) // MOE_BLOCK) * MOE_BLOCK
    pad_end = jnp.cumsum(padded)
    pad_start = pad_end - padded
    start = jnp.cumsum(counts) - counts
    order = jnp.argsort(e_flat)
    e_sorted = e_flat[order]
    dest_sorted = pad_start[e_sorted] + (jnp.arange(A, dtype=jnp.int32) - start[e_sorted])
    dest = jnp.zeros((A,), jnp.int32).at[order].set(dest_sorted.astype(jnp.int32))
    n_blocks = -(-A // MOE_BLOCK) + N_EXPERTS
    R = n_blocks * MOE_BLOCK
    x_buf = jnp.zeros((R, D), xt.dtype).at[dest].set(xt[tok])
    blk_start = jnp.arange(n_blocks, dtype=jnp.int32) * MOE_BLOCK
    blk_e = jnp.minimum(jnp.searchsorted(pad_end, blk_start, side='right'), N_EXPERTS - 1)

    def expert_block(args):
        xb, e = args
        hid = jax.nn.silu(xb @ w_gate[e]) * (xb @ w_up[e])
        return hid @ w_down[e]

    y_buf = lax.map(expert_block, (x_buf.reshape(n_blocks, MOE_BLOCK, D), blk_e)).reshape(R, D)
    contrib = y_buf[dest] * w_flat[:, None].astype(y_buf.dtype)
    y = jax.ops.segment_sum(contrib, tok, num_segments=T)
    return y.reshape(B, S, D)


def setup_inputs(seed: int = 0) -> dict:
    key = jax.random.key(seed)
    ks = jax.random.split(key, 24)
    L = DEPTH

    def w(k, shape, fan_in, mult=1.0):
        return jax.random.normal(k, shape, jnp.float32) * (mult * fan_in ** -0.5)

    def gain(k, shape):
        return 1.0 + 0.02 * jax.random.normal(k, shape, jnp.float32)

    x = jax.random.normal(ks[0], (BATCH, SEQ, D_MODEL), jnp.float32)
    c = jax.random.normal(ks[1], (BATCH, D_MODEL), jnp.float32)
    offsets = jax.random.randint(ks[2], (BATCH, 1), 0, 512, dtype=jnp.int32)
    positions = offsets + jnp.arange(SEQ, dtype=jnp.int32)[None, :]
    return {
        "x": x,
        "c": c,
        "positions": positions,
        "w_ada": w(ks[3], (L, D_MODEL, 6 * D_MODEL), D_MODEL, 0.5),
        "b_ada": 0.01 * jax.random.normal(ks[4], (L, 6 * D_MODEL), jnp.float32),
        "norm1_g": gain(ks[5], (L, D_MODEL)),
        "w_in": w(ks[6], (L, D_MODEL, IN_COLS), D_MODEL),
        "hgrn_lb_logits": 1.0 + 0.1 * jax.random.normal(ks[7], (L + 1, HGRN_WIDTH), jnp.float32),
        "hgrn_onorm_g": gain(ks[8], (L, HGRN_DV)),
        "q_a_norm_g": gain(ks[9], (L, Q_LORA)),
        "w_q_up": w(ks[10], (L, Q_LORA, MLA_HEADS * MLA_QK), Q_LORA),
        "kv_a_norm_g": gain(ks[11], (L, KV_LORA)),
        "w_kv_up": w(ks[12], (L, KV_LORA, MLA_HEADS * (MLA_NOPE + MLA_V_DIM)), KV_LORA),
        "q_norm_g": gain(ks[13], (L, MLA_QK)),
        "k_norm_g": gain(ks[14], (L, MLA_QK)),
        "attn_onorm_g": gain(ks[15], (L, MLA_V_DIM)),
        "w_out": w(ks[16], (L, MIX_WIDTH, D_MODEL), MIX_WIDTH),
        "norm2_g": gain(ks[17], (L, D_MODEL)),
        "w_group": w(ks[18], (L, D_MODEL, N_GROUPS), D_MODEL),
        "b_group": 0.01 * jax.random.normal(ks[19], (L, N_GROUPS), jnp.float32),
        "w_router": w(ks[20], (L, D_MODEL, N_EXPERTS), D_MODEL),
        "b_router": 0.01 * jax.random.normal(ks[21], (L, N_EXPERTS), jnp.float32),
        "w_gate": w(ks[22], (L, N_EXPERTS, D_MODEL, D_EXPERT), D_MODEL),
        "w_up": w(jax.random.fold_in(ks[22], 1), (L, N_EXPERTS, D_MODEL, D_EXPERT), D_MODEL),
        "w_down": w(ks[23], (L, N_EXPERTS, D_EXPERT, D_MODEL), D_EXPERT),
    }


def reference(x, c, positions, w_ada, b_ada, norm1_g, w_in, hgrn_lb_logits, hgrn_onorm_g,
              q_a_norm_g, w_q_up, kv_a_norm_g, w_kv_up, q_norm_g, k_norm_g, attn_onorm_g,
              w_out, norm2_g, w_group, b_group, w_router, b_router, w_gate, w_up, w_down):
    B, S, D = x.shape
    lb_all = jnp.cumsum(jax.nn.softmax(hgrn_lb_logits.astype(jnp.float32), axis=0), axis=0)
    c_act = jax.nn.silu(c)
    split_idx = [HGRN_WIDTH, 2 * HGRN_WIDTH, 3 * HGRN_WIDTH, 4 * HGRN_WIDTH,
                 4 * HGRN_WIDTH + Q_LORA, 4 * HGRN_WIDTH + Q_LORA + KV_LORA]
    for l in range(DEPTH):
        mod = c_act @ w_ada[l] + b_ada[l]
        sh1, sc1, g1, sh2, sc2, g2 = jnp.split(mod, 6, axis=-1)

        h = _modulate(_rmsnorm(x, norm1_g[l]), sh1, sc1)
        proj = h @ w_in[l]
        hq, hf, hi, hg, q_a, kv_a, k_pe = jnp.split(proj, split_idx, axis=-1)

        def heads_a(t):
            return t.reshape(B, S, HGRN_HEADS, HGRN_DK)
        lb = lb_all[l].reshape(HGRN_HEADS, HGRN_DK)
        o_a = _hgrn2_chunked(heads_a(hq), heads_a(hf), heads_a(hi), lb).astype(x.dtype)
        o_a = _rmsnorm(o_a, hgrn_onorm_g[l]) * jax.nn.silu(heads_a(hg))
        o_a = o_a.reshape(B, S, HGRN_WIDTH)

        qh = (_rmsnorm(q_a, q_a_norm_g[l]) @ w_q_up[l]).reshape(B, S, MLA_HEADS, MLA_QK)
        kv = (_rmsnorm(kv_a, kv_a_norm_g[l]) @ w_kv_up[l]).reshape(B, S, MLA_HEADS, MLA_NOPE + MLA_V_DIM)
        k_nope, v = kv[..., :MLA_NOPE], kv[..., MLA_NOPE:]
        kh = jnp.concatenate(
            [k_nope, jnp.broadcast_to(k_pe[:, :, None, :], (B, S, MLA_HEADS, MLA_ROPE))], axis=-1)
        qh = _rmsnorm(qh, q_norm_g[l])
        kh = _rmsnorm(kh, k_norm_g[l])
        qh = jnp.concatenate([qh[..., :MLA_NOPE], _rope(qh[..., MLA_NOPE:], positions)], axis=-1)
        kh = jnp.concatenate([kh[..., :MLA_NOPE], _rope(kh[..., MLA_NOPE:], positions)], axis=-1)
        o_b = _causal_block_attention(qh.transpose(0, 2, 1, 3), kh.transpose(0, 2, 1, 3),
                                      v.transpose(0, 2, 1, 3))
        o_b = _rmsnorm(o_b, attn_onorm_g[l]).reshape(B, S, MLA_WIDTH)

        mix = jnp.concatenate([o_a, o_b], axis=-1) @ w_out[l]
        x = x + g1[:, None, :] * mix

        h2 = _modulate(_rmsnorm(x, norm2_g[l]), sh2, sc2)
        y = _hier_moe(h2, w_group[l], b_group[l], w_router[l], b_router[l],
                      w_gate[l], w_up[l], w_down[l])
        x = x + g2[:, None, :] * y
    return x
```

```python
import functools

import jax
import jax.numpy as jnp
from jax import lax
from jax.experimental import pallas as pl
from jax.experimental.pallas import tpu as pltpu

F32 = jnp.float32
BF16 = jnp.bfloat16
EPS = 1e-6

D_MODEL = 2048
HGRN_WIDTH = 1024
HGRN_DK = 128
HGRN_HEADS = 8
HGRN_CHUNK = 64
MLA_HEADS = 8
MLA_NOPE = 128
MLA_ROPE = 64
MLA_QK = MLA_NOPE + MLA_ROPE
MLA_V = 128
MLA_QPAD = 256
Q_LORA = 512
KV_LORA = 256
ROPE_BASE = 10000.0
IN_COLS = 4 * HGRN_WIDTH + Q_LORA + KV_LORA + MLA_ROPE
IN_COLS_PAD = 5120
N_GROUPS = 4
EXPERTS_PER_GROUP = 8
N_EXPERTS = 32
D_EXPERT = 512
ROUTE_ROWS = 40
MOE_BLOCK = 256
LANES = 128
VMEM_LIMIT = 56 * 1024 * 1024


def _cparams(sem):
    return pltpu.CompilerParams(dimension_semantics=sem, vmem_limit_bytes=VMEM_LIMIT)


def _dot(a, b):
    return jnp.dot(a, b, preferred_element_type=F32)


def _dot_nt(a, b):
    return lax.dot_general(a, b, (((1,), (1,)), ((), ())), preferred_element_type=F32)


def _rms(x, g):
    return x * lax.rsqrt(jnp.mean(x * x, axis=-1, keepdims=True) + EPS) * g


def _silu(x):
    return x * jax.nn.sigmoid(x)


def _ada_kernel(c_ref, w_ref, b_ref, o_ref):
    ca = _silu(c_ref[...]).astype(BF16)
    o_ref[...] = _dot(ca, w_ref[...].astype(BF16)) + b_ref[...]


def _ada(c, w, b):
    bsz, d = c.shape
    n = w.shape[1]
    tn = 1024
    return pl.pallas_call(
        _ada_kernel,
        grid=(n // tn,),
        in_specs=[pl.BlockSpec((bsz, d), lambda j: (0, 0)),
                  pl.BlockSpec((d, tn), lambda j: (0, j)),
                  pl.BlockSpec((1, tn), lambda j: (0, j))],
        out_specs=pl.BlockSpec((bsz, tn), lambda j: (0, j)),
        out_shape=jax.ShapeDtypeStruct((bsz, n), F32),
        compiler_params=_cparams(("arbitrary",)),
        name="ada",
    )(c, w, b.reshape(1, n))


def _in_kernel(x_ref, g_ref, sh_ref, sc_ref, w_ref, o_ref, h_ref, *, rows):
    @pl.when(pl.program_id(1) == 0)
    def _():
        def body(c, carry):
            r = pl.ds(pl.multiple_of(c * rows, rows), rows)
            h = _rms(x_ref[r, :], g_ref[...]) * (1.0 + sc_ref[0]) + sh_ref[0]
            h_ref[r, :] = h.astype(BF16)
            return carry
        lax.fori_loop(0, x_ref.shape[0] // rows, body, 0)

    o_ref[...] = _dot(h_ref[...], w_ref[...]).astype(BF16)


def _in_proj(x2, g, sh, sc, w, seq):
    t, d = x2.shape
    n = w.shape[1]
    tm = min(1024, seq)
    tn = 640
    rows = min(256, tm)
    return pl.pallas_call(
        functools.partial(_in_kernel, rows=rows),
        grid=(t // tm, n // tn),
        in_specs=[pl.BlockSpec((tm, d), lambda i, j: (i, 0)),
                  pl.BlockSpec((1, d), lambda i, j: (0, 0)),
                  pl.BlockSpec((1, 1, d), lambda i, j: (i * tm // seq, 0, 0)),
                  pl.BlockSpec((1, 1, d), lambda i, j: (i * tm // seq, 0, 0)),
                  pl.BlockSpec((d, tn), lambda i, j: (0, j))],
        out_specs=pl.BlockSpec((tm, tn), lambda i, j: (i, j)),
        out_shape=jax.ShapeDtypeStruct((t, n), BF16),
        scratch_shapes=[pltpu.VMEM((tm, d), BF16)],
        compiler_params=_cparams(("arbitrary", "arbitrary")),
        name="in_proj",
    )(x2, g, sh, sc, w)


HG_ROWS = 256


def _chunk_mask(n):
    row = lax.broadcasted_iota(jnp.int32, (n, n), 0)
    col = lax.broadcasted_iota(jnp.int32, (n, n), 1)
    return jnp.logical_and(row // HGRN_CHUNK == col // HGRN_CHUNK, col <= row)


def _hgrn_kernel(q_ref, f_ref, i_ref, g_ref, lbl_ref, og_ref, o_ref,
                 qin_s, kin_s, ku_s, qb_s, dec_s, oi_s, *, layer):
    seq = q_ref.shape[1]
    cs = HGRN_CHUNK
    rb = min(HG_ROWS, seq)
    nc = rb // cs
    lg = lbl_ref[...]
    ex = jnp.exp(lg - jnp.max(lg, axis=0, keepdims=True))
    sm = ex / jnp.sum(ex, axis=0, keepdims=True)
    lb = jnp.sum(sm[0:layer + 1], axis=0, keepdims=True)
    mask = _chunk_mask(rb)
    tri = mask.astype(BF16)

    def phase1(blk, carry):
        r = pl.ds(pl.multiple_of(blk * rb, rb), rb)
        f = lb + (1.0 - lb) * jax.nn.sigmoid(f_ref[0, r, :].astype(F32))
        lf = jnp.log(f)
        k = 1.0 - f
        hi = lf.astype(BF16)
        r1 = lf - hi.astype(F32)
        mid = r1.astype(BF16)
        lo = (r1 - mid.astype(F32)).astype(BF16)
        b = _dot(tri, hi) + _dot(tri, mid) + _dot(tri, lo)
        b3 = b.reshape(nc, cs, HGRN_DK)
        bmid = b3[:, cs // 2 - 1:cs // 2, :]
        blast = b3[:, cs - 1:cs, :]
        q3 = (q_ref[0, r, :].astype(F32) * HGRN_DK ** -0.5).reshape(nc, cs, HGRN_DK)
        k3 = k.reshape(nc, cs, HGRN_DK)
        qin_s[r, :] = (q3 * jnp.exp(b3 - bmid)).reshape(rb, HGRN_DK).astype(BF16)
        kin_s[r, :] = (k3 * jnp.exp(bmid - b3)).reshape(rb, HGRN_DK).astype(BF16)
        ku_s[r, :] = (k3 * jnp.exp(blast - b3)).reshape(rb, HGRN_DK).astype(BF16)
        qb_s[r, :] = (q3 * jnp.exp(b3)).reshape(rb, HGRN_DK).astype(BF16)
        dec_s[pl.ds(blk * nc, nc)] = jnp.exp(blast)
        a = _dot_nt(qin_s[r, :], kin_s[r, :])
        a = jnp.where(mask, a, 0.0).astype(BF16)
        oi_s[r, :] = _dot(a, i_ref[0, r, :])
        return carry

    lax.fori_loop(0, seq // rb, phase1, 0)

    def phase2(n, st):
        r = pl.ds(pl.multiple_of(n * cs, cs), cs)
        o = oi_s[r, :] + _dot_nt(qb_s[r, :], st.astype(BF16))
        o = _rms(o, og_ref[...]) * _silu(g_ref[0, r, :].astype(F32))
        o_ref[0, r, :] = o.astype(BF16)
        vt = i_ref[0, r, :].astype(F32).T.astype(BF16)
        return st * dec_s[n] + _dot(vt, ku_s[r, :])

    lax.fori_loop(0, seq // cs, phase2, jnp.zeros((HGRN_DK, HGRN_DK), F32))


def _hgrn(proj3, lb_logits, onorm_g, layer):
    bsz, seq, _ = proj3.shape
    nh = HGRN_HEADS

    def col(off):
        return pl.BlockSpec((1, seq, HGRN_DK), lambda b, h: (b, 0, off * nh + h))

    nl = lb_logits.shape[0]
    return pl.pallas_call(
        functools.partial(_hgrn_kernel, layer=layer),
        grid=(bsz, nh),
        in_specs=[col(0), col(1), col(2), col(3),
                  pl.BlockSpec((nl, HGRN_DK), lambda b, h: (0, h)),
                  pl.BlockSpec((1, HGRN_DK), lambda b, h: (0, 0))],
        out_specs=pl.BlockSpec((1, seq, HGRN_DK), lambda b, h: (b, 0, h)),
        out_shape=jax.ShapeDtypeStruct((bsz, seq, HGRN_WIDTH), BF16),
        scratch_shapes=[pltpu.VMEM((seq, HGRN_DK), BF16)] * 4
        + [pltpu.VMEM((seq // HGRN_CHUNK, 1, HGRN_DK), F32), pltpu.VMEM((seq, HGRN_DK), F32)],
        compiler_params=_cparams(("arbitrary", "arbitrary")),
        name="hgrn",
    )(proj3, proj3, proj3, proj3, lb_logits, onorm_g)


def _rope(x, cos, sin_signed, lane):
    half = MLA_ROPE // 2
    swapped = jnp.where(lane < half, pltpu.roll(x, LANES - half, 1), pltpu.roll(x, half, 1))
    return x * cos + swapped * sin_signed


def _up_kernel(p_ref, pos_ref, wq_ref, wkv_ref, qag_ref, kvag_ref, qg_ref, kg_ref, freq_ref,
               q_ref, k_ref, v_ref):
    p = p_ref[...].astype(F32)
    q_a = p[:, 0:Q_LORA]
    kv_a = p[:, Q_LORA:Q_LORA + KV_LORA]
    k_pe = p[:, Q_LORA + KV_LORA:Q_LORA + KV_LORA + LANES]
    qf = _dot(_rms(q_a, qag_ref[...]).astype(BF16), wq_ref[...])
    kvf = _dot(_rms(kv_a, kvag_ref[...]).astype(BF16), wkv_ref[...])

    tm = p.shape[0]
    lane = lax.broadcasted_iota(jnp.int32, (tm, LANES), 1)
    half = MLA_ROPE // 2
    ang = pos_ref[...].astype(F32) * freq_ref[...]
    valid = lane < MLA_ROPE
    cos = jnp.where(valid, jnp.cos(ang), 0.0)
    sin_signed = jnp.where(valid, jnp.where(lane < half, -jnp.sin(ang), jnp.sin(ang)), 0.0)

    qg = qg_ref[...]
    kg = kg_ref[...]
    kpe_ss = jnp.sum(k_pe * k_pe, axis=-1, keepdims=True)
    kpe_rot = _rope(k_pe * kg[:, MLA_NOPE:], cos, sin_signed, lane)
    scale = MLA_QK ** -0.5
    for h in range(MLA_HEADS):
        qh = qf[:, h * MLA_QPAD:(h + 1) * MLA_QPAD]
        rq = lax.rsqrt(jnp.sum(qh * qh, axis=-1, keepdims=True) / MLA_QK + EPS)
        qn = qh * rq * qg
        q_ref[0, h, :, 0:MLA_NOPE] = (qn[:, 0:MLA_NOPE] * scale).astype(BF16)
        q_ref[0, h, :, MLA_NOPE:] = (_rope(qn[:, MLA_NOPE:], cos, sin_signed, lane) * scale).astype(BF16)
        kn = kvf[:, h * MLA_QPAD:h * MLA_QPAD + MLA_NOPE]
        rk = lax.rsqrt((jnp.sum(kn * kn, axis=-1, keepdims=True) + kpe_ss) / MLA_QK + EPS)
        k_ref[0, h, :, 0:MLA_NOPE] = (kn * rk * kg[:, 0:MLA_NOPE]).astype(BF16)
        k_ref[0, h, :, MLA_NOPE:] = (kpe_rot * rk).astype(BF16)
        v_ref[0, h, :, :] = kvf[:, h * MLA_QPAD + MLA_NOPE:(h + 1) * MLA_QPAD].astype(BF16)


def _mla_up(proj3, pos3, wq, wkv, qag, kvag, qg, kg, freq):
    bsz, seq, _ = proj3.shape
    tm = min(512, seq)
    nh = MLA_HEADS
    mla_block = 4 * HGRN_WIDTH // 1024

    def const(shape):
        return pl.BlockSpec(shape, lambda b, i: (0,) * len(shape))

    return pl.pallas_call(
        _up_kernel,
        grid=(bsz, seq // tm),
        in_specs=[pl.BlockSpec((None, tm, 1024), lambda b, i: (b, i, mla_block)),
                  pl.BlockSpec((None, tm, 1), lambda b, i: (b, i, 0)),
                  const(wq.shape), const(wkv.shape), const(qag.shape), const(kvag.shape),
                  const(qg.shape), const(kg.shape), const(freq.shape)],
        out_specs=[pl.BlockSpec((1, nh, tm, MLA_QPAD), lambda b, i: (b, 0, i, 0)),
                   pl.BlockSpec((1, nh, tm, MLA_QPAD), lambda b, i: (b, 0, i, 0)),
                   pl.BlockSpec((1, nh, tm, MLA_V), lambda b, i: (b, 0, i, 0))],
        out_shape=[jax.ShapeDtypeStruct((bsz, nh, seq, MLA_QPAD), BF16),
                   jax.ShapeDtypeStruct((bsz, nh, seq, MLA_QPAD), BF16),
                   jax.ShapeDtypeStruct((bsz, nh, seq, MLA_V), BF16)],
        compiler_params=_cparams(("arbitrary", "arbitrary")),
        name="mla_up",
    )(proj3, pos3, wq, wkv, qag, kvag, qg, kg, freq)


ATT_T = 256


def _attn_kernel(q_ref, k_ref, v_ref, g_ref, o_ref):
    seq = q_ref.shape[2]
    t = min(ATT_T, seq)
    row = lax.broadcasted_iota(jnp.int32, (t, t), 0)
    col = lax.broadcasted_iota(jnp.int32, (t, t), 1)
    causal = col <= row
    neg = jnp.finfo(F32).min

    def q_tile(qi, carry):
        qr = pl.ds(pl.multiple_of(qi * t, t), t)
        q = q_ref[0, 0, qr, :]
        s = jnp.where(causal, _dot_nt(q, k_ref[0, 0, qr, :]), neg)
        m = jnp.max(s, axis=-1, keepdims=True)
        p = jnp.exp(s - m)
        l = jnp.sum(p, axis=-1, keepdims=True)
        acc = _dot(p.astype(BF16), v_ref[0, 0, qr, :])

        def kv_tile(j, c):
            m, l, acc = c
            kr = pl.ds(pl.multiple_of(j * t, t), t)
            s = _dot_nt(q, k_ref[0, 0, kr, :])
            m_new = jnp.maximum(m, jnp.max(s, axis=-1, keepdims=True))
            alpha = jnp.exp(m - m_new)
            p = jnp.exp(s - m_new)
            l = alpha * l + jnp.sum(p, axis=-1, keepdims=True)
            acc = alpha * acc + _dot(p.astype(BF16), v_ref[0, 0, kr, :])
            return m_new, l, acc

        m, l, acc = lax.fori_loop(0, qi, kv_tile, (m, l, acc))
        o = _rms(acc / l, g_ref[...])
        o_ref[0, qr, :] = o.astype(BF16)
        return carry

    lax.fori_loop(0, seq // t, q_tile, 0)


def _attention(q, k, v, g):
    bsz, nh, seq, _ = q.shape
    return pl.pallas_call(
        _attn_kernel,
        grid=(bsz, nh),
        in_specs=[pl.BlockSpec((1, 1, seq, MLA_QPAD), lambda b, h: (b, h, 0, 0)),
                  pl.BlockSpec((1, 1, seq, MLA_QPAD), lambda b, h: (b, h, 0, 0)),
                  pl.BlockSpec((1, 1, seq, MLA_V), lambda b, h: (b, h, 0, 0)),
                  pl.BlockSpec((1, MLA_V), lambda b, h: (0, 0))],
        out_specs=pl.BlockSpec((1, seq, MLA_V), lambda b, h: (b, 0, h)),
        out_shape=jax.ShapeDtypeStruct((bsz, seq, nh * MLA_V), BF16),
        compiler_params=_cparams(("arbitrary", "arbitrary")),
        name="attn",
    )(q, k, v, g)


def _out_kernel(oa_ref, ob_ref, x_ref, wa_ref, wb_ref, g1_ref, n2g_ref, sh2_ref, sc2_ref,
                wr_ref, br_ref, x1_ref, h2_ref, lg_ref):
    mix = _dot(oa_ref[...], wa_ref[...]) + _dot(ob_ref[...], wb_ref[...])
    x1 = x_ref[...] + g1_ref[0] * mix
    x1_ref[...] = x1
    h2 = _rms(x1, n2g_ref[...]) * (1.0 + sc2_ref[0]) + sh2_ref[0]
    h2_ref[...] = h2
    lg = _dot(h2.astype(BF16), wr_ref[...]) + br_ref[...]
    lg_ref[...] = lg.T[0:ROUTE_ROWS, :]


def _out_proj(oa, ob, x2, wa, wb, g1, n2g, sh2, sc2, wr, br, seq):
    t, d = x2.shape
    tm = min(512, seq)

    def const(shape):
        return pl.BlockSpec(shape, lambda i: (0,) * len(shape))

    def per_batch():
        return pl.BlockSpec((1, 1, d), lambda i: (i * tm // seq, 0, 0))

    return pl.pallas_call(
        _out_kernel,
        grid=(t // tm,),
        in_specs=[pl.BlockSpec((tm, HGRN_WIDTH), lambda i: (i, 0)),
                  pl.BlockSpec((tm, HGRN_WIDTH), lambda i: (i, 0)),
                  pl.BlockSpec((tm, d), lambda i: (i, 0)),
                  const(wa.shape), const(wb.shape), per_batch(), const(n2g.shape),
                  per_batch(), per_batch(), const(wr.shape), const(br.shape)],
        out_specs=[pl.BlockSpec((tm, d), lambda i: (i, 0)),
                   pl.BlockSpec((tm, d), lambda i: (i, 0)),
                   pl.BlockSpec((ROUTE_ROWS, tm), lambda i: (0, i))],
        out_shape=[jax.ShapeDtypeStruct((t, d), F32),
                   jax.ShapeDtypeStruct((t, d), F32),
                   jax.ShapeDtypeStruct((ROUTE_ROWS, t), F32)],
        compiler_params=_cparams(("arbitrary",)),
        name="out_proj",
    )(oa, ob, x2, wa, wb, g1, n2g, sh2, sc2, wr, br)


def _route_kernel(lg_ref, tri_ref, ri_ref, rw_ref, cnt_ref, carry_s):
    step = pl.program_id(0)

    @pl.when(step == 0)
    def _():
        carry_s[...] = jnp.zeros_like(carry_s)

    lg = lg_ref[...]
    tr = lg.shape[1]
    epg = EXPERTS_PER_GROUP
    gl = lg[N_EXPERTS:N_EXPERTS + N_GROUPS, :]
    row_g = lax.broadcasted_iota(jnp.int32, (N_GROUPS, tr), 0)
    gmax = jnp.max(gl, axis=0, keepdims=True)
    g_sel = jnp.min(jnp.where(gl == gmax, row_g, N_GROUPS), axis=0, keepdims=True)
    p_group = 1.0 / jnp.sum(jnp.exp(gl - gmax), axis=0, keepdims=True)

    e_in = lg[0:epg, :]
    for g in range(1, N_GROUPS):
        e_in = jnp.where(g_sel == g, lg[g * epg:(g + 1) * epg, :], e_in)
    row_e = lax.broadcasted_iota(jnp.int32, (epg, tr), 0)
    top1 = jnp.max(e_in, axis=0, keepdims=True)
    i1 = jnp.min(jnp.where(e_in == top1, row_e, epg), axis=0, keepdims=True)
    rest = jnp.where(row_e == i1, -jnp.inf, e_in)
    top2 = jnp.max(rest, axis=0, keepdims=True)
    i2 = jnp.min(jnp.where(rest == top2, row_e, epg), axis=0, keepdims=True)
    e2w = jnp.exp(top2 - top1)
    w1 = p_group / (1.0 + e2w)
    w2 = p_group * e2w / (1.0 + e2w)
    ex1 = g_sel * epg + i1
    ex2 = g_sel * epg + i2

    row_x = lax.broadcasted_iota(jnp.int32, (N_EXPERTS, tr), 0)
    oh1 = row_x == ex1
    oh2 = row_x == ex2
    oh = jnp.logical_or(oh1, oh2)
    before = _dot(oh.astype(BF16), tri_ref[...]) + carry_s[:, 0:1]
    rank1 = jnp.sum(jnp.where(oh1, before, 0.0), axis=0, keepdims=True)
    rank2 = jnp.sum(jnp.where(oh2, before, 0.0), axis=0, keepdims=True)
    carry_s[...] = carry_s[...] + jnp.sum(oh.astype(F32), axis=1, keepdims=True)

    zi = jnp.zeros((4, tr), jnp.int32)
    ri_ref[...] = jnp.concatenate([ex1, ex2, rank1.astype(jnp.int32), rank2.astype(jnp.int32), zi], axis=0)
    rw_ref[...] = jnp.concatenate([w1, w2, jnp.zeros((6, tr), F32)], axis=0)
    cnt_ref[...] = carry_s[...].astype(jnp.int32)


def _route(lg_t, tri):
    t = lg_t.shape[1]
    tr = tri.shape[0]
    return pl.pallas_call(
        _route_kernel,
        grid=(t // tr,),
        in_specs=[pl.BlockSpec((ROUTE_ROWS, tr), lambda i: (0, i)),
                  pl.BlockSpec((tr, tr), lambda i: (0, 0))],
        out_specs=[pl.BlockSpec((8, tr), lambda i: (0, i)),
                   pl.BlockSpec((8, tr), lambda i: (0, i)),
                   pl.BlockSpec((N_EXPERTS, LANES), lambda i: (0, 0))],
        out_shape=[jax.ShapeDtypeStruct((8, t), jnp.int32),
                   jax.ShapeDtypeStruct((8, t), F32),
                   jax.ShapeDtypeStruct((N_EXPERTS, LANES), jnp.int32)],
        scratch_shapes=[pltpu.VMEM((N_EXPERTS, LANES), F32)],
        compiler_params=_cparams(("arbitrary",)),
        name="route",
    )(lg_t, tri)


def _row_copy(src, s, dst, d, sem):
    return pltpu.make_async_copy(src.at[pl.ds(s, 1), :], dst.at[pl.ds(d, 1), :], sem)


def _dispatch_kernel(d1_ref, d2_ref, h_ref, xb_ref, sem):
    td = h_ref.shape[0]
    base = pl.program_id(0) * td

    def start(r, carry):
        _row_copy(h_ref, r, xb_ref, d1_ref[base + r], sem).start()
        _row_copy(h_ref, r, xb_ref, d2_ref[base + r], sem).start()
        return carry

    lax.fori_loop(0, td, start, 0)

    def wait(r, carry):
        _row_copy(h_ref, 0, xb_ref, 0, sem).wait()
        _row_copy(h_ref, 0, xb_ref, 0, sem).wait()
        return carry

    lax.fori_loop(0, td, wait, 0)


def _dispatch(dest1, dest2, h2):
    t, d = h2.shape
    td = min(256, t)
    return pl.pallas_call(
        _dispatch_kernel,
        grid_spec=pltpu.PrefetchScalarGridSpec(
            num_scalar_prefetch=2,
            grid=(t // td,),
            in_specs=[pl.BlockSpec((td, d), lambda i, d1, d2: (i, 0))],
            out_specs=pl.BlockSpec(memory_space=pl.ANY),
            scratch_shapes=[pltpu.SemaphoreType.DMA(())]),
        out_shape=jax.ShapeDtypeStruct((2 * t, d), h2.dtype),
        compiler_params=_cparams(("arbitrary",)),
        name="dispatch",
    )(dest1, dest2, h2)


def _moe_kernel(blk_ref, exp_ref, lo_ref, hi_ref, first_ref, newe_ref,
                x_ref, wg_ref, wu_ref, wd_ref, o_ref, wg_s, wu_s, wd_s):
    w = pl.program_id(0)

    @pl.when(newe_ref[w] == 1)
    def _():
        wg_s[...] = wg_ref[0].astype(BF16)
        wu_s[...] = wu_ref[0].astype(BF16)
        wd_s[...] = wd_ref[0].astype(BF16)

    @pl.when(first_ref[w] == 1)
    def _():
        o_ref[...] = jnp.zeros_like(o_ref)

    lo = lo_ref[w]
    hi = hi_ref[w]

    @pl.when(hi > lo)
    def _():
        x = x_ref[...].astype(BF16)
        hid = _silu(_dot(x, wg_s[...])) * _dot(x, wu_s[...])
        y = _dot(hid.astype(BF16), wd_s[...])
        row = blk_ref[w] * x_ref.shape[0] + lax.broadcasted_iota(jnp.int32, (x_ref.shape[0], 1), 0)
        own = jnp.logical_and(row >= lo, row < hi)
        o_ref[...] = jnp.where(own, y, o_ref[...])


def _moe(items, x_buf, wg, wu, wd):
    r, d = x_buf.shape
    n_items = items[0].shape[0]
    tm = MOE_BLOCK
    de = wg.shape[2]
    return pl.pallas_call(
        _moe_kernel,
        grid_spec=pltpu.PrefetchScalarGridSpec(
            num_scalar_prefetch=6,
            grid=(n_items,),
            in_specs=[pl.BlockSpec((tm, d), lambda w, blk, ex, lo, hi, fi, ne: (blk[w], 0)),
                      pl.BlockSpec((1, d, de), lambda w, blk, ex, lo, hi, fi, ne: (ex[w], 0, 0)),
                      pl.BlockSpec((1, d, de), lambda w, blk, ex, lo, hi, fi, ne: (ex[w], 0, 0)),
                      pl.BlockSpec((1, de, d), lambda w, blk, ex, lo, hi, fi, ne: (ex[w], 0, 0))],
            out_specs=pl.BlockSpec((tm, d), lambda w, blk, ex, lo, hi, fi, ne: (blk[w], 0)),
            scratch_shapes=[pltpu.VMEM((d, de), BF16), pltpu.VMEM((d, de), BF16), pltpu.VMEM((de, d), BF16)]),
        out_shape=jax.ShapeDtypeStruct((r, d), F32),
        compiler_params=_cparams(("arbitrary",)),
        name="moe",
    )(*items, x_buf, wg, wu, wd)


def _moe_items(counts, n_rows):
    n_blocks = n_rows // MOE_BLOCK
    end = jnp.cumsum(counts)
    start = end - counts
    bnd = jnp.sort(jnp.concatenate([jnp.arange(n_blocks, dtype=jnp.int32) * MOE_BLOCK, start.astype(jnp.int32)]))
    nxt = jnp.concatenate([bnd[1:], jnp.array([n_rows], jnp.int32)])
    blk = jnp.minimum(bnd // MOE_BLOCK, n_blocks - 1)
    exp = jnp.minimum(jnp.searchsorted(end, bnd, side="right"), N_EXPERTS - 1).astype(jnp.int32)
    prev_blk = jnp.concatenate([jnp.array([-1], jnp.int32), blk[:-1]])
    prev_exp = jnp.concatenate([jnp.array([-1], jnp.int32), exp[:-1]])
    first = (blk != prev_blk).astype(jnp.int32)
    newe = (exp != prev_exp).astype(jnp.int32)
    return blk, exp, bnd, nxt, first, newe


def _combine_kernel(d1_ref, d2_ref, x1_ref, g2_ref, w1_ref, w2_ref, yb_ref, o_ref, y1_s, y2_s, sem):
    tc = x1_ref.shape[0]
    base = pl.program_id(0) * tc

    def start(r, carry):
        _row_copy(yb_ref, d1_ref[base + r], y1_s, r, sem).start()
        _row_copy(yb_ref, d2_ref[base + r], y2_s, r, sem).start()
        return carry

    lax.fori_loop(0, tc, start, 0)

    def wait(r, carry):
        _row_copy(yb_ref, 0, y1_s, 0, sem).wait()
        _row_copy(yb_ref, 0, y2_s, 0, sem).wait()
        return carry

    lax.fori_loop(0, tc, wait, 0)
    y = w1_ref[...] * y1_s[...] + w2_ref[...] * y2_s[...]
    o_ref[...] = x1_ref[...] + g2_ref[0] * y


def _combine(dest1, dest2, x1, g2, w1, w2, y_buf, seq):
    t, d = x1.shape
    tc = min(256, seq)
    return pl.pallas_call(
        _combine_kernel,
        grid_spec=pltpu.PrefetchScalarGridSpec(
            num_scalar_prefetch=2,
            grid=(t // tc,),
            in_specs=[pl.BlockSpec((tc, d), lambda i, d1, d2: (i, 0)),
                      pl.BlockSpec((1, 1, d), lambda i, d1, d2: (i * tc // seq, 0, 0)),
                      pl.BlockSpec((tc, 1), lambda i, d1, d2: (i, 0)),
                      pl.BlockSpec((tc, 1), lambda i, d1, d2: (i, 0)),
                      pl.BlockSpec(memory_space=pl.ANY)],
            out_specs=pl.BlockSpec((tc, d), lambda i, d1, d2: (i, 0)),
            scratch_shapes=[pltpu.VMEM((tc, d), F32), pltpu.VMEM((tc, d), F32), pltpu.SemaphoreType.DMA(())]),
        out_shape=jax.ShapeDtypeStruct((t, d), F32),
        compiler_params=_cparams(("arbitrary",)),
        name="combine",
    )(dest1, dest2, x1, g2, w1, w2, y_buf)


def _q_up_layout(w_q_up):
    w = w_q_up.reshape(Q_LORA, MLA_HEADS, MLA_QK)
    w = jnp.pad(w, ((0, 0), (0, 0), (0, MLA_QPAD - MLA_QK)))
    return w.reshape(Q_LORA, MLA_HEADS * MLA_QPAD).astype(BF16)


def _pad_lanes(g, width):
    return jnp.pad(g, (0, width - g.shape[0])).reshape(1, width)


def kernel(x, c, positions, w_ada, b_ada, norm1_g, w_in, hgrn_lb_logits, hgrn_onorm_g, q_a_norm_g, w_q_up,
           kv_a_norm_g, w_kv_up, q_norm_g, k_norm_g, attn_onorm_g, w_out, norm2_g, w_group, b_group,
           w_router, b_router, w_gate, w_up, w_down):
    bsz, seq, d = x.shape
    t = bsz * seq
    depth = w_ada.shape[0]
    half = MLA_ROPE // 2
    inv_freq = ROPE_BASE ** (-jnp.arange(0, MLA_ROPE, 2, dtype=F32) / MLA_ROPE)
    freq = jnp.concatenate([inv_freq, inv_freq, jnp.zeros((LANES - 2 * half,), F32)]).reshape(1, LANES)
    pos3 = positions.reshape(bsz, seq, 1)
    tr = min(512, t)
    tri = jnp.triu(jnp.ones((tr, tr), BF16), 1)

    x2 = x.reshape(t, d)
    for l in range(depth):
        mod = _ada(c, w_ada[l], b_ada[l]).reshape(bsz, 6, 1, d)
        sh1, sc1, g1, sh2, sc2, g2 = (mod[:, i] for i in range(6))

        w_in_p = jnp.pad(w_in[l], ((0, 0), (0, IN_COLS_PAD - IN_COLS))).astype(BF16)
        proj = _in_proj(x2, norm1_g[l].reshape(1, d), sh1, sc1, w_in_p, seq)
        proj3 = proj.reshape(bsz, seq, IN_COLS_PAD)

        o_a = _hgrn(proj3, hgrn_lb_logits, hgrn_onorm_g[l].reshape(1, HGRN_DK), l)

        q, k, v = _mla_up(proj3, pos3, _q_up_layout(w_q_up[l]), w_kv_up[l].astype(BF16),
                          q_a_norm_g[l].reshape(1, Q_LORA), kv_a_norm_g[l].reshape(1, KV_LORA),
                          _pad_lanes(q_norm_g[l], MLA_QPAD), _pad_lanes(k_norm_g[l], MLA_QPAD), freq)
        o_b = _attention(q, k, v, attn_onorm_g[l].reshape(1, MLA_V))

        w_o = w_out[l].astype(BF16)
        wr = jnp.pad(jnp.concatenate([w_router[l], w_group[l]], axis=1),
                     ((0, 0), (0, LANES - N_EXPERTS - N_GROUPS))).astype(BF16)
        br = _pad_lanes(jnp.concatenate([b_router[l], b_group[l]]), LANES)
        x1, h2, lg_t = _out_proj(o_a.reshape(t, HGRN_WIDTH), o_b.reshape(t, HGRN_WIDTH), x2,
                                 w_o[:HGRN_WIDTH], w_o[HGRN_WIDTH:], g1, norm2_g[l].reshape(1, d),
                                 sh2, sc2, wr, br, seq)

        ri, rw, cnt = _route(lg_t, tri)
        counts = cnt[:, 0]
        start = jnp.cumsum(counts) - counts
        dest1 = start[ri[0]] + ri[2]
        dest2 = start[ri[1]] + ri[3]
        x_buf = _dispatch(dest1, dest2, h2)
        y_buf = _moe(_moe_items(counts, 2 * t), x_buf, w_gate[l], w_up[l], w_down[l])
        x2 = _combine(dest1, dest2, x1, g2, rw[0].reshape(t, 1), rw[1].reshape(t, 1), y_buf, seq)
    return x2.reshape(bsz, seq, d)
```

```python
import functools

import jax
import jax.numpy as jnp
from jax import lax
from jax.experimental import pallas as pl
from jax.experimental.pallas import tpu as pltpu

F32 = jnp.float32
BF16 = jnp.bfloat16
EPS = 1e-6
LOG2E = 1.4426950408889634

D_MODEL = 2048
HGRN_WIDTH = 1024
HGRN_DK = 128
HGRN_HEADS = 8
HGRN_CHUNK = 64
MLA_HEADS = 8
MLA_NOPE = 128
MLA_ROPE = 64
MLA_QK = MLA_NOPE + MLA_ROPE
MLA_V = 128
MLA_QPAD = 256
Q_LORA = 512
KV_LORA = 256
ROPE_BASE = 10000.0
IN_COLS = 4 * HGRN_WIDTH + Q_LORA + KV_LORA + MLA_ROPE
IN_COLS_PAD = 5120
N_GROUPS = 4
EXPERTS_PER_GROUP = 8
N_EXPERTS = 32
D_EXPERT = 512
ROUTE_ROWS = 40
MOE_BLOCK = 256
LANES = 128
VMEM_LIMIT = 56 * 1024 * 1024


def _cparams(sem):
    return pltpu.CompilerParams(dimension_semantics=sem, vmem_limit_bytes=VMEM_LIMIT)


def _dot(a, b):
    return jnp.dot(a, b, preferred_element_type=F32)


def _dot_nt(a, b):
    return lax.dot_general(a, b, (((1,), (1,)), ((), ())), preferred_element_type=F32)


def _rms(x, g):
    return x * lax.rsqrt(jnp.mean(x * x, axis=-1, keepdims=True) + EPS) * g


def _silu(x):
    return x * jax.nn.sigmoid(x)


def _ada_kernel(c_ref, w_ref, b_ref, o_ref):
    ca = _silu(c_ref[...]).astype(BF16)
    o_ref[...] = _dot(ca, w_ref[...].astype(BF16)) + b_ref[...]


def _ada(c, w, b):
    bsz, d = c.shape
    n = w.shape[1]
    tn = 1024
    return pl.pallas_call(
        _ada_kernel,
        grid=(n // tn,),
        in_specs=[pl.BlockSpec((bsz, d), lambda j: (0, 0)),
                  pl.BlockSpec((d, tn), lambda j: (0, j)),
                  pl.BlockSpec((1, tn), lambda j: (0, j))],
        out_specs=pl.BlockSpec((bsz, tn), lambda j: (0, j)),
        out_shape=jax.ShapeDtypeStruct((bsz, n), F32),
        compiler_params=_cparams(("arbitrary",)),
        name="ada",
    )(c, w, b.reshape(1, n))


def _in_kernel(x_ref, g_ref, sh_ref, sc_ref, w_ref, o_ref, h_ref, *, rows):
    @pl.when(pl.program_id(1) == 0)
    def _():
        def body(c, carry):
            r = pl.ds(pl.multiple_of(c * rows, rows), rows)
            h = _rms(x_ref[r, :], g_ref[...]) * (1.0 + sc_ref[0]) + sh_ref[0]
            h_ref[r, :] = h.astype(BF16)
            return carry
        lax.fori_loop(0, x_ref.shape[0] // rows, body, 0)

    o_ref[...] = _dot(h_ref[...], w_ref[...]).astype(BF16)


def _in_proj(x2, g, sh, sc, w, seq):
    t, d = x2.shape
    n = w.shape[1]
    tm = min(1024, seq)
    tn = 640
    rows = min(256, tm)
    return pl.pallas_call(
        functools.partial(_in_kernel, rows=rows),
        grid=(t // tm, n // tn),
        in_specs=[pl.BlockSpec((tm, d), lambda i, j: (i, 0)),
                  pl.BlockSpec((1, d), lambda i, j: (0, 0)),
                  pl.BlockSpec((1, 1, d), lambda i, j: (i * tm // seq, 0, 0)),
                  pl.BlockSpec((1, 1, d), lambda i, j: (i * tm // seq, 0, 0)),
                  pl.BlockSpec((d, tn), lambda i, j: (0, j))],
        out_specs=pl.BlockSpec((tm, tn), lambda i, j: (i, j)),
        out_shape=jax.ShapeDtypeStruct((t, n), BF16),
        scratch_shapes=[pltpu.VMEM((tm, d), BF16)],
        compiler_params=_cparams(("arbitrary", "arbitrary")),
        name="in_proj",
    )(x2, g, sh, sc, w)


HG_ROWS = 256


def _chunk_mask(n):
    row = lax.broadcasted_iota(jnp.int32, (n, n), 0)
    col = lax.broadcasted_iota(jnp.int32, (n, n), 1)
    return jnp.logical_and(row // HGRN_CHUNK == col // HGRN_CHUNK, col <= row)


def _hgrn_kernel(q_ref, f_ref, i_ref, g_ref, lbl_ref, og_ref, o_ref,
                 qb_s, u_s, sp_s, dec_s, oi_s, *, layer):
    seq = q_ref.shape[1]
    cs = HGRN_CHUNK
    rb = min(HG_ROWS, seq)
    nc = rb // cs
    dk = HGRN_DK
    lg = lbl_ref[...]
    ex = jnp.exp(lg - jnp.max(lg, axis=0, keepdims=True))
    sm = ex / jnp.sum(ex, axis=0, keepdims=True)
    lb = jnp.sum(sm[0:layer + 1], axis=0, keepdims=True)
    mask = _chunk_mask(rb)
    tri = mask.astype(BF16)
    row_chunk = lax.broadcasted_iota(jnp.int32, (rb, dk), 0) // cs

    def phase1(blk, carry):
        r = pl.ds(pl.multiple_of(blk * rb, rb), rb)
        f = lb + (1.0 - lb) * jax.nn.sigmoid(f_ref[0, r, :].astype(F32))
        lf = jnp.log(f)
        k = 1.0 - f
        hi = lf.astype(BF16)
        r1 = lf - hi.astype(F32)
        mid = r1.astype(BF16)
        lo = (r1 - mid.astype(F32)).astype(BF16)
        bhm = _dot(tri, jnp.concatenate([hi, mid], axis=1))
        b = bhm[:, 0:dk] + bhm[:, dk:] + _dot(tri, lo)
        b3 = b.reshape(nc, cs, dk)
        bmid = b3[:, cs // 2 - 1:cs // 2, :]
        blast = b3[:, cs - 1:cs, :]
        q3 = (q_ref[0, r, :].astype(F32) * dk ** -0.5).reshape(nc, cs, dk)
        k3 = k.reshape(nc, cs, dk)
        qin = (q3 * jnp.exp(b3 - bmid)).reshape(rb, dk).astype(BF16)
        kin = (k3 * jnp.exp(bmid - b3)).reshape(rb, dk).astype(BF16)
        ku = (k3 * jnp.exp(blast - b3)).reshape(rb, dk)
        qb_s[r, :] = (q3 * jnp.exp(b3)).reshape(rb, dk).astype(BF16)
        dec_s[pl.ds(blk * nc, nc)] = jnp.exp(blast)
        v = i_ref[0, r, :]
        a = jnp.where(mask, _dot_nt(qin, kin), 0.0).astype(BF16)
        oi_s[r, :] = _dot(a, v)
        vt = v.astype(F32).T.astype(BF16)
        ku_exp = jnp.concatenate(
            [jnp.where(row_chunk == c, ku, 0.0).astype(BF16) for c in range(nc)], axis=1)
        ut = _dot(vt, ku_exp)
        for c in range(nc):
            u_s[blk * nc + c] = ut[:, c * dk:(c + 1) * dk]
        return carry

    lax.fori_loop(0, seq // rb, phase1, 0, unroll=2)

    def phase2(n, st):
        sp_s[n] = st.astype(BF16)
        return st * dec_s[n] + u_s[n]

    lax.fori_loop(0, seq // cs, phase2, jnp.zeros((dk, dk), F32), unroll=4)

    def phase3(blk, carry):
        r = pl.ds(pl.multiple_of(blk * rb, rb), rb)
        inter = [_dot_nt(qb_s[pl.ds(pl.multiple_of(blk * rb + c * cs, cs), cs), :], sp_s[blk * nc + c])
                 for c in range(nc)]
        o = oi_s[r, :] + jnp.concatenate(inter, axis=0)
        o = _rms(o, og_ref[...]) * _silu(g_ref[0, r, :].astype(F32))
        o_ref[0, r, :] = o.astype(BF16)
        return carry

    lax.fori_loop(0, seq // rb, phase3, 0, unroll=2)


def _hgrn(proj3, lb_logits, onorm_g, layer):
    bsz, seq, _ = proj3.shape
    nh = HGRN_HEADS

    def col(off):
        return pl.BlockSpec((1, seq, HGRN_DK), lambda b, h: (b, 0, off * nh + h))

    nl = lb_logits.shape[0]
    return pl.pallas_call(
        functools.partial(_hgrn_kernel, layer=layer),
        grid=(bsz, nh),
        in_specs=[col(0), col(1), col(2), col(3),
                  pl.BlockSpec((nl, HGRN_DK), lambda b, h: (0, h)),
                  pl.BlockSpec((1, HGRN_DK), lambda b, h: (0, 0))],
        out_specs=pl.BlockSpec((1, seq, HGRN_DK), lambda b, h: (b, 0, h)),
        out_shape=jax.ShapeDtypeStruct((bsz, seq, HGRN_WIDTH), BF16),
        scratch_shapes=[pltpu.VMEM((seq, HGRN_DK), BF16),
                        pltpu.VMEM((seq // HGRN_CHUNK, HGRN_DK, HGRN_DK), F32),
                        pltpu.VMEM((seq // HGRN_CHUNK, HGRN_DK, HGRN_DK), BF16),
                        pltpu.VMEM((seq // HGRN_CHUNK, 1, HGRN_DK), F32),
                        pltpu.VMEM((seq, HGRN_DK), F32)],
        compiler_params=_cparams(("arbitrary", "arbitrary")),
        name="hgrn",
    )(proj3, proj3, proj3, proj3, lb_logits, onorm_g)


def _rope(x, cos, sin_signed, lane):
    half = MLA_ROPE // 2
    swapped = jnp.where(lane < half, pltpu.roll(x, LANES - half, 1), pltpu.roll(x, half, 1))
    return x * cos + swapped * sin_signed


def _up_kernel(p_ref, pos_ref, wq_ref, wkv_ref, qag_ref, kvag_ref, qg_ref, kg_ref, freq_ref,
               q_ref, k_ref, v_ref):
    p = p_ref[...].astype(F32)
    q_a = p[:, 0:Q_LORA]
    kv_a = p[:, Q_LORA:Q_LORA + KV_LORA]
    k_pe = p[:, Q_LORA + KV_LORA:Q_LORA + KV_LORA + LANES]
    qf = _dot(_rms(q_a, qag_ref[...]).astype(BF16), wq_ref[...])
    kvf = _dot(_rms(kv_a, kvag_ref[...]).astype(BF16), wkv_ref[...])

    tm = p.shape[0]
    lane = lax.broadcasted_iota(jnp.int32, (tm, LANES), 1)
    half = MLA_ROPE // 2
    ang = pos_ref[...].astype(F32) * freq_ref[...]
    valid = lane < MLA_ROPE
    cos = jnp.where(valid, jnp.cos(ang), 0.0)
    sin_signed = jnp.where(valid, jnp.where(lane < half, -jnp.sin(ang), jnp.sin(ang)), 0.0)

    qg = qg_ref[...]
    kg = kg_ref[...]
    kpe_ss = jnp.sum(k_pe * k_pe, axis=-1, keepdims=True)
    kpe_rot = _rope(k_pe * kg[:, MLA_NOPE:], cos, sin_signed, lane)
    scale = MLA_QK ** -0.5 * LOG2E
    for h in range(MLA_HEADS):
        qh = qf[:, h * MLA_QPAD:(h + 1) * MLA_QPAD]
        rq = lax.rsqrt(jnp.sum(qh * qh, axis=-1, keepdims=True) / MLA_QK + EPS)
        qn = qh * rq * qg
        q_ref[0, h, :, 0:MLA_NOPE] = (qn[:, 0:MLA_NOPE] * scale).astype(BF16)
        q_ref[0, h, :, MLA_NOPE:] = (_rope(qn[:, MLA_NOPE:], cos, sin_signed, lane) * scale).astype(BF16)
        kn = kvf[:, h * MLA_QPAD:h * MLA_QPAD + MLA_NOPE]
        rk = lax.rsqrt((jnp.sum(kn * kn, axis=-1, keepdims=True) + kpe_ss) / MLA_QK + EPS)
        k_ref[0, h, :, 0:MLA_NOPE] = (kn * rk * kg[:, 0:MLA_NOPE]).astype(BF16)
        k_ref[0, h, :, MLA_NOPE:] = (kpe_rot * rk).astype(BF16)
        v_ref[0, h, :, :] = kvf[:, h * MLA_QPAD + MLA_NOPE:(h + 1) * MLA_QPAD].T.astype(BF16)


def _mla_up(proj3, pos3, wq, wkv, qag, kvag, qg, kg, freq):
    bsz, seq, _ = proj3.shape
    tm = min(512, seq)
    nh = MLA_HEADS
    mla_block = 4 * HGRN_WIDTH // 1024

    def const(shape):
        return pl.BlockSpec(shape, lambda b, i: (0,) * len(shape))

    return pl.pallas_call(
        _up_kernel,
        grid=(bsz, seq // tm),
        in_specs=[pl.BlockSpec((None, tm, 1024), lambda b, i: (b, i, mla_block)),
                  pl.BlockSpec((None, tm, 1), lambda b, i: (b, i, 0)),
                  const(wq.shape), const(wkv.shape), const(qag.shape), const(kvag.shape),
                  const(qg.shape), const(kg.shape), const(freq.shape)],
        out_specs=[pl.BlockSpec((1, nh, tm, MLA_QPAD), lambda b, i: (b, 0, i, 0)),
                   pl.BlockSpec((1, nh, tm, MLA_QPAD), lambda b, i: (b, 0, i, 0)),
                   pl.BlockSpec((1, nh, MLA_V, tm), lambda b, i: (b, 0, 0, i))],
        out_shape=[jax.ShapeDtypeStruct((bsz, nh, seq, MLA_QPAD), BF16),
                   jax.ShapeDtypeStruct((bsz, nh, seq, MLA_QPAD), BF16),
                   jax.ShapeDtypeStruct((bsz, nh, MLA_V, seq), BF16)],
        compiler_params=_cparams(("arbitrary", "arbitrary")),
        name="mla_up",
    )(proj3, pos3, wq, wkv, qag, kvag, qg, kg, freq)


ATT_T = 256


def _attn_kernel(q_ref, k_ref, vt_ref, g_ref, o_ref):
    seq = q_ref.shape[2]
    t = min(ATT_T, seq)
    key = lax.broadcasted_iota(jnp.int32, (t, t), 0)
    qry = lax.broadcasted_iota(jnp.int32, (t, t), 1)
    causal = key <= qry
    neg = jnp.finfo(F32).min
    for qi in range(seq // t):
        off = qi * t
        q = q_ref[0, 0, off:off + t, :]
        sd = jnp.where(causal, _dot_nt(k_ref[0, 0, off:off + t, :], q), neg)
        m = jnp.max(sd, axis=0, keepdims=True)
        if qi > 0:
            so = _dot_nt(k_ref[0, 0, 0:off, :], q)
            m = jnp.maximum(m, jnp.max(so, axis=0, keepdims=True))
        pd = jnp.exp2(sd - m)
        l = jnp.sum(pd, axis=0, keepdims=True)
        ot = _dot(vt_ref[0, 0, :, off:off + t], pd.astype(BF16))
        if qi > 0:
            po = jnp.exp2(so - m)
            l = l + jnp.sum(po, axis=0, keepdims=True)
            ot = ot + _dot(vt_ref[0, 0, :, 0:off], po.astype(BF16))
        ot = ot * (1.0 / l)
        ot = ot * lax.rsqrt(jnp.mean(ot * ot, axis=0, keepdims=True) + EPS) * g_ref[...]
        o_ref[0, off:off + t, :] = ot.T.astype(BF16)


def _attention(q, k, v, g):
    bsz, nh, seq, _ = q.shape
    return pl.pallas_call(
        _attn_kernel,
        grid=(bsz, nh),
        in_specs=[pl.BlockSpec((1, 1, seq, MLA_QPAD), lambda b, h: (b, h, 0, 0)),
                  pl.BlockSpec((1, 1, seq, MLA_QPAD), lambda b, h: (b, h, 0, 0)),
                  pl.BlockSpec((1, 1, MLA_V, seq), lambda b, h: (b, h, 0, 0)),
                  pl.BlockSpec((MLA_V, 1), lambda b, h: (0, 0))],
        out_specs=pl.BlockSpec((1, seq, MLA_V), lambda b, h: (b, 0, h)),
        out_shape=jax.ShapeDtypeStruct((bsz, seq, nh * MLA_V), BF16),
        compiler_params=_cparams(("arbitrary", "arbitrary")),
        name="attn",
    )(q, k, v, g)


def _out_kernel(oa_ref, ob_ref, x_ref, wa_ref, wb_ref, g1_ref, n2g_ref, sh2_ref, sc2_ref,
                wr_ref, br_ref, x1_ref, h2_ref, lg_ref):
    mix = _dot(oa_ref[...], wa_ref[...]) + _dot(ob_ref[...], wb_ref[...])
    x1 = x_ref[...] + g1_ref[0] * mix
    x1_ref[...] = x1
    h2 = _rms(x1, n2g_ref[...]) * (1.0 + sc2_ref[0]) + sh2_ref[0]
    h2_ref[...] = h2
    lg = _dot(h2.astype(BF16), wr_ref[...]) + br_ref[...]
    lg_ref[...] = lg.T[0:ROUTE_ROWS, :]


def _out_proj(oa, ob, x2, wa, wb, g1, n2g, sh2, sc2, wr, br, seq):
    t, d = x2.shape
    tm = min(512, seq)

    def const(shape):
        return pl.BlockSpec(shape, lambda i: (0,) * len(shape))

    def per_batch():
        return pl.BlockSpec((1, 1, d), lambda i: (i * tm // seq, 0, 0))

    return pl.pallas_call(
        _out_kernel,
        grid=(t // tm,),
        in_specs=[pl.BlockSpec((tm, HGRN_WIDTH), lambda i: (i, 0)),
                  pl.BlockSpec((tm, HGRN_WIDTH), lambda i: (i, 0)),
                  pl.BlockSpec((tm, d), lambda i: (i, 0)),
                  const(wa.shape), const(wb.shape), per_batch(), const(n2g.shape),
                  per_batch(), per_batch(), const(wr.shape), const(br.shape)],
        out_specs=[pl.BlockSpec((tm, d), lambda i: (i, 0)),
                   pl.BlockSpec((tm, d), lambda i: (i, 0)),
                   pl.BlockSpec((ROUTE_ROWS, tm), lambda i: (0, i))],
        out_shape=[jax.ShapeDtypeStruct((t, d), F32),
                   jax.ShapeDtypeStruct((t, d), F32),
                   jax.ShapeDtypeStruct((ROUTE_ROWS, t), F32)],
        compiler_params=_cparams(("arbitrary",)),
        name="out_proj",
    )(oa, ob, x2, wa, wb, g1, n2g, sh2, sc2, wr, br)


def _route_kernel(lg_ref, tri_ref, ri_ref, rw_ref, cnt_ref, carry_s):
    step = pl.program_id(0)

    @pl.when(step == 0)
    def _():
        carry_s[...] = jnp.zeros_like(carry_s)

    lg = lg_ref[...]
    tr = lg.shape[1]
    epg = EXPERTS_PER_GROUP
    gl = lg[N_EXPERTS:N_EXPERTS + N_GROUPS, :]
    row_g = lax.broadcasted_iota(jnp.int32, (N_GROUPS, tr), 0)
    gmax = jnp.max(gl, axis=0, keepdims=True)
    g_sel = jnp.min(jnp.where(gl == gmax, row_g, N_GROUPS), axis=0, keepdims=True)
    p_group = 1.0 / jnp.sum(jnp.exp(gl - gmax), axis=0, keepdims=True)

    e_in = lg[0:epg, :]
    for g in range(1, N_GROUPS):
        e_in = jnp.where(g_sel == g, lg[g * epg:(g + 1) * epg, :], e_in)
    row_e = lax.broadcasted_iota(jnp.int32, (epg, tr), 0)
    top1 = jnp.max(e_in, axis=0, keepdims=True)
    i1 = jnp.min(jnp.where(e_in == top1, row_e, epg), axis=0, keepdims=True)
    rest = jnp.where(row_e == i1, -jnp.inf, e_in)
    top2 = jnp.max(rest, axis=0, keepdims=True)
    i2 = jnp.min(jnp.where(rest == top2, row_e, epg), axis=0, keepdims=True)
    e2w = jnp.exp(top2 - top1)
    w1 = p_group / (1.0 + e2w)
    w2 = p_group * e2w / (1.0 + e2w)
    ex1 = g_sel * epg + i1
    ex2 = g_sel * epg + i2

    row_x = lax.broadcasted_iota(jnp.int32, (N_EXPERTS, tr), 0)
    oh1 = row_x == ex1
    oh2 = row_x == ex2
    oh = jnp.logical_or(oh1, oh2)
    before = _dot(oh.astype(BF16), tri_ref[...]) + carry_s[:, 0:1]
    rank1 = jnp.sum(jnp.where(oh1, before, 0.0), axis=0, keepdims=True)
    rank2 = jnp.sum(jnp.where(oh2, before, 0.0), axis=0, keepdims=True)
    carry_s[...] = carry_s[...] + jnp.sum(oh.astype(F32), axis=1, keepdims=True)

    zi = jnp.zeros((4, tr), jnp.int32)
    ri_ref[...] = jnp.concatenate([ex1, ex2, rank1.astype(jnp.int32), rank2.astype(jnp.int32), zi], axis=0)
    rw_ref[...] = jnp.concatenate([w1, w2, jnp.zeros((6, tr), F32)], axis=0)
    cnt_ref[...] = carry_s[...].astype(jnp.int32)


def _route(lg_t, tri):
    t = lg_t.shape[1]
    tr = tri.shape[0]
    return pl.pallas_call(
        _route_kernel,
        grid=(t // tr,),
        in_specs=[pl.BlockSpec((ROUTE_ROWS, tr), lambda i: (0, i)),
                  pl.BlockSpec((tr, tr), lambda i: (0, 0))],
        out_specs=[pl.BlockSpec((8, tr), lambda i: (0, i)),
                   pl.BlockSpec((8, tr), lambda i: (0, i)),
                   pl.BlockSpec((N_EXPERTS, LANES), lambda i: (0, 0))],
        out_shape=[jax.ShapeDtypeStruct((8, t), jnp.int32),
                   jax.ShapeDtypeStruct((8, t), F32),
                   jax.ShapeDtypeStruct((N_EXPERTS, LANES), jnp.int32)],
        scratch_shapes=[pltpu.VMEM((N_EXPERTS, LANES), F32)],
        compiler_params=_cparams(("arbitrary",)),
        name="route",
    )(lg_t, tri)


def _row_copy(src, s, dst, d, sem):
    return pltpu.make_async_copy(src.at[pl.ds(s, 1), :], dst.at[pl.ds(d, 1), :], sem)


def _dispatch_kernel(d1_ref, d2_ref, h_ref, xb_ref, sem):
    td = h_ref.shape[0]
    base = pl.program_id(0) * td

    def start(r, carry):
        _row_copy(h_ref, r, xb_ref, d1_ref[base + r], sem).start()
        _row_copy(h_ref, r, xb_ref, d2_ref[base + r], sem).start()
        return carry

    lax.fori_loop(0, td, start, 0)

    def wait(r, carry):
        _row_copy(h_ref, 0, xb_ref, 0, sem).wait()
        _row_copy(h_ref, 0, xb_ref, 0, sem).wait()
        return carry

    lax.fori_loop(0, td, wait, 0)


def _dispatch(dest1, dest2, h2):
    t, d = h2.shape
    td = min(256, t)
    return pl.pallas_call(
        _dispatch_kernel,
        grid_spec=pltpu.PrefetchScalarGridSpec(
            num_scalar_prefetch=2,
            grid=(t // td,),
            in_specs=[pl.BlockSpec((td, d), lambda i, d1, d2: (i, 0))],
            out_specs=pl.BlockSpec(memory_space=pl.ANY),
            scratch_shapes=[pltpu.SemaphoreType.DMA(())]),
        out_shape=jax.ShapeDtypeStruct((2 * t, d), h2.dtype),
        compiler_params=_cparams(("arbitrary",)),
        name="dispatch",
    )(dest1, dest2, h2)


def _moe_kernel(blk_ref, exp_ref, lo_ref, hi_ref, first_ref, newe_ref,
                x_ref, wg_ref, wu_ref, wd_ref, o_ref, wg_s, wu_s, wd_s):
    w = pl.program_id(0)

    @pl.when(newe_ref[w] == 1)
    def _():
        wg_s[...] = wg_ref[0].astype(BF16)
        wu_s[...] = wu_ref[0].astype(BF16)
        wd_s[...] = wd_ref[0].astype(BF16)

    @pl.when(first_ref[w] == 1)
    def _():
        o_ref[...] = jnp.zeros_like(o_ref)

    lo = lo_ref[w]
    hi = hi_ref[w]

    @pl.when(hi > lo)
    def _():
        x = x_ref[...].astype(BF16)
        hid = _silu(_dot(x, wg_s[...])) * _dot(x, wu_s[...])
        y = _dot(hid.astype(BF16), wd_s[...])
        row = blk_ref[w] * x_ref.shape[0] + lax.broadcasted_iota(jnp.int32, (x_ref.shape[0], 1), 0)
        own = jnp.logical_and(row >= lo, row < hi)
        o_ref[...] = jnp.where(own, y, o_ref[...])


def _moe(items, x_buf, wg, wu, wd):
    r, d = x_buf.shape
    n_items = items[0].shape[0]
    tm = MOE_BLOCK
    de = wg.shape[2]
    return pl.pallas_call(
        _moe_kernel,
        grid_spec=pltpu.PrefetchScalarGridSpec(
            num_scalar_prefetch=6,
            grid=(n_items,),
            in_specs=[pl.BlockSpec((tm, d), lambda w, blk, ex, lo, hi, fi, ne: (blk[w], 0)),
                      pl.BlockSpec((1, d, de), lambda w, blk, ex, lo, hi, fi, ne: (ex[w], 0, 0)),
                      pl.BlockSpec((1, d, de), lambda w, blk, ex, lo, hi, fi, ne: (ex[w], 0, 0)),
                      pl.BlockSpec((1, de, d), lambda w, blk, ex, lo, hi, fi, ne: (ex[w], 0, 0))],
            out_specs=pl.BlockSpec((tm, d), lambda w, blk, ex, lo, hi, fi, ne: (blk[w], 0)),
            scratch_shapes=[pltpu.VMEM((d, de), BF16), pltpu.VMEM((d, de), BF16), pltpu.VMEM((de, d), BF16)]),
        out_shape=jax.ShapeDtypeStruct((r, d), F32),
        compiler_params=_cparams(("arbitrary",)),
        name="moe",
    )(*items, x_buf, wg, wu, wd)


def _moe_items(counts, n_rows):
    n_blocks = n_rows // MOE_BLOCK
    end = jnp.cumsum(counts)
    start = end - counts
    bnd = jnp.sort(jnp.concatenate([jnp.arange(n_blocks, dtype=jnp.int32) * MOE_BLOCK, start.astype(jnp.int32)]))
    nxt = jnp.concatenate([bnd[1:], jnp.array([n_rows], jnp.int32)])
    blk = jnp.minimum(bnd // MOE_BLOCK, n_blocks - 1)
    exp = jnp.minimum(jnp.searchsorted(end, bnd, side="right"), N_EXPERTS - 1).astype(jnp.int32)
    prev_blk = jnp.concatenate([jnp.array([-1], jnp.int32), blk[:-1]])
    prev_exp = jnp.concatenate([jnp.array([-1], jnp.int32), exp[:-1]])
    first = (blk != prev_blk).astype(jnp.int32)
    newe = (exp != prev_exp).astype(jnp.int32)
    return blk, exp, bnd, nxt, first, newe


def _combine_kernel(d1_ref, d2_ref, x1_ref, g2_ref, w1_ref, w2_ref, yb_ref, o_ref, y1_s, y2_s, sem):
    tc = x1_ref.shape[0]
    base = pl.program_id(0) * tc

    def start(r, carry):
        _row_copy(yb_ref, d1_ref[base + r], y1_s, r, sem).start()
        _row_copy(yb_ref, d2_ref[base + r], y2_s, r, sem).start()
        return carry

    lax.fori_loop(0, tc, start, 0)

    def wait(r, carry):
        _row_copy(yb_ref, 0, y1_s, 0, sem).wait()
        _row_copy(yb_ref, 0, y2_s, 0, sem).wait()
        return carry

    lax.fori_loop(0, tc, wait, 0)
    y = w1_ref[...] * y1_s[...] + w2_ref[...] * y2_s[...]
    o_ref[...] = x1_ref[...] + g2_ref[0] * y


def _combine(dest1, dest2, x1, g2, w1, w2, y_buf, seq):
    t, d = x1.shape
    tc = min(256, seq)
    return pl.pallas_call(
        _combine_kernel,
        grid_spec=pltpu.PrefetchScalarGridSpec(
            num_scalar_prefetch=2,
            grid=(t // tc,),
            in_specs=[pl.BlockSpec((tc, d), lambda i, d1, d2: (i, 0)),
                      pl.BlockSpec((1, 1, d), lambda i, d1, d2: (i * tc // seq, 0, 0)),
                      pl.BlockSpec((tc, 1), lambda i, d1, d2: (i, 0)),
                      pl.BlockSpec((tc, 1), lambda i, d1, d2: (i, 0)),
                      pl.BlockSpec(memory_space=pl.ANY)],
            out_specs=pl.BlockSpec((tc, d), lambda i, d1, d2: (i, 0)),
            scratch_shapes=[pltpu.VMEM((tc, d), F32), pltpu.VMEM((tc, d), F32), pltpu.SemaphoreType.DMA(())]),
        out_shape=jax.ShapeDtypeStruct((t, d), F32),
        compiler_params=_cparams(("arbitrary",)),
        name="combine",
    )(dest1, dest2, x1, g2, w1, w2, y_buf)


def _q_up_layout(w_q_up):
    w = w_q_up.reshape(Q_LORA, MLA_HEADS, MLA_QK)
    w = jnp.pad(w, ((0, 0), (0, 0), (0, MLA_QPAD - MLA_QK)))
    return w.reshape(Q_LORA, MLA_HEADS * MLA_QPAD).astype(BF16)


def _pad_lanes(g, width):
    return jnp.pad(g, (0, width - g.shape[0])).reshape(1, width)


def kernel(x, c, positions, w_ada, b_ada, norm1_g, w_in, hgrn_lb_logits, hgrn_onorm_g, q_a_norm_g, w_q_up,
           kv_a_norm_g, w_kv_up, q_norm_g, k_norm_g, attn_onorm_g, w_out, norm2_g, w_group, b_group,
           w_router, b_router, w_gate, w_up, w_down):
    bsz, seq, d = x.shape
    t = bsz * seq
    depth = w_ada.shape[0]
    half = MLA_ROPE // 2
    inv_freq = ROPE_BASE ** (-jnp.arange(0, MLA_ROPE, 2, dtype=F32) / MLA_ROPE)
    freq = jnp.concatenate([inv_freq, inv_freq, jnp.zeros((LANES - 2 * half,), F32)]).reshape(1, LANES)
    pos3 = positions.reshape(bsz, seq, 1)
    tr = min(512, t)
    tri = jnp.triu(jnp.ones((tr, tr), BF16), 1)

    x2 = x.reshape(t, d)
    for l in range(depth):
        mod = _ada(c, w_ada[l], b_ada[l]).reshape(bsz, 6, 1, d)
        sh1, sc1, g1, sh2, sc2, g2 = (mod[:, i] for i in range(6))

        w_in_p = jnp.pad(w_in[l], ((0, 0), (0, IN_COLS_PAD - IN_COLS))).astype(BF16)
        proj = _in_proj(x2, norm1_g[l].reshape(1, d), sh1, sc1, w_in_p, seq)
        proj3 = proj.reshape(bsz, seq, IN_COLS_PAD)

        o_a = _hgrn(proj3, hgrn_lb_logits, hgrn_onorm_g[l].reshape(1, HGRN_DK), l)

        q, k, v = _mla_up(proj3, pos3, _q_up_layout(w_q_up[l]), w_kv_up[l].astype(BF16),
                          q_a_norm_g[l].reshape(1, Q_LORA), kv_a_norm_g[l].reshape(1, KV_LORA),
                          _pad_lanes(q_norm_g[l], MLA_QPAD), _pad_lanes(k_norm_g[l], MLA_QPAD), freq)
        o_b = _attention(q, k, v, attn_onorm_g[l].reshape(MLA_V, 1))

        w_o = w_out[l].astype(BF16)
        wr = jnp.pad(jnp.concatenate([w_router[l], w_group[l]], axis=1),
                     ((0, 0), (0, LANES - N_EXPERTS - N_GROUPS))).astype(BF16)
        br = _pad_lanes(jnp.concatenate([b_router[l], b_group[l]]), LANES)
        x1, h2, lg_t = _out_proj(o_a.reshape(t, HGRN_WIDTH), o_b.reshape(t, HGRN_WIDTH), x2,
                                 w_o[:HGRN_WIDTH], w_o[HGRN_WIDTH:], g1, norm2_g[l].reshape(1, d),
                                 sh2, sc2, wr, br, seq)

        ri, rw, cnt = _route(lg_t, tri)
        counts = cnt[:, 0]
        start = jnp.cumsum(counts) - counts
        dest1 = start[ri[0]] + ri[2]
        dest2 = start[ri[1]] + ri[3]
        x_buf = _dispatch(dest1, dest2, h2)
        y_buf = _moe(_moe_items(counts, 2 * t), x_buf, w_gate[l], w_up[l], w_down[l])
        x2 = _combine(dest1, dest2, x1, g2, rw[0].reshape(t, 1), rw[1].reshape(t, 1), y_buf, seq)
    return x2.reshape(bsz, seq, d)
```

```python
import functools

import jax
import jax.numpy as jnp
from jax import lax
from jax.experimental import pallas as pl
from jax.experimental.pallas import tpu as pltpu

F32 = jnp.float32
BF16 = jnp.bfloat16
EPS = 1e-6
LOG2E = 1.4426950408889634

D_MODEL = 2048
HGRN_WIDTH = 1024
HGRN_DK = 128
HGRN_HEADS = 8
HGRN_CHUNK = 64
MLA_HEADS = 8
MLA_NOPE = 128
MLA_ROPE = 64
MLA_QK = MLA_NOPE + MLA_ROPE
MLA_V = 128
MLA_QPAD = 256
Q_LORA = 512
KV_LORA = 256
ROPE_BASE = 10000.0
IN_COLS = 4 * HGRN_WIDTH + Q_LORA + KV_LORA + MLA_ROPE
IN_COLS_PAD = 5120
N_GROUPS = 4
EXPERTS_PER_GROUP = 8
N_EXPERTS = 32
D_EXPERT = 512
ROUTE_ROWS = 40
MOE_BLOCK = 256
LANES = 128
TOK_SUBLANES = 8
VMEM_LIMIT = 56 * 1024 * 1024


def _cparams(sem):
    return pltpu.CompilerParams(dimension_semantics=sem, vmem_limit_bytes=VMEM_LIMIT)


def _dot(a, b):
    return jnp.dot(a, b, preferred_element_type=F32)


def _dot_nt(a, b):
    return lax.dot_general(a, b, (((1,), (1,)), ((), ())), preferred_element_type=F32)


def _rms(x, g):
    return x * lax.rsqrt(jnp.mean(x * x, axis=-1, keepdims=True) + EPS) * g


def _silu(x):
    return x * jax.nn.sigmoid(x)


def _pack_pair(lo, hi):
    lo_b = lax.bitcast_convert_type(lo.astype(BF16).astype(F32), jnp.uint32)
    hi_b = lax.bitcast_convert_type(hi.astype(BF16).astype(F32), jnp.uint32)
    return hi_b | (lo_b >> 16)


def _unpack_pair(w):
    lo = lax.bitcast_convert_type(w << 16, F32)
    hi = lax.bitcast_convert_type(w & jnp.uint32(0xFFFF0000), F32)
    return lo, hi


def _ada_kernel(c_ref, w_ref, b_ref, o_ref):
    ca = _silu(c_ref[...]).astype(BF16)
    o_ref[...] = _dot(ca, w_ref[...].astype(BF16)) + b_ref[...]


def _ada(c, w, b):
    bsz, d = c.shape
    n = w.shape[1]
    tn = 1024
    return pl.pallas_call(
        _ada_kernel,
        grid=(n // tn,),
        in_specs=[pl.BlockSpec((bsz, d), lambda j: (0, 0)),
                  pl.BlockSpec((d, tn), lambda j: (0, j)),
                  pl.BlockSpec((1, tn), lambda j: (0, j))],
        out_specs=pl.BlockSpec((bsz, tn), lambda j: (0, j)),
        out_shape=jax.ShapeDtypeStruct((bsz, n), F32),
        compiler_params=_cparams(("arbitrary",)),
        name="ada",
    )(c, w, b.reshape(1, n))


def _in_kernel(x_ref, g_ref, sh_ref, sc_ref, w_ref, o_ref, h_ref, *, rows):
    @pl.when(pl.program_id(1) == 0)
    def _():
        def body(c, carry):
            r = pl.ds(pl.multiple_of(c * rows, rows), rows)
            h = _rms(x_ref[r, :], g_ref[...]) * (1.0 + sc_ref[0]) + sh_ref[0]
            h_ref[r, :] = h.astype(BF16)
            return carry
        lax.fori_loop(0, x_ref.shape[0] // rows, body, 0, unroll=2)

    o_ref[...] = _dot(h_ref[...], w_ref[...]).astype(BF16)


def _in_proj(x2, g, sh, sc, w, seq):
    t, d = x2.shape
    n = w.shape[1]
    tm = min(1024, seq)
    tn = 1024
    rows = 32
    return pl.pallas_call(
        functools.partial(_in_kernel, rows=rows),
        grid=(t // tm, n // tn),
        in_specs=[pl.BlockSpec((tm, d), lambda i, j: (i, 0)),
                  pl.BlockSpec((1, d), lambda i, j: (0, 0)),
                  pl.BlockSpec((1, 1, d), lambda i, j: (i * tm // seq, 0, 0)),
                  pl.BlockSpec((1, 1, d), lambda i, j: (i * tm // seq, 0, 0)),
                  pl.BlockSpec((d, tn), lambda i, j: (0, j))],
        out_specs=pl.BlockSpec((tm, tn), lambda i, j: (i, j)),
        out_shape=jax.ShapeDtypeStruct((t, n), BF16),
        scratch_shapes=[pltpu.VMEM((tm, d), BF16)],
        compiler_params=_cparams(("arbitrary", "arbitrary")),
        name="in_proj",
    )(x2, g, sh, sc, w)


HG_ROWS = 256


def _chunk_mask(n):
    row = lax.broadcasted_iota(jnp.int32, (n, n), 0)
    col = lax.broadcasted_iota(jnp.int32, (n, n), 1)
    return jnp.logical_and(row // HGRN_CHUNK == col // HGRN_CHUNK, col <= row)


def _hgrn_kernel(q_ref, f_ref, i_ref, g_ref, lbl_ref, og_ref, o_ref,
                 qb_s, u_s, sp_s, dec_s, oi_s, *, layer):
    seq = q_ref.shape[1]
    cs = HGRN_CHUNK
    rb = min(HG_ROWS, seq)
    nc = rb // cs
    dk = HGRN_DK
    lg = lbl_ref[...]
    ex = jnp.exp(lg - jnp.max(lg, axis=0, keepdims=True))
    sm = ex / jnp.sum(ex, axis=0, keepdims=True)
    lb = jnp.sum(sm[0:layer + 1], axis=0, keepdims=True)
    mask = _chunk_mask(rb)
    tri = mask.astype(BF16)
    row_chunk = lax.broadcasted_iota(jnp.int32, (rb, dk), 0) // cs

    def phase1(blk, carry):
        r = pl.ds(pl.multiple_of(blk * rb, rb), rb)
        f = lb + (1.0 - lb) * jax.nn.sigmoid(f_ref[0, r, :].astype(F32))
        lf = jnp.log(f)
        k = 1.0 - f
        hi = lf.astype(BF16)
        r1 = lf - hi.astype(F32)
        mid = r1.astype(BF16)
        lo = (r1 - mid.astype(F32)).astype(BF16)
        bhm = _dot(tri, jnp.concatenate([hi, mid], axis=1))
        b = bhm[:, 0:dk] + bhm[:, dk:] + _dot(tri, lo)
        b3 = b.reshape(nc, cs, dk)
        bmid = b3[:, cs // 2 - 1:cs // 2, :]
        blast = b3[:, cs - 1:cs, :]
        q3 = (q_ref[0, r, :].astype(F32) * dk ** -0.5).reshape(nc, cs, dk)
        k3 = k.reshape(nc, cs, dk)
        qin = (q3 * jnp.exp(b3 - bmid)).reshape(rb, dk).astype(BF16)
        kin = (k3 * jnp.exp(bmid - b3)).reshape(rb, dk).astype(BF16)
        ku = (k3 * jnp.exp(blast - b3)).reshape(rb, dk)
        qb_s[r, :] = (q3 * jnp.exp(b3)).reshape(rb, dk).astype(BF16)
        dec_s[pl.ds(blk * nc, nc)] = jnp.exp(blast)
        v = i_ref[0, r, :]
        a = jnp.where(mask, _dot_nt(qin, kin), 0.0).astype(BF16)
        oi_s[r, :] = _dot(a, v)
        vt = v.astype(F32).T.astype(BF16)
        ku_exp = jnp.concatenate(
            [jnp.where(row_chunk == c, ku, 0.0).astype(BF16) for c in range(nc)], axis=1)
        ut = _dot(vt, ku_exp)
        for c in range(nc):
            u_s[blk * nc + c] = ut[:, c * dk:(c + 1) * dk]
        return carry

    lax.fori_loop(0, seq // rb, phase1, 0, unroll=2)

    def phase2(n, st):
        sp_s[n] = st.astype(BF16)
        return st * dec_s[n] + u_s[n]

    lax.fori_loop(0, seq // cs, phase2, jnp.zeros((dk, dk), F32), unroll=4)

    def phase3(blk, carry):
        r = pl.ds(pl.multiple_of(blk * rb, rb), rb)
        inter = [_dot_nt(qb_s[pl.ds(pl.multiple_of(blk * rb + c * cs, cs), cs), :], sp_s[blk * nc + c])
                 for c in range(nc)]
        o = oi_s[r, :] + jnp.concatenate(inter, axis=0)
        o = _rms(o, og_ref[...]) * _silu(g_ref[0, r, :].astype(F32))
        o_ref[0, r, :] = o.astype(BF16)
        return carry

    lax.fori_loop(0, seq // rb, phase3, 0, unroll=2)


def _hgrn(proj3, lb_logits, onorm_g, layer):
    bsz, seq, _ = proj3.shape
    nh = HGRN_HEADS

    def col(off):
        return pl.BlockSpec((1, seq, HGRN_DK), lambda b, h: (b, 0, off * nh + h))

    nl = lb_logits.shape[0]
    return pl.pallas_call(
        functools.partial(_hgrn_kernel, layer=layer),
        grid=(bsz, nh),
        in_specs=[col(0), col(1), col(2), col(3),
                  pl.BlockSpec((nl, HGRN_DK), lambda b, h: (0, h)),
                  pl.BlockSpec((1, HGRN_DK), lambda b, h: (0, 0))],
        out_specs=pl.BlockSpec((1, seq, HGRN_DK), lambda b, h: (b, 0, h)),
        out_shape=jax.ShapeDtypeStruct((bsz, seq, HGRN_WIDTH), BF16),
        scratch_shapes=[pltpu.VMEM((seq, HGRN_DK), BF16),
                        pltpu.VMEM((seq // HGRN_CHUNK, HGRN_DK, HGRN_DK), F32),
                        pltpu.VMEM((seq // HGRN_CHUNK, HGRN_DK, HGRN_DK), BF16),
                        pltpu.VMEM((seq // HGRN_CHUNK, 1, HGRN_DK), F32),
                        pltpu.VMEM((seq, HGRN_DK), F32)],
        compiler_params=_cparams(("arbitrary", "arbitrary")),
        name="hgrn",
    )(proj3, proj3, proj3, proj3, lb_logits, onorm_g)


def _rope(x, cos, sin_signed, lane):
    half = MLA_ROPE // 2
    swapped = jnp.where(lane < half, pltpu.roll(x, LANES - half, 1), pltpu.roll(x, half, 1))
    return x * cos + swapped * sin_signed


def _up_kernel(p_ref, pos_ref, wq_ref, wkv_ref, qag_ref, kvag_ref, qg_ref, kg_ref, freq_ref,
               q_ref, k_ref, v_ref):
    p = p_ref[...].astype(F32)
    q_a = p[:, 0:Q_LORA]
    kv_a = p[:, Q_LORA:Q_LORA + KV_LORA]
    k_pe = p[:, Q_LORA + KV_LORA:Q_LORA + KV_LORA + LANES]
    qf = _dot(_rms(q_a, qag_ref[...]).astype(BF16), wq_ref[...])
    kvf = _dot(_rms(kv_a, kvag_ref[...]).astype(BF16), wkv_ref[...])

    tm = p.shape[0]
    lane = lax.broadcasted_iota(jnp.int32, (tm, LANES), 1)
    half = MLA_ROPE // 2
    ang = pos_ref[...].astype(F32) * freq_ref[...]
    valid = lane < MLA_ROPE
    cos = jnp.where(valid, jnp.cos(ang), 0.0)
    sin_signed = jnp.where(valid, jnp.where(lane < half, -jnp.sin(ang), jnp.sin(ang)), 0.0)

    qg = qg_ref[...]
    kg = kg_ref[...]
    kpe_ss = jnp.sum(k_pe * k_pe, axis=-1, keepdims=True)
    kpe_rot = _rope(k_pe * kg[:, MLA_NOPE:], cos, sin_signed, lane)
    scale = MLA_QK ** -0.5 * LOG2E
    for h in range(MLA_HEADS):
        qh = qf[:, h * MLA_QPAD:(h + 1) * MLA_QPAD]
        rq = lax.rsqrt(jnp.sum(qh * qh, axis=-1, keepdims=True) / MLA_QK + EPS)
        qn = qh * rq * qg
        q_ref[0, h, :, 0:MLA_NOPE] = (qn[:, 0:MLA_NOPE] * scale).astype(BF16)
        q_ref[0, h, :, MLA_NOPE:] = (_rope(qn[:, MLA_NOPE:], cos, sin_signed, lane) * scale).astype(BF16)
        kn = kvf[:, h * MLA_QPAD:h * MLA_QPAD + MLA_NOPE]
        rk = lax.rsqrt((jnp.sum(kn * kn, axis=-1, keepdims=True) + kpe_ss) / MLA_QK + EPS)
        k_ref[0, h, :, 0:MLA_NOPE] = (kn * rk * kg[:, 0:MLA_NOPE]).astype(BF16)
        k_ref[0, h, :, MLA_NOPE:] = (kpe_rot * rk).astype(BF16)
        v_ref[0, h, :, :] = kvf[:, h * MLA_QPAD + MLA_NOPE:(h + 1) * MLA_QPAD].T.astype(BF16)


def _mla_up(proj3, pos3, wq, wkv, qag, kvag, qg, kg, freq):
    bsz, seq, _ = proj3.shape
    tm = min(512, seq)
    nh = MLA_HEADS
    mla_block = 4 * HGRN_WIDTH // 1024

    def const(shape):
        return pl.BlockSpec(shape, lambda b, i: (0,) * len(shape))

    return pl.pallas_call(
        _up_kernel,
        grid=(bsz, seq // tm),
        in_specs=[pl.BlockSpec((None, tm, 1024), lambda b, i: (b, i, mla_block)),
                  pl.BlockSpec((None, tm, 1), lambda b, i: (b, i, 0)),
                  const(wq.shape), const(wkv.shape), const(qag.shape), const(kvag.shape),
                  const(qg.shape), const(kg.shape), const(freq.shape)],
        out_specs=[pl.BlockSpec((1, nh, tm, MLA_QPAD), lambda b, i: (b, 0, i, 0)),
                   pl.BlockSpec((1, nh, tm, MLA_QPAD), lambda b, i: (b, 0, i, 0)),
                   pl.BlockSpec((1, nh, MLA_V, tm), lambda b, i: (b, 0, 0, i))],
        out_shape=[jax.ShapeDtypeStruct((bsz, nh, seq, MLA_QPAD), BF16),
                   jax.ShapeDtypeStruct((bsz, nh, seq, MLA_QPAD), BF16),
                   jax.ShapeDtypeStruct((bsz, nh, MLA_V, seq), BF16)],
        compiler_params=_cparams(("arbitrary", "arbitrary")),
        name="mla_up",
    )(proj3, pos3, wq, wkv, qag, kvag, qg, kg, freq)


ATT_T = 256


def _attn_kernel(q_ref, k_ref, vt_ref, g_ref, o_ref):
    seq = q_ref.shape[2]
    t = min(ATT_T, seq)
    key = lax.broadcasted_iota(jnp.int32, (t, t), 0)
    qry = lax.broadcasted_iota(jnp.int32, (t, t), 1)
    causal = key <= qry
    neg = jnp.finfo(F32).min
    for qi in range(seq // t):
        off = qi * t
        q = q_ref[0, 0, off:off + t, :]
        sd = jnp.where(causal, _dot_nt(k_ref[0, 0, off:off + t, :], q), neg)
        m = jnp.max(sd, axis=0, keepdims=True)
        if qi > 0:
            so = _dot_nt(k_ref[0, 0, 0:off, :], q)
            m = jnp.maximum(m, jnp.max(so, axis=0, keepdims=True))
        pd = jnp.exp2(sd - m)
        l = jnp.sum(pd, axis=0, keepdims=True)
        ot = _dot(vt_ref[0, 0, :, off:off + t], pd.astype(BF16))
        if qi > 0:
            po = jnp.exp2(so - m)
            l = l + jnp.sum(po, axis=0, keepdims=True)
            ot = ot + _dot(vt_ref[0, 0, :, 0:off], po.astype(BF16))
        ot = ot * (1.0 / l)
        ot = ot * lax.rsqrt(jnp.mean(ot * ot, axis=0, keepdims=True) + EPS) * g_ref[...]
        o_ref[0, off:off + t, :] = ot.T.astype(BF16)


def _attention(q, k, v, g):
    bsz, nh, seq, _ = q.shape
    return pl.pallas_call(
        _attn_kernel,
        grid=(bsz, nh),
        in_specs=[pl.BlockSpec((1, 1, seq, MLA_QPAD), lambda b, h: (b, h, 0, 0)),
                  pl.BlockSpec((1, 1, seq, MLA_QPAD), lambda b, h: (b, h, 0, 0)),
                  pl.BlockSpec((1, 1, MLA_V, seq), lambda b, h: (b, h, 0, 0)),
                  pl.BlockSpec((MLA_V, 1), lambda b, h: (0, 0))],
        out_specs=pl.BlockSpec((1, seq, MLA_V), lambda b, h: (b, 0, h)),
        out_shape=jax.ShapeDtypeStruct((bsz, seq, nh * MLA_V), BF16),
        compiler_params=_cparams(("arbitrary", "arbitrary")),
        name="attn",
    )(q, k, v, g)


def _out_kernel(oa_ref, ob_ref, x_ref, wa_ref, wb_ref, g1_ref, n2g_ref, sh2_ref, sc2_ref,
                wr_ref, br_ref, x1_ref, h2_ref, lg_ref):
    mix = _dot(oa_ref[...], wa_ref[...]) + _dot(ob_ref[...], wb_ref[...])
    x1 = x_ref[...] + g1_ref[0] * mix
    x1_ref[...] = x1
    h2 = _rms(x1, n2g_ref[...]) * (1.0 + sc2_ref[0]) + sh2_ref[0]
    tm = h2.shape[0]
    for s in range(TOK_SUBLANES):
        h2_ref[pl.ds(s, tm, stride=TOK_SUBLANES), :] = _pack_pair(
            h2[:, s * LANES:(s + 1) * LANES], h2[:, (s + TOK_SUBLANES) * LANES:(s + TOK_SUBLANES + 1) * LANES])
    lg = _dot(h2.astype(BF16), wr_ref[...]) + br_ref[...]
    lg_ref[...] = lg.T[0:ROUTE_ROWS, :]


def _out_proj(oa, ob, x2, wa, wb, g1, n2g, sh2, sc2, wr, br, seq):
    t, d = x2.shape
    tm = min(512, seq)

    def const(shape):
        return pl.BlockSpec(shape, lambda i: (0,) * len(shape))

    def per_batch():
        return pl.BlockSpec((1, 1, d), lambda i: (i * tm // seq, 0, 0))

    return pl.pallas_call(
        _out_kernel,
        grid=(t // tm,),
        in_specs=[pl.BlockSpec((tm, HGRN_WIDTH), lambda i: (i, 0)),
                  pl.BlockSpec((tm, HGRN_WIDTH), lambda i: (i, 0)),
                  pl.BlockSpec((tm, d), lambda i: (i, 0)),
                  const(wa.shape), const(wb.shape), per_batch(), const(n2g.shape),
                  per_batch(), per_batch(), const(wr.shape), const(br.shape)],
        out_specs=[pl.BlockSpec((tm, d), lambda i: (i, 0)),
                   pl.BlockSpec((tm * TOK_SUBLANES, LANES), lambda i: (i, 0)),
                   pl.BlockSpec((ROUTE_ROWS, tm), lambda i: (0, i))],
        out_shape=[jax.ShapeDtypeStruct((t, d), F32),
                   jax.ShapeDtypeStruct((t * TOK_SUBLANES, LANES), jnp.uint32),
                   jax.ShapeDtypeStruct((ROUTE_ROWS, t), F32)],
        compiler_params=_cparams(("arbitrary",)),
        name="out_proj",
    )(oa, ob, x2, wa, wb, g1, n2g, sh2, sc2, wr, br)


def _route_kernel(lg_ref, tri_ref, ri_ref, rw_ref, cnt_ref, carry_s):
    step = pl.program_id(0)

    @pl.when(step == 0)
    def _():
        carry_s[...] = jnp.zeros_like(carry_s)

    lg = lg_ref[...]
    tr = lg.shape[1]
    epg = EXPERTS_PER_GROUP
    gl = lg[N_EXPERTS:N_EXPERTS + N_GROUPS, :]
    row_g = lax.broadcasted_iota(jnp.int32, (N_GROUPS, tr), 0)
    gmax = jnp.max(gl, axis=0, keepdims=True)
    g_sel = jnp.min(jnp.where(gl == gmax, row_g, N_GROUPS), axis=0, keepdims=True)
    p_group = 1.0 / jnp.sum(jnp.exp(gl - gmax), axis=0, keepdims=True)

    e_in = lg[0:epg, :]
    for g in range(1, N_GROUPS):
        e_in = jnp.where(g_sel == g, lg[g * epg:(g + 1) * epg, :], e_in)
    row_e = lax.broadcasted_iota(jnp.int32, (epg, tr), 0)
    top1 = jnp.max(e_in, axis=0, keepdims=True)
    i1 = jnp.min(jnp.where(e_in == top1, row_e, epg), axis=0, keepdims=True)
    rest = jnp.where(row_e == i1, -jnp.inf, e_in)
    top2 = jnp.max(rest, axis=0, keepdims=True)
    i2 = jnp.min(jnp.where(rest == top2, row_e, epg), axis=0, keepdims=True)
    e2w = jnp.exp(top2 - top1)
    w1 = p_group / (1.0 + e2w)
    w2 = p_group * e2w / (1.0 + e2w)
    ex1 = g_sel * epg + i1
    ex2 = g_sel * epg + i2

    row_x = lax.broadcasted_iota(jnp.int32, (N_EXPERTS, tr), 0)
    oh1 = row_x == ex1
    oh2 = row_x == ex2
    oh = jnp.logical_or(oh1, oh2)
    before = _dot(oh.astype(BF16), tri_ref[...]) + carry_s[:, 0:1]
    rank1 = jnp.sum(jnp.where(oh1, before, 0.0), axis=0, keepdims=True)
    rank2 = jnp.sum(jnp.where(oh2, before, 0.0), axis=0, keepdims=True)
    carry_s[...] = carry_s[...] + jnp.sum(oh.astype(F32), axis=1, keepdims=True)

    zi = jnp.zeros((4, tr), jnp.int32)
    ri_ref[...] = jnp.concatenate([ex1, ex2, rank1.astype(jnp.int32), rank2.astype(jnp.int32), zi], axis=0)
    rw_ref[...] = jnp.concatenate([w1, w2, jnp.zeros((6, tr), F32)], axis=0)
    cnt_ref[...] = carry_s[...].astype(jnp.int32)


def _route(lg_t, tri):
    t = lg_t.shape[1]
    tr = tri.shape[0]
    return pl.pallas_call(
        _route_kernel,
        grid=(t // tr,),
        in_specs=[pl.BlockSpec((ROUTE_ROWS, tr), lambda i: (0, i)),
                  pl.BlockSpec((tr, tr), lambda i: (0, 0))],
        out_specs=[pl.BlockSpec((8, tr), lambda i: (0, i)),
                   pl.BlockSpec((8, tr), lambda i: (0, i)),
                   pl.BlockSpec((N_EXPERTS, LANES), lambda i: (0, 0))],
        out_shape=[jax.ShapeDtypeStruct((8, t), jnp.int32),
                   jax.ShapeDtypeStruct((8, t), F32),
                   jax.ShapeDtypeStruct((N_EXPERTS, LANES), jnp.int32)],
        scratch_shapes=[pltpu.VMEM((N_EXPERTS, LANES), F32)],
        compiler_params=_cparams(("arbitrary",)),
        name="route",
    )(lg_t, tri)


def _moe_kernel(rt_ref, rd_ref, be_ref, ne_ref, nb_ref, h_ref, wg_ref, wu_ref, wd_ref, y_ref,
                xg, ys, wg_s, wu_s, wd_s, gsem, ssem):
    b = pl.program_id(0)
    last = pl.num_programs(0) - 1
    nb = nb_ref[0]
    slot = b % 2
    tm = MOE_BLOCK
    ts = TOK_SUBLANES

    def gather(blk, sl):
        base = blk * tm
        for j in range(tm):
            src = pl.multiple_of(rt_ref[base + j] * ts, ts)
            pltpu.make_async_copy(h_ref.at[pl.ds(src, ts), :], xg.at[sl, pl.ds(j * ts, ts), :], gsem.at[sl]).start()

    def wait_gather(sl):
        pltpu.make_async_copy(h_ref.at[pl.ds(0, tm * ts), :], xg.at[sl], gsem.at[sl]).wait()

    def scatter(blk):
        base = blk * tm
        for j in range(tm):
            dst = pl.multiple_of(rd_ref[base + j] * ts, ts)
            pltpu.make_async_copy(ys.at[pl.ds(j * ts, ts), :], y_ref.at[pl.ds(dst, ts), :], ssem).start()

    def wait_scatter():
        pltpu.make_async_copy(ys, y_ref.at[pl.ds(0, tm * ts), :], ssem).wait()

    @pl.when(b == 0)
    def _():
        ys[...] = jnp.zeros_like(ys)
        spare = pltpu.make_async_copy(ys, y_ref.at[pl.ds(y_ref.shape[0] - tm * ts, tm * ts), :], ssem)
        spare.start()
        spare.wait()
        gather(0, 0)

    @pl.when(ne_ref[b] == 1)
    def _():
        wg_s[...] = wg_ref[0].astype(BF16)
        wu_s[...] = wu_ref[0].astype(BF16)
        wd_s[...] = wd_ref[0].astype(BF16)

    @pl.when(b < nb)
    def _():
        wait_gather(slot)
        gather(jnp.minimum(b + 1, last), 1 - slot)
        scatter(jnp.maximum(b - 1, 0))
        parts = [_unpack_pair(xg[slot, pl.ds(s, tm, stride=ts), :]) for s in range(ts)]
        x = jnp.concatenate([p[0] for p in parts] + [p[1] for p in parts], axis=1).astype(BF16)
        hid = _silu(_dot(x, wg_s[...])) * _dot(x, wu_s[...])
        y = _dot(hid.astype(BF16), wd_s[...])
        wait_scatter()
        for s in range(ts):
            ys[pl.ds(s, tm, stride=ts), :] = _pack_pair(y[:, s * LANES:(s + 1) * LANES],
                                                        y[:, (s + ts) * LANES:(s + ts + 1) * LANES])

        @pl.when(b == last)
        def _():
            wait_gather(1 - slot)
            scatter(b)
            wait_scatter()

    @pl.when(b == nb)
    def _():
        wait_gather(slot)
        scatter(b - 1)
        wait_scatter()


def _moe(row_tok, row_dst, blk_e, new_e, nb_real, h2p, wg, wu, wd, n_out_rows):
    n_blocks = blk_e.shape[0]
    d, de = wg.shape[1], wg.shape[2]
    tm = MOE_BLOCK

    def wspec(shape):
        return pl.BlockSpec(shape, lambda b, rt, rd, be, ne, nb: (be[b], 0, 0))

    return pl.pallas_call(
        _moe_kernel,
        grid_spec=pltpu.PrefetchScalarGridSpec(
            num_scalar_prefetch=5,
            grid=(n_blocks,),
            in_specs=[pl.BlockSpec(memory_space=pl.ANY), wspec((1, d, de)), wspec((1, d, de)), wspec((1, de, d))],
            out_specs=pl.BlockSpec(memory_space=pl.ANY),
            scratch_shapes=[pltpu.VMEM((2, tm * TOK_SUBLANES, LANES), jnp.uint32),
                            pltpu.VMEM((tm * TOK_SUBLANES, LANES), jnp.uint32),
                            pltpu.VMEM((d, de), BF16), pltpu.VMEM((d, de), BF16), pltpu.VMEM((de, d), BF16),
                            pltpu.SemaphoreType.DMA((2,)), pltpu.SemaphoreType.DMA(())]),
        out_shape=jax.ShapeDtypeStruct((n_out_rows * TOK_SUBLANES, LANES), jnp.uint32),
        compiler_params=_cparams(("arbitrary",)),
        name="moe",
    )(row_tok, row_dst, blk_e, new_e, nb_real, h2p, wg, wu, wd)


def _moe_plan(ri, counts, t):
    tm = MOE_BLOCK
    n_blocks = 2 * t // tm + N_EXPERTS
    n_rows = n_blocks * tm
    padded = ((counts + tm - 1) // tm) * tm
    pad_end = jnp.cumsum(padded)
    pad_start = pad_end - padded
    dest = jnp.concatenate([pad_start[ri[0]] + ri[2], pad_start[ri[1]] + ri[3]])
    tok = jnp.arange(t, dtype=jnp.int32)
    row_tok = jnp.zeros((n_rows,), jnp.int32).at[dest].set(jnp.concatenate([tok, tok]), unique_indices=True)
    spare = 2 * t + jnp.arange(n_rows, dtype=jnp.int32) % tm
    row_dst = spare.at[dest].set(jnp.concatenate([2 * tok, 2 * tok + 1]), unique_indices=True)
    blk_start = jnp.arange(n_blocks, dtype=jnp.int32) * tm
    blk_e = jnp.minimum(jnp.searchsorted(pad_end, blk_start, side="right"), N_EXPERTS - 1).astype(jnp.int32)
    new_e = jnp.concatenate([jnp.ones((1,), jnp.int32), (blk_e[1:] != blk_e[:-1]).astype(jnp.int32)])
    nb_real = (pad_end[-1:] // tm).astype(jnp.int32)
    return row_tok, row_dst, blk_e, new_e, nb_real


def _combine_kernel(x1_ref, g2_ref, w1_ref, w2_ref, y_ref, o_ref):
    tc = x1_ref.shape[0]
    ts = TOK_SUBLANES
    w1 = w1_ref[...]
    w2 = w2_ref[...]
    for s in range(ts):
        a_lo, a_hi = _unpack_pair(y_ref[pl.ds(s, tc, stride=2 * ts), :])
        b_lo, b_hi = _unpack_pair(y_ref[pl.ds(ts + s, tc, stride=2 * ts), :])
        for c, ya, yb in ((s, a_lo, b_lo), (s + ts, a_hi, b_hi)):
            cols = slice(c * LANES, (c + 1) * LANES)
            o_ref[:, cols] = x1_ref[:, cols] + g2_ref[0][:, cols] * (w1 * ya + w2 * yb)


def _combine(x1, g2, w1, w2, y2, seq):
    t, d = x1.shape
    tc = min(512, seq)
    return pl.pallas_call(
        _combine_kernel,
        grid=(t // tc,),
        in_specs=[pl.BlockSpec((tc, d), lambda i: (i, 0)),
                  pl.BlockSpec((1, 1, d), lambda i: (i * tc // seq, 0, 0)),
                  pl.BlockSpec((tc, 1), lambda i: (i, 0)),
                  pl.BlockSpec((tc, 1), lambda i: (i, 0)),
                  pl.BlockSpec((tc * 2 * TOK_SUBLANES, LANES), lambda i: (i, 0))],
        out_specs=pl.BlockSpec((tc, d), lambda i: (i, 0)),
        out_shape=jax.ShapeDtypeStruct((t, d), F32),
        compiler_params=_cparams(("arbitrary",)),
        name="combine",
    )(x1, g2, w1, w2, y2)


def _q_up_layout(w_q_up):
    w = w_q_up.reshape(Q_LORA, MLA_HEADS, MLA_QK)
    w = jnp.pad(w, ((0, 0), (0, 0), (0, MLA_QPAD - MLA_QK)))
    return w.reshape(Q_LORA, MLA_HEADS * MLA_QPAD).astype(BF16)


def _pad_lanes(g, width):
    return jnp.pad(g, (0, width - g.shape[0])).reshape(1, width)


def kernel(x, c, positions, w_ada, b_ada, norm1_g, w_in, hgrn_lb_logits, hgrn_onorm_g, q_a_norm_g, w_q_up,
           kv_a_norm_g, w_kv_up, q_norm_g, k_norm_g, attn_onorm_g, w_out, norm2_g, w_group, b_group,
           w_router, b_router, w_gate, w_up, w_down):
    bsz, seq, d = x.shape
    t = bsz * seq
    depth = w_ada.shape[0]
    half = MLA_ROPE // 2
    inv_freq = ROPE_BASE ** (-jnp.arange(0, MLA_ROPE, 2, dtype=F32) / MLA_ROPE)
    freq = jnp.concatenate([inv_freq, inv_freq, jnp.zeros((LANES - 2 * half,), F32)]).reshape(1, LANES)
    pos3 = positions.reshape(bsz, seq, 1)
    tr = min(512, t)
    tri = jnp.triu(jnp.ones((tr, tr), BF16), 1)

    x2 = x.reshape(t, d)
    for l in range(depth):
        mod = _ada(c, w_ada[l], b_ada[l]).reshape(bsz, 6, 1, d)
        sh1, sc1, g1, sh2, sc2, g2 = (mod[:, i] for i in range(6))

        w_in_p = jnp.pad(w_in[l], ((0, 0), (0, IN_COLS_PAD - IN_COLS))).astype(BF16)
        proj = _in_proj(x2, norm1_g[l].reshape(1, d), sh1, sc1, w_in_p, seq)
        proj3 = proj.reshape(bsz, seq, IN_COLS_PAD)

        o_a = _hgrn(proj3, hgrn_lb_logits, hgrn_onorm_g[l].reshape(1, HGRN_DK), l)

        q, k, v = _mla_up(proj3, pos3, _q_up_layout(w_q_up[l]), w_kv_up[l].astype(BF16),
                          q_a_norm_g[l].reshape(1, Q_LORA), kv_a_norm_g[l].reshape(1, KV_LORA),
                          _pad_lanes(q_norm_g[l], MLA_QPAD), _pad_lanes(k_norm_g[l], MLA_QPAD), freq)
        o_b = _attention(q, k, v, attn_onorm_g[l].reshape(MLA_V, 1))

        w_o = w_out[l].astype(BF16)
        wr = jnp.pad(jnp.concatenate([w_router[l], w_group[l]], axis=1),
                     ((0, 0), (0, LANES - N_EXPERTS - N_GROUPS))).astype(BF16)
        br = _pad_lanes(jnp.concatenate([b_router[l], b_group[l]]), LANES)
        x1, h2, lg_t = _out_proj(o_a.reshape(t, HGRN_WIDTH), o_b.reshape(t, HGRN_WIDTH), x2,
                                 w_o[:HGRN_WIDTH], w_o[HGRN_WIDTH:], g1, norm2_g[l].reshape(1, d),
                                 sh2, sc2, wr, br, seq)

        ri, rw, cnt = _route(lg_t, tri)
        counts = cnt[:, 0]
        y2 = _moe(*_moe_plan(ri, counts, t), h2, w_gate[l], w_up[l], w_down[l], 2 * t + MOE_BLOCK)
        x2 = _combine(x1, g2, rw[0].reshape(t, 1), rw[1].reshape(t, 1), y2, seq)
    return x2.reshape(bsz, seq, d)
```

```python
import functools

import jax
import jax.numpy as jnp
from jax import lax
from jax.experimental import pallas as pl
from jax.experimental.pallas import tpu as pltpu

F32 = jnp.float32
BF16 = jnp.bfloat16
EPS = 1e-6
LOG2E = 1.4426950408889634

D_MODEL = 2048
HGRN_WIDTH = 1024
HGRN_DK = 128
HGRN_HEADS = 8
HGRN_CHUNK = 64
MLA_HEADS = 8
MLA_NOPE = 128
MLA_ROPE = 64
MLA_QK = MLA_NOPE + MLA_ROPE
MLA_V = 128
MLA_QPAD = 256
Q_LORA = 512
KV_LORA = 256
ROPE_BASE = 10000.0
IN_COLS = 4 * HGRN_WIDTH + Q_LORA + KV_LORA + MLA_ROPE
IN_COLS_PAD = 5120
N_GROUPS = 4
EXPERTS_PER_GROUP = 8
N_EXPERTS = 32
D_EXPERT = 512
ROUTE_ROWS = 40
MOE_BLOCK = 256
LANES = 128
TOK_SUBLANES = 8
VMEM_LIMIT = 56 * 1024 * 1024


def _cparams(sem):
    return pltpu.CompilerParams(dimension_semantics=sem, vmem_limit_bytes=VMEM_LIMIT)


def _dot(a, b):
    return jnp.dot(a, b, preferred_element_type=F32)


def _dot_nt(a, b):
    return lax.dot_general(a, b, (((1,), (1,)), ((), ())), preferred_element_type=F32)


def _rms(x, g):
    return x * lax.rsqrt(jnp.mean(x * x, axis=-1, keepdims=True) + EPS) * g


def _silu(x):
    return x * jax.nn.sigmoid(x)


def _pack_pair(lo, hi):
    lo_b = lax.bitcast_convert_type(lo.astype(BF16).astype(F32), jnp.uint32)
    hi_b = lax.bitcast_convert_type(hi.astype(BF16).astype(F32), jnp.uint32)
    return hi_b | (lo_b >> 16)


def _unpack_pair(w):
    lo = lax.bitcast_convert_type(w << 16, F32)
    hi = lax.bitcast_convert_type(w & jnp.uint32(0xFFFF0000), F32)
    return lo, hi


def _ada_kernel(c_ref, w_ref, b_ref, o_ref):
    ca = _silu(c_ref[...]).astype(BF16)
    o_ref[...] = _dot(ca, w_ref[...].astype(BF16)) + b_ref[...]


def _ada(c, w, b):
    bsz, d = c.shape
    n = w.shape[1]
    tn = 1024
    return pl.pallas_call(
        _ada_kernel,
        grid=(n // tn,),
        in_specs=[pl.BlockSpec((bsz, d), lambda j: (0, 0)),
                  pl.BlockSpec((d, tn), lambda j: (0, j)),
                  pl.BlockSpec((1, tn), lambda j: (0, j))],
        out_specs=pl.BlockSpec((bsz, tn), lambda j: (0, j)),
        out_shape=jax.ShapeDtypeStruct((bsz, n), F32),
        compiler_params=_cparams(("arbitrary",)),
        name="ada",
    )(c, w, b.reshape(1, n))


def _in_kernel(x_ref, g_ref, sh_ref, sc_ref, w_ref, o_ref, h_ref, *, rows):
    @pl.when(pl.program_id(1) == 0)
    def _():
        def body(c, carry):
            r = pl.ds(pl.multiple_of(c * rows, rows), rows)
            h = _rms(x_ref[r, :], g_ref[...]) * (1.0 + sc_ref[0]) + sh_ref[0]
            h_ref[r, :] = h.astype(BF16)
            return carry
        lax.fori_loop(0, x_ref.shape[0] // rows, body, 0, unroll=2)

    o_ref[...] = _dot(h_ref[...], w_ref[...]).astype(BF16)


def _in_proj(x2, g, sh, sc, w, seq):
    t, d = x2.shape
    n = w.shape[1]
    tm = min(1024, seq)
    tn = 1024
    rows = 32
    return pl.pallas_call(
        functools.partial(_in_kernel, rows=rows),
        grid=(t // tm, n // tn),
        in_specs=[pl.BlockSpec((tm, d), lambda i, j: (i, 0)),
                  pl.BlockSpec((1, d), lambda i, j: (0, 0)),
                  pl.BlockSpec((1, 1, d), lambda i, j: (i * tm // seq, 0, 0)),
                  pl.BlockSpec((1, 1, d), lambda i, j: (i * tm // seq, 0, 0)),
                  pl.BlockSpec((d, tn), lambda i, j: (0, j))],
        out_specs=pl.BlockSpec((tm, tn), lambda i, j: (i, j)),
        out_shape=jax.ShapeDtypeStruct((t, n), BF16),
        scratch_shapes=[pltpu.VMEM((tm, d), BF16)],
        compiler_params=_cparams(("arbitrary", "arbitrary")),
        name="in_proj",
    )(x2, g, sh, sc, w)


HG_ROWS = 256


def _chunk_mask(n):
    row = lax.broadcasted_iota(jnp.int32, (n, n), 0)
    col = lax.broadcasted_iota(jnp.int32, (n, n), 1)
    return jnp.logical_and(row // HGRN_CHUNK == col // HGRN_CHUNK, col <= row)


def _hgrn_kernel(q_ref, f_ref, i_ref, g_ref, lbl_ref, og_ref, o_ref,
                 qb_s, u_s, sp_s, dec_s, oi_s, *, layer):
    seq = q_ref.shape[1]
    cs = HGRN_CHUNK
    rb = min(HG_ROWS, seq)
    nc = rb // cs
    dk = HGRN_DK
    lg = lbl_ref[...]
    ex = jnp.exp(lg - jnp.max(lg, axis=0, keepdims=True))
    sm = ex / jnp.sum(ex, axis=0, keepdims=True)
    lb = jnp.sum(sm[0:layer + 1], axis=0, keepdims=True)
    mask = _chunk_mask(rb)
    tri = mask.astype(BF16)
    row_chunk = lax.broadcasted_iota(jnp.int32, (rb, dk), 0) // cs

    def phase1(blk, carry):
        r = pl.ds(pl.multiple_of(blk * rb, rb), rb)
        f = lb + (1.0 - lb) * jax.nn.sigmoid(f_ref[0, r, :].astype(F32))
        lf = jnp.log(f)
        k = 1.0 - f
        hi = lf.astype(BF16)
        r1 = lf - hi.astype(F32)
        mid = r1.astype(BF16)
        lo = (r1 - mid.astype(F32)).astype(BF16)
        bhm = _dot(tri, jnp.concatenate([hi, mid], axis=1))
        b = bhm[:, 0:dk] + bhm[:, dk:] + _dot(tri, lo)
        b3 = b.reshape(nc, cs, dk)
        bmid = b3[:, cs // 2 - 1:cs // 2, :]
        blast = b3[:, cs - 1:cs, :]
        q3 = (q_ref[0, r, :].astype(F32) * dk ** -0.5).reshape(nc, cs, dk)
        k3 = k.reshape(nc, cs, dk)
        qin = (q3 * jnp.exp(b3 - bmid)).reshape(rb, dk).astype(BF16)
        kin = (k3 * jnp.exp(bmid - b3)).reshape(rb, dk).astype(BF16)
        ku = (k3 * jnp.exp(blast - b3)).reshape(rb, dk)
        qb_s[r, :] = (q3 * jnp.exp(b3)).reshape(rb, dk).astype(BF16)
        dec_s[pl.ds(blk * nc, nc)] = jnp.exp(blast)
        v = i_ref[0, r, :]
        a = jnp.where(mask, _dot_nt(qin, kin), 0.0).astype(BF16)
        oi_s[r, :] = _dot(a, v)
        vt = v.astype(F32).T.astype(BF16)
        ku_exp = jnp.concatenate(
            [jnp.where(row_chunk == c, ku, 0.0).astype(BF16) for c in range(nc)], axis=1)
        ut = _dot(vt, ku_exp)
        for c in range(nc):
            u_s[blk * nc + c] = ut[:, c * dk:(c + 1) * dk]
        return carry

    lax.fori_loop(0, seq // rb, phase1, 0, unroll=2)

    def phase2(n, st):
        sp_s[n] = st.astype(BF16)
        return st * dec_s[n] + u_s[n]

    lax.fori_loop(0, seq // cs, phase2, jnp.zeros((dk, dk), F32), unroll=4)

    def phase3(blk, carry):
        r = pl.ds(pl.multiple_of(blk * rb, rb), rb)
        inter = [_dot_nt(qb_s[pl.ds(pl.multiple_of(blk * rb + c * cs, cs), cs), :], sp_s[blk * nc + c])
                 for c in range(nc)]
        o = oi_s[r, :] + jnp.concatenate(inter, axis=0)
        o = _rms(o, og_ref[...]) * _silu(g_ref[0, r, :].astype(F32))
        o_ref[0, r, :] = o.astype(BF16)
        return carry

    lax.fori_loop(0, seq // rb, phase3, 0, unroll=2)


def _hgrn(proj3, lb_logits, onorm_g, layer):
    bsz, seq, _ = proj3.shape
    nh = HGRN_HEADS

    def col(off):
        return pl.BlockSpec((1, seq, HGRN_DK), lambda b, h: (b, 0, off * nh + h))

    nl = lb_logits.shape[0]
    return pl.pallas_call(
        functools.partial(_hgrn_kernel, layer=layer),
        grid=(bsz, nh),
        in_specs=[col(0), col(1), col(2), col(3),
                  pl.BlockSpec((nl, HGRN_DK), lambda b, h: (0, h)),
                  pl.BlockSpec((1, HGRN_DK), lambda b, h: (0, 0))],
        out_specs=pl.BlockSpec((1, seq, HGRN_DK), lambda b, h: (b, 0, h)),
        out_shape=jax.ShapeDtypeStruct((bsz, seq, HGRN_WIDTH), BF16),
        scratch_shapes=[pltpu.VMEM((seq, HGRN_DK), BF16),
                        pltpu.VMEM((seq // HGRN_CHUNK, HGRN_DK, HGRN_DK), F32),
                        pltpu.VMEM((seq // HGRN_CHUNK, HGRN_DK, HGRN_DK), BF16),
                        pltpu.VMEM((seq // HGRN_CHUNK, 1, HGRN_DK), F32),
                        pltpu.VMEM((seq, HGRN_DK), F32)],
        compiler_params=_cparams(("arbitrary", "arbitrary")),
        name="hgrn",
    )(proj3, proj3, proj3, proj3, lb_logits, onorm_g)


def _rope(x, cos, sin_signed, lane):
    half = MLA_ROPE // 2
    swapped = jnp.where(lane < half, pltpu.roll(x, LANES - half, 1), pltpu.roll(x, half, 1))
    return x * cos + swapped * sin_signed


def _up_kernel(p_ref, pos_ref, wq_ref, wkv_ref, qag_ref, kvag_ref, qg_ref, kg_ref, freq_ref,
               q_ref, k_ref, v_ref):
    p = p_ref[...].astype(F32)
    q_a = p[:, 0:Q_LORA]
    kv_a = p[:, Q_LORA:Q_LORA + KV_LORA]
    k_pe = p[:, Q_LORA + KV_LORA:Q_LORA + KV_LORA + LANES]
    qf = _dot(_rms(q_a, qag_ref[...]).astype(BF16), wq_ref[...])
    kvf = _dot(_rms(kv_a, kvag_ref[...]).astype(BF16), wkv_ref[...])

    tm = p.shape[0]
    lane = lax.broadcasted_iota(jnp.int32, (tm, LANES), 1)
    half = MLA_ROPE // 2
    ang = pos_ref[...].astype(F32) * freq_ref[...]
    valid = lane < MLA_ROPE
    cos = jnp.where(valid, jnp.cos(ang), 0.0)
    sin_signed = jnp.where(valid, jnp.where(lane < half, -jnp.sin(ang), jnp.sin(ang)), 0.0)

    qg = qg_ref[...]
    kg = kg_ref[...]
    kpe_ss = jnp.sum(k_pe * k_pe, axis=-1, keepdims=True)
    kpe_rot = _rope(k_pe * kg[:, MLA_NOPE:], cos, sin_signed, lane)
    scale = MLA_QK ** -0.5 * LOG2E
    for h in range(MLA_HEADS):
        qh = qf[:, h * MLA_QPAD:(h + 1) * MLA_QPAD]
        rq = lax.rsqrt(jnp.sum(qh * qh, axis=-1, keepdims=True) / MLA_QK + EPS)
        qn = qh * rq * qg
        q_ref[0, h, :, 0:MLA_NOPE] = (qn[:, 0:MLA_NOPE] * scale).astype(BF16)
        q_ref[0, h, :, MLA_NOPE:] = (_rope(qn[:, MLA_NOPE:], cos, sin_signed, lane) * scale).astype(BF16)
        kn = kvf[:, h * MLA_QPAD:h * MLA_QPAD + MLA_NOPE]
        rk = lax.rsqrt((jnp.sum(kn * kn, axis=-1, keepdims=True) + kpe_ss) / MLA_QK + EPS)
        k_ref[0, h, :, 0:MLA_NOPE] = (kn * rk * kg[:, 0:MLA_NOPE]).astype(BF16)
        k_ref[0, h, :, MLA_NOPE:] = (kpe_rot * rk).astype(BF16)
        v_ref[0, h, :, :] = kvf[:, h * MLA_QPAD + MLA_NOPE:(h + 1) * MLA_QPAD].T.astype(BF16)


def _mla_up(proj3, pos3, wq, wkv, qag, kvag, qg, kg, freq):
    bsz, seq, _ = proj3.shape
    tm = min(512, seq)
    nh = MLA_HEADS
    mla_block = 4 * HGRN_WIDTH // 1024

    def const(shape):
        return pl.BlockSpec(shape, lambda b, i: (0,) * len(shape))

    return pl.pallas_call(
        _up_kernel,
        grid=(bsz, seq // tm),
        in_specs=[pl.BlockSpec((None, tm, 1024), lambda b, i: (b, i, mla_block)),
                  pl.BlockSpec((None, tm, 1), lambda b, i: (b, i, 0)),
                  const(wq.shape), const(wkv.shape), const(qag.shape), const(kvag.shape),
                  const(qg.shape), const(kg.shape), const(freq.shape)],
        out_specs=[pl.BlockSpec((1, nh, tm, MLA_QPAD), lambda b, i: (b, 0, i, 0)),
                   pl.BlockSpec((1, nh, tm, MLA_QPAD), lambda b, i: (b, 0, i, 0)),
                   pl.BlockSpec((1, nh, MLA_V, tm), lambda b, i: (b, 0, 0, i))],
        out_shape=[jax.ShapeDtypeStruct((bsz, nh, seq, MLA_QPAD), BF16),
                   jax.ShapeDtypeStruct((bsz, nh, seq, MLA_QPAD), BF16),
                   jax.ShapeDtypeStruct((bsz, nh, MLA_V, seq), BF16)],
        compiler_params=_cparams(("arbitrary", "arbitrary")),
        name="mla_up",
    )(proj3, pos3, wq, wkv, qag, kvag, qg, kg, freq)


ATT_T = 256


def _attn_kernel(q_ref, k_ref, vt_ref, g_ref, o_ref):
    seq = q_ref.shape[2]
    t = min(ATT_T, seq)
    key = lax.broadcasted_iota(jnp.int32, (t, t), 0)
    qry = lax.broadcasted_iota(jnp.int32, (t, t), 1)
    causal = key <= qry
    neg = jnp.finfo(F32).min
    for qi in range(seq // t):
        off = qi * t
        q = q_ref[0, 0, off:off + t, :]
        sd = jnp.where(causal, _dot_nt(k_ref[0, 0, off:off + t, :], q), neg)
        m = jnp.max(sd, axis=0, keepdims=True)
        if qi > 0:
            so = _dot_nt(k_ref[0, 0, 0:off, :], q)
            m = jnp.maximum(m, jnp.max(so, axis=0, keepdims=True))
        pd = jnp.exp2(sd - m)
        l = jnp.sum(pd, axis=0, keepdims=True)
        ot = _dot(vt_ref[0, 0, :, off:off + t], pd.astype(BF16))
        if qi > 0:
            po = jnp.exp2(so - m)
            l = l + jnp.sum(po, axis=0, keepdims=True)
            ot = ot + _dot(vt_ref[0, 0, :, 0:off], po.astype(BF16))
        ot = ot * (1.0 / l)
        ot = ot * lax.rsqrt(jnp.mean(ot * ot, axis=0, keepdims=True) + EPS) * g_ref[...]
        o_ref[0, off:off + t, :] = ot.T.astype(BF16)


def _attention(q, k, v, g):
    bsz, nh, seq, _ = q.shape
    return pl.pallas_call(
        _attn_kernel,
        grid=(bsz, nh),
        in_specs=[pl.BlockSpec((1, 1, seq, MLA_QPAD), lambda b, h: (b, h, 0, 0)),
                  pl.BlockSpec((1, 1, seq, MLA_QPAD), lambda b, h: (b, h, 0, 0)),
                  pl.BlockSpec((1, 1, MLA_V, seq), lambda b, h: (b, h, 0, 0)),
                  pl.BlockSpec((MLA_V, 1), lambda b, h: (0, 0))],
        out_specs=pl.BlockSpec((1, seq, MLA_V), lambda b, h: (b, 0, h)),
        out_shape=jax.ShapeDtypeStruct((bsz, seq, nh * MLA_V), BF16),
        compiler_params=_cparams(("arbitrary", "arbitrary")),
        name="attn",
    )(q, k, v, g)


def _out_kernel(oa_ref, ob_ref, x_ref, wa_ref, wb_ref, g1_ref, n2g_ref, sh2_ref, sc2_ref,
                wr_ref, br_ref, x1_ref, h2_ref, lg_ref):
    mix = _dot(oa_ref[...], wa_ref[...]) + _dot(ob_ref[...], wb_ref[...])
    x1 = x_ref[...] + g1_ref[0] * mix
    x1_ref[...] = x1
    h2 = _rms(x1, n2g_ref[...]) * (1.0 + sc2_ref[0]) + sh2_ref[0]
    tm = h2.shape[0]
    for s in range(TOK_SUBLANES):
        h2_ref[pl.ds(s, tm, stride=TOK_SUBLANES), :] = _pack_pair(
            h2[:, s * LANES:(s + 1) * LANES], h2[:, (s + TOK_SUBLANES) * LANES:(s + TOK_SUBLANES + 1) * LANES])
    lg = _dot(h2.astype(BF16), wr_ref[...]) + br_ref[...]
    lg_ref[...] = lg.T[0:ROUTE_ROWS, :]


def _out_proj(oa, ob, x2, wa, wb, g1, n2g, sh2, sc2, wr, br, seq):
    t, d = x2.shape
    tm = min(512, seq)

    def const(shape):
        return pl.BlockSpec(shape, lambda i: (0,) * len(shape))

    def per_batch():
        return pl.BlockSpec((1, 1, d), lambda i: (i * tm // seq, 0, 0))

    return pl.pallas_call(
        _out_kernel,
        grid=(t // tm,),
        in_specs=[pl.BlockSpec((tm, HGRN_WIDTH), lambda i: (i, 0)),
                  pl.BlockSpec((tm, HGRN_WIDTH), lambda i: (i, 0)),
                  pl.BlockSpec((tm, d), lambda i: (i, 0)),
                  const(wa.shape), const(wb.shape), per_batch(), const(n2g.shape),
                  per_batch(), per_batch(), const(wr.shape), const(br.shape)],
        out_specs=[pl.BlockSpec((tm, d), lambda i: (i, 0)),
                   pl.BlockSpec((tm * TOK_SUBLANES, LANES), lambda i: (i, 0)),
                   pl.BlockSpec((ROUTE_ROWS, tm), lambda i: (0, i))],
        out_shape=[jax.ShapeDtypeStruct((t, d), F32),
                   jax.ShapeDtypeStruct((t * TOK_SUBLANES, LANES), jnp.uint32),
                   jax.ShapeDtypeStruct((ROUTE_ROWS, t), F32)],
        compiler_params=_cparams(("arbitrary",)),
        name="out_proj",
    )(oa, ob, x2, wa, wb, g1, n2g, sh2, sc2, wr, br)


def _route_kernel(lg_ref, tri_ref, ri_ref, rw_ref, cnt_ref, carry_s):
    step = pl.program_id(0)

    @pl.when(step == 0)
    def _():
        carry_s[...] = jnp.zeros_like(carry_s)

    lg = lg_ref[...]
    tr = lg.shape[1]
    epg = EXPERTS_PER_GROUP
    gl = lg[N_EXPERTS:N_EXPERTS + N_GROUPS, :]
    row_g = lax.broadcasted_iota(jnp.int32, (N_GROUPS, tr), 0)
    gmax = jnp.max(gl, axis=0, keepdims=True)
    g_sel = jnp.min(jnp.where(gl == gmax, row_g, N_GROUPS), axis=0, keepdims=True)
    p_group = 1.0 / jnp.sum(jnp.exp(gl - gmax), axis=0, keepdims=True)

    e_in = lg[0:epg, :]
    for g in range(1, N_GROUPS):
        e_in = jnp.where(g_sel == g, lg[g * epg:(g + 1) * epg, :], e_in)
    row_e = lax.broadcasted_iota(jnp.int32, (epg, tr), 0)
    top1 = jnp.max(e_in, axis=0, keepdims=True)
    i1 = jnp.min(jnp.where(e_in == top1, row_e, epg), axis=0, keepdims=True)
    rest = jnp.where(row_e == i1, -jnp.inf, e_in)
    top2 = jnp.max(rest, axis=0, keepdims=True)
    i2 = jnp.min(jnp.where(rest == top2, row_e, epg), axis=0, keepdims=True)
    e2w = jnp.exp(top2 - top1)
    w1 = p_group / (1.0 + e2w)
    w2 = p_group * e2w / (1.0 + e2w)
    ex1 = g_sel * epg + i1
    ex2 = g_sel * epg + i2

    row_x = lax.broadcasted_iota(jnp.int32, (N_EXPERTS, tr), 0)
    oh1 = row_x == ex1
    oh2 = row_x == ex2
    oh = jnp.logical_or(oh1, oh2)
    before = _dot(oh.astype(BF16), tri_ref[...]) + carry_s[:, 0:1]
    rank1 = jnp.sum(jnp.where(oh1, before, 0.0), axis=0, keepdims=True)
    rank2 = jnp.sum(jnp.where(oh2, before, 0.0), axis=0, keepdims=True)
    carry_s[...] = carry_s[...] + jnp.sum(oh.astype(F32), axis=1, keepdims=True)

    zi = jnp.zeros((4, tr), jnp.int32)
    ri_ref[...] = jnp.concatenate([ex1, ex2, rank1.astype(jnp.int32), rank2.astype(jnp.int32), zi], axis=0)
    rw_ref[...] = jnp.concatenate([w1, w2, jnp.zeros((6, tr), F32)], axis=0)
    cnt_ref[...] = carry_s[...].astype(jnp.int32)


def _route(lg_t, tri):
    t = lg_t.shape[1]
    tr = tri.shape[0]
    return pl.pallas_call(
        _route_kernel,
        grid=(t // tr,),
        in_specs=[pl.BlockSpec((ROUTE_ROWS, tr), lambda i: (0, i)),
                  pl.BlockSpec((tr, tr), lambda i: (0, 0))],
        out_specs=[pl.BlockSpec((8, tr), lambda i: (0, i)),
                   pl.BlockSpec((8, tr), lambda i: (0, i)),
                   pl.BlockSpec((N_EXPERTS, LANES), lambda i: (0, 0))],
        out_shape=[jax.ShapeDtypeStruct((8, t), jnp.int32),
                   jax.ShapeDtypeStruct((8, t), F32),
                   jax.ShapeDtypeStruct((N_EXPERTS, LANES), jnp.int32)],
        scratch_shapes=[pltpu.VMEM((N_EXPERTS, LANES), F32)],
        compiler_params=_cparams(("arbitrary",)),
        name="route",
    )(lg_t, tri)


def _moe_kernel(rd_ref, be_ref, ne_ref, nb_ref, x_ref, wg_ref, wu_ref, wd_ref, y_ref,
                ys, wg_s, wu_s, wd_s, ssem):
    b = pl.program_id(0)
    last = pl.num_programs(0) - 1
    nb = nb_ref[0]
    tm = MOE_BLOCK
    ts = TOK_SUBLANES

    def scatter(blk):
        base = blk * tm
        for j in range(tm):
            dst = pl.multiple_of(rd_ref[base + j] * ts, ts)
            pltpu.make_async_copy(ys.at[pl.ds(j * ts, ts), :], y_ref.at[pl.ds(dst, ts), :], ssem).start()

    def wait_scatter():
        pltpu.make_async_copy(ys, y_ref.at[pl.ds(0, tm * ts), :], ssem).wait()

    @pl.when(b == 0)
    def _():
        ys[...] = jnp.zeros_like(ys)
        spare = pltpu.make_async_copy(ys, y_ref.at[pl.ds(y_ref.shape[0] - tm * ts, tm * ts), :], ssem)
        spare.start()
        spare.wait()

    @pl.when(ne_ref[b] == 1)
    def _():
        wg_s[...] = wg_ref[0].astype(BF16)
        wu_s[...] = wu_ref[0].astype(BF16)
        wd_s[...] = wd_ref[0].astype(BF16)

    @pl.when(b < nb)
    def _():
        scatter(jnp.maximum(b - 1, 0))
        parts = [_unpack_pair(x_ref[pl.ds(s, tm, stride=ts), :]) for s in range(ts)]
        x = jnp.concatenate([p[0] for p in parts] + [p[1] for p in parts], axis=1).astype(BF16)
        hid = _silu(_dot(x, wg_s[...])) * _dot(x, wu_s[...])
        y = _dot(hid.astype(BF16), wd_s[...])
        wait_scatter()
        for s in range(ts):
            ys[pl.ds(s, tm, stride=ts), :] = _pack_pair(y[:, s * LANES:(s + 1) * LANES],
                                                        y[:, (s + ts) * LANES:(s + ts + 1) * LANES])

        @pl.when(b == last)
        def _():
            scatter(b)
            wait_scatter()

    @pl.when(b == nb)
    def _():
        scatter(b - 1)
        wait_scatter()


def _moe(row_dst, blk_e, new_e, nb_real, x_sorted, wg, wu, wd, n_out_rows):
    n_blocks = blk_e.shape[0]
    d, de = wg.shape[1], wg.shape[2]
    tm = MOE_BLOCK

    def wspec(shape):
        return pl.BlockSpec(shape, lambda b, rd, be, ne, nb: (be[b], 0, 0))

    return pl.pallas_call(
        _moe_kernel,
        grid_spec=pltpu.PrefetchScalarGridSpec(
            num_scalar_prefetch=4,
            grid=(n_blocks,),
            in_specs=[pl.BlockSpec((tm * TOK_SUBLANES, LANES),
                                   lambda b, rd, be, ne, nb: (jnp.minimum(b, nb[0] - 1), 0)),
                      wspec((1, d, de)), wspec((1, d, de)), wspec((1, de, d))],
            out_specs=pl.BlockSpec(memory_space=pl.ANY),
            scratch_shapes=[pltpu.VMEM((tm * TOK_SUBLANES, LANES), jnp.uint32),
                            pltpu.VMEM((d, de), BF16), pltpu.VMEM((d, de), BF16), pltpu.VMEM((de, d), BF16),
                            pltpu.SemaphoreType.DMA(())]),
        out_shape=jax.ShapeDtypeStruct((n_out_rows * TOK_SUBLANES, LANES), jnp.uint32),
        compiler_params=_cparams(("arbitrary",)),
        name="moe",
    )(row_dst, blk_e, new_e, nb_real, x_sorted, wg, wu, wd)


def _dispatch_kernel(d1_ref, d2_ref, zf_ref, h_ref, x_ref, zbuf, sem, zsem):
    i = pl.program_id(0)
    ts = TOK_SUBLANES
    td = h_ref.shape[0] // ts
    base = i * td
    blk_rows = MOE_BLOCK * ts

    @pl.when(i == 0)
    def _():
        zbuf[...] = jnp.zeros_like(zbuf)

        def fill(blk, carry):
            @pl.when(zf_ref[blk] == 1)
            def _():
                zero = pltpu.make_async_copy(
                    zbuf, x_ref.at[pl.ds(pl.multiple_of(blk * blk_rows, blk_rows), blk_rows), :], zsem)
                zero.start()
                zero.wait()
            return carry

        lax.fori_loop(0, zf_ref.shape[0], fill, 0)

    def start(t, carry):
        src = h_ref.at[pl.ds(pl.multiple_of(t * ts, ts), ts), :]
        for d_ref in (d1_ref, d2_ref):
            dst = pl.multiple_of(d_ref[base + t] * ts, ts)
            pltpu.make_async_copy(src, x_ref.at[pl.ds(dst, ts), :], sem).start()
        return carry

    lax.fori_loop(0, td, start, 0, unroll=8)
    for _ in range(2):
        pltpu.make_async_copy(h_ref, x_ref.at[pl.ds(0, td * ts), :], sem).wait()


def _dispatch(dest1, dest2, zero_blk, h2p, n_rows):
    t = dest1.shape[0]
    td = min(512, t)
    ts = TOK_SUBLANES
    return pl.pallas_call(
        _dispatch_kernel,
        grid_spec=pltpu.PrefetchScalarGridSpec(
            num_scalar_prefetch=3,
            grid=(t // td,),
            in_specs=[pl.BlockSpec((td * ts, LANES), lambda i, d1, d2, zf: (i, 0))],
            out_specs=pl.BlockSpec(memory_space=pl.ANY),
            scratch_shapes=[pltpu.VMEM((MOE_BLOCK * ts, LANES), jnp.uint32),
                            pltpu.SemaphoreType.DMA(()), pltpu.SemaphoreType.DMA(())]),
        out_shape=jax.ShapeDtypeStruct((n_rows * ts, LANES), jnp.uint32),
        compiler_params=_cparams(("arbitrary",)),
        name="dispatch",
    )(dest1, dest2, zero_blk, h2p)


def _invert_kernel(d1_ref, d2_ref, o_ref, *, n_tok):
    def init(blk, carry):
        for j in range(MOE_BLOCK):
            o_ref[blk * MOE_BLOCK + j] = 2 * n_tok + j
        return carry

    lax.fori_loop(0, o_ref.shape[0] // MOE_BLOCK, init, 0)

    def body(t, carry):
        o_ref[d1_ref[t]] = 2 * t
        o_ref[d2_ref[t]] = 2 * t + 1
        return carry

    lax.fori_loop(0, n_tok, body, 0, unroll=8)


def _invert(dest1, dest2, n_rows):
    smem = pl.BlockSpec(memory_space=pltpu.SMEM)
    return pl.pallas_call(
        functools.partial(_invert_kernel, n_tok=dest1.shape[0]),
        in_specs=[smem, smem],
        out_specs=smem,
        out_shape=jax.ShapeDtypeStruct((n_rows,), jnp.int32),
        name="invert",
    )(dest1, dest2)


def _moe_plan(ri, counts, t):
    tm = MOE_BLOCK
    n_blocks = 2 * t // tm + N_EXPERTS
    padded = ((counts + tm - 1) // tm) * tm
    pad_end = jnp.cumsum(padded)
    pad_start = pad_end - padded
    dest1 = pad_start[ri[0]] + ri[2]
    dest2 = pad_start[ri[1]] + ri[3]
    blk_start = jnp.arange(n_blocks, dtype=jnp.int32) * tm
    blk_e = jnp.minimum(jnp.searchsorted(pad_end, blk_start, side="right"), N_EXPERTS - 1).astype(jnp.int32)
    new_e = jnp.concatenate([jnp.ones((1,), jnp.int32), (blk_e[1:] != blk_e[:-1]).astype(jnp.int32)])
    nb_real = (pad_end[-1:] // tm).astype(jnp.int32)
    blk = blk_start // tm
    last_partial = jnp.any((blk[:, None] == (pad_end // tm - 1)[None, :]) & (counts % tm != 0)[None, :], axis=1)
    zero_blk = jnp.logical_or(last_partial, blk >= nb_real[0]).astype(jnp.int32)
    return dest1, dest2, blk_e, new_e, nb_real, zero_blk, n_blocks * tm


def _combine_kernel(x1_ref, g2_ref, w1_ref, w2_ref, y_ref, o_ref):
    tc = x1_ref.shape[0]
    ts = TOK_SUBLANES
    w1 = w1_ref[...]
    w2 = w2_ref[...]
    for s in range(ts):
        a_lo, a_hi = _unpack_pair(y_ref[pl.ds(s, tc, stride=2 * ts), :])
        b_lo, b_hi = _unpack_pair(y_ref[pl.ds(ts + s, tc, stride=2 * ts), :])
        for c, ya, yb in ((s, a_lo, b_lo), (s + ts, a_hi, b_hi)):
            cols = slice(c * LANES, (c + 1) * LANES)
            o_ref[:, cols] = x1_ref[:, cols] + g2_ref[0][:, cols] * (w1 * ya + w2 * yb)


def _combine(x1, g2, w1, w2, y2, seq):
    t, d = x1.shape
    tc = min(512, seq)
    return pl.pallas_call(
        _combine_kernel,
        grid=(t // tc,),
        in_specs=[pl.BlockSpec((tc, d), lambda i: (i, 0)),
                  pl.BlockSpec((1, 1, d), lambda i: (i * tc // seq, 0, 0)),
                  pl.BlockSpec((tc, 1), lambda i: (i, 0)),
                  pl.BlockSpec((tc, 1), lambda i: (i, 0)),
                  pl.BlockSpec((tc * 2 * TOK_SUBLANES, LANES), lambda i: (i, 0))],
        out_specs=pl.BlockSpec((tc, d), lambda i: (i, 0)),
        out_shape=jax.ShapeDtypeStruct((t, d), F32),
        compiler_params=_cparams(("arbitrary",)),
        name="combine",
    )(x1, g2, w1, w2, y2)


def _q_up_layout(w_q_up):
    w = w_q_up.reshape(Q_LORA, MLA_HEADS, MLA_QK)
    w = jnp.pad(w, ((0, 0), (0, 0), (0, MLA_QPAD - MLA_QK)))
    return w.reshape(Q_LORA, MLA_HEADS * MLA_QPAD).astype(BF16)


def _pad_lanes(g, width):
    return jnp.pad(g, (0, width - g.shape[0])).reshape(1, width)


def kernel(x, c, positions, w_ada, b_ada, norm1_g, w_in, hgrn_lb_logits, hgrn_onorm_g, q_a_norm_g, w_q_up,
           kv_a_norm_g, w_kv_up, q_norm_g, k_norm_g, attn_onorm_g, w_out, norm2_g, w_group, b_group,
           w_router, b_router, w_gate, w_up, w_down):
    bsz, seq, d = x.shape
    t = bsz * seq
    depth = w_ada.shape[0]
    half = MLA_ROPE // 2
    inv_freq = ROPE_BASE ** (-jnp.arange(0, MLA_ROPE, 2, dtype=F32) / MLA_ROPE)
    freq = jnp.concatenate([inv_freq, inv_freq, jnp.zeros((LANES - 2 * half,), F32)]).reshape(1, LANES)
    pos3 = positions.reshape(bsz, seq, 1)
    tr = min(512, t)
    tri = jnp.triu(jnp.ones((tr, tr), BF16), 1)

    x2 = x.reshape(t, d)
    for l in range(depth):
        mod = _ada(c, w_ada[l], b_ada[l]).reshape(bsz, 6, 1, d)
        sh1, sc1, g1, sh2, sc2, g2 = (mod[:, i] for i in range(6))

        w_in_p = jnp.pad(w_in[l], ((0, 0), (0, IN_COLS_PAD - IN_COLS))).astype(BF16)
        proj = _in_proj(x2, norm1_g[l].reshape(1, d), sh1, sc1, w_in_p, seq)
        proj3 = proj.reshape(bsz, seq, IN_COLS_PAD)

        o_a = _hgrn(proj3, hgrn_lb_logits, hgrn_onorm_g[l].reshape(1, HGRN_DK), l)

        q, k, v = _mla_up(proj3, pos3, _q_up_layout(w_q_up[l]), w_kv_up[l].astype(BF16),
                          q_a_norm_g[l].reshape(1, Q_LORA), kv_a_norm_g[l].reshape(1, KV_LORA),
                          _pad_lanes(q_norm_g[l], MLA_QPAD), _pad_lanes(k_norm_g[l], MLA_QPAD), freq)
        o_b = _attention(q, k, v, attn_onorm_g[l].reshape(MLA_V, 1))

        w_o = w_out[l].astype(BF16)
        wr = jnp.pad(jnp.concatenate([w_router[l], w_group[l]], axis=1),
                     ((0, 0), (0, LANES - N_EXPERTS - N_GROUPS))).astype(BF16)
        br = _pad_lanes(jnp.concatenate([b_router[l], b_group[l]]), LANES)
        x1, h2, lg_t = _out_proj(o_a.reshape(t, HGRN_WIDTH), o_b.reshape(t, HGRN_WIDTH), x2,
                                 w_o[:HGRN_WIDTH], w_o[HGRN_WIDTH:], g1, norm2_g[l].reshape(1, d),
                                 sh2, sc2, wr, br, seq)

        ri, rw, cnt = _route(lg_t, tri)
        counts = cnt[:, 0]
        dest1, dest2, blk_e, new_e, nb_real, zero_blk, n_rows = _moe_plan(ri, counts, t)
        x_sorted = _dispatch(dest1, dest2, zero_blk, h2, n_rows)
        row_dst = _invert(dest1, dest2, n_rows)
        y2 = _moe(row_dst, blk_e, new_e, nb_real, x_sorted, w_gate[l], w_up[l], w_down[l], 2 * t + MOE_BLOCK)
        x2 = _combine(x1, g2, rw[0].reshape(t, 1), rw[1].reshape(t, 1), y2, seq)
    return x2.reshape(bsz, seq, d)
```

```python
import functools

import jax
import jax.numpy as jnp
from jax import lax
from jax.experimental import pallas as pl
from jax.experimental.pallas import tpu as pltpu

F32 = jnp.float32
BF16 = jnp.bfloat16
EPS = 1e-6
LOG2E = 1.4426950408889634

D_MODEL = 2048
HGRN_WIDTH = 1024
HGRN_DK = 128
HGRN_HEADS = 8
HGRN_CHUNK = 64
MLA_HEADS = 8
MLA_NOPE = 128
MLA_ROPE = 64
MLA_QK = MLA_NOPE + MLA_ROPE
MLA_V = 128
MLA_QPAD = 256
Q_LORA = 512
KV_LORA = 256
ROPE_BASE = 10000.0
IN_COLS = 4 * HGRN_WIDTH + Q_LORA + KV_LORA + MLA_ROPE
IN_COLS_PAD = 5120
N_GROUPS = 4
EXPERTS_PER_GROUP = 8
N_EXPERTS = 32
D_EXPERT = 512
ROUTE_ROWS = 40
MOE_BLOCK = 256
LANES = 128
TOK_SUBLANES = 8
VMEM_LIMIT = 56 * 1024 * 1024


def _cparams(sem):
    return pltpu.CompilerParams(dimension_semantics=sem, vmem_limit_bytes=VMEM_LIMIT)


def _dot(a, b):
    return jnp.dot(a, b, preferred_element_type=F32)


def _dot_nt(a, b):
    return lax.dot_general(a, b, (((1,), (1,)), ((), ())), preferred_element_type=F32)


def _rms(x, g):
    return x * lax.rsqrt(jnp.mean(x * x, axis=-1, keepdims=True) + EPS) * g


def _silu(x):
    return x * jax.nn.sigmoid(x)


def _pack_pair(lo, hi):
    lo_b = lax.bitcast_convert_type(lo.astype(BF16).astype(F32), jnp.uint32)
    hi_b = lax.bitcast_convert_type(hi.astype(BF16).astype(F32), jnp.uint32)
    return hi_b | (lo_b >> 16)


def _unpack_pair(w):
    lo = lax.bitcast_convert_type(w << 16, F32)
    hi = lax.bitcast_convert_type(w & jnp.uint32(0xFFFF0000), F32)
    return lo, hi


def _ada_kernel(c_ref, w_ref, b_ref, o_ref):
    ca = _silu(c_ref[...]).astype(BF16)
    o_ref[...] = _dot(ca, w_ref[...].astype(BF16)) + b_ref[...]


def _ada(c, w, b):
    bsz, d = c.shape
    n = w.shape[1]
    tn = 1024
    return pl.pallas_call(
        _ada_kernel,
        grid=(n // tn,),
        in_specs=[pl.BlockSpec((bsz, d), lambda j: (0, 0)),
                  pl.BlockSpec((d, tn), lambda j: (0, j)),
                  pl.BlockSpec((1, tn), lambda j: (0, j))],
        out_specs=pl.BlockSpec((bsz, tn), lambda j: (0, j)),
        out_shape=jax.ShapeDtypeStruct((bsz, n), F32),
        compiler_params=_cparams(("arbitrary",)),
        name="ada",
    )(c, w, b.reshape(1, n))


def _in_kernel(x_ref, g_ref, sh_ref, sc_ref, w_ref, o_ref, h_ref, *, rows):
    @pl.when(pl.program_id(1) == 0)
    def _():
        def body(c, carry):
            r = pl.ds(pl.multiple_of(c * rows, rows), rows)
            h = _rms(x_ref[r, :], g_ref[...]) * (1.0 + sc_ref[0]) + sh_ref[0]
            h_ref[r, :] = h.astype(BF16)
            return carry
        lax.fori_loop(0, x_ref.shape[0] // rows, body, 0, unroll=2)

    o_ref[...] = _dot(h_ref[...], w_ref[...]).astype(BF16)


def _in_proj(x2, g, sh, sc, w, seq):
    t, d = x2.shape
    n = w.shape[1]
    tm = min(1024, seq)
    tn = 1024
    rows = 32
    return pl.pallas_call(
        functools.partial(_in_kernel, rows=rows),
        grid=(t // tm, n // tn),
        in_specs=[pl.BlockSpec((tm, d), lambda i, j: (i, 0)),
                  pl.BlockSpec((1, d), lambda i, j: (0, 0)),
                  pl.BlockSpec((1, 1, d), lambda i, j: (i * tm // seq, 0, 0)),
                  pl.BlockSpec((1, 1, d), lambda i, j: (i * tm // seq, 0, 0)),
                  pl.BlockSpec((d, tn), lambda i, j: (0, j))],
        out_specs=pl.BlockSpec((tm, tn), lambda i, j: (i, j)),
        out_shape=jax.ShapeDtypeStruct((t, n), BF16),
        scratch_shapes=[pltpu.VMEM((tm, d), BF16)],
        compiler_params=_cparams(("arbitrary", "arbitrary")),
        name="in_proj",
    )(x2, g, sh, sc, w)


HG_ROWS = 256


def _software_pipeline(stages, n_blocks):
    depth = len(stages)

    def step(i, static):
        for k in reversed(range(depth)):
            if static and not 0 <= i - k < n_blocks:
                continue
            stages[k](i - k)

    if n_blocks < depth:
        for i in range(n_blocks + depth - 1):
            step(i, True)
        return
    for i in range(depth - 1):
        step(i, True)

    def steady(i, carry):
        step(i, False)
        return carry

    lax.fori_loop(depth - 1, n_blocks, steady, 0)
    for i in range(n_blocks, n_blocks + depth - 1):
        step(i, True)


def _chunk_mask(n):
    row = lax.broadcasted_iota(jnp.int32, (n, n), 0)
    col = lax.broadcasted_iota(jnp.int32, (n, n), 1)
    return jnp.logical_and(row // HGRN_CHUNK == col // HGRN_CHUNK, col <= row)


def _hgrn_kernel(q_ref, f_ref, i_ref, g_ref, lbl_ref, og_ref, o_ref,
                 qb_s, u_s, sp_s, dec_s, oi_s, b_s, k_s, qin_s, kin_s, ku_s, a_s, *, layer):
    seq = q_ref.shape[1]
    cs = HGRN_CHUNK
    rb = min(HG_ROWS, seq)
    nc = rb // cs
    dk = HGRN_DK
    lg = lbl_ref[...]
    ex = jnp.exp(lg - jnp.max(lg, axis=0, keepdims=True))
    sm = ex / jnp.sum(ex, axis=0, keepdims=True)
    lb = jnp.sum(sm[0:layer + 1], axis=0, keepdims=True)
    mask = _chunk_mask(rb)
    tri = mask.astype(BF16)
    row_chunk = lax.broadcasted_iota(jnp.int32, (rb, dk), 0) // cs

    chunk_sel = [(row_chunk == c).astype(BF16) for c in range(nc)]

    def rows(blk):
        return pl.ds(pl.multiple_of(blk * rb, rb), rb)

    def stage1(blk):
        r = rows(blk)
        f = lb + (1.0 - lb) * jax.nn.sigmoid(f_ref[0, r, :].astype(F32))
        lf = jnp.log(f)
        k_s[r, :] = 1.0 - f
        hi = lf.astype(BF16)
        r1 = lf - hi.astype(F32)
        mid = r1.astype(BF16)
        lo = (r1 - mid.astype(F32)).astype(BF16)
        bhm = _dot(tri, jnp.concatenate([hi, mid], axis=1))
        b_s[r, :] = bhm[:, 0:dk] + bhm[:, dk:] + _dot(tri, lo)

    def stage2(blk):
        r = rows(blk)
        b3 = b_s[r, :].reshape(nc, cs, dk)
        bmid = b3[:, cs // 2 - 1:cs // 2, :]
        blast = b3[:, cs - 1:cs, :]
        q3 = (q_ref[0, r, :].astype(F32) * dk ** -0.5).reshape(nc, cs, dk)
        k3 = k_s[r, :].reshape(nc, cs, dk)
        qin_s[r, :] = (q3 * jnp.exp(b3 - bmid)).reshape(rb, dk).astype(BF16)
        kin_s[r, :] = (k3 * jnp.exp(bmid - b3)).reshape(rb, dk).astype(BF16)
        ku_s[r, :] = (k3 * jnp.exp(blast - b3)).reshape(rb, dk).astype(BF16)
        qb_s[r, :] = (q3 * jnp.exp(b3)).reshape(rb, dk).astype(BF16)
        dec_s[pl.ds(blk * nc, nc)] = jnp.exp(blast)

    def stage3(blk):
        r = rows(blk)
        a_s[r, :] = jnp.where(mask, _dot_nt(qin_s[r, :], kin_s[r, :]), 0.0).astype(BF16)
        vt = i_ref[0, r, :].astype(F32).T.astype(BF16)
        ku = ku_s[r, :]
        ut = _dot(vt, jnp.concatenate([ku * sel for sel in chunk_sel], axis=1))
        for c in range(nc):
            u_s[blk * nc + c] = ut[:, c * dk:(c + 1) * dk]

    def stage4(blk):
        r = rows(blk)
        oi_s[r, :] = _dot(a_s[r, :], i_ref[0, r, :])

    _software_pipeline([stage1, stage2, stage3, stage4], seq // rb)

    def phase2(n, st):
        sp_s[n] = st.astype(BF16)
        return st * dec_s[n] + u_s[n]

    lax.fori_loop(0, seq // cs, phase2, jnp.zeros((dk, dk), F32), unroll=4)

    def phase3(blk, carry):
        r = pl.ds(pl.multiple_of(blk * rb, rb), rb)
        inter = [_dot_nt(qb_s[pl.ds(pl.multiple_of(blk * rb + c * cs, cs), cs), :], sp_s[blk * nc + c])
                 for c in range(nc)]
        o = oi_s[r, :] + jnp.concatenate(inter, axis=0)
        o = _rms(o, og_ref[...]) * _silu(g_ref[0, r, :].astype(F32))
        o_ref[0, r, :] = o.astype(BF16)
        return carry

    lax.fori_loop(0, seq // rb, phase3, 0, unroll=2)


def _hgrn(proj3, lb_logits, onorm_g, layer):
    bsz, seq, _ = proj3.shape
    nh = HGRN_HEADS

    def col(off):
        return pl.BlockSpec((1, seq, HGRN_DK), lambda b, h: (b, 0, off * nh + h))

    nl = lb_logits.shape[0]
    return pl.pallas_call(
        functools.partial(_hgrn_kernel, layer=layer),
        grid=(bsz, nh),
        in_specs=[col(0), col(1), col(2), col(3),
                  pl.BlockSpec((nl, HGRN_DK), lambda b, h: (0, h)),
                  pl.BlockSpec((1, HGRN_DK), lambda b, h: (0, 0))],
        out_specs=pl.BlockSpec((1, seq, HGRN_DK), lambda b, h: (b, 0, h)),
        out_shape=jax.ShapeDtypeStruct((bsz, seq, HGRN_WIDTH), BF16),
        scratch_shapes=[pltpu.VMEM((seq, HGRN_DK), BF16),
                        pltpu.VMEM((seq // HGRN_CHUNK, HGRN_DK, HGRN_DK), F32),
                        pltpu.VMEM((seq // HGRN_CHUNK, HGRN_DK, HGRN_DK), BF16),
                        pltpu.VMEM((seq // HGRN_CHUNK, 1, HGRN_DK), F32),
                        pltpu.VMEM((seq, HGRN_DK), F32),
                        pltpu.VMEM((seq, HGRN_DK), F32), pltpu.VMEM((seq, HGRN_DK), F32),
                        pltpu.VMEM((seq, HGRN_DK), BF16), pltpu.VMEM((seq, HGRN_DK), BF16),
                        pltpu.VMEM((seq, HGRN_DK), BF16), pltpu.VMEM((seq, min(HG_ROWS, seq)), BF16)],
        compiler_params=_cparams(("arbitrary", "arbitrary")),
        name="hgrn",
    )(proj3, proj3, proj3, proj3, lb_logits, onorm_g)


def _rope(x, cos, sin_signed, lane):
    half = MLA_ROPE // 2
    swapped = jnp.where(lane < half, pltpu.roll(x, LANES - half, 1), pltpu.roll(x, half, 1))
    return x * cos + swapped * sin_signed


def _up_kernel(p_ref, pos_ref, wq_ref, wkv_ref, qag_ref, kvag_ref, qg_ref, kg_ref, freq_ref,
               q_ref, k_ref, v_ref):
    p = p_ref[...].astype(F32)
    q_a = p[:, 0:Q_LORA]
    kv_a = p[:, Q_LORA:Q_LORA + KV_LORA]
    k_pe = p[:, Q_LORA + KV_LORA:Q_LORA + KV_LORA + LANES]
    qf = _dot(_rms(q_a, qag_ref[...]).astype(BF16), wq_ref[...])
    kvf = _dot(_rms(kv_a, kvag_ref[...]).astype(BF16), wkv_ref[...])

    tm = p.shape[0]
    lane = lax.broadcasted_iota(jnp.int32, (tm, LANES), 1)
    half = MLA_ROPE // 2
    ang = pos_ref[...].astype(F32) * freq_ref[...]
    valid = lane < MLA_ROPE
    cos = jnp.where(valid, jnp.cos(ang), 0.0)
    sin_signed = jnp.where(valid, jnp.where(lane < half, -jnp.sin(ang), jnp.sin(ang)), 0.0)

    qg = qg_ref[...]
    kg = kg_ref[...]
    kpe_ss = jnp.sum(k_pe * k_pe, axis=-1, keepdims=True)
    kpe_rot = _rope(k_pe * kg[:, MLA_NOPE:], cos, sin_signed, lane)
    scale = MLA_QK ** -0.5 * LOG2E
    for h in range(MLA_HEADS):
        qh = qf[:, h * MLA_QPAD:(h + 1) * MLA_QPAD]
        rq = lax.rsqrt(jnp.sum(qh * qh, axis=-1, keepdims=True) / MLA_QK + EPS)
        qn = qh * rq * qg
        q_ref[0, h, :, 0:MLA_NOPE] = (qn[:, 0:MLA_NOPE] * scale).astype(BF16)
        q_ref[0, h, :, MLA_NOPE:] = (_rope(qn[:, MLA_NOPE:], cos, sin_signed, lane) * scale).astype(BF16)
        kn = kvf[:, h * MLA_QPAD:h * MLA_QPAD + MLA_NOPE]
        rk = lax.rsqrt((jnp.sum(kn * kn, axis=-1, keepdims=True) + kpe_ss) / MLA_QK + EPS)
        k_ref[0, h, :, 0:MLA_NOPE] = (kn * rk * kg[:, 0:MLA_NOPE]).astype(BF16)
        k_ref[0, h, :, MLA_NOPE:] = (kpe_rot * rk).astype(BF16)
        v_ref[0, h, :, :] = kvf[:, h * MLA_QPAD + MLA_NOPE:(h + 1) * MLA_QPAD].T.astype(BF16)


def _mla_up(proj3, pos3, wq, wkv, qag, kvag, qg, kg, freq):
    bsz, seq, _ = proj3.shape
    tm = min(512, seq)
    nh = MLA_HEADS
    mla_block = 4 * HGRN_WIDTH // 1024

    def const(shape):
        return pl.BlockSpec(shape, lambda b, i: (0,) * len(shape))

    return pl.pallas_call(
        _up_kernel,
        grid=(bsz, seq // tm),
        in_specs=[pl.BlockSpec((None, tm, 1024), lambda b, i: (b, i, mla_block)),
                  pl.BlockSpec((None, tm, 1), lambda b, i: (b, i, 0)),
                  const(wq.shape), const(wkv.shape), const(qag.shape), const(kvag.shape),
                  const(qg.shape), const(kg.shape), const(freq.shape)],
        out_specs=[pl.BlockSpec((1, nh, tm, MLA_QPAD), lambda b, i: (b, 0, i, 0)),
                   pl.BlockSpec((1, nh, tm, MLA_QPAD), lambda b, i: (b, 0, i, 0)),
                   pl.BlockSpec((1, nh, MLA_V, tm), lambda b, i: (b, 0, 0, i))],
        out_shape=[jax.ShapeDtypeStruct((bsz, nh, seq, MLA_QPAD), BF16),
                   jax.ShapeDtypeStruct((bsz, nh, seq, MLA_QPAD), BF16),
                   jax.ShapeDtypeStruct((bsz, nh, MLA_V, seq), BF16)],
        compiler_params=_cparams(("arbitrary", "arbitrary")),
        name="mla_up",
    )(proj3, pos3, wq, wkv, qag, kvag, qg, kg, freq)


ATT_T = 256


def _attn_kernel(q_ref, k_ref, vt_ref, g_ref, o_ref):
    seq = q_ref.shape[2]
    t = min(ATT_T, seq)
    key = lax.broadcasted_iota(jnp.int32, (t, t), 0)
    qry = lax.broadcasted_iota(jnp.int32, (t, t), 1)
    causal = key <= qry
    neg = jnp.finfo(F32).min
    for qi in range(seq // t):
        off = qi * t
        q = q_ref[0, 0, off:off + t, :]
        sd = jnp.where(causal, _dot_nt(k_ref[0, 0, off:off + t, :], q), neg)
        m = jnp.max(sd, axis=0, keepdims=True)
        if qi > 0:
            so = _dot_nt(k_ref[0, 0, 0:off, :], q)
            m = jnp.maximum(m, jnp.max(so, axis=0, keepdims=True))
        pd = jnp.exp2(sd - m)
        l = jnp.sum(pd, axis=0, keepdims=True)
        ot = _dot(vt_ref[0, 0, :, off:off + t], pd.astype(BF16))
        if qi > 0:
            po = jnp.exp2(so - m)
            l = l + jnp.sum(po, axis=0, keepdims=True)
            ot = ot + _dot(vt_ref[0, 0, :, 0:off], po.astype(BF16))
        ot = ot * (1.0 / l)
        ot = ot * lax.rsqrt(jnp.mean(ot * ot, axis=0, keepdims=True) + EPS) * g_ref[...]
        o_ref[0, off:off + t, :] = ot.T.astype(BF16)


def _attention(q, k, v, g):
    bsz, nh, seq, _ = q.shape
    return pl.pallas_call(
        _attn_kernel,
        grid=(bsz, nh),
        in_specs=[pl.BlockSpec((1, 1, seq, MLA_QPAD), lambda b, h: (b, h, 0, 0)),
                  pl.BlockSpec((1, 1, seq, MLA_QPAD), lambda b, h: (b, h, 0, 0)),
                  pl.BlockSpec((1, 1, MLA_V, seq), lambda b, h: (b, h, 0, 0)),
                  pl.BlockSpec((MLA_V, 1), lambda b, h: (0, 0))],
        out_specs=pl.BlockSpec((1, seq, MLA_V), lambda b, h: (b, 0, h)),
        out_shape=jax.ShapeDtypeStruct((bsz, seq, nh * MLA_V), BF16),
        compiler_params=_cparams(("arbitrary", "arbitrary")),
        name="attn",
    )(q, k, v, g)


def _out_kernel(oa_ref, ob_ref, x_ref, wa_ref, wb_ref, g1_ref, n2g_ref, sh2_ref, sc2_ref,
                wr_ref, br_ref, x1_ref, h2_ref, lg_ref):
    mix = _dot(oa_ref[...], wa_ref[...]) + _dot(ob_ref[...], wb_ref[...])
    x1 = x_ref[...] + g1_ref[0] * mix
    x1_ref[...] = x1
    h2 = _rms(x1, n2g_ref[...]) * (1.0 + sc2_ref[0]) + sh2_ref[0]
    tm = h2.shape[0]
    for s in range(TOK_SUBLANES):
        h2_ref[pl.ds(s, tm, stride=TOK_SUBLANES), :] = _pack_pair(
            h2[:, s * LANES:(s + 1) * LANES], h2[:, (s + TOK_SUBLANES) * LANES:(s + TOK_SUBLANES + 1) * LANES])
    lg = _dot(h2.astype(BF16), wr_ref[...]) + br_ref[...]
    lg_ref[...] = lg.T[0:ROUTE_ROWS, :]


def _out_proj(oa, ob, x2, wa, wb, g1, n2g, sh2, sc2, wr, br, seq):
    t, d = x2.shape
    tm = min(512, seq)

    def const(shape):
        return pl.BlockSpec(shape, lambda i: (0,) * len(shape))

    def per_batch():
        return pl.BlockSpec((1, 1, d), lambda i: (i * tm // seq, 0, 0))

    return pl.pallas_call(
        _out_kernel,
        grid=(t // tm,),
        in_specs=[pl.BlockSpec((tm, HGRN_WIDTH), lambda i: (i, 0)),
                  pl.BlockSpec((tm, HGRN_WIDTH), lambda i: (i, 0)),
                  pl.BlockSpec((tm, d), lambda i: (i, 0)),
                  const(wa.shape), const(wb.shape), per_batch(), const(n2g.shape),
                  per_batch(), per_batch(), const(wr.shape), const(br.shape)],
        out_specs=[pl.BlockSpec((tm, d), lambda i: (i, 0)),
                   pl.BlockSpec((tm * TOK_SUBLANES, LANES), lambda i: (i, 0)),
                   pl.BlockSpec((ROUTE_ROWS, tm), lambda i: (0, i))],
        out_shape=[jax.ShapeDtypeStruct((t, d), F32),
                   jax.ShapeDtypeStruct((t * TOK_SUBLANES, LANES), jnp.uint32),
                   jax.ShapeDtypeStruct((ROUTE_ROWS, t), F32)],
        compiler_params=_cparams(("arbitrary",)),
        name="out_proj",
    )(oa, ob, x2, wa, wb, g1, n2g, sh2, sc2, wr, br)


def _route_kernel(lg_ref, tri_ref, ri_ref, rw_ref, cnt_ref, carry_s):
    step = pl.program_id(0)

    @pl.when(step == 0)
    def _():
        carry_s[...] = jnp.zeros_like(carry_s)

    lg = lg_ref[...]
    tr = lg.shape[1]
    epg = EXPERTS_PER_GROUP
    gl = lg[N_EXPERTS:N_EXPERTS + N_GROUPS, :]
    row_g = lax.broadcasted_iota(jnp.int32, (N_GROUPS, tr), 0)
    gmax = jnp.max(gl, axis=0, keepdims=True)
    g_sel = jnp.min(jnp.where(gl == gmax, row_g, N_GROUPS), axis=0, keepdims=True)
    p_group = 1.0 / jnp.sum(jnp.exp(gl - gmax), axis=0, keepdims=True)

    e_in = lg[0:epg, :]
    for g in range(1, N_GROUPS):
        e_in = jnp.where(g_sel == g, lg[g * epg:(g + 1) * epg, :], e_in)
    row_e = lax.broadcasted_iota(jnp.int32, (epg, tr), 0)
    top1 = jnp.max(e_in, axis=0, keepdims=True)
    i1 = jnp.min(jnp.where(e_in == top1, row_e, epg), axis=0, keepdims=True)
    rest = jnp.where(row_e == i1, -jnp.inf, e_in)
    top2 = jnp.max(rest, axis=0, keepdims=True)
    i2 = jnp.min(jnp.where(rest == top2, row_e, epg), axis=0, keepdims=True)
    e2w = jnp.exp(top2 - top1)
    w1 = p_group / (1.0 + e2w)
    w2 = p_group * e2w / (1.0 + e2w)
    ex1 = g_sel * epg + i1
    ex2 = g_sel * epg + i2

    row_x = lax.broadcasted_iota(jnp.int32, (N_EXPERTS, tr), 0)
    oh1 = row_x == ex1
    oh2 = row_x == ex2
    oh = jnp.logical_or(oh1, oh2)
    before = _dot(oh.astype(BF16), tri_ref[...]) + carry_s[:, 0:1]
    rank1 = jnp.sum(jnp.where(oh1, before, 0.0), axis=0, keepdims=True)
    rank2 = jnp.sum(jnp.where(oh2, before, 0.0), axis=0, keepdims=True)
    carry_s[...] = carry_s[...] + jnp.sum(oh.astype(F32), axis=1, keepdims=True)

    zi = jnp.zeros((4, tr), jnp.int32)
    ri_ref[...] = jnp.concatenate([ex1, ex2, rank1.astype(jnp.int32), rank2.astype(jnp.int32), zi], axis=0)
    rw_ref[...] = jnp.concatenate([w1, w2, jnp.zeros((6, tr), F32)], axis=0)
    cnt_ref[...] = carry_s[...].astype(jnp.int32)


def _route(lg_t, tri):
    t = lg_t.shape[1]
    tr = tri.shape[0]
    return pl.pallas_call(
        _route_kernel,
        grid=(t // tr,),
        in_specs=[pl.BlockSpec((ROUTE_ROWS, tr), lambda i: (0, i)),
                  pl.BlockSpec((tr, tr), lambda i: (0, 0))],
        out_specs=[pl.BlockSpec((8, tr), lambda i: (0, i)),
                   pl.BlockSpec((8, tr), lambda i: (0, i)),
                   pl.BlockSpec((N_EXPERTS, LANES), lambda i: (0, 0))],
        out_shape=[jax.ShapeDtypeStruct((8, t), jnp.int32),
                   jax.ShapeDtypeStruct((8, t), F32),
                   jax.ShapeDtypeStruct((N_EXPERTS, LANES), jnp.int32)],
        scratch_shapes=[pltpu.VMEM((N_EXPERTS, LANES), F32)],
        compiler_params=_cparams(("arbitrary",)),
        name="route",
    )(lg_t, tri)


def _moe_kernel(rd_ref, be_ref, ne_ref, nb_ref, x_ref, wg_ref, wu_ref, wd_ref, y_ref,
                ys, wg_s, wu_s, wd_s, ssem):
    b = pl.program_id(0)
    last = pl.num_programs(0) - 1
    nb = nb_ref[0]
    tm = MOE_BLOCK
    ts = TOK_SUBLANES

    def scatter(blk):
        base = blk * tm
        for j in range(tm):
            dst = pl.multiple_of(rd_ref[base + j] * ts, ts)
            pltpu.make_async_copy(ys.at[pl.ds(j * ts, ts), :], y_ref.at[pl.ds(dst, ts), :], ssem).start()

    def wait_scatter():
        pltpu.make_async_copy(ys, y_ref.at[pl.ds(0, tm * ts), :], ssem).wait()

    @pl.when(b == 0)
    def _():
        ys[...] = jnp.zeros_like(ys)
        spare = pltpu.make_async_copy(ys, y_ref.at[pl.ds(y_ref.shape[0] - tm * ts, tm * ts), :], ssem)
        spare.start()
        spare.wait()

    @pl.when(ne_ref[b] == 1)
    def _():
        wg_s[...] = wg_ref[0].astype(BF16)
        wu_s[...] = wu_ref[0].astype(BF16)
        wd_s[...] = wd_ref[0].astype(BF16)

    @pl.when(b < nb)
    def _():
        scatter(jnp.maximum(b - 1, 0))
        parts = [_unpack_pair(x_ref[pl.ds(s, tm, stride=ts), :]) for s in range(ts)]
        x = jnp.concatenate([p[0] for p in parts] + [p[1] for p in parts], axis=1).astype(BF16)
        hid = _silu(_dot(x, wg_s[...])) * _dot(x, wu_s[...])
        y = _dot(hid.astype(BF16), wd_s[...])
        wait_scatter()
        for s in range(ts):
            ys[pl.ds(s, tm, stride=ts), :] = _pack_pair(y[:, s * LANES:(s + 1) * LANES],
                                                        y[:, (s + ts) * LANES:(s + ts + 1) * LANES])

        @pl.when(b == last)
        def _():
            scatter(b)
            wait_scatter()

    @pl.when(b == nb)
    def _():
        scatter(b - 1)
        wait_scatter()


def _moe(row_dst, blk_e, new_e, nb_real, x_sorted, wg, wu, wd, n_out_rows):
    n_blocks = blk_e.shape[0]
    d, de = wg.shape[1], wg.shape[2]
    tm = MOE_BLOCK

    def wspec(shape):
        return pl.BlockSpec(shape, lambda b, rd, be, ne, nb: (be[b], 0, 0))

    return pl.pallas_call(
        _moe_kernel,
        grid_spec=pltpu.PrefetchScalarGridSpec(
            num_scalar_prefetch=4,
            grid=(n_blocks,),
            in_specs=[pl.BlockSpec((tm * TOK_SUBLANES, LANES),
                                   lambda b, rd, be, ne, nb: (jnp.minimum(b, nb[0] - 1), 0)),
                      wspec((1, d, de)), wspec((1, d, de)), wspec((1, de, d))],
            out_specs=pl.BlockSpec(memory_space=pl.ANY),
            scratch_shapes=[pltpu.VMEM((tm * TOK_SUBLANES, LANES), jnp.uint32),
                            pltpu.VMEM((d, de), BF16), pltpu.VMEM((d, de), BF16), pltpu.VMEM((de, d), BF16),
                            pltpu.SemaphoreType.DMA(())]),
        out_shape=jax.ShapeDtypeStruct((n_out_rows * TOK_SUBLANES, LANES), jnp.uint32),
        compiler_params=_cparams(("arbitrary",)),
        name="moe",
    )(row_dst, blk_e, new_e, nb_real, x_sorted, wg, wu, wd)


def _dispatch_kernel(e1_ref, e2_ref, r1_ref, r2_ref, ps_ref, zf_ref, nz_ref, h_ref, x_ref, rd_ref, zbuf, sem, zsem):
    i = pl.program_id(0)
    ts = TOK_SUBLANES
    td = h_ref.shape[0] // ts
    n_tok = e1_ref.shape[0]
    base = i * td
    blk_rows = MOE_BLOCK * ts

    def zero_fill(blk):
        return pltpu.make_async_copy(
            zbuf, x_ref.at[pl.ds(pl.multiple_of(blk * blk_rows, blk_rows), blk_rows), :], zsem)

    @pl.when(i == 0)
    def _():
        zbuf[...] = jnp.zeros_like(zbuf)

        def fill(blk, carry):
            @pl.when(zf_ref[blk] == 1)
            def _():
                zero_fill(blk).start()
            return carry

        lax.fori_loop(0, zf_ref.shape[0], fill, 0)

        def init(blk, carry):
            for j in range(MOE_BLOCK):
                rd_ref[blk * MOE_BLOCK + j] = 2 * n_tok + j
            return carry

        lax.fori_loop(0, rd_ref.shape[0] // MOE_BLOCK, init, 0)

        def drain(k, carry):
            zero_fill(0).wait()
            return carry

        lax.fori_loop(0, nz_ref[0], drain, 0)

    def start(t, carry):
        src = h_ref.at[pl.ds(pl.multiple_of(t * ts, ts), ts), :]
        tok = base + t
        for slot, (e_ref, r_ref) in enumerate(((e1_ref, r1_ref), (e2_ref, r2_ref))):
            row = ps_ref[e_ref[tok]] + r_ref[tok]
            pltpu.make_async_copy(src, x_ref.at[pl.ds(pl.multiple_of(row * ts, ts), ts), :], sem).start()
            rd_ref[row] = 2 * tok + slot
        return carry

    lax.fori_loop(0, td, start, 0, unroll=8)
    for _ in range(2):
        pltpu.make_async_copy(h_ref, x_ref.at[pl.ds(0, td * ts), :], sem).wait()


def _dispatch(ri, pad_start, zero_blk, h2p, n_rows):
    t = ri.shape[1]
    td = min(512, t)
    ts = TOK_SUBLANES
    n_zero = jnp.sum(zero_blk, keepdims=True)
    return pl.pallas_call(
        _dispatch_kernel,
        grid_spec=pltpu.PrefetchScalarGridSpec(
            num_scalar_prefetch=7,
            grid=(t // td,),
            in_specs=[pl.BlockSpec((td * ts, LANES), lambda i, *_: (i, 0))],
            out_specs=[pl.BlockSpec(memory_space=pl.ANY), pl.BlockSpec(memory_space=pltpu.SMEM)],
            scratch_shapes=[pltpu.VMEM((MOE_BLOCK * ts, LANES), jnp.uint32),
                            pltpu.SemaphoreType.DMA(()), pltpu.SemaphoreType.DMA(())]),
        out_shape=[jax.ShapeDtypeStruct((n_rows * ts, LANES), jnp.uint32),
                   jax.ShapeDtypeStruct((n_rows,), jnp.int32)],
        compiler_params=_cparams(("arbitrary",)),
        name="dispatch",
    )(ri[0], ri[1], ri[2], ri[3], pad_start, zero_blk, n_zero, h2p)


def _moe_plan(counts, t):
    tm = MOE_BLOCK
    n_blocks = 2 * t // tm + N_EXPERTS
    padded = ((counts + tm - 1) // tm) * tm
    pad_end = jnp.cumsum(padded)
    pad_start = pad_end - padded
    blk = jnp.arange(n_blocks, dtype=jnp.int32)
    blk_e = jnp.minimum(jnp.sum(pad_end[None, :] <= (blk * tm)[:, None], axis=1), N_EXPERTS - 1).astype(jnp.int32)
    new_e = jnp.concatenate([jnp.ones((1,), jnp.int32), (blk_e[1:] != blk_e[:-1]).astype(jnp.int32)])
    nb_real = (pad_end[-1:] // tm).astype(jnp.int32)
    last_partial = jnp.any((blk[:, None] == (pad_end // tm - 1)[None, :]) & (counts % tm != 0)[None, :], axis=1)
    zero_blk = jnp.logical_or(last_partial, blk >= nb_real[0]).astype(jnp.int32)
    return pad_start, blk_e, new_e, nb_real, zero_blk, n_blocks * tm


def _combine_kernel(x1_ref, g2_ref, w1_ref, w2_ref, y_ref, o_ref):
    tc = x1_ref.shape[0]
    ts = TOK_SUBLANES
    w1 = w1_ref[...]
    w2 = w2_ref[...]
    for s in range(ts):
        a_lo, a_hi = _unpack_pair(y_ref[pl.ds(s, tc, stride=2 * ts), :])
        b_lo, b_hi = _unpack_pair(y_ref[pl.ds(ts + s, tc, stride=2 * ts), :])
        for c, ya, yb in ((s, a_lo, b_lo), (s + ts, a_hi, b_hi)):
            cols = slice(c * LANES, (c + 1) * LANES)
            o_ref[:, cols] = x1_ref[:, cols] + g2_ref[0][:, cols] * (w1 * ya + w2 * yb)


def _combine(x1, g2, w1, w2, y2, seq):
    t, d = x1.shape
    tc = min(512, seq)
    return pl.pallas_call(
        _combine_kernel,
        grid=(t // tc,),
        in_specs=[pl.BlockSpec((tc, d), lambda i: (i, 0)),
                  pl.BlockSpec((1, 1, d), lambda i: (i * tc // seq, 0, 0)),
                  pl.BlockSpec((tc, 1), lambda i: (i, 0)),
                  pl.BlockSpec((tc, 1), lambda i: (i, 0)),
                  pl.BlockSpec((tc * 2 * TOK_SUBLANES, LANES), lambda i: (i, 0))],
        out_specs=pl.BlockSpec((tc, d), lambda i: (i, 0)),
        out_shape=jax.ShapeDtypeStruct((t, d), F32),
        compiler_params=_cparams(("arbitrary",)),
        name="combine",
    )(x1, g2, w1, w2, y2)


def _q_up_layout(w_q_up):
    w = w_q_up.reshape(Q_LORA, MLA_HEADS, MLA_QK)
    w = jnp.pad(w, ((0, 0), (0, 0), (0, MLA_QPAD - MLA_QK)))
    return w.reshape(Q_LORA, MLA_HEADS * MLA_QPAD).astype(BF16)


def _pad_lanes(g, width):
    return jnp.pad(g, (0, width - g.shape[0])).reshape(1, width)


def kernel(x, c, positions, w_ada, b_ada, norm1_g, w_in, hgrn_lb_logits, hgrn_onorm_g, q_a_norm_g, w_q_up,
           kv_a_norm_g, w_kv_up, q_norm_g, k_norm_g, attn_onorm_g, w_out, norm2_g, w_group, b_group,
           w_router, b_router, w_gate, w_up, w_down):
    bsz, seq, d = x.shape
    t = bsz * seq
    depth = w_ada.shape[0]
    half = MLA_ROPE // 2
    inv_freq = ROPE_BASE ** (-jnp.arange(0, MLA_ROPE, 2, dtype=F32) / MLA_ROPE)
    freq = jnp.concatenate([inv_freq, inv_freq, jnp.zeros((LANES - 2 * half,), F32)]).reshape(1, LANES)
    pos3 = positions.reshape(bsz, seq, 1)
    tr = min(512, t)
    tri = jnp.triu(jnp.ones((tr, tr), BF16), 1)

    x2 = x.reshape(t, d)
    for l in range(depth):
        mod = _ada(c, w_ada[l], b_ada[l]).reshape(bsz, 6, 1, d)
        sh1, sc1, g1, sh2, sc2, g2 = (mod[:, i] for i in range(6))

        w_in_p = jnp.pad(w_in[l], ((0, 0), (0, IN_COLS_PAD - IN_COLS))).astype(BF16)
        proj = _in_proj(x2, norm1_g[l].reshape(1, d), sh1, sc1, w_in_p, seq)
        proj3 = proj.reshape(bsz, seq, IN_COLS_PAD)

        o_a = _hgrn(proj3, hgrn_lb_logits, hgrn_onorm_g[l].reshape(1, HGRN_DK), l)

        q, k, v = _mla_up(proj3, pos3, _q_up_layout(w_q_up[l]), w_kv_up[l].astype(BF16),
                          q_a_norm_g[l].reshape(1, Q_LORA), kv_a_norm_g[l].reshape(1, KV_LORA),
                          _pad_lanes(q_norm_g[l], MLA_QPAD), _pad_lanes(k_norm_g[l], MLA_QPAD), freq)
        o_b = _attention(q, k, v, attn_onorm_g[l].reshape(MLA_V, 1))

        w_o = w_out[l].astype(BF16)
        wr = jnp.pad(jnp.concatenate([w_router[l], w_group[l]], axis=1),
                     ((0, 0), (0, LANES - N_EXPERTS - N_GROUPS))).astype(BF16)
        br = _pad_lanes(jnp.concatenate([b_router[l], b_group[l]]), LANES)
        x1, h2, lg_t = _out_proj(o_a.reshape(t, HGRN_WIDTH), o_b.reshape(t, HGRN_WIDTH), x2,
                                 w_o[:HGRN_WIDTH], w_o[HGRN_WIDTH:], g1, norm2_g[l].reshape(1, d),
                                 sh2, sc2, wr, br, seq)

        ri, rw, cnt = _route(lg_t, tri)
        counts = cnt[:, 0]
        pad_start, blk_e, new_e, nb_real, zero_blk, n_rows = _moe_plan(counts, t)
        x_sorted, row_dst = _dispatch(ri, pad_start, zero_blk, h2, n_rows)
        y2 = _moe(row_dst, blk_e, new_e, nb_real, x_sorted, w_gate[l], w_up[l], w_down[l], 2 * t + MOE_BLOCK)
        x2 = _combine(x1, g2, rw[0].reshape(t, 1), rw[1].reshape(t, 1), y2, seq)
    return x2.reshape(bsz, seq, d)
```

```python
import functools

import jax
import jax.numpy as jnp
from jax import lax
from jax.experimental import pallas as pl
from jax.experimental.pallas import tpu as pltpu

F32 = jnp.float32
BF16 = jnp.bfloat16
EPS = 1e-6
LOG2E = 1.4426950408889634

D_MODEL = 2048
HGRN_WIDTH = 1024
HGRN_DK = 128
HGRN_HEADS = 8
HGRN_CHUNK = 64
MLA_HEADS = 8
MLA_NOPE = 128
MLA_ROPE = 64
MLA_QK = MLA_NOPE + MLA_ROPE
MLA_V = 128
MLA_QPAD = 256
Q_LORA = 512
KV_LORA = 256
ROPE_BASE = 10000.0
IN_COLS = 4 * HGRN_WIDTH + Q_LORA + KV_LORA + MLA_ROPE
IN_COLS_PAD = 5120
N_GROUPS = 4
EXPERTS_PER_GROUP = 8
N_EXPERTS = 32
D_EXPERT = 512
ROUTE_ROWS = 40
MOE_BLOCK = 256
LANES = 128
TOK_SUBLANES = 8
SCATTER_PACE = 3
SCATTER_PACE_TABLE = 1024
VMEM_LIMIT = 56 * 1024 * 1024


def _cparams(sem):
    return pltpu.CompilerParams(dimension_semantics=sem, vmem_limit_bytes=VMEM_LIMIT)


def _dot(a, b):
    return jnp.dot(a, b, preferred_element_type=F32)


def _dot_nt(a, b):
    return lax.dot_general(a, b, (((1,), (1,)), ((), ())), preferred_element_type=F32)


def _rms(x, g):
    return x * lax.rsqrt(jnp.mean(x * x, axis=-1, keepdims=True) + EPS) * g


def _silu(x):
    return x * jax.nn.sigmoid(x)


def _pack_pair(lo, hi):
    lo_b = lax.bitcast_convert_type(lo.astype(BF16).astype(F32), jnp.uint32)
    hi_b = lax.bitcast_convert_type(hi.astype(BF16).astype(F32), jnp.uint32)
    return hi_b | (lo_b >> 16)


def _unpack_pair(w):
    lo = lax.bitcast_convert_type(w << 16, F32)
    hi = lax.bitcast_convert_type(w & jnp.uint32(0xFFFF0000), F32)
    return lo, hi


def _ada_kernel(c_ref, w_ref, b_ref, o_ref):
    ca = _silu(c_ref[...]).astype(BF16)
    o_ref[...] = _dot(ca, w_ref[...].astype(BF16)) + b_ref[...]


def _ada(c, w, b):
    bsz, d = c.shape
    n = w.shape[1]
    tn = 1024
    return pl.pallas_call(
        _ada_kernel,
        grid=(n // tn,),
        in_specs=[pl.BlockSpec((bsz, d), lambda j: (0, 0)),
                  pl.BlockSpec((d, tn), lambda j: (0, j)),
                  pl.BlockSpec((1, tn), lambda j: (0, j))],
        out_specs=pl.BlockSpec((bsz, tn), lambda j: (0, j)),
        out_shape=jax.ShapeDtypeStruct((bsz, n), F32),
        compiler_params=_cparams(("arbitrary",)),
        name="ada",
    )(c, w, b.reshape(1, n))


NORM_ROWS = 32


def _norm_kernel(x_ref, g_ref, sh_ref, sc_ref, h_ref):
    def body(c, carry):
        r = pl.ds(pl.multiple_of(c * NORM_ROWS, NORM_ROWS), NORM_ROWS)
        h = _rms(x_ref[r, :], g_ref[...]) * (1.0 + sc_ref[0]) + sh_ref[0]
        h_ref[r, :] = h.astype(BF16)
        return carry
    lax.fori_loop(0, x_ref.shape[0] // NORM_ROWS, body, 0, unroll=2)


def _norm_mod(x2, g, sh, sc, seq):
    t, d = x2.shape
    tm = min(512, seq)
    return pl.pallas_call(
        _norm_kernel,
        grid=(t // tm,),
        in_specs=[pl.BlockSpec((tm, d), lambda i: (i, 0)),
                  pl.BlockSpec((1, d), lambda i: (0, 0)),
                  pl.BlockSpec((1, 1, d), lambda i: (i * tm // seq, 0, 0)),
                  pl.BlockSpec((1, 1, d), lambda i: (i * tm // seq, 0, 0))],
        out_specs=pl.BlockSpec((tm, d), lambda i: (i, 0)),
        out_shape=jax.ShapeDtypeStruct((t, d), BF16),
        compiler_params=_cparams(("arbitrary",)),
        name="norm_mod",
    )(x2, g, sh, sc)


def _in_kernel(h_ref, w_ref, wt_ref, o_ref, wb_s, *, n_main):
    j = pl.program_id(0)
    first_row_tile = pl.program_id(1) == 0

    @pl.when(jnp.logical_and(first_row_tile, j < n_main))
    def _():
        wb_s[...] = w_ref[...].astype(BF16)

    @pl.when(jnp.logical_and(first_row_tile, j == n_main))
    def _():
        wb_s[...] = wt_ref[...]

    o_ref[...] = _dot(h_ref[...], wb_s[...]).astype(BF16)


def _in_proj(h, w, seq):
    t, d = h.shape
    tn = 1024
    n_main = IN_COLS // tn
    w_tail = jnp.pad(w[:, n_main * tn:], ((0, 0), (0, (n_main + 1) * tn - IN_COLS))).astype(BF16)
    tm = min(1024, t)
    return pl.pallas_call(
        functools.partial(_in_kernel, n_main=n_main),
        grid=(IN_COLS_PAD // tn, t // tm),
        in_specs=[pl.BlockSpec((tm, d), lambda j, i: (i, 0)),
                  pl.BlockSpec((d, tn), lambda j, i: (0, jnp.minimum(j, n_main - 1))),
                  pl.BlockSpec((d, tn), lambda j, i: (0, 0))],
        out_specs=pl.BlockSpec((tm, tn), lambda j, i: (i, j)),
        out_shape=jax.ShapeDtypeStruct((t, IN_COLS_PAD), BF16),
        scratch_shapes=[pltpu.VMEM((d, tn), BF16)],
        compiler_params=_cparams(("arbitrary", "arbitrary")),
        name="in_proj",
    )(h, w, w_tail)


HG_ROWS = 256


def _software_pipeline(stages, n_blocks):
    depth = len(stages)

    def step(i, static):
        for k in reversed(range(depth)):
            if static and not 0 <= i - k < n_blocks:
                continue
            stages[k](i - k)

    if n_blocks < depth:
        for i in range(n_blocks + depth - 1):
            step(i, True)
        return
    for i in range(depth - 1):
        step(i, True)

    def steady(i, carry):
        step(i, False)
        return carry

    lax.fori_loop(depth - 1, n_blocks, steady, 0)
    for i in range(n_blocks, n_blocks + depth - 1):
        step(i, True)


def _chunk_mask(n):
    row = lax.broadcasted_iota(jnp.int32, (n, n), 0)
    col = lax.broadcasted_iota(jnp.int32, (n, n), 1)
    return jnp.logical_and(row // HGRN_CHUNK == col // HGRN_CHUNK, col <= row)


def _hgrn_kernel(q_ref, f_ref, i_ref, g_ref, lbl_ref, og_ref, o_ref,
                 qb_s, u_s, sp_s, dec_s, oi_s, b_s, k_s, qin_s, kin_s, ku_s, a_s, *, layer):
    seq = q_ref.shape[1]
    cs = HGRN_CHUNK
    rb = min(HG_ROWS, seq)
    nc = rb // cs
    dk = HGRN_DK
    lg = lbl_ref[...]
    ex = jnp.exp(lg - jnp.max(lg, axis=0, keepdims=True))
    sm = ex / jnp.sum(ex, axis=0, keepdims=True)
    lb = jnp.sum(sm[0:layer + 1], axis=0, keepdims=True)
    mask = _chunk_mask(rb)
    tri = mask.astype(BF16)
    row_chunk = lax.broadcasted_iota(jnp.int32, (rb, dk), 0) // cs

    chunk_sel = [(row_chunk == c).astype(BF16) for c in range(nc)]

    def rows(blk):
        return pl.ds(pl.multiple_of(blk * rb, rb), rb)

    def stage1(blk):
        r = rows(blk)
        f = lb + (1.0 - lb) * jax.nn.sigmoid(f_ref[0, r, :].astype(F32))
        lf = jnp.log(f)
        k_s[r, :] = 1.0 - f
        hi = lf.astype(BF16)
        r1 = lf - hi.astype(F32)
        mid = r1.astype(BF16)
        lo = (r1 - mid.astype(F32)).astype(BF16)
        bhm = _dot(tri, jnp.concatenate([hi, mid], axis=1))
        b_s[r, :] = bhm[:, 0:dk] + bhm[:, dk:] + _dot(tri, lo)

    def stage2(blk):
        r = rows(blk)
        b3 = b_s[r, :].reshape(nc, cs, dk)
        bmid = b3[:, cs // 2 - 1:cs // 2, :]
        blast = b3[:, cs - 1:cs, :]
        q3 = (q_ref[0, r, :].astype(F32) * dk ** -0.5).reshape(nc, cs, dk)
        k3 = k_s[r, :].reshape(nc, cs, dk)
        qin_s[r, :] = (q3 * jnp.exp(b3 - bmid)).reshape(rb, dk).astype(BF16)
        kin_s[r, :] = (k3 * jnp.exp(bmid - b3)).reshape(rb, dk).astype(BF16)
        ku_s[r, :] = (k3 * jnp.exp(blast - b3)).reshape(rb, dk).astype(BF16)
        qb_s[r, :] = (q3 * jnp.exp(b3)).reshape(rb, dk).astype(BF16)
        dec_s[pl.ds(blk * nc, nc)] = jnp.exp(blast)

    def stage3(blk):
        r = rows(blk)
        a_s[r, :] = jnp.where(mask, _dot_nt(qin_s[r, :], kin_s[r, :]), 0.0).astype(BF16)
        vt = i_ref[0, r, :].astype(F32).T.astype(BF16)
        ku = ku_s[r, :]
        ut = _dot(vt, jnp.concatenate([ku * sel for sel in chunk_sel], axis=1))
        for c in range(nc):
            u_s[blk * nc + c] = ut[:, c * dk:(c + 1) * dk]

    def stage4(blk):
        r = rows(blk)
        oi_s[r, :] = _dot(a_s[r, :], i_ref[0, r, :])

    _software_pipeline([stage1, stage2, stage3, stage4], seq // rb)

    def phase2(n, st):
        sp_s[n] = st.astype(BF16)
        return st * dec_s[n] + u_s[n]

    lax.fori_loop(0, seq // cs, phase2, jnp.zeros((dk, dk), F32), unroll=4)

    def phase3(blk, carry):
        r = pl.ds(pl.multiple_of(blk * rb, rb), rb)
        inter = [_dot_nt(qb_s[pl.ds(pl.multiple_of(blk * rb + c * cs, cs), cs), :], sp_s[blk * nc + c])
                 for c in range(nc)]
        o = oi_s[r, :] + jnp.concatenate(inter, axis=0)
        o = _rms(o, og_ref[...]) * _silu(g_ref[0, r, :].astype(F32))
        o_ref[0, r, :] = o.astype(BF16)
        return carry

    lax.fori_loop(0, seq // rb, phase3, 0, unroll=2)


def _hgrn(proj3, lb_logits, onorm_g, layer):
    bsz, seq, _ = proj3.shape
    nh = HGRN_HEADS

    def col(off):
        return pl.BlockSpec((1, seq, HGRN_DK), lambda b, h: (b, 0, off * nh + h))

    nl = lb_logits.shape[0]
    return pl.pallas_call(
        functools.partial(_hgrn_kernel, layer=layer),
        grid=(bsz, nh),
        in_specs=[col(0), col(1), col(2), col(3),
                  pl.BlockSpec((nl, HGRN_DK), lambda b, h: (0, h)),
                  pl.BlockSpec((1, HGRN_DK), lambda b, h: (0, 0))],
        out_specs=pl.BlockSpec((1, seq, HGRN_DK), lambda b, h: (b, 0, h)),
        out_shape=jax.ShapeDtypeStruct((bsz, seq, HGRN_WIDTH), BF16),
        scratch_shapes=[pltpu.VMEM((seq, HGRN_DK), BF16),
                        pltpu.VMEM((seq // HGRN_CHUNK, HGRN_DK, HGRN_DK), F32),
                        pltpu.VMEM((seq // HGRN_CHUNK, HGRN_DK, HGRN_DK), BF16),
                        pltpu.VMEM((seq // HGRN_CHUNK, 1, HGRN_DK), F32),
                        pltpu.VMEM((seq, HGRN_DK), F32),
                        pltpu.VMEM((seq, HGRN_DK), F32), pltpu.VMEM((seq, HGRN_DK), F32),
                        pltpu.VMEM((seq, HGRN_DK), BF16), pltpu.VMEM((seq, HGRN_DK), BF16),
                        pltpu.VMEM((seq, HGRN_DK), BF16), pltpu.VMEM((seq, min(HG_ROWS, seq)), BF16)],
        compiler_params=_cparams(("arbitrary", "arbitrary")),
        name="hgrn",
    )(proj3, proj3, proj3, proj3, lb_logits, onorm_g)


def _rope(x, cos, sin_signed, lane):
    half = MLA_ROPE // 2
    swapped = jnp.where(lane < half, pltpu.roll(x, LANES - half, 1), pltpu.roll(x, half, 1))
    return x * cos + swapped * sin_signed


def _up_kernel(p_ref, pos_ref, wq_ref, wkv_ref, qag_ref, kvag_ref, qg_ref, kg_ref, freq_ref,
               q_ref, k_ref, v_ref):
    p = p_ref[...].astype(F32)
    q_a = p[:, 0:Q_LORA]
    kv_a = p[:, Q_LORA:Q_LORA + KV_LORA]
    k_pe = p[:, Q_LORA + KV_LORA:Q_LORA + KV_LORA + LANES]
    qf = _dot(_rms(q_a, qag_ref[...]).astype(BF16), wq_ref[...])
    kvf = _dot(_rms(kv_a, kvag_ref[...]).astype(BF16), wkv_ref[...])

    tm = p.shape[0]
    lane = lax.broadcasted_iota(jnp.int32, (tm, LANES), 1)
    half = MLA_ROPE // 2
    sin_all = jnp.sin(pos_ref[...].astype(F32) * freq_ref[0:1, :] + freq_ref[1:2, :])
    valid = lane < MLA_ROPE
    cos = jnp.where(valid, pltpu.roll(sin_all, MLA_ROPE, 1), 0.0)
    sin_signed = jnp.where(valid, jnp.where(lane < half, -sin_all, sin_all), 0.0)

    qg = qg_ref[...]
    kg = kg_ref[...]
    kpe_ss = jnp.sum(k_pe * k_pe, axis=-1, keepdims=True)
    kpe_rot = _rope(k_pe * kg[:, MLA_NOPE:], cos, sin_signed, lane)
    scale = MLA_QK ** -0.5 * LOG2E
    for h in range(MLA_HEADS):
        qh = qf[:, h * MLA_QPAD:(h + 1) * MLA_QPAD]
        rq = lax.rsqrt(jnp.sum(qh * qh, axis=-1, keepdims=True) / MLA_QK + EPS)
        qn = qh * rq * qg
        q_ref[0, h, :, 0:MLA_NOPE] = (qn[:, 0:MLA_NOPE] * scale).astype(BF16)
        q_ref[0, h, :, MLA_NOPE:] = (_rope(qn[:, MLA_NOPE:], cos, sin_signed, lane) * scale).astype(BF16)
        kn = kvf[:, h * MLA_QPAD:h * MLA_QPAD + MLA_NOPE]
        rk = lax.rsqrt((jnp.sum(kn * kn, axis=-1, keepdims=True) + kpe_ss) / MLA_QK + EPS)
        k_ref[0, h, :, 0:MLA_NOPE] = (kn * rk * kg[:, 0:MLA_NOPE]).astype(BF16)
        k_ref[0, h, :, MLA_NOPE:] = (kpe_rot * rk).astype(BF16)
        v_ref[0, h, :, :] = kvf[:, h * MLA_QPAD + MLA_NOPE:(h + 1) * MLA_QPAD].T.astype(BF16)


def _mla_up(proj3, pos3, wq, wkv, qag, kvag, qg, kg, freq):
    bsz, seq, _ = proj3.shape
    tm = min(512, seq)
    nh = MLA_HEADS
    mla_block = 4 * HGRN_WIDTH // 1024

    def const(shape):
        return pl.BlockSpec(shape, lambda b, i: (0,) * len(shape))

    return pl.pallas_call(
        _up_kernel,
        grid=(bsz, seq // tm),
        in_specs=[pl.BlockSpec((None, tm, 1024), lambda b, i: (b, i, mla_block)),
                  pl.BlockSpec((None, tm, 1), lambda b, i: (b, i, 0)),
                  const(wq.shape), const(wkv.shape), const(qag.shape), const(kvag.shape),
                  const(qg.shape), const(kg.shape), const(freq.shape)],
        out_specs=[pl.BlockSpec((1, nh, tm, MLA_QPAD), lambda b, i: (b, 0, i, 0)),
                   pl.BlockSpec((1, nh, tm, MLA_QPAD), lambda b, i: (b, 0, i, 0)),
                   pl.BlockSpec((1, nh, MLA_V, tm), lambda b, i: (b, 0, 0, i))],
        out_shape=[jax.ShapeDtypeStruct((bsz, nh, seq, MLA_QPAD), BF16),
                   jax.ShapeDtypeStruct((bsz, nh, seq, MLA_QPAD), BF16),
                   jax.ShapeDtypeStruct((bsz, nh, MLA_V, seq), BF16)],
        compiler_params=_cparams(("arbitrary", "arbitrary")),
        name="mla_up",
    )(proj3, pos3, wq, wkv, qag, kvag, qg, kg, freq)


ATT_T = 256


def _attn_kernel(q_ref, k_ref, vt_ref, g_ref, o_ref):
    seq = q_ref.shape[2]
    t = min(ATT_T, seq)
    key = lax.broadcasted_iota(jnp.int32, (t, t), 0)
    qry = lax.broadcasted_iota(jnp.int32, (t, t), 1)
    causal = key <= qry
    neg = jnp.finfo(F32).min
    for qi in range(seq // t):
        off = qi * t
        q = q_ref[0, 0, off:off + t, :]
        sd = jnp.where(causal, _dot_nt(k_ref[0, 0, off:off + t, :], q), neg)
        m = jnp.max(sd, axis=0, keepdims=True)
        if qi > 0:
            so = _dot_nt(k_ref[0, 0, 0:off, :], q)
            m = jnp.maximum(m, jnp.max(so, axis=0, keepdims=True))
        pd = jnp.exp2(sd - m)
        l = jnp.sum(pd, axis=0, keepdims=True)
        ot = _dot(vt_ref[0, 0, :, off:off + t], pd.astype(BF16))
        if qi > 0:
            po = jnp.exp2(so - m)
            l = l + jnp.sum(po, axis=0, keepdims=True)
            ot = ot + _dot(vt_ref[0, 0, :, 0:off], po.astype(BF16))
        ot = ot * (1.0 / l)
        ot = ot * lax.rsqrt(jnp.mean(ot * ot, axis=0, keepdims=True) + EPS) * g_ref[...]
        o_ref[0, off:off + t, :] = ot.T.astype(BF16)


def _attention(q, k, v, g):
    bsz, nh, seq, _ = q.shape
    return pl.pallas_call(
        _attn_kernel,
        grid=(bsz, nh),
        in_specs=[pl.BlockSpec((1, 1, seq, MLA_QPAD), lambda b, h: (b, h, 0, 0)),
                  pl.BlockSpec((1, 1, seq, MLA_QPAD), lambda b, h: (b, h, 0, 0)),
                  pl.BlockSpec((1, 1, MLA_V, seq), lambda b, h: (b, h, 0, 0)),
                  pl.BlockSpec((MLA_V, 1), lambda b, h: (0, 0))],
        out_specs=pl.BlockSpec((1, seq, MLA_V), lambda b, h: (b, 0, h)),
        out_shape=jax.ShapeDtypeStruct((bsz, seq, nh * MLA_V), BF16),
        compiler_params=_cparams(("arbitrary", "arbitrary")),
        name="attn",
    )(q, k, v, g)


def _out_kernel(oa_ref, ob_ref, x_ref, wa_ref, wb_ref, g1_ref, n2g_ref, sh2_ref, sc2_ref,
                wr_ref, br_ref, x1_ref, h2_ref, lg_ref):
    mix = _dot(oa_ref[...], wa_ref[...]) + _dot(ob_ref[...], wb_ref[...])
    x1 = x_ref[...] + g1_ref[0] * mix
    x1_ref[...] = x1
    h2 = _rms(x1, n2g_ref[...]) * (1.0 + sc2_ref[0]) + sh2_ref[0]
    tm = h2.shape[0]
    for s in range(TOK_SUBLANES):
        h2_ref[pl.ds(s, tm, stride=TOK_SUBLANES), :] = _pack_pair(
            h2[:, s * LANES:(s + 1) * LANES], h2[:, (s + TOK_SUBLANES) * LANES:(s + TOK_SUBLANES + 1) * LANES])
    lg = _dot(h2.astype(BF16), wr_ref[...]) + br_ref[...]
    lg_ref[...] = lg.T[0:ROUTE_ROWS, :]


def _out_proj(oa, ob, x2, wa, wb, g1, n2g, sh2, sc2, wr, br, seq):
    t, d = x2.shape
    tm = min(512, seq)

    def const(shape):
        return pl.BlockSpec(shape, lambda i: (0,) * len(shape))

    def per_batch():
        return pl.BlockSpec((1, 1, d), lambda i: (i * tm // seq, 0, 0))

    return pl.pallas_call(
        _out_kernel,
        grid=(t // tm,),
        in_specs=[pl.BlockSpec((tm, HGRN_WIDTH), lambda i: (i, 0)),
                  pl.BlockSpec((tm, HGRN_WIDTH), lambda i: (i, 0)),
                  pl.BlockSpec((tm, d), lambda i: (i, 0)),
                  const(wa.shape), const(wb.shape), per_batch(), const(n2g.shape),
                  per_batch(), per_batch(), const(wr.shape), const(br.shape)],
        out_specs=[pl.BlockSpec((tm, d), lambda i: (i, 0)),
                   pl.BlockSpec((tm * TOK_SUBLANES, LANES), lambda i: (i, 0)),
                   pl.BlockSpec((ROUTE_ROWS, tm), lambda i: (0, i))],
        out_shape=[jax.ShapeDtypeStruct((t, d), F32),
                   jax.ShapeDtypeStruct((t * TOK_SUBLANES, LANES), jnp.uint32),
                   jax.ShapeDtypeStruct((ROUTE_ROWS, t), F32)],
        compiler_params=_cparams(("arbitrary",)),
        name="out_proj",
    )(oa, ob, x2, wa, wb, g1, n2g, sh2, sc2, wr, br)


def _route_kernel(lg_ref, tri_ref, ri_ref, rw_ref, cnt_ref, carry_s):
    step = pl.program_id(0)

    @pl.when(step == 0)
    def _():
        carry_s[...] = jnp.zeros_like(carry_s)

    lg = lg_ref[...]
    tr = lg.shape[1]
    epg = EXPERTS_PER_GROUP
    gl = lg[N_EXPERTS:N_EXPERTS + N_GROUPS, :]
    row_g = lax.broadcasted_iota(jnp.int32, (N_GROUPS, tr), 0)
    gmax = jnp.max(gl, axis=0, keepdims=True)
    g_sel = jnp.min(jnp.where(gl == gmax, row_g, N_GROUPS), axis=0, keepdims=True)
    p_group = 1.0 / jnp.sum(jnp.exp(gl - gmax), axis=0, keepdims=True)

    e_in = lg[0:epg, :]
    for g in range(1, N_GROUPS):
        e_in = jnp.where(g_sel == g, lg[g * epg:(g + 1) * epg, :], e_in)
    row_e = lax.broadcasted_iota(jnp.int32, (epg, tr), 0)
    top1 = jnp.max(e_in, axis=0, keepdims=True)
    i1 = jnp.min(jnp.where(e_in == top1, row_e, epg), axis=0, keepdims=True)
    rest = jnp.where(row_e == i1, -jnp.inf, e_in)
    top2 = jnp.max(rest, axis=0, keepdims=True)
    i2 = jnp.min(jnp.where(rest == top2, row_e, epg), axis=0, keepdims=True)
    e2w = jnp.exp(top2 - top1)
    w1 = p_group / (1.0 + e2w)
    w2 = p_group * e2w / (1.0 + e2w)
    ex1 = g_sel * epg + i1
    ex2 = g_sel * epg + i2

    row_x = lax.broadcasted_iota(jnp.int32, (N_EXPERTS, tr), 0)
    oh1 = row_x == ex1
    oh2 = row_x == ex2
    oh = jnp.logical_or(oh1, oh2)
    before = _dot(oh.astype(BF16), tri_ref[...]) + carry_s[:, 0:1]
    rank1 = jnp.sum(jnp.where(oh1, before, 0.0), axis=0, keepdims=True)
    rank2 = jnp.sum(jnp.where(oh2, before, 0.0), axis=0, keepdims=True)
    carry_s[...] = carry_s[...] + jnp.sum(oh.astype(F32), axis=1, keepdims=True)

    zi = jnp.zeros((4, tr), jnp.int32)
    ri_ref[...] = jnp.concatenate([ex1, ex2, rank1.astype(jnp.int32), rank2.astype(jnp.int32), zi], axis=0)
    rw_ref[...] = jnp.concatenate([w1, w2, jnp.zeros((6, tr), F32)], axis=0)
    cnt_ref[...] = carry_s[...].astype(jnp.int32)


def _route(lg_t, tri):
    t = lg_t.shape[1]
    tr = tri.shape[0]
    return pl.pallas_call(
        _route_kernel,
        grid=(t // tr,),
        in_specs=[pl.BlockSpec((ROUTE_ROWS, tr), lambda i: (0, i)),
                  pl.BlockSpec((tr, tr), lambda i: (0, 0))],
        out_specs=[pl.BlockSpec((8, tr), lambda i: (0, i)),
                   pl.BlockSpec((8, tr), lambda i: (0, i)),
                   pl.BlockSpec((N_EXPERTS, LANES), lambda i: (0, 0))],
        out_shape=[jax.ShapeDtypeStruct((8, t), jnp.int32),
                   jax.ShapeDtypeStruct((8, t), F32),
                   jax.ShapeDtypeStruct((N_EXPERTS, LANES), jnp.int32)],
        scratch_shapes=[pltpu.VMEM((N_EXPERTS, LANES), F32)],
        compiler_params=_cparams(("arbitrary",)),
        name="route",
    )(lg_t, tri)


def _moe_kernel(rd_ref, zr_ref, be_ref, ne_ref, nb_ref, x_ref, wg_ref, wu_ref, wd_ref, y_ref,
                ys, wg_s, wu_s, wd_s, ssem):
    b = pl.program_id(0)
    last = pl.num_programs(0) - 1
    nb = nb_ref[0]
    tm = MOE_BLOCK
    ts = TOK_SUBLANES

    def scatter(blk):
        base = blk * tm
        for j in range(tm):
            row = rd_ref[base + j]
            for k in range(SCATTER_PACE):
                row = row + zr_ref[(row + k) & (zr_ref.shape[0] - 1)]
            dst = pl.multiple_of(row * ts, ts)
            pltpu.make_async_copy(ys.at[pl.ds(j * ts, ts), :], y_ref.at[pl.ds(dst, ts), :], ssem).start()

    def wait_scatter():
        pltpu.make_async_copy(ys, y_ref.at[pl.ds(0, tm * ts), :], ssem).wait()

    @pl.when(b == 0)
    def _():
        ys[...] = jnp.zeros_like(ys)
        spare = pltpu.make_async_copy(ys, y_ref.at[pl.ds(y_ref.shape[0] - tm * ts, tm * ts), :], ssem)
        spare.start()
        spare.wait()

    @pl.when(ne_ref[b] == 1)
    def _():
        wg_s[...] = wg_ref[0].astype(BF16)
        wu_s[...] = wu_ref[0].astype(BF16)
        wd_s[...] = wd_ref[0].astype(BF16)

    @pl.when(b < nb)
    def _():
        scatter(jnp.maximum(b - 1, 0))
        parts = [_unpack_pair(x_ref[pl.ds(s, tm, stride=ts), :]) for s in range(ts)]
        x = jnp.concatenate([p[0] for p in parts] + [p[1] for p in parts], axis=1).astype(BF16)
        hid = _silu(_dot(x, wg_s[...])) * _dot(x, wu_s[...])
        y = _dot(hid.astype(BF16), wd_s[...])
        wait_scatter()
        for s in range(ts):
            ys[pl.ds(s, tm, stride=ts), :] = _pack_pair(y[:, s * LANES:(s + 1) * LANES],
                                                        y[:, (s + ts) * LANES:(s + ts + 1) * LANES])

        @pl.when(b == last)
        def _():
            scatter(b)
            wait_scatter()

    @pl.when(b == nb)
    def _():
        scatter(b - 1)
        wait_scatter()


def _moe(row_dst, blk_e, new_e, nb_real, x_sorted, wg, wu, wd, n_out_rows):
    n_blocks = blk_e.shape[0]
    d, de = wg.shape[1], wg.shape[2]
    tm = MOE_BLOCK

    def wspec(shape):
        return pl.BlockSpec(shape, lambda b, rd, zr, be, ne, nb: (be[b], 0, 0))

    return pl.pallas_call(
        _moe_kernel,
        grid_spec=pltpu.PrefetchScalarGridSpec(
            num_scalar_prefetch=5,
            grid=(n_blocks,),
            in_specs=[pl.BlockSpec((tm * TOK_SUBLANES, LANES),
                                   lambda b, rd, zr, be, ne, nb: (jnp.minimum(b, nb[0] - 1), 0)),
                      wspec((1, d, de)), wspec((1, d, de)), wspec((1, de, d))],
            out_specs=pl.BlockSpec(memory_space=pl.ANY),
            scratch_shapes=[pltpu.VMEM((tm * TOK_SUBLANES, LANES), jnp.uint32),
                            pltpu.VMEM((d, de), BF16), pltpu.VMEM((d, de), BF16), pltpu.VMEM((de, d), BF16),
                            pltpu.SemaphoreType.DMA(())]),
        out_shape=jax.ShapeDtypeStruct((n_out_rows * TOK_SUBLANES, LANES), jnp.uint32),
        compiler_params=_cparams(("arbitrary",)),
        name="moe",
    )(row_dst, jnp.zeros((SCATTER_PACE_TABLE,), jnp.int32), blk_e, new_e, nb_real, x_sorted, wg, wu, wd)


def _dispatch_kernel(e1_ref, e2_ref, r1_ref, r2_ref, ps_ref, zf_ref, nz_ref, h_ref, x_ref, rd_ref, zbuf, sem, zsem):
    i = pl.program_id(0)
    ts = TOK_SUBLANES
    td = h_ref.shape[0] // ts
    n_tok = e1_ref.shape[0]
    base = i * td
    blk_rows = MOE_BLOCK * ts

    def zero_fill(blk):
        return pltpu.make_async_copy(
            zbuf, x_ref.at[pl.ds(pl.multiple_of(blk * blk_rows, blk_rows), blk_rows), :], zsem)

    @pl.when(i == 0)
    def _():
        zbuf[...] = jnp.zeros_like(zbuf)

        def fill(blk, carry):
            @pl.when(zf_ref[blk] == 1)
            def _():
                zero_fill(blk).start()
            return carry

        lax.fori_loop(0, zf_ref.shape[0], fill, 0)

        def init(blk, carry):
            for j in range(MOE_BLOCK):
                rd_ref[blk * MOE_BLOCK + j] = 2 * n_tok + j
            return carry

        lax.fori_loop(0, rd_ref.shape[0] // MOE_BLOCK, init, 0)

        def drain(k, carry):
            zero_fill(0).wait()
            return carry

        lax.fori_loop(0, nz_ref[0], drain, 0)

    def start(t, carry):
        src = h_ref.at[pl.ds(pl.multiple_of(t * ts, ts), ts), :]
        tok = base + t
        for slot, (e_ref, r_ref) in enumerate(((e1_ref, r1_ref), (e2_ref, r2_ref))):
            row = ps_ref[e_ref[tok]] + r_ref[tok]
            pltpu.make_async_copy(src, x_ref.at[pl.ds(pl.multiple_of(row * ts, ts), ts), :], sem).start()
            rd_ref[row] = 2 * tok + slot
        return carry

    lax.fori_loop(0, td, start, 0, unroll=8)
    for _ in range(2):
        pltpu.make_async_copy(h_ref, x_ref.at[pl.ds(0, td * ts), :], sem).wait()


def _dispatch(ri, pad_start, zero_blk, h2p, n_rows):
    t = ri.shape[1]
    td = min(512, t)
    ts = TOK_SUBLANES
    n_zero = jnp.sum(zero_blk, keepdims=True)
    return pl.pallas_call(
        _dispatch_kernel,
        grid_spec=pltpu.PrefetchScalarGridSpec(
            num_scalar_prefetch=7,
            grid=(t // td,),
            in_specs=[pl.BlockSpec((td * ts, LANES), lambda i, *_: (i, 0))],
            out_specs=[pl.BlockSpec(memory_space=pl.ANY), pl.BlockSpec(memory_space=pltpu.SMEM)],
            scratch_shapes=[pltpu.VMEM((MOE_BLOCK * ts, LANES), jnp.uint32),
                            pltpu.SemaphoreType.DMA(()), pltpu.SemaphoreType.DMA(())]),
        out_shape=[jax.ShapeDtypeStruct((n_rows * ts, LANES), jnp.uint32),
                   jax.ShapeDtypeStruct((n_rows,), jnp.int32)],
        compiler_params=_cparams(("arbitrary",)),
        name="dispatch",
    )(ri[0], ri[1], ri[2], ri[3], pad_start, zero_blk, n_zero, h2p)


def _moe_plan(counts, t):
    tm = MOE_BLOCK
    n_blocks = 2 * t // tm + N_EXPERTS
    padded = ((counts + tm - 1) // tm) * tm
    pad_end = jnp.cumsum(padded)
    pad_start = pad_end - padded
    blk = jnp.arange(n_blocks, dtype=jnp.int32)
    blk_e = jnp.minimum(jnp.sum(pad_end[None, :] <= (blk * tm)[:, None], axis=1), N_EXPERTS - 1).astype(jnp.int32)
    new_e = jnp.concatenate([jnp.ones((1,), jnp.int32), (blk_e[1:] != blk_e[:-1]).astype(jnp.int32)])
    nb_real = (pad_end[-1:] // tm).astype(jnp.int32)
    last_partial = jnp.any((blk[:, None] == (pad_end // tm - 1)[None, :]) & (counts % tm != 0)[None, :], axis=1)
    zero_blk = jnp.logical_or(last_partial, blk >= nb_real[0]).astype(jnp.int32)
    return pad_start, blk_e, new_e, nb_real, zero_blk, n_blocks * tm


def _combine_kernel(x1_ref, g2_ref, w1_ref, w2_ref, y_ref, o_ref):
    tc = x1_ref.shape[0]
    ts = TOK_SUBLANES
    w1 = w1_ref[...]
    w2 = w2_ref[...]
    for s in range(ts):
        a_lo, a_hi = _unpack_pair(y_ref[pl.ds(s, tc, stride=2 * ts), :])
        b_lo, b_hi = _unpack_pair(y_ref[pl.ds(ts + s, tc, stride=2 * ts), :])
        for c, ya, yb in ((s, a_lo, b_lo), (s + ts, a_hi, b_hi)):
            cols = slice(c * LANES, (c + 1) * LANES)
            o_ref[:, cols] = x1_ref[:, cols] + g2_ref[0][:, cols] * (w1 * ya + w2 * yb)


def _combine(x1, g2, w1, w2, y2, seq):
    t, d = x1.shape
    tc = min(512, seq)
    return pl.pallas_call(
        _combine_kernel,
        grid=(t // tc,),
        in_specs=[pl.BlockSpec((tc, d), lambda i: (i, 0)),
                  pl.BlockSpec((1, 1, d), lambda i: (i * tc // seq, 0, 0)),
                  pl.BlockSpec((tc, 1), lambda i: (i, 0)),
                  pl.BlockSpec((tc, 1), lambda i: (i, 0)),
                  pl.BlockSpec((tc * 2 * TOK_SUBLANES, LANES), lambda i: (i, 0))],
        out_specs=pl.BlockSpec((tc, d), lambda i: (i, 0)),
        out_shape=jax.ShapeDtypeStruct((t, d), F32),
        compiler_params=_cparams(("arbitrary",)),
        name="combine",
    )(x1, g2, w1, w2, y2)


def _q_up_layout(w_q_up):
    w = w_q_up.reshape(Q_LORA, MLA_HEADS, MLA_QK)
    w = jnp.pad(w, ((0, 0), (0, 0), (0, MLA_QPAD - MLA_QK)))
    return w.reshape(Q_LORA, MLA_HEADS * MLA_QPAD).astype(BF16)


def _pad_lanes(g, width):
    return jnp.pad(g, (0, width - g.shape[0])).reshape(1, width)


def kernel(x, c, positions, w_ada, b_ada, norm1_g, w_in, hgrn_lb_logits, hgrn_onorm_g, q_a_norm_g, w_q_up,
           kv_a_norm_g, w_kv_up, q_norm_g, k_norm_g, attn_onorm_g, w_out, norm2_g, w_group, b_group,
           w_router, b_router, w_gate, w_up, w_down):
    bsz, seq, d = x.shape
    t = bsz * seq
    depth = w_ada.shape[0]
    half = MLA_ROPE // 2
    inv_freq = ROPE_BASE ** (-jnp.arange(0, MLA_ROPE, 2, dtype=F32) / MLA_ROPE)
    freq = jnp.stack([jnp.tile(inv_freq, LANES // half),
                      jnp.where(jnp.arange(LANES) < MLA_ROPE, 0.0, jnp.pi / 2).astype(F32)])
    pos3 = positions.reshape(bsz, seq, 1)
    tr = min(512, t)
    tri = jnp.triu(jnp.ones((tr, tr), BF16), 1)

    x2 = x.reshape(t, d)
    for l in range(depth):
        mod = _ada(c, w_ada[l], b_ada[l]).reshape(bsz, 6, 1, d)
        sh1, sc1, g1, sh2, sc2, g2 = (mod[:, i] for i in range(6))

        proj = _in_proj(_norm_mod(x2, norm1_g[l].reshape(1, d), sh1, sc1, seq), w_in[l], seq)
        proj3 = proj.reshape(bsz, seq, IN_COLS_PAD)

        o_a = _hgrn(proj3, hgrn_lb_logits, hgrn_onorm_g[l].reshape(1, HGRN_DK), l)

        q, k, v = _mla_up(proj3, pos3, _q_up_layout(w_q_up[l]), w_kv_up[l].astype(BF16),
                          q_a_norm_g[l].reshape(1, Q_LORA), kv_a_norm_g[l].reshape(1, KV_LORA),
                          _pad_lanes(q_norm_g[l], MLA_QPAD), _pad_lanes(k_norm_g[l], MLA_QPAD), freq)
        o_b = _attention(q, k, v, attn_onorm_g[l].reshape(MLA_V, 1))

        w_o = w_out[l].astype(BF16)
        wr = jnp.pad(jnp.concatenate([w_router[l], w_group[l]], axis=1),
                     ((0, 0), (0, LANES - N_EXPERTS - N_GROUPS))).astype(BF16)
        br = _pad_lanes(jnp.concatenate([b_router[l], b_group[l]]), LANES)
        x1, h2, lg_t = _out_proj(o_a.reshape(t, HGRN_WIDTH), o_b.reshape(t, HGRN_WIDTH), x2,
                                 w_o[:HGRN_WIDTH], w_o[HGRN_WIDTH:], g1, norm2_g[l].reshape(1, d),
                                 sh2, sc2, wr, br, seq)

        ri, rw, cnt = _route(lg_t, tri)
        counts = cnt[:, 0]
        pad_start, blk_e, new_e, nb_real, zero_blk, n_rows = _moe_plan(counts, t)
        x_sorted, row_dst = _dispatch(ri, pad_start, zero_blk, h2, n_rows)
        y2 = _moe(row_dst, blk_e, new_e, nb_real, x_sorted, w_gate[l], w_up[l], w_down[l], 2 * t + MOE_BLOCK)
        x2 = _combine(x1, g2, rw[0].reshape(t, 1), rw[1].reshape(t, 1), y2, seq)
    return x2.reshape(bsz, seq, d)
```

```python
import functools

import jax
import jax.numpy as jnp
from jax import lax
from jax.experimental import pallas as pl
from jax.experimental.pallas import tpu as pltpu

F32 = jnp.float32
BF16 = jnp.bfloat16
EPS = 1e-6
LOG2E = 1.4426950408889634

D_MODEL = 2048
HGRN_WIDTH = 1024
HGRN_DK = 128
HGRN_HEADS = 8
HGRN_CHUNK = 64
MLA_HEADS = 8
MLA_NOPE = 128
MLA_ROPE = 64
MLA_QK = MLA_NOPE + MLA_ROPE
MLA_V = 128
MLA_QPAD = 256
Q_LORA = 512
KV_LORA = 256
ROPE_BASE = 10000.0
IN_COLS = 4 * HGRN_WIDTH + Q_LORA + KV_LORA + MLA_ROPE
IN_COLS_PAD = 5120
N_GROUPS = 4
EXPERTS_PER_GROUP = 8
N_EXPERTS = 32
D_EXPERT = 512
ROUTE_ROWS = 40
MOE_BLOCK = 256
LANES = 128
TOK_SUBLANES = 8
VMEM_LIMIT = 56 * 1024 * 1024


def _cparams(sem):
    return pltpu.CompilerParams(dimension_semantics=sem, vmem_limit_bytes=VMEM_LIMIT)


def _dot(a, b):
    return jnp.dot(a, b, preferred_element_type=F32)


def _dot_nt(a, b):
    return lax.dot_general(a, b, (((1,), (1,)), ((), ())), preferred_element_type=F32)


def _rms(x, g):
    return x * lax.rsqrt(jnp.mean(x * x, axis=-1, keepdims=True) + EPS) * g


def _silu(x):
    return x * jax.nn.sigmoid(x)


def _pack_pair(lo, hi):
    lo_b = lax.bitcast_convert_type(lo.astype(BF16).astype(F32), jnp.uint32)
    hi_b = lax.bitcast_convert_type(hi.astype(BF16).astype(F32), jnp.uint32)
    return hi_b | (lo_b >> 16)


def _unpack_pair(w):
    lo = lax.bitcast_convert_type(w << 16, F32)
    hi = lax.bitcast_convert_type(w & jnp.uint32(0xFFFF0000), F32)
    return lo, hi


def _ada_kernel(c_ref, w_ref, b_ref, o_ref):
    ca = _silu(c_ref[...]).astype(BF16)
    o_ref[...] = _dot(ca, w_ref[...].astype(BF16)) + b_ref[...]


def _ada(c, w, b):
    bsz, d = c.shape
    n = w.shape[1]
    tn = 1024
    return pl.pallas_call(
        _ada_kernel,
        grid=(n // tn,),
        in_specs=[pl.BlockSpec((bsz, d), lambda j: (0, 0)),
                  pl.BlockSpec((d, tn), lambda j: (0, j)),
                  pl.BlockSpec((1, tn), lambda j: (0, j))],
        out_specs=pl.BlockSpec((bsz, tn), lambda j: (0, j)),
        out_shape=jax.ShapeDtypeStruct((bsz, n), F32),
        compiler_params=_cparams(("arbitrary",)),
        name="ada",
    )(c, w, b.reshape(1, n))


NORM_ROWS = 32


def _norm_kernel(x_ref, g_ref, sh_ref, sc_ref, h_ref):
    def body(c, carry):
        r = pl.ds(pl.multiple_of(c * NORM_ROWS, NORM_ROWS), NORM_ROWS)
        h = _rms(x_ref[r, :], g_ref[...]) * (1.0 + sc_ref[0]) + sh_ref[0]
        h_ref[r, :] = h.astype(BF16)
        return carry
    lax.fori_loop(0, x_ref.shape[0] // NORM_ROWS, body, 0, unroll=2)


def _norm_mod(x2, g, sh, sc, seq):
    t, d = x2.shape
    tm = min(512, seq)
    return pl.pallas_call(
        _norm_kernel,
        grid=(t // tm,),
        in_specs=[pl.BlockSpec((tm, d), lambda i: (i, 0)),
                  pl.BlockSpec((1, d), lambda i: (0, 0)),
                  pl.BlockSpec((1, 1, d), lambda i: (i * tm // seq, 0, 0)),
                  pl.BlockSpec((1, 1, d), lambda i: (i * tm // seq, 0, 0))],
        out_specs=pl.BlockSpec((tm, d), lambda i: (i, 0)),
        out_shape=jax.ShapeDtypeStruct((t, d), BF16),
        compiler_params=_cparams(("arbitrary",)),
        name="norm_mod",
    )(x2, g, sh, sc)


def _in_kernel(h_ref, w_ref, wt_ref, o_ref, wb_s, *, n_main):
    j = pl.program_id(0)
    first_row_tile = pl.program_id(1) == 0

    @pl.when(jnp.logical_and(first_row_tile, j < n_main))
    def _():
        wb_s[...] = w_ref[...].astype(BF16)

    @pl.when(jnp.logical_and(first_row_tile, j == n_main))
    def _():
        wb_s[...] = wt_ref[...]

    o_ref[...] = _dot_nt(h_ref[...], wb_s[...]).astype(BF16)


def _in_proj(h, w_t, seq):
    t, d = h.shape
    tn = 1024
    n_main = IN_COLS // tn
    w_tail = jnp.pad(w_t[n_main * tn:], ((0, (n_main + 1) * tn - IN_COLS), (0, 0))).astype(BF16)
    tm = min(1024, t)
    return pl.pallas_call(
        functools.partial(_in_kernel, n_main=n_main),
        grid=(IN_COLS_PAD // tn, t // tm),
        in_specs=[pl.BlockSpec((tm, d), lambda j, i: (i, 0)),
                  pl.BlockSpec((tn, d), lambda j, i: (jnp.minimum(j, n_main - 1), 0)),
                  pl.BlockSpec((tn, d), lambda j, i: (0, 0))],
        out_specs=pl.BlockSpec((tm, tn), lambda j, i: (i, j)),
        out_shape=jax.ShapeDtypeStruct((t, IN_COLS_PAD), BF16),
        scratch_shapes=[pltpu.VMEM((tn, d), BF16)],
        compiler_params=_cparams(("arbitrary", "arbitrary")),
        name="in_proj",
    )(h, w_t, w_tail)


HG_ROWS = 256


def _software_pipeline(stages, n_blocks):
    depth = len(stages)

    def step(i, static):
        for k in reversed(range(depth)):
            if static and not 0 <= i - k < n_blocks:
                continue
            stages[k](i - k)

    if n_blocks < depth:
        for i in range(n_blocks + depth - 1):
            step(i, True)
        return
    for i in range(depth - 1):
        step(i, True)

    def steady(i, carry):
        step(i, False)
        return carry

    lax.fori_loop(depth - 1, n_blocks, steady, 0)
    for i in range(n_blocks, n_blocks + depth - 1):
        step(i, True)


def _chunk_mask(n):
    row = lax.broadcasted_iota(jnp.int32, (n, n), 0)
    col = lax.broadcasted_iota(jnp.int32, (n, n), 1)
    return jnp.logical_and(row // HGRN_CHUNK == col // HGRN_CHUNK, col <= row)


def _hgrn_kernel(q_ref, f_ref, i_ref, g_ref, lbl_ref, og_ref, o_ref,
                 qb_s, u_s, sp_s, dec_s, oi_s, b_s, k_s, qin_s, kin_s, ku_s, a_s, *, layer):
    seq = q_ref.shape[1]
    cs = HGRN_CHUNK
    rb = min(HG_ROWS, seq)
    nc = rb // cs
    dk = HGRN_DK
    lg = lbl_ref[...]
    ex = jnp.exp(lg - jnp.max(lg, axis=0, keepdims=True))
    sm = ex / jnp.sum(ex, axis=0, keepdims=True)
    lb = jnp.sum(sm[0:layer + 1], axis=0, keepdims=True)
    mask = _chunk_mask(rb)
    tri = mask.astype(BF16)
    row_chunk = lax.broadcasted_iota(jnp.int32, (rb, dk), 0) // cs

    chunk_sel = [(row_chunk == c).astype(BF16) for c in range(nc)]

    def rows(blk):
        return pl.ds(pl.multiple_of(blk * rb, rb), rb)

    def stage1(blk):
        r = rows(blk)
        f = lb + (1.0 - lb) * jax.nn.sigmoid(f_ref[0, r, :].astype(F32))
        lf = jnp.log(f)
        k_s[r, :] = 1.0 - f
        hi = lf.astype(BF16)
        r1 = lf - hi.astype(F32)
        mid = r1.astype(BF16)
        lo = (r1 - mid.astype(F32)).astype(BF16)
        bhm = _dot(tri, jnp.concatenate([hi, mid], axis=1))
        b_s[r, :] = bhm[:, 0:dk] + bhm[:, dk:] + _dot(tri, lo)

    def stage2(blk):
        r = rows(blk)
        b3 = b_s[r, :].reshape(nc, cs, dk)
        bmid = b3[:, cs // 2 - 1:cs // 2, :]
        blast = b3[:, cs - 1:cs, :]
        q3 = (q_ref[0, r, :].astype(F32) * dk ** -0.5).reshape(nc, cs, dk)
        k3 = k_s[r, :].reshape(nc, cs, dk)
        qin_s[r, :] = (q3 * jnp.exp(b3 - bmid)).reshape(rb, dk).astype(BF16)
        kin_s[r, :] = (k3 * jnp.exp(bmid - b3)).reshape(rb, dk).astype(BF16)
        ku_s[r, :] = (k3 * jnp.exp(blast - b3)).reshape(rb, dk).astype(BF16)
        qb_s[r, :] = (q3 * jnp.exp(b3)).reshape(rb, dk).astype(BF16)
        dec_s[pl.ds(blk * nc, nc)] = jnp.exp(blast)

    def stage3(blk):
        r = rows(blk)
        a_s[r, :] = jnp.where(mask, _dot_nt(qin_s[r, :], kin_s[r, :]), 0.0).astype(BF16)
        vt = i_ref[0, r, :].astype(F32).T.astype(BF16)
        ku = ku_s[r, :]
        ut = _dot(vt, jnp.concatenate([ku * sel for sel in chunk_sel], axis=1))
        for c in range(nc):
            u_s[blk * nc + c] = ut[:, c * dk:(c + 1) * dk]

    def stage4(blk):
        r = rows(blk)
        oi_s[r, :] = _dot(a_s[r, :], i_ref[0, r, :])

    _software_pipeline([stage1, stage2, stage3, stage4], seq // rb)

    def phase2(n, st):
        sp_s[n] = st.astype(BF16)
        return st * dec_s[n] + u_s[n]

    lax.fori_loop(0, seq // cs, phase2, jnp.zeros((dk, dk), F32), unroll=4)

    def phase3(blk, carry):
        r = pl.ds(pl.multiple_of(blk * rb, rb), rb)
        inter = [_dot_nt(qb_s[pl.ds(pl.multiple_of(blk * rb + c * cs, cs), cs), :], sp_s[blk * nc + c])
                 for c in range(nc)]
        o = oi_s[r, :] + jnp.concatenate(inter, axis=0)
        o = _rms(o, og_ref[...]) * _silu(g_ref[0, r, :].astype(F32))
        o_ref[0, r, :] = o.astype(BF16)
        return carry

    lax.fori_loop(0, seq // rb, phase3, 0, unroll=2)


def _hgrn(proj3, lb_logits, onorm_g, layer):
    bsz, seq, _ = proj3.shape
    nh = HGRN_HEADS

    def col(off):
        return pl.BlockSpec((1, seq, HGRN_DK), lambda b, h: (b, 0, off * nh + h))

    nl = lb_logits.shape[0]
    return pl.pallas_call(
        functools.partial(_hgrn_kernel, layer=layer),
        grid=(bsz, nh),
        in_specs=[col(0), col(1), col(2), col(3),
                  pl.BlockSpec((nl, HGRN_DK), lambda b, h: (0, h)),
                  pl.BlockSpec((1, HGRN_DK), lambda b, h: (0, 0))],
        out_specs=pl.BlockSpec((1, seq, HGRN_DK), lambda b, h: (b, 0, h)),
        out_shape=jax.ShapeDtypeStruct((bsz, seq, HGRN_WIDTH), BF16),
        scratch_shapes=[pltpu.VMEM((seq, HGRN_DK), BF16),
                        pltpu.VMEM((seq // HGRN_CHUNK, HGRN_DK, HGRN_DK), F32),
                        pltpu.VMEM((seq // HGRN_CHUNK, HGRN_DK, HGRN_DK), BF16),
                        pltpu.VMEM((seq // HGRN_CHUNK, 1, HGRN_DK), F32),
                        pltpu.VMEM((seq, HGRN_DK), F32),
                        pltpu.VMEM((seq, HGRN_DK), F32), pltpu.VMEM((seq, HGRN_DK), F32),
                        pltpu.VMEM((seq, HGRN_DK), BF16), pltpu.VMEM((seq, HGRN_DK), BF16),
                        pltpu.VMEM((seq, HGRN_DK), BF16), pltpu.VMEM((seq, min(HG_ROWS, seq)), BF16)],
        compiler_params=_cparams(("arbitrary", "arbitrary")),
        name="hgrn",
    )(proj3, proj3, proj3, proj3, lb_logits, onorm_g)


def _rope(x, cos, sin_signed, lane):
    half = MLA_ROPE // 2
    swapped = jnp.where(lane < half, pltpu.roll(x, LANES - half, 1), pltpu.roll(x, half, 1))
    return x * cos + swapped * sin_signed


def _up_kernel(p_ref, pos_ref, wq_ref, wkv_ref, qag_ref, kvag_ref, qg_ref, kg_ref, freq_ref,
               q_ref, k_ref, v_ref):
    p = p_ref[...].astype(F32)
    q_a = p[:, 0:Q_LORA]
    kv_a = p[:, Q_LORA:Q_LORA + KV_LORA]
    k_pe = p[:, Q_LORA + KV_LORA:Q_LORA + KV_LORA + LANES]
    qf = _dot(_rms(q_a, qag_ref[...]).astype(BF16), wq_ref[...])
    kvf = _dot(_rms(kv_a, kvag_ref[...]).astype(BF16), wkv_ref[...])

    tm = p.shape[0]
    lane = lax.broadcasted_iota(jnp.int32, (tm, LANES), 1)
    half = MLA_ROPE // 2
    sin_all = jnp.sin(pos_ref[...].astype(F32) * freq_ref[0:1, :] + freq_ref[1:2, :])
    valid = lane < MLA_ROPE
    cos = jnp.where(valid, pltpu.roll(sin_all, MLA_ROPE, 1), 0.0)
    sin_signed = jnp.where(valid, jnp.where(lane < half, -sin_all, sin_all), 0.0)

    qg = qg_ref[...]
    kg = kg_ref[...]
    kpe_ss = jnp.sum(k_pe * k_pe, axis=-1, keepdims=True)
    kpe_rot = _rope(k_pe * kg[:, MLA_NOPE:], cos, sin_signed, lane)
    scale = MLA_QK ** -0.5 * LOG2E
    for h in range(MLA_HEADS):
        qh = qf[:, h * MLA_QPAD:(h + 1) * MLA_QPAD]
        rq = lax.rsqrt(jnp.sum(qh * qh, axis=-1, keepdims=True) / MLA_QK + EPS)
        qn = qh * rq * qg
        q_ref[0, h, :, 0:MLA_NOPE] = (qn[:, 0:MLA_NOPE] * scale).astype(BF16)
        q_ref[0, h, :, MLA_NOPE:] = (_rope(qn[:, MLA_NOPE:], cos, sin_signed, lane) * scale).astype(BF16)
        kn = kvf[:, h * MLA_QPAD:h * MLA_QPAD + MLA_NOPE]
        rk = lax.rsqrt((jnp.sum(kn * kn, axis=-1, keepdims=True) + kpe_ss) / MLA_QK + EPS)
        k_ref[0, h, :, 0:MLA_NOPE] = (kn * rk * kg[:, 0:MLA_NOPE]).astype(BF16)
        k_ref[0, h, :, MLA_NOPE:] = (kpe_rot * rk).astype(BF16)
        v_ref[0, h, :, :] = kvf[:, h * MLA_QPAD + MLA_NOPE:(h + 1) * MLA_QPAD].T.astype(BF16)


def _mla_up(proj3, pos3, wq, wkv, qag, kvag, qg, kg, freq):
    bsz, seq, _ = proj3.shape
    tm = min(512, seq)
    nh = MLA_HEADS
    mla_block = 4 * HGRN_WIDTH // 1024

    def const(shape):
        return pl.BlockSpec(shape, lambda b, i: (0,) * len(shape))

    return pl.pallas_call(
        _up_kernel,
        grid=(bsz, seq // tm),
        in_specs=[pl.BlockSpec((None, tm, 1024), lambda b, i: (b, i, mla_block)),
                  pl.BlockSpec((None, tm, 1), lambda b, i: (b, i, 0)),
                  const(wq.shape), const(wkv.shape), const(qag.shape), const(kvag.shape),
                  const(qg.shape), const(kg.shape), const(freq.shape)],
        out_specs=[pl.BlockSpec((1, nh, tm, MLA_QPAD), lambda b, i: (b, 0, i, 0)),
                   pl.BlockSpec((1, nh, tm, MLA_QPAD), lambda b, i: (b, 0, i, 0)),
                   pl.BlockSpec((1, nh, MLA_V, tm), lambda b, i: (b, 0, 0, i))],
        out_shape=[jax.ShapeDtypeStruct((bsz, nh, seq, MLA_QPAD), BF16),
                   jax.ShapeDtypeStruct((bsz, nh, seq, MLA_QPAD), BF16),
                   jax.ShapeDtypeStruct((bsz, nh, MLA_V, seq), BF16)],
        compiler_params=_cparams(("arbitrary", "arbitrary")),
        name="mla_up",
    )(proj3, pos3, wq, wkv, qag, kvag, qg, kg, freq)


ATT_T = 256


def _attn_kernel(q_ref, k_ref, vt_ref, g_ref, o_ref):
    seq = q_ref.shape[2]
    t = min(ATT_T, seq)
    key = lax.broadcasted_iota(jnp.int32, (t, t), 0)
    qry = lax.broadcasted_iota(jnp.int32, (t, t), 1)
    causal = key <= qry
    neg = jnp.finfo(F32).min
    for qi in range(seq // t):
        off = qi * t
        q = q_ref[0, 0, off:off + t, :]
        sd = jnp.where(causal, _dot_nt(k_ref[0, 0, off:off + t, :], q), neg)
        m = jnp.max(sd, axis=0, keepdims=True)
        if qi > 0:
            so = _dot_nt(k_ref[0, 0, 0:off, :], q)
            m = jnp.maximum(m, jnp.max(so, axis=0, keepdims=True))
        pd = jnp.exp2(sd - m)
        l = jnp.sum(pd, axis=0, keepdims=True)
        ot = _dot(vt_ref[0, 0, :, off:off + t], pd.astype(BF16))
        if qi > 0:
            po = jnp.exp2(so - m)
            l = l + jnp.sum(po, axis=0, keepdims=True)
            ot = ot + _dot(vt_ref[0, 0, :, 0:off], po.astype(BF16))
        ot = ot * (1.0 / l)
        ot = ot * lax.rsqrt(jnp.mean(ot * ot, axis=0, keepdims=True) + EPS) * g_ref[...]
        o_ref[0, off:off + t, :] = ot.T.astype(BF16)


def _attention(q, k, v, g):
    bsz, nh, seq, _ = q.shape
    return pl.pallas_call(
        _attn_kernel,
        grid=(bsz, nh),
        in_specs=[pl.BlockSpec((1, 1, seq, MLA_QPAD), lambda b, h: (b, h, 0, 0)),
                  pl.BlockSpec((1, 1, seq, MLA_QPAD), lambda b, h: (b, h, 0, 0)),
                  pl.BlockSpec((1, 1, MLA_V, seq), lambda b, h: (b, h, 0, 0)),
                  pl.BlockSpec((MLA_V, 1), lambda b, h: (0, 0))],
        out_specs=pl.BlockSpec((1, seq, MLA_V), lambda b, h: (b, 0, h)),
        out_shape=jax.ShapeDtypeStruct((bsz, seq, nh * MLA_V), BF16),
        compiler_params=_cparams(("arbitrary", "arbitrary")),
        name="attn",
    )(q, k, v, g)


def _out_kernel(oa_ref, ob_ref, x_ref, wa_ref, wb_ref, g1_ref, n2g_ref, sh2_ref, sc2_ref,
                wr_ref, br_ref, x1_ref, h2_ref, lg_ref):
    mix = _dot(oa_ref[...], wa_ref[...]) + _dot(ob_ref[...], wb_ref[...])
    x1 = x_ref[...] + g1_ref[0] * mix
    x1_ref[...] = x1
    h2 = _rms(x1, n2g_ref[...]) * (1.0 + sc2_ref[0]) + sh2_ref[0]
    tm = h2.shape[0]
    for s in range(TOK_SUBLANES):
        h2_ref[pl.ds(s, tm, stride=TOK_SUBLANES), :] = _pack_pair(
            h2[:, s * LANES:(s + 1) * LANES], h2[:, (s + TOK_SUBLANES) * LANES:(s + TOK_SUBLANES + 1) * LANES])
    lg = _dot(h2.astype(BF16), wr_ref[...]) + br_ref[...]
    lg_ref[...] = lg.T[0:ROUTE_ROWS, :]


def _out_proj(oa, ob, x2, wa, wb, g1, n2g, sh2, sc2, wr, br, seq):
    t, d = x2.shape
    tm = min(512, seq)

    def const(shape):
        return pl.BlockSpec(shape, lambda i: (0,) * len(shape))

    def per_batch():
        return pl.BlockSpec((1, 1, d), lambda i: (i * tm // seq, 0, 0))

    return pl.pallas_call(
        _out_kernel,
        grid=(t // tm,),
        in_specs=[pl.BlockSpec((tm, HGRN_WIDTH), lambda i: (i, 0)),
                  pl.BlockSpec((tm, HGRN_WIDTH), lambda i: (i, 0)),
                  pl.BlockSpec((tm, d), lambda i: (i, 0)),
                  const(wa.shape), const(wb.shape), per_batch(), const(n2g.shape),
                  per_batch(), per_batch(), const(wr.shape), const(br.shape)],
        out_specs=[pl.BlockSpec((tm, d), lambda i: (i, 0)),
                   pl.BlockSpec((tm * TOK_SUBLANES, LANES), lambda i: (i, 0)),
                   pl.BlockSpec((ROUTE_ROWS, tm), lambda i: (0, i))],
        out_shape=[jax.ShapeDtypeStruct((t, d), F32),
                   jax.ShapeDtypeStruct((t * TOK_SUBLANES, LANES), jnp.uint32),
                   jax.ShapeDtypeStruct((ROUTE_ROWS, t), F32)],
        compiler_params=_cparams(("arbitrary",)),
        name="out_proj",
    )(oa, ob, x2, wa, wb, g1, n2g, sh2, sc2, wr, br)


def _route_kernel(lg_ref, tri_ref, ri_ref, rw_ref, cnt_ref, carry_s):
    step = pl.program_id(0)

    @pl.when(step == 0)
    def _():
        carry_s[...] = jnp.zeros_like(carry_s)

    lg = lg_ref[...]
    tr = lg.shape[1]
    epg = EXPERTS_PER_GROUP
    gl = lg[N_EXPERTS:N_EXPERTS + N_GROUPS, :]
    row_g = lax.broadcasted_iota(jnp.int32, (N_GROUPS, tr), 0)
    gmax = jnp.max(gl, axis=0, keepdims=True)
    g_sel = jnp.min(jnp.where(gl == gmax, row_g, N_GROUPS), axis=0, keepdims=True)
    p_group = 1.0 / jnp.sum(jnp.exp(gl - gmax), axis=0, keepdims=True)

    e_in = lg[0:epg, :]
    for g in range(1, N_GROUPS):
        e_in = jnp.where(g_sel == g, lg[g * epg:(g + 1) * epg, :], e_in)
    row_e = lax.broadcasted_iota(jnp.int32, (epg, tr), 0)
    top1 = jnp.max(e_in, axis=0, keepdims=True)
    i1 = jnp.min(jnp.where(e_in == top1, row_e, epg), axis=0, keepdims=True)
    rest = jnp.where(row_e == i1, -jnp.inf, e_in)
    top2 = jnp.max(rest, axis=0, keepdims=True)
    i2 = jnp.min(jnp.where(rest == top2, row_e, epg), axis=0, keepdims=True)
    e2w = jnp.exp(top2 - top1)
    w1 = p_group / (1.0 + e2w)
    w2 = p_group * e2w / (1.0 + e2w)
    ex1 = g_sel * epg + i1
    ex2 = g_sel * epg + i2

    row_x = lax.broadcasted_iota(jnp.int32, (N_EXPERTS, tr), 0)
    oh1 = row_x == ex1
    oh2 = row_x == ex2
    oh = jnp.logical_or(oh1, oh2)
    before = _dot(oh.astype(BF16), tri_ref[...]) + carry_s[:, 0:1]
    rank1 = jnp.sum(jnp.where(oh1, before, 0.0), axis=0, keepdims=True)
    rank2 = jnp.sum(jnp.where(oh2, before, 0.0), axis=0, keepdims=True)
    carry_s[...] = carry_s[...] + jnp.sum(oh.astype(F32), axis=1, keepdims=True)

    zi = jnp.zeros((4, tr), jnp.int32)
    ri_ref[...] = jnp.concatenate([ex1, ex2, rank1.astype(jnp.int32), rank2.astype(jnp.int32), zi], axis=0)
    rw_ref[...] = jnp.concatenate([w1, w2, jnp.zeros((6, tr), F32)], axis=0)
    cnt_ref[...] = carry_s[...].astype(jnp.int32)


def _route(lg_t, tri):
    t = lg_t.shape[1]
    tr = tri.shape[0]
    return pl.pallas_call(
        _route_kernel,
        grid=(t // tr,),
        in_specs=[pl.BlockSpec((ROUTE_ROWS, tr), lambda i: (0, i)),
                  pl.BlockSpec((tr, tr), lambda i: (0, 0))],
        out_specs=[pl.BlockSpec((8, tr), lambda i: (0, i)),
                   pl.BlockSpec((8, tr), lambda i: (0, i)),
                   pl.BlockSpec((N_EXPERTS, LANES), lambda i: (0, 0))],
        out_shape=[jax.ShapeDtypeStruct((8, t), jnp.int32),
                   jax.ShapeDtypeStruct((8, t), F32),
                   jax.ShapeDtypeStruct((N_EXPERTS, LANES), jnp.int32)],
        scratch_shapes=[pltpu.VMEM((N_EXPERTS, LANES), F32)],
        compiler_params=_cparams(("arbitrary",)),
        name="route",
    )(lg_t, tri)


def _moe_kernel(rd_ref, be_ref, ne_ref, nx_ref, nb_ref, x_ref, wg_ref, wu_ref, wd_ref, y_ref,
                ys, wg_f, wu_f, wd_f, wg_s, wu_s, wd_s, ssem, wsem):
    b = pl.program_id(0)
    last = pl.num_programs(0) - 1
    nb = nb_ref[0]
    tm = MOE_BLOCK
    ts = TOK_SUBLANES
    weights = ((wg_ref, wg_f, wg_s), (wu_ref, wu_f, wu_s), (wd_ref, wd_f, wd_s))

    def fetch(e):
        return [pltpu.make_async_copy(src.at[e], stage, wsem.at[k]) for k, (src, stage, _) in enumerate(weights)]

    def scatter(blk):
        base = blk * tm
        for j in range(tm):
            dst = pl.multiple_of(rd_ref[base + j] * ts, ts)
            pltpu.make_async_copy(ys.at[pl.ds(j * ts, ts), :], y_ref.at[pl.ds(dst, ts), :], ssem).start()

    def wait_scatter():
        pltpu.make_async_copy(ys, y_ref.at[pl.ds(0, tm * ts), :], ssem).wait()

    @pl.when(b == 0)
    def _():
        for copy in fetch(be_ref[0]):
            copy.start()
        ys[...] = jnp.zeros_like(ys)
        spare = pltpu.make_async_copy(ys, y_ref.at[pl.ds(y_ref.shape[0] - tm * ts, tm * ts), :], ssem)
        spare.start()
        spare.wait()

    @pl.when(ne_ref[b] == 1)
    def _():
        for copy in fetch(0):
            copy.wait()
        for _, stage, dst in weights:
            dst[...] = stage[...].astype(BF16)

        @pl.when(nx_ref[b] >= 0)
        def _():
            for copy in fetch(nx_ref[b]):
                copy.start()

    @pl.when(b < nb)
    def _():
        scatter(jnp.maximum(b - 1, 0))
        parts = [_unpack_pair(x_ref[pl.ds(s, tm, stride=ts), :]) for s in range(ts)]
        x = jnp.concatenate([p[0] for p in parts] + [p[1] for p in parts], axis=1).astype(BF16)
        hid = _silu(_dot(x, wg_s[...])) * _dot(x, wu_s[...])
        y = _dot(hid.astype(BF16), wd_s[...])
        wait_scatter()
        for s in range(ts):
            ys[pl.ds(s, tm, stride=ts), :] = _pack_pair(y[:, s * LANES:(s + 1) * LANES],
                                                        y[:, (s + ts) * LANES:(s + ts + 1) * LANES])

        @pl.when(b == last)
        def _():
            scatter(b)
            wait_scatter()

    @pl.when(b == nb)
    def _():
        scatter(b - 1)
        wait_scatter()


def _moe(row_dst, blk_e, new_e, next_e, nb_real, x_sorted, wg, wu, wd, n_out_rows):
    n_blocks = blk_e.shape[0]
    d, de = wg.shape[1], wg.shape[2]
    tm = MOE_BLOCK
    hbm = pl.BlockSpec(memory_space=pl.ANY)
    return pl.pallas_call(
        _moe_kernel,
        grid_spec=pltpu.PrefetchScalarGridSpec(
            num_scalar_prefetch=5,
            grid=(n_blocks,),
            in_specs=[pl.BlockSpec((tm * TOK_SUBLANES, LANES),
                                   lambda b, rd, be, ne, nx, nb: (jnp.minimum(b, nb[0] - 1), 0)),
                      hbm, hbm, hbm],
            out_specs=hbm,
            scratch_shapes=[pltpu.VMEM((tm * TOK_SUBLANES, LANES), jnp.uint32),
                            pltpu.VMEM((d, de), F32), pltpu.VMEM((d, de), F32), pltpu.VMEM((de, d), F32),
                            pltpu.VMEM((d, de), BF16), pltpu.VMEM((d, de), BF16), pltpu.VMEM((de, d), BF16),
                            pltpu.SemaphoreType.DMA(()), pltpu.SemaphoreType.DMA((3,))]),
        out_shape=jax.ShapeDtypeStruct((n_out_rows * TOK_SUBLANES, LANES), jnp.uint32),
        compiler_params=_cparams(("arbitrary",)),
        name="moe",
    )(row_dst, blk_e, new_e, next_e, nb_real, x_sorted, wg, wu, wd)


def _dispatch_kernel(e1_ref, e2_ref, r1_ref, r2_ref, ps_ref, zf_ref, nz_ref, h_ref, x_ref, rd_ref, zbuf, sem, zsem):
    i = pl.program_id(0)
    ts = TOK_SUBLANES
    td = h_ref.shape[0] // ts
    n_tok = e1_ref.shape[0]
    base = i * td
    blk_rows = MOE_BLOCK * ts

    def zero_fill(blk):
        return pltpu.make_async_copy(
            zbuf, x_ref.at[pl.ds(pl.multiple_of(blk * blk_rows, blk_rows), blk_rows), :], zsem)

    @pl.when(i == 0)
    def _():
        zbuf[...] = jnp.zeros_like(zbuf)

        def fill(blk, carry):
            @pl.when(zf_ref[blk] == 1)
            def _():
                zero_fill(blk).start()
            return carry

        lax.fori_loop(0, zf_ref.shape[0], fill, 0)

        def init(blk, carry):
            for j in range(MOE_BLOCK):
                rd_ref[blk * MOE_BLOCK + j] = 2 * n_tok + j
            return carry

        lax.fori_loop(0, rd_ref.shape[0] // MOE_BLOCK, init, 0)

        def drain(k, carry):
            zero_fill(0).wait()
            return carry

        lax.fori_loop(0, nz_ref[0], drain, 0)

    def start(t, carry):
        src = h_ref.at[pl.ds(pl.multiple_of(t * ts, ts), ts), :]
        tok = base + t
        for slot, (e_ref, r_ref) in enumerate(((e1_ref, r1_ref), (e2_ref, r2_ref))):
            row = ps_ref[e_ref[tok]] + r_ref[tok]
            pltpu.make_async_copy(src, x_ref.at[pl.ds(pl.multiple_of(row * ts, ts), ts), :], sem).start()
            rd_ref[row] = 2 * tok + slot
        return carry

    lax.fori_loop(0, td, start, 0, unroll=8)
    for _ in range(2):
        pltpu.make_async_copy(h_ref, x_ref.at[pl.ds(0, td * ts), :], sem).wait()


def _dispatch(ri, pad_start, zero_blk, h2p, n_rows):
    t = ri.shape[1]
    td = min(512, t)
    ts = TOK_SUBLANES
    n_zero = jnp.sum(zero_blk, keepdims=True)
    return pl.pallas_call(
        _dispatch_kernel,
        grid_spec=pltpu.PrefetchScalarGridSpec(
            num_scalar_prefetch=7,
            grid=(t // td,),
            in_specs=[pl.BlockSpec((td * ts, LANES), lambda i, *_: (i, 0))],
            out_specs=[pl.BlockSpec(memory_space=pl.ANY), pl.BlockSpec(memory_space=pltpu.SMEM)],
            scratch_shapes=[pltpu.VMEM((MOE_BLOCK * ts, LANES), jnp.uint32),
                            pltpu.SemaphoreType.DMA(()), pltpu.SemaphoreType.DMA(())]),
        out_shape=[jax.ShapeDtypeStruct((n_rows * ts, LANES), jnp.uint32),
                   jax.ShapeDtypeStruct((n_rows,), jnp.int32)],
        compiler_params=_cparams(("arbitrary",)),
        name="dispatch",
    )(ri[0], ri[1], ri[2], ri[3], pad_start, zero_blk, n_zero, h2p)


def _moe_plan(counts, t):
    tm = MOE_BLOCK
    n_blocks = 2 * t // tm + N_EXPERTS
    padded = ((counts + tm - 1) // tm) * tm
    pad_end = jnp.cumsum(padded)
    pad_start = pad_end - padded
    blk = jnp.arange(n_blocks, dtype=jnp.int32)
    blk_e = jnp.minimum(jnp.sum(pad_end[None, :] <= (blk * tm)[:, None], axis=1), N_EXPERTS - 1).astype(jnp.int32)
    nb_real = (pad_end[-1:] // tm).astype(jnp.int32)
    changed = jnp.concatenate([jnp.ones((1,), bool), blk_e[1:] != blk_e[:-1]])
    new_e = jnp.logical_and(changed, blk < nb_real[0]).astype(jnp.int32)
    ex = jnp.arange(N_EXPERTS, dtype=jnp.int32)
    later = jnp.where((counts > 0)[None, :] & (ex[None, :] > ex[:, None]), ex[None, :], N_EXPERTS).min(axis=1)
    next_e = jnp.where(later < N_EXPERTS, later, -1).astype(jnp.int32)[blk_e]
    last_partial = jnp.any((blk[:, None] == (pad_end // tm - 1)[None, :]) & (counts % tm != 0)[None, :], axis=1)
    zero_blk = jnp.logical_or(last_partial, blk >= nb_real[0]).astype(jnp.int32)
    return pad_start, blk_e, new_e, next_e, nb_real, zero_blk, n_blocks * tm


def _combine_kernel(x1_ref, g2_ref, w1_ref, w2_ref, y_ref, o_ref):
    tc = x1_ref.shape[0]
    ts = TOK_SUBLANES
    w1 = w1_ref[...]
    w2 = w2_ref[...]
    for s in range(ts):
        a_lo, a_hi = _unpack_pair(y_ref[pl.ds(s, tc, stride=2 * ts), :])
        b_lo, b_hi = _unpack_pair(y_ref[pl.ds(ts + s, tc, stride=2 * ts), :])
        for c, ya, yb in ((s, a_lo, b_lo), (s + ts, a_hi, b_hi)):
            cols = slice(c * LANES, (c + 1) * LANES)
            o_ref[:, cols] = x1_ref[:, cols] + g2_ref[0][:, cols] * (w1 * ya + w2 * yb)


def _combine(x1, g2, w1, w2, y2, seq):
    t, d = x1.shape
    tc = min(512, seq)
    return pl.pallas_call(
        _combine_kernel,
        grid=(t // tc,),
        in_specs=[pl.BlockSpec((tc, d), lambda i: (i, 0)),
                  pl.BlockSpec((1, 1, d), lambda i: (i * tc // seq, 0, 0)),
                  pl.BlockSpec((tc, 1), lambda i: (i, 0)),
                  pl.BlockSpec((tc, 1), lambda i: (i, 0)),
                  pl.BlockSpec((tc * 2 * TOK_SUBLANES, LANES), lambda i: (i, 0))],
        out_specs=pl.BlockSpec((tc, d), lambda i: (i, 0)),
        out_shape=jax.ShapeDtypeStruct((t, d), F32),
        compiler_params=_cparams(("arbitrary",)),
        name="combine",
    )(x1, g2, w1, w2, y2)


def _q_up_layout(w_q_up):
    w = w_q_up.reshape(Q_LORA, MLA_HEADS, MLA_QK)
    w = jnp.pad(w, ((0, 0), (0, 0), (0, MLA_QPAD - MLA_QK)))
    return w.reshape(Q_LORA, MLA_HEADS * MLA_QPAD).astype(BF16)


def _pad_lanes(g, width):
    return jnp.pad(g, (0, width - g.shape[0])).reshape(1, width)


def kernel(x, c, positions, w_ada, b_ada, norm1_g, w_in, hgrn_lb_logits, hgrn_onorm_g, q_a_norm_g, w_q_up,
           kv_a_norm_g, w_kv_up, q_norm_g, k_norm_g, attn_onorm_g, w_out, norm2_g, w_group, b_group,
           w_router, b_router, w_gate, w_up, w_down):
    bsz, seq, d = x.shape
    t = bsz * seq
    depth = w_ada.shape[0]
    half = MLA_ROPE // 2
    inv_freq = ROPE_BASE ** (-jnp.arange(0, MLA_ROPE, 2, dtype=F32) / MLA_ROPE)
    freq = jnp.stack([jnp.tile(inv_freq, LANES // half),
                      jnp.where(jnp.arange(LANES) < MLA_ROPE, 0.0, jnp.pi / 2).astype(F32)])
    pos3 = positions.reshape(bsz, seq, 1)
    tr = min(512, t)
    tri = jnp.triu(jnp.ones((tr, tr), BF16), 1)

    x2 = x.reshape(t, d)
    for l in range(depth):
        mod = _ada(c, w_ada[l], b_ada[l]).reshape(bsz, 6, 1, d)
        sh1, sc1, g1, sh2, sc2, g2 = (mod[:, i] for i in range(6))

        proj = _in_proj(_norm_mod(x2, norm1_g[l].reshape(1, d), sh1, sc1, seq), w_in[l].T, seq)
        proj3 = proj.reshape(bsz, seq, IN_COLS_PAD)

        o_a = _hgrn(proj3, hgrn_lb_logits, hgrn_onorm_g[l].reshape(1, HGRN_DK), l)

        q, k, v = _mla_up(proj3, pos3, _q_up_layout(w_q_up[l]), w_kv_up[l].astype(BF16),
                          q_a_norm_g[l].reshape(1, Q_LORA), kv_a_norm_g[l].reshape(1, KV_LORA),
                          _pad_lanes(q_norm_g[l], MLA_QPAD), _pad_lanes(k_norm_g[l], MLA_QPAD), freq)
        o_b = _attention(q, k, v, attn_onorm_g[l].reshape(MLA_V, 1))

        w_o = w_out[l].astype(BF16)
        wr = jnp.pad(jnp.concatenate([w_router[l], w_group[l]], axis=1),
                     ((0, 0), (0, LANES - N_EXPERTS - N_GROUPS))).astype(BF16)
        br = _pad_lanes(jnp.concatenate([b_router[l], b_group[l]]), LANES)
        x1, h2, lg_t = _out_proj(o_a.reshape(t, HGRN_WIDTH), o_b.reshape(t, HGRN_WIDTH), x2,
                                 w_o[:HGRN_WIDTH], w_o[HGRN_WIDTH:], g1, norm2_g[l].reshape(1, d),
                                 sh2, sc2, wr, br, seq)

        ri, rw, cnt = _route(lg_t, tri)
        counts = cnt[:, 0]
        pad_start, blk_e, new_e, next_e, nb_real, zero_blk, n_rows = _moe_plan(counts, t)
        x_sorted, row_dst = _dispatch(ri, pad_start, zero_blk, h2, n_rows)
        y2 = _moe(row_dst, blk_e, new_e, next_e, nb_real, x_sorted, w_gate[l], w_up[l], w_down[l], 2 * t + MOE_BLOCK)
        x2 = _combine(x1, g2, rw[0].reshape(t, 1), rw[1].reshape(t, 1), y2, seq)
    return x2.reshape(bsz, seq, d)
```

```python
import functools

import jax
import jax.numpy as jnp
from jax import lax
from jax.experimental import pallas as pl
from jax.experimental.pallas import tpu as pltpu

F32 = jnp.float32
BF16 = jnp.bfloat16
EPS = 1e-6
LOG2E = 1.4426950408889634

D_MODEL = 2048
HGRN_WIDTH = 1024
HGRN_DK = 128
HGRN_HEADS = 8
HGRN_CHUNK = 64
MLA_HEADS = 8
MLA_NOPE = 128
MLA_ROPE = 64
MLA_QK = MLA_NOPE + MLA_ROPE
MLA_V = 128
MLA_QPAD = 256
Q_LORA = 512
KV_LORA = 256
ROPE_BASE = 10000.0
IN_COLS = 4 * HGRN_WIDTH + Q_LORA + KV_LORA + MLA_ROPE
IN_COLS_PAD = 5120
N_GROUPS = 4
EXPERTS_PER_GROUP = 8
N_EXPERTS = 32
D_EXPERT = 512
ROUTE_ROWS = 40
MOE_BLOCK = 256
LANES = 128
TOK_SUBLANES = 8
VMEM_LIMIT = 56 * 1024 * 1024


def _cparams(sem):
    return pltpu.CompilerParams(dimension_semantics=sem, vmem_limit_bytes=VMEM_LIMIT)


def _dot(a, b):
    return jnp.dot(a, b, preferred_element_type=F32)


def _dot_nt(a, b):
    return lax.dot_general(a, b, (((1,), (1,)), ((), ())), preferred_element_type=F32)


def _rms(x, g):
    return x * lax.rsqrt(jnp.mean(x * x, axis=-1, keepdims=True) + EPS) * g


def _silu(x):
    return x * jax.nn.sigmoid(x)


def _pack_pair(lo, hi):
    lo_b = lax.bitcast_convert_type(lo.astype(BF16).astype(F32), jnp.uint32)
    hi_b = lax.bitcast_convert_type(hi.astype(BF16).astype(F32), jnp.uint32)
    return hi_b | (lo_b >> 16)


def _unpack_pair(w):
    lo = lax.bitcast_convert_type(w << 16, F32)
    hi = lax.bitcast_convert_type(w & jnp.uint32(0xFFFF0000), F32)
    return lo, hi


def _ada_kernel(c_ref, w_ref, b_ref, o_ref):
    ca = _silu(c_ref[...]).astype(BF16)
    o_ref[...] = _dot(ca, w_ref[...].astype(BF16)) + b_ref[...]


def _ada(c, w, b):
    bsz, d = c.shape
    n = w.shape[1]
    tn = 1024
    return pl.pallas_call(
        _ada_kernel,
        grid=(n // tn,),
        in_specs=[pl.BlockSpec((bsz, d), lambda j: (0, 0)),
                  pl.BlockSpec((d, tn), lambda j: (0, j)),
                  pl.BlockSpec((1, tn), lambda j: (0, j))],
        out_specs=pl.BlockSpec((bsz, tn), lambda j: (0, j)),
        out_shape=jax.ShapeDtypeStruct((bsz, n), F32),
        compiler_params=_cparams(("arbitrary",)),
        name="ada",
    )(c, w, b.reshape(1, n))


NORM_ROWS = 32


def _norm_kernel(x_ref, g_ref, sh_ref, sc_ref, h_ref):
    def body(c, carry):
        r = pl.ds(pl.multiple_of(c * NORM_ROWS, NORM_ROWS), NORM_ROWS)
        h = _rms(x_ref[r, :], g_ref[...]) * (1.0 + sc_ref[0]) + sh_ref[0]
        h_ref[r, :] = h.astype(BF16)
        return carry
    lax.fori_loop(0, x_ref.shape[0] // NORM_ROWS, body, 0, unroll=2)


def _norm_mod(x2, g, sh, sc, seq):
    t, d = x2.shape
    tm = min(512, seq)
    return pl.pallas_call(
        _norm_kernel,
        grid=(t // tm,),
        in_specs=[pl.BlockSpec((tm, d), lambda i: (i, 0)),
                  pl.BlockSpec((1, d), lambda i: (0, 0)),
                  pl.BlockSpec((1, 1, d), lambda i: (i * tm // seq, 0, 0)),
                  pl.BlockSpec((1, 1, d), lambda i: (i * tm // seq, 0, 0))],
        out_specs=pl.BlockSpec((tm, d), lambda i: (i, 0)),
        out_shape=jax.ShapeDtypeStruct((t, d), BF16),
        compiler_params=_cparams(("arbitrary",)),
        name="norm_mod",
    )(x2, g, sh, sc)


def _in_kernel(h_ref, w_ref, wt_ref, o_ref, wb_s, *, n_main):
    j = pl.program_id(0)
    first_row_tile = pl.program_id(1) == 0

    @pl.when(jnp.logical_and(first_row_tile, j < n_main))
    def _():
        wb_s[...] = w_ref[...].astype(BF16)

    @pl.when(jnp.logical_and(first_row_tile, j == n_main))
    def _():
        wb_s[...] = wt_ref[...]

    o_ref[...] = _dot_nt(h_ref[...], wb_s[...]).astype(BF16)


def _in_proj(h, w_t, seq):
    t, d = h.shape
    tn = 1024
    n_main = IN_COLS // tn
    w_tail = jnp.pad(w_t[n_main * tn:], ((0, (n_main + 1) * tn - IN_COLS), (0, 0))).astype(BF16)
    tm = min(1024, t)
    return pl.pallas_call(
        functools.partial(_in_kernel, n_main=n_main),
        grid=(IN_COLS_PAD // tn, t // tm),
        in_specs=[pl.BlockSpec((tm, d), lambda j, i: (i, 0)),
                  pl.BlockSpec((tn, d), lambda j, i: (jnp.minimum(j, n_main - 1), 0)),
                  pl.BlockSpec((tn, d), lambda j, i: (0, 0))],
        out_specs=pl.BlockSpec((tm, tn), lambda j, i: (i, j)),
        out_shape=jax.ShapeDtypeStruct((t, IN_COLS_PAD), BF16),
        scratch_shapes=[pltpu.VMEM((tn, d), BF16)],
        compiler_params=_cparams(("arbitrary", "arbitrary")),
        name="in_proj",
    )(h, w_t, w_tail)


HG_ROWS = 256


def _software_pipeline(stages, n_blocks):
    depth = len(stages)

    def step(i, static):
        for k in reversed(range(depth)):
            if static and not 0 <= i - k < n_blocks:
                continue
            stages[k](i - k)

    if n_blocks < depth:
        for i in range(n_blocks + depth - 1):
            step(i, True)
        return
    for i in range(depth - 1):
        step(i, True)

    def steady(i, carry):
        step(i, False)
        return carry

    lax.fori_loop(depth - 1, n_blocks, steady, 0)
    for i in range(n_blocks, n_blocks + depth - 1):
        step(i, True)


def _chunk_mask(n):
    row = lax.broadcasted_iota(jnp.int32, (n, n), 0)
    col = lax.broadcasted_iota(jnp.int32, (n, n), 1)
    return jnp.logical_and(row // HGRN_CHUNK == col // HGRN_CHUNK, col <= row)


def _hgrn_kernel(q_ref, f_ref, i_ref, g_ref, lbl_ref, og_ref, o_ref,
                 qb_s, u_s, sp_s, dec_s, oi_s, b_s, k_s, qin_s, kin_s, ku_s, a_s, st_s, *, layer):
    seq = q_ref.shape[1]
    cs = HGRN_CHUNK
    rb = min(HG_ROWS, seq)
    nc = rb // cs
    dk = HGRN_DK
    lg = lbl_ref[...]
    ex = jnp.exp(lg - jnp.max(lg, axis=0, keepdims=True))
    sm = ex / jnp.sum(ex, axis=0, keepdims=True)
    lb = jnp.sum(sm[0:layer + 1], axis=0, keepdims=True)
    mask = _chunk_mask(rb)
    tri = mask.astype(BF16)
    row_chunk = lax.broadcasted_iota(jnp.int32, (rb, dk), 0) // cs

    chunk_sel = [(row_chunk == c).astype(BF16) for c in range(nc)]

    def rows(blk):
        return pl.ds(pl.multiple_of(blk * rb, rb), rb)

    def stage1(blk):
        r = rows(blk)
        f = lb + (1.0 - lb) * jax.nn.sigmoid(f_ref[0, r, :].astype(F32))
        lf = jnp.log(f)
        k_s[r, :] = 1.0 - f
        hi = lf.astype(BF16)
        r1 = lf - hi.astype(F32)
        mid = r1.astype(BF16)
        lo = (r1 - mid.astype(F32)).astype(BF16)
        bhm = _dot(tri, jnp.concatenate([hi, mid], axis=1))
        b_s[r, :] = bhm[:, 0:dk] + bhm[:, dk:] + _dot(tri, lo)

    def stage2(blk):
        r = rows(blk)
        b3 = b_s[r, :].reshape(nc, cs, dk)
        bmid = b3[:, cs // 2 - 1:cs // 2, :]
        blast = b3[:, cs - 1:cs, :]
        q3 = (q_ref[0, r, :].astype(F32) * dk ** -0.5).reshape(nc, cs, dk)
        k3 = k_s[r, :].reshape(nc, cs, dk)
        qin_s[r, :] = (q3 * jnp.exp(b3 - bmid)).reshape(rb, dk).astype(BF16)
        kin_s[r, :] = (k3 * jnp.exp(bmid - b3)).reshape(rb, dk).astype(BF16)
        ku_s[r, :] = (k3 * jnp.exp(blast - b3)).reshape(rb, dk).astype(BF16)
        qb_s[r, :] = (q3 * jnp.exp(b3)).reshape(rb, dk).astype(BF16)
        dec_s[pl.ds(blk * nc, nc)] = jnp.exp(blast)

    def stage3(blk):
        r = rows(blk)
        a_s[r, :] = jnp.where(mask, _dot_nt(qin_s[r, :], kin_s[r, :]), 0.0).astype(BF16)
        vt = i_ref[0, r, :].astype(F32).T.astype(BF16)
        ku = ku_s[r, :]
        ut = _dot(vt, jnp.concatenate([ku * sel for sel in chunk_sel], axis=1))
        for c in range(nc):
            u_s[blk * nc + c] = ut[:, c * dk:(c + 1) * dk]

    def stage4(blk):
        r = rows(blk)
        oi_s[r, :] = _dot(a_s[r, :], i_ref[0, r, :])
        st = st_s[...]
        for c in range(nc):
            sp_s[blk * nc + c] = st.astype(BF16)
            st = st * dec_s[blk * nc + c] + u_s[blk * nc + c]
        st_s[...] = st

    def stage5(blk):
        r = rows(blk)
        inter = [_dot_nt(qb_s[pl.ds(pl.multiple_of(blk * rb + c * cs, cs), cs), :], sp_s[blk * nc + c])
                 for c in range(nc)]
        o = oi_s[r, :] + jnp.concatenate(inter, axis=0)
        o = _rms(o, og_ref[...]) * _silu(g_ref[0, r, :].astype(F32))
        o_ref[0, r, :] = o.astype(BF16)

    st_s[...] = jnp.zeros_like(st_s)
    _software_pipeline([stage1, stage2, stage3, stage4, stage5], seq // rb)


def _hgrn(proj3, lb_logits, onorm_g, layer):
    bsz, seq, _ = proj3.shape
    nh = HGRN_HEADS

    def col(off):
        return pl.BlockSpec((1, seq, HGRN_DK), lambda b, h: (b, 0, off * nh + h))

    nl = lb_logits.shape[0]
    return pl.pallas_call(
        functools.partial(_hgrn_kernel, layer=layer),
        grid=(bsz, nh),
        in_specs=[col(0), col(1), col(2), col(3),
                  pl.BlockSpec((nl, HGRN_DK), lambda b, h: (0, h)),
                  pl.BlockSpec((1, HGRN_DK), lambda b, h: (0, 0))],
        out_specs=pl.BlockSpec((1, seq, HGRN_DK), lambda b, h: (b, 0, h)),
        out_shape=jax.ShapeDtypeStruct((bsz, seq, HGRN_WIDTH), BF16),
        scratch_shapes=[pltpu.VMEM((seq, HGRN_DK), BF16),
                        pltpu.VMEM((seq // HGRN_CHUNK, HGRN_DK, HGRN_DK), F32),
                        pltpu.VMEM((seq // HGRN_CHUNK, HGRN_DK, HGRN_DK), BF16),
                        pltpu.VMEM((seq // HGRN_CHUNK, 1, HGRN_DK), F32),
                        pltpu.VMEM((seq, HGRN_DK), F32),
                        pltpu.VMEM((seq, HGRN_DK), F32), pltpu.VMEM((seq, HGRN_DK), F32),
                        pltpu.VMEM((seq, HGRN_DK), BF16), pltpu.VMEM((seq, HGRN_DK), BF16),
                        pltpu.VMEM((seq, HGRN_DK), BF16), pltpu.VMEM((seq, min(HG_ROWS, seq)), BF16),
                        pltpu.VMEM((HGRN_DK, HGRN_DK), F32)],
        compiler_params=_cparams(("arbitrary", "arbitrary")),
        name="hgrn",
    )(proj3, proj3, proj3, proj3, lb_logits, onorm_g)


def _rope(x, cos, sin_signed, lane):
    half = MLA_ROPE // 2
    swapped = jnp.where(lane < half, pltpu.roll(x, LANES - half, 1), pltpu.roll(x, half, 1))
    return x * cos + swapped * sin_signed


UP_ROWS = 512


def _up_kernel(p_ref, pos_ref, wq_ref, wkv_ref, qag_ref, kvag_ref, qg_ref, kg_ref, freq_ref,
               q_ref, k_ref, v_ref):
    rows = min(UP_ROWS, p_ref.shape[0])
    lane = lax.broadcasted_iota(jnp.int32, (rows, LANES), 1)
    half = MLA_ROPE // 2
    valid = lane < MLA_ROPE
    qg = qg_ref[...] * (MLA_QK ** -0.5 * LOG2E)
    kg = kg_ref[...]

    def chunk(c, carry):
        r = pl.ds(pl.multiple_of(c * rows, rows), rows)
        p = p_ref[r, :].astype(F32)
        q_a = p[:, 0:Q_LORA]
        kv_a = p[:, Q_LORA:Q_LORA + KV_LORA]
        k_pe = p[:, Q_LORA + KV_LORA:Q_LORA + KV_LORA + LANES]
        qf = _dot(_rms(q_a, qag_ref[...]).astype(BF16), wq_ref[...])
        kvf = _dot(_rms(kv_a, kvag_ref[...]).astype(BF16), wkv_ref[...])
        sin_all = jnp.sin(pos_ref[r, :].astype(F32) * freq_ref[0:1, :] + freq_ref[1:2, :])
        cos = jnp.where(valid, pltpu.roll(sin_all, MLA_ROPE, 1), 0.0)
        sin_signed = jnp.where(valid, jnp.where(lane < half, -sin_all, sin_all), 0.0)
        kpe_ss = jnp.sum(k_pe * k_pe, axis=-1, keepdims=True)
        kpe_rot = _rope(k_pe * kg[:, MLA_NOPE:], cos, sin_signed, lane)
        for h in range(MLA_HEADS):
            qh = qf[:, h * MLA_QPAD:(h + 1) * MLA_QPAD]
            qn = qh * lax.rsqrt(jnp.sum(qh * qh, axis=-1, keepdims=True) / MLA_QK + EPS) * qg
            q_ref[0, h, r, 0:MLA_NOPE] = qn[:, 0:MLA_NOPE].astype(BF16)
            q_ref[0, h, r, MLA_NOPE:] = _rope(qn[:, MLA_NOPE:], cos, sin_signed, lane).astype(BF16)
            kn = kvf[:, h * MLA_QPAD:h * MLA_QPAD + MLA_NOPE]
            rk = lax.rsqrt((jnp.sum(kn * kn, axis=-1, keepdims=True) + kpe_ss) / MLA_QK + EPS)
            k_ref[0, h, r, 0:MLA_NOPE] = (kn * rk * kg[:, 0:MLA_NOPE]).astype(BF16)
            k_ref[0, h, r, MLA_NOPE:] = (kpe_rot * rk).astype(BF16)
            v_ref[0, h, :, r] = kvf[:, h * MLA_QPAD + MLA_NOPE:(h + 1) * MLA_QPAD].T.astype(BF16)
        return carry

    lax.fori_loop(0, p_ref.shape[0] // rows, chunk, 0)


def _mla_up(proj3, pos3, wq, wkv, qag, kvag, qg, kg, freq):
    bsz, seq, _ = proj3.shape
    tm = min(512, seq)
    nh = MLA_HEADS
    mla_block = 4 * HGRN_WIDTH // 1024

    def const(shape):
        return pl.BlockSpec(shape, lambda b, i: (0,) * len(shape))

    return pl.pallas_call(
        _up_kernel,
        grid=(bsz, seq // tm),
        in_specs=[pl.BlockSpec((None, tm, 1024), lambda b, i: (b, i, mla_block)),
                  pl.BlockSpec((None, tm, 1), lambda b, i: (b, i, 0)),
                  const(wq.shape), const(wkv.shape), const(qag.shape), const(kvag.shape),
                  const(qg.shape), const(kg.shape), const(freq.shape)],
        out_specs=[pl.BlockSpec((1, nh, tm, MLA_QPAD), lambda b, i: (b, 0, i, 0)),
                   pl.BlockSpec((1, nh, tm, MLA_QPAD), lambda b, i: (b, 0, i, 0)),
                   pl.BlockSpec((1, nh, MLA_V, tm), lambda b, i: (b, 0, 0, i))],
        out_shape=[jax.ShapeDtypeStruct((bsz, nh, seq, MLA_QPAD), BF16),
                   jax.ShapeDtypeStruct((bsz, nh, seq, MLA_QPAD), BF16),
                   jax.ShapeDtypeStruct((bsz, nh, MLA_V, seq), BF16)],
        compiler_params=_cparams(("arbitrary", "arbitrary")),
        name="mla_up",
    )(proj3, pos3, wq, wkv, qag, kvag, qg, kg, freq)


ATT_T = 256


def _attn_kernel(q_ref, k_ref, vt_ref, g_ref, o_ref):
    seq = q_ref.shape[2]
    t = min(ATT_T, seq)
    key = lax.broadcasted_iota(jnp.int32, (t, t), 0)
    qry = lax.broadcasted_iota(jnp.int32, (t, t), 1)
    causal = key <= qry
    neg = jnp.finfo(F32).min
    for qi in range(seq // t):
        off = qi * t
        q = q_ref[0, 0, off:off + t, :]
        sd = jnp.where(causal, _dot_nt(k_ref[0, 0, off:off + t, :], q), neg)
        m = jnp.max(sd, axis=0, keepdims=True)
        if qi > 0:
            so = _dot_nt(k_ref[0, 0, 0:off, :], q)
            m = jnp.maximum(m, jnp.max(so, axis=0, keepdims=True))
        pd = jnp.exp2(sd - m)
        l = jnp.sum(pd, axis=0, keepdims=True)
        ot = _dot(vt_ref[0, 0, :, off:off + t], pd.astype(BF16))
        if qi > 0:
            po = jnp.exp2(so - m)
            l = l + jnp.sum(po, axis=0, keepdims=True)
            ot = ot + _dot(vt_ref[0, 0, :, 0:off], po.astype(BF16))
        ot = ot * (1.0 / l)
        ot = ot * lax.rsqrt(jnp.mean(ot * ot, axis=0, keepdims=True) + EPS) * g_ref[...]
        o_ref[0, off:off + t, :] = ot.T.astype(BF16)


def _attention(q, k, v, g):
    bsz, nh, seq, _ = q.shape
    return pl.pallas_call(
        _attn_kernel,
        grid=(bsz, nh),
        in_specs=[pl.BlockSpec((1, 1, seq, MLA_QPAD), lambda b, h: (b, h, 0, 0)),
                  pl.BlockSpec((1, 1, seq, MLA_QPAD), lambda b, h: (b, h, 0, 0)),
                  pl.BlockSpec((1, 1, MLA_V, seq), lambda b, h: (b, h, 0, 0)),
                  pl.BlockSpec((MLA_V, 1), lambda b, h: (0, 0))],
        out_specs=pl.BlockSpec((1, seq, MLA_V), lambda b, h: (b, 0, h)),
        out_shape=jax.ShapeDtypeStruct((bsz, seq, nh * MLA_V), BF16),
        compiler_params=_cparams(("arbitrary", "arbitrary")),
        name="attn",
    )(q, k, v, g)


def _out_kernel(oa_ref, ob_ref, x_ref, wa_ref, wb_ref, g1_ref, n2g_ref, sh2_ref, sc2_ref,
                wr_ref, br_ref, x1_ref, h2_ref, lg_ref):
    mix = _dot(oa_ref[...], wa_ref[...]) + _dot(ob_ref[...], wb_ref[...])
    x1 = x_ref[...] + g1_ref[0] * mix
    x1_ref[...] = x1
    h2 = _rms(x1, n2g_ref[...]) * (1.0 + sc2_ref[0]) + sh2_ref[0]
    tm = h2.shape[0]
    for s in range(TOK_SUBLANES):
        h2_ref[pl.ds(s, tm, stride=TOK_SUBLANES), :] = _pack_pair(
            h2[:, s * LANES:(s + 1) * LANES], h2[:, (s + TOK_SUBLANES) * LANES:(s + TOK_SUBLANES + 1) * LANES])
    lg = _dot(h2.astype(BF16), wr_ref[...]) + br_ref[...]
    lg_ref[...] = lg.T[0:ROUTE_ROWS, :]


def _out_proj(oa, ob, x2, wa, wb, g1, n2g, sh2, sc2, wr, br, seq):
    t, d = x2.shape
    tm = min(512, seq)

    def const(shape):
        return pl.BlockSpec(shape, lambda i: (0,) * len(shape))

    def per_batch():
        return pl.BlockSpec((1, 1, d), lambda i: (i * tm // seq, 0, 0))

    return pl.pallas_call(
        _out_kernel,
        grid=(t // tm,),
        in_specs=[pl.BlockSpec((tm, HGRN_WIDTH), lambda i: (i, 0)),
                  pl.BlockSpec((tm, HGRN_WIDTH), lambda i: (i, 0)),
                  pl.BlockSpec((tm, d), lambda i: (i, 0)),
                  const(wa.shape), const(wb.shape), per_batch(), const(n2g.shape),
                  per_batch(), per_batch(), const(wr.shape), const(br.shape)],
        out_specs=[pl.BlockSpec((tm, d), lambda i: (i, 0)),
                   pl.BlockSpec((tm * TOK_SUBLANES, LANES), lambda i: (i, 0)),
                   pl.BlockSpec((ROUTE_ROWS, tm), lambda i: (0, i))],
        out_shape=[jax.ShapeDtypeStruct((t, d), F32),
                   jax.ShapeDtypeStruct((t * TOK_SUBLANES, LANES), jnp.uint32),
                   jax.ShapeDtypeStruct((ROUTE_ROWS, t), F32)],
        compiler_params=_cparams(("arbitrary",)),
        name="out_proj",
    )(oa, ob, x2, wa, wb, g1, n2g, sh2, sc2, wr, br)


def _route_kernel(lg_ref, tri_ref, ri_ref, rw_ref, cnt_ref, carry_s):
    step = pl.program_id(0)

    @pl.when(step == 0)
    def _():
        carry_s[...] = jnp.zeros_like(carry_s)

    lg = lg_ref[...]
    tr = lg.shape[1]
    epg = EXPERTS_PER_GROUP
    gl = lg[N_EXPERTS:N_EXPERTS + N_GROUPS, :]
    row_g = lax.broadcasted_iota(jnp.int32, (N_GROUPS, tr), 0)
    gmax = jnp.max(gl, axis=0, keepdims=True)
    g_sel = jnp.min(jnp.where(gl == gmax, row_g, N_GROUPS), axis=0, keepdims=True)
    p_group = 1.0 / jnp.sum(jnp.exp(gl - gmax), axis=0, keepdims=True)

    e_in = lg[0:epg, :]
    for g in range(1, N_GROUPS):
        e_in = jnp.where(g_sel == g, lg[g * epg:(g + 1) * epg, :], e_in)
    row_e = lax.broadcasted_iota(jnp.int32, (epg, tr), 0)
    top1 = jnp.max(e_in, axis=0, keepdims=True)
    i1 = jnp.min(jnp.where(e_in == top1, row_e, epg), axis=0, keepdims=True)
    rest = jnp.where(row_e == i1, -jnp.inf, e_in)
    top2 = jnp.max(rest, axis=0, keepdims=True)
    i2 = jnp.min(jnp.where(rest == top2, row_e, epg), axis=0, keepdims=True)
    e2w = jnp.exp(top2 - top1)
    w1 = p_group / (1.0 + e2w)
    w2 = p_group * e2w / (1.0 + e2w)
    ex1 = g_sel * epg + i1
    ex2 = g_sel * epg + i2

    row_x = lax.broadcasted_iota(jnp.int32, (N_EXPERTS, tr), 0)
    oh1 = row_x == ex1
    oh2 = row_x == ex2
    oh = jnp.logical_or(oh1, oh2)
    before = _dot(oh.astype(BF16), tri_ref[...]) + carry_s[:, 0:1]
    rank1 = jnp.sum(jnp.where(oh1, before, 0.0), axis=0, keepdims=True)
    rank2 = jnp.sum(jnp.where(oh2, before, 0.0), axis=0, keepdims=True)
    carry_s[...] = carry_s[...] + jnp.sum(oh.astype(F32), axis=1, keepdims=True)

    zi = jnp.zeros((4, tr), jnp.int32)
    ri_ref[...] = jnp.concatenate([ex1, ex2, rank1.astype(jnp.int32), rank2.astype(jnp.int32), zi], axis=0)
    rw_ref[...] = jnp.concatenate([w1, w2, jnp.zeros((6, tr), F32)], axis=0)
    cnt_ref[...] = carry_s[...].astype(jnp.int32)


def _route(lg_t, tri):
    t = lg_t.shape[1]
    tr = tri.shape[0]
    return pl.pallas_call(
        _route_kernel,
        grid=(t // tr,),
        in_specs=[pl.BlockSpec((ROUTE_ROWS, tr), lambda i: (0, i)),
                  pl.BlockSpec((tr, tr), lambda i: (0, 0))],
        out_specs=[pl.BlockSpec((8, tr), lambda i: (0, i)),
                   pl.BlockSpec((8, tr), lambda i: (0, i)),
                   pl.BlockSpec((N_EXPERTS, LANES), lambda i: (0, 0))],
        out_shape=[jax.ShapeDtypeStruct((8, t), jnp.int32),
                   jax.ShapeDtypeStruct((8, t), F32),
                   jax.ShapeDtypeStruct((N_EXPERTS, LANES), jnp.int32)],
        scratch_shapes=[pltpu.VMEM((N_EXPERTS, LANES), F32)],
        compiler_params=_cparams(("arbitrary",)),
        name="route",
    )(lg_t, tri)


def _moe_kernel(rd_ref, be_ref, ne_ref, nx_ref, nb_ref, x_ref, wg_ref, wu_ref, wd_ref, y_ref,
                ys, wg_f, wu_f, wd_f, wg_s, wu_s, wd_s, ssem, wsem):
    b = pl.program_id(0)
    last = pl.num_programs(0) - 1
    nb = nb_ref[0]
    tm = MOE_BLOCK
    ts = TOK_SUBLANES
    weights = ((wg_ref, wg_f, wg_s), (wu_ref, wu_f, wu_s), (wd_ref, wd_f, wd_s))

    def fetch(e):
        return [pltpu.make_async_copy(src.at[e], stage, wsem.at[k]) for k, (src, stage, _) in enumerate(weights)]

    def scatter(blk):
        base = blk * tm
        for j in range(tm):
            dst = pl.multiple_of(rd_ref[base + j] * ts, ts)
            pltpu.make_async_copy(ys.at[pl.ds(j * ts, ts), :], y_ref.at[pl.ds(dst, ts), :], ssem).start()

    def wait_scatter():
        pltpu.make_async_copy(ys, y_ref.at[pl.ds(0, tm * ts), :], ssem).wait()

    @pl.when(b == 0)
    def _():
        for copy in fetch(be_ref[0]):
            copy.start()
        ys[...] = jnp.zeros_like(ys)
        spare = pltpu.make_async_copy(ys, y_ref.at[pl.ds(y_ref.shape[0] - tm * ts, tm * ts), :], ssem)
        spare.start()
        spare.wait()

    @pl.when(ne_ref[b] == 1)
    def _():
        for copy in fetch(0):
            copy.wait()
        for _, stage, dst in weights:
            dst[...] = stage[...].astype(BF16)

        @pl.when(nx_ref[b] >= 0)
        def _():
            for copy in fetch(nx_ref[b]):
                copy.start()

    @pl.when(b < nb)
    def _():
        scatter(jnp.maximum(b - 1, 0))
        parts = [_unpack_pair(x_ref[pl.ds(s, tm, stride=ts), :]) for s in range(ts)]
        x = jnp.concatenate([p[0] for p in parts] + [p[1] for p in parts], axis=1).astype(BF16)
        hid = _silu(_dot(x, wg_s[...])) * _dot(x, wu_s[...])
        y = _dot(hid.astype(BF16), wd_s[...])
        wait_scatter()
        for s in range(ts):
            ys[pl.ds(s, tm, stride=ts), :] = _pack_pair(y[:, s * LANES:(s + 1) * LANES],
                                                        y[:, (s + ts) * LANES:(s + ts + 1) * LANES])

        @pl.when(b == last)
        def _():
            scatter(b)
            wait_scatter()

    @pl.when(b == nb)
    def _():
        scatter(b - 1)
        wait_scatter()


def _moe(row_dst, blk_e, new_e, next_e, nb_real, x_sorted, wg, wu, wd, n_out_rows):
    n_blocks = blk_e.shape[0]
    d, de = wg.shape[1], wg.shape[2]
    tm = MOE_BLOCK
    hbm = pl.BlockSpec(memory_space=pl.ANY)
    return pl.pallas_call(
        _moe_kernel,
        grid_spec=pltpu.PrefetchScalarGridSpec(
            num_scalar_prefetch=5,
            grid=(n_blocks,),
            in_specs=[pl.BlockSpec((tm * TOK_SUBLANES, LANES),
                                   lambda b, rd, be, ne, nx, nb: (jnp.minimum(b, nb[0] - 1), 0)),
                      hbm, hbm, hbm],
            out_specs=hbm,
            scratch_shapes=[pltpu.VMEM((tm * TOK_SUBLANES, LANES), jnp.uint32),
                            pltpu.VMEM((d, de), F32), pltpu.VMEM((d, de), F32), pltpu.VMEM((de, d), F32),
                            pltpu.VMEM((d, de), BF16), pltpu.VMEM((d, de), BF16), pltpu.VMEM((de, d), BF16),
                            pltpu.SemaphoreType.DMA(()), pltpu.SemaphoreType.DMA((3,))]),
        out_shape=jax.ShapeDtypeStruct((n_out_rows * TOK_SUBLANES, LANES), jnp.uint32),
        compiler_params=_cparams(("arbitrary",)),
        name="moe",
    )(row_dst, blk_e, new_e, next_e, nb_real, x_sorted, wg, wu, wd)


def _dispatch_kernel(e1_ref, e2_ref, r1_ref, r2_ref, ps_ref, zf_ref, nz_ref, h_ref, x_ref, rd_ref, zbuf, sem, zsem):
    i = pl.program_id(0)
    ts = TOK_SUBLANES
    td = h_ref.shape[0] // ts
    n_tok = e1_ref.shape[0]
    base = i * td
    blk_rows = MOE_BLOCK * ts

    def zero_fill(blk):
        return pltpu.make_async_copy(
            zbuf, x_ref.at[pl.ds(pl.multiple_of(blk * blk_rows, blk_rows), blk_rows), :], zsem)

    @pl.when(i == 0)
    def _():
        zbuf[...] = jnp.zeros_like(zbuf)

        def fill(blk, carry):
            @pl.when(zf_ref[blk] == 1)
            def _():
                zero_fill(blk).start()
            return carry

        lax.fori_loop(0, zf_ref.shape[0], fill, 0)

        def init(blk, carry):
            for j in range(MOE_BLOCK):
                rd_ref[blk * MOE_BLOCK + j] = 2 * n_tok + j
            return carry

        lax.fori_loop(0, rd_ref.shape[0] // MOE_BLOCK, init, 0)

        def drain(k, carry):
            zero_fill(0).wait()
            return carry

        lax.fori_loop(0, nz_ref[0], drain, 0)

    def start(t, carry):
        src = h_ref.at[pl.ds(pl.multiple_of(t * ts, ts), ts), :]
        tok = base + t
        for slot, (e_ref, r_ref) in enumerate(((e1_ref, r1_ref), (e2_ref, r2_ref))):
            row = ps_ref[e_ref[tok]] + r_ref[tok]
            pltpu.make_async_copy(src, x_ref.at[pl.ds(pl.multiple_of(row * ts, ts), ts), :], sem).start()
            rd_ref[row] = 2 * tok + slot
        return carry

    lax.fori_loop(0, td, start, 0, unroll=8)
    for _ in range(2):
        pltpu.make_async_copy(h_ref, x_ref.at[pl.ds(0, td * ts), :], sem).wait()


def _dispatch(ri, pad_start, zero_blk, h2p, n_rows):
    t = ri.shape[1]
    td = min(512, t)
    ts = TOK_SUBLANES
    n_zero = jnp.sum(zero_blk, keepdims=True)
    return pl.pallas_call(
        _dispatch_kernel,
        grid_spec=pltpu.PrefetchScalarGridSpec(
            num_scalar_prefetch=7,
            grid=(t // td,),
            in_specs=[pl.BlockSpec((td * ts, LANES), lambda i, *_: (i, 0))],
            out_specs=[pl.BlockSpec(memory_space=pl.ANY), pl.BlockSpec(memory_space=pltpu.SMEM)],
            scratch_shapes=[pltpu.VMEM((MOE_BLOCK * ts, LANES), jnp.uint32),
                            pltpu.SemaphoreType.DMA(()), pltpu.SemaphoreType.DMA(())]),
        out_shape=[jax.ShapeDtypeStruct((n_rows * ts, LANES), jnp.uint32),
                   jax.ShapeDtypeStruct((n_rows,), jnp.int32)],
        compiler_params=_cparams(("arbitrary",)),
        name="dispatch",
    )(ri[0], ri[1], ri[2], ri[3], pad_start, zero_blk, n_zero, h2p)


def _moe_plan(counts, t):
    tm = MOE_BLOCK
    n_blocks = 2 * t // tm + N_EXPERTS
    padded = ((counts + tm - 1) // tm) * tm
    pad_end = jnp.cumsum(padded)
    pad_start = pad_end - padded
    blk = jnp.arange(n_blocks, dtype=jnp.int32)
    blk_e = jnp.minimum(jnp.sum(pad_end[None, :] <= (blk * tm)[:, None], axis=1), N_EXPERTS - 1).astype(jnp.int32)
    nb_real = (pad_end[-1:] // tm).astype(jnp.int32)
    changed = jnp.concatenate([jnp.ones((1,), bool), blk_e[1:] != blk_e[:-1]])
    new_e = jnp.logical_and(changed, blk < nb_real[0]).astype(jnp.int32)
    ex = jnp.arange(N_EXPERTS, dtype=jnp.int32)
    later = jnp.where((counts > 0)[None, :] & (ex[None, :] > ex[:, None]), ex[None, :], N_EXPERTS).min(axis=1)
    next_e = jnp.where(later < N_EXPERTS, later, -1).astype(jnp.int32)[blk_e]
    last_partial = jnp.any((blk[:, None] == (pad_end // tm - 1)[None, :]) & (counts % tm != 0)[None, :], axis=1)
    zero_blk = jnp.logical_or(last_partial, blk >= nb_real[0]).astype(jnp.int32)
    return pad_start, blk_e, new_e, next_e, nb_real, zero_blk, n_blocks * tm


COMBINE_ROWS = 64


def _combine_kernel(x1_ref, g2_ref, w1_ref, w2_ref, y_ref, o_ref):
    ts = TOK_SUBLANES
    rows = min(COMBINE_ROWS, x1_ref.shape[0])

    def chunk(c, carry):
        r = pl.ds(pl.multiple_of(c * rows, rows), rows)
        w1 = w1_ref[r, :]
        w2 = w2_ref[r, :]
        tile0 = pl.multiple_of(c * rows * 2 * ts, rows * 2 * ts)
        for s in range(ts):
            a_lo, a_hi = _unpack_pair(y_ref[pl.ds(tile0 + s, rows, stride=2 * ts), :])
            b_lo, b_hi = _unpack_pair(y_ref[pl.ds(tile0 + ts + s, rows, stride=2 * ts), :])
            for col, ya, yb in ((s, a_lo, b_lo), (s + ts, a_hi, b_hi)):
                cols = slice(col * LANES, (col + 1) * LANES)
                o_ref[r, cols] = x1_ref[r, cols] + g2_ref[0][:, cols] * (w1 * ya + w2 * yb)
        return carry

    lax.fori_loop(0, x1_ref.shape[0] // rows, chunk, 0)


def _combine(x1, g2, w1, w2, y2, seq):
    t, d = x1.shape
    tc = min(512, seq)
    return pl.pallas_call(
        _combine_kernel,
        grid=(t // tc,),
        in_specs=[pl.BlockSpec((tc, d), lambda i: (i, 0)),
                  pl.BlockSpec((1, 1, d), lambda i: (i * tc // seq, 0, 0)),
                  pl.BlockSpec((tc, 1), lambda i: (i, 0)),
                  pl.BlockSpec((tc, 1), lambda i: (i, 0)),
                  pl.BlockSpec((tc * 2 * TOK_SUBLANES, LANES), lambda i: (i, 0))],
        out_specs=pl.BlockSpec((tc, d), lambda i: (i, 0)),
        out_shape=jax.ShapeDtypeStruct((t, d), F32),
        compiler_params=_cparams(("arbitrary",)),
        name="combine",
    )(x1, g2, w1, w2, y2)


def _q_up_layout(w_q_up):
    w = w_q_up.reshape(Q_LORA, MLA_HEADS, MLA_QK)
    w = jnp.pad(w, ((0, 0), (0, 0), (0, MLA_QPAD - MLA_QK)))
    return w.reshape(Q_LORA, MLA_HEADS * MLA_QPAD).astype(BF16)


def _pad_lanes(g, width):
    return jnp.pad(g, (0, width - g.shape[0])).reshape(1, width)


def kernel(x, c, positions, w_ada, b_ada, norm1_g, w_in, hgrn_lb_logits, hgrn_onorm_g, q_a_norm_g, w_q_up,
           kv_a_norm_g, w_kv_up, q_norm_g, k_norm_g, attn_onorm_g, w_out, norm2_g, w_group, b_group,
           w_router, b_router, w_gate, w_up, w_down):
    bsz, seq, d = x.shape
    t = bsz * seq
    depth = w_ada.shape[0]
    half = MLA_ROPE // 2
    inv_freq = ROPE_BASE ** (-jnp.arange(0, MLA_ROPE, 2, dtype=F32) / MLA_ROPE)
    freq = jnp.stack([jnp.tile(inv_freq, LANES // half),
                      jnp.where(jnp.arange(LANES) < MLA_ROPE, 0.0, jnp.pi / 2).astype(F32)])
    pos3 = positions.reshape(bsz, seq, 1)
    tr = min(512, t)
    tri = jnp.triu(jnp.ones((tr, tr), BF16), 1)

    x2 = x.reshape(t, d)
    for l in range(depth):
        mod = _ada(c, w_ada[l], b_ada[l]).reshape(bsz, 6, 1, d)
        sh1, sc1, g1, sh2, sc2, g2 = (mod[:, i] for i in range(6))

        proj = _in_proj(_norm_mod(x2, norm1_g[l].reshape(1, d), sh1, sc1, seq), w_in[l].T, seq)
        proj3 = proj.reshape(bsz, seq, IN_COLS_PAD)

        o_a = _hgrn(proj3, hgrn_lb_logits, hgrn_onorm_g[l].reshape(1, HGRN_DK), l)

        q, k, v = _mla_up(proj3, pos3, _q_up_layout(w_q_up[l]), w_kv_up[l].astype(BF16),
                          q_a_norm_g[l].reshape(1, Q_LORA), kv_a_norm_g[l].reshape(1, KV_LORA),
                          _pad_lanes(q_norm_g[l], MLA_QPAD), _pad_lanes(k_norm_g[l], MLA_QPAD), freq)
        o_b = _attention(q, k, v, attn_onorm_g[l].reshape(MLA_V, 1))

        w_o = w_out[l].astype(BF16)
        wr = jnp.pad(jnp.concatenate([w_router[l], w_group[l]], axis=1),
                     ((0, 0), (0, LANES - N_EXPERTS - N_GROUPS))).astype(BF16)
        br = _pad_lanes(jnp.concatenate([b_router[l], b_group[l]]), LANES)
        x1, h2, lg_t = _out_proj(o_a.reshape(t, HGRN_WIDTH), o_b.reshape(t, HGRN_WIDTH), x2,
                                 w_o[:HGRN_WIDTH], w_o[HGRN_WIDTH:], g1, norm2_g[l].reshape(1, d),
                                 sh2, sc2, wr, br, seq)

        ri, rw, cnt = _route(lg_t, tri)
        counts = cnt[:, 0]
        pad_start, blk_e, new_e, next_e, nb_real, zero_blk, n_rows = _moe_plan(counts, t)
        x_sorted, row_dst = _dispatch(ri, pad_start, zero_blk, h2, n_rows)
        y2 = _moe(row_dst, blk_e, new_e, next_e, nb_real, x_sorted, w_gate[l], w_up[l], w_down[l], 2 * t + MOE_BLOCK)
        x2 = _combine(x1, g2, rw[0].reshape(t, 1), rw[1].reshape(t, 1), y2, seq)
    return x2.reshape(bsz, seq, d)
```

```python
import functools

import jax
import jax.numpy as jnp
from jax import lax
from jax.experimental import pallas as pl
from jax.experimental.pallas import tpu as pltpu

F32 = jnp.float32
BF16 = jnp.bfloat16
EPS = 1e-6
LOG2E = 1.4426950408889634

D_MODEL = 2048
HGRN_WIDTH = 1024
HGRN_DK = 128
HGRN_HEADS = 8
HGRN_CHUNK = 64
MLA_HEADS = 8
MLA_NOPE = 128
MLA_ROPE = 64
MLA_QK = MLA_NOPE + MLA_ROPE
MLA_V = 128
MLA_QPAD = 256
Q_LORA = 512
KV_LORA = 256
ROPE_BASE = 10000.0
IN_COLS = 4 * HGRN_WIDTH + Q_LORA + KV_LORA + MLA_ROPE
IN_COLS_PAD = 5120
N_GROUPS = 4
EXPERTS_PER_GROUP = 8
N_EXPERTS = 32
D_EXPERT = 512
ROUTE_ROWS = 40
MOE_BLOCK = 256
LANES = 128
TOK_SUBLANES = 8
VMEM_LIMIT = 56 * 1024 * 1024


def _cparams(sem):
    return pltpu.CompilerParams(dimension_semantics=sem, vmem_limit_bytes=VMEM_LIMIT)


def _dot(a, b):
    return jnp.dot(a, b, preferred_element_type=F32)


def _dot_nt(a, b):
    return lax.dot_general(a, b, (((1,), (1,)), ((), ())), preferred_element_type=F32)


def _rms(x, g):
    return x * lax.rsqrt(jnp.mean(x * x, axis=-1, keepdims=True) + EPS) * g


def _silu(x):
    return x * jax.nn.sigmoid(x)


def _pack_pair(lo, hi):
    lo_b = lax.bitcast_convert_type(lo.astype(BF16).astype(F32), jnp.uint32)
    hi_b = lax.bitcast_convert_type(hi.astype(BF16).astype(F32), jnp.uint32)
    return hi_b | (lo_b >> 16)


def _unpack_pair(w):
    lo = lax.bitcast_convert_type(w << 16, F32)
    hi = lax.bitcast_convert_type(w & jnp.uint32(0xFFFF0000), F32)
    return lo, hi


def _ada_kernel(c_ref, w_ref, b_ref, o_ref):
    ca = _silu(c_ref[...]).astype(BF16)
    o_ref[...] = _dot(ca, w_ref[...].astype(BF16)) + b_ref[...]


def _ada(c, w, b):
    bsz, d = c.shape
    n = w.shape[1]
    tn = 1024
    return pl.pallas_call(
        _ada_kernel,
        grid=(n // tn,),
        in_specs=[pl.BlockSpec((bsz, d), lambda j: (0, 0)),
                  pl.BlockSpec((d, tn), lambda j: (0, j)),
                  pl.BlockSpec((1, tn), lambda j: (0, j))],
        out_specs=pl.BlockSpec((bsz, tn), lambda j: (0, j)),
        out_shape=jax.ShapeDtypeStruct((bsz, n), F32),
        compiler_params=_cparams(("arbitrary",)),
        name="ada",
    )(c, w, b.reshape(1, n))


NORM_ROWS = 32


def _norm_kernel(x_ref, g_ref, sh_ref, sc_ref, h_ref):
    def body(c, carry):
        r = pl.ds(pl.multiple_of(c * NORM_ROWS, NORM_ROWS), NORM_ROWS)
        h = _rms(x_ref[r, :], g_ref[...]) * (1.0 + sc_ref[0]) + sh_ref[0]
        h_ref[r, :] = h.astype(BF16)
        return carry
    lax.fori_loop(0, x_ref.shape[0] // NORM_ROWS, body, 0, unroll=2)


def _norm_mod(x2, g, sh, sc, seq):
    t, d = x2.shape
    tm = min(512, seq)
    return pl.pallas_call(
        _norm_kernel,
        grid=(t // tm,),
        in_specs=[pl.BlockSpec((tm, d), lambda i: (i, 0)),
                  pl.BlockSpec((1, d), lambda i: (0, 0)),
                  pl.BlockSpec((1, 1, d), lambda i: (i * tm // seq, 0, 0)),
                  pl.BlockSpec((1, 1, d), lambda i: (i * tm // seq, 0, 0))],
        out_specs=pl.BlockSpec((tm, d), lambda i: (i, 0)),
        out_shape=jax.ShapeDtypeStruct((t, d), BF16),
        compiler_params=_cparams(("arbitrary",)),
        name="norm_mod",
    )(x2, g, sh, sc)


def _in_kernel(h_ref, w_ref, wt_ref, o_ref, wb_s, *, n_main):
    j = pl.program_id(0)
    first_row_tile = pl.program_id(1) == 0

    @pl.when(jnp.logical_and(first_row_tile, j < n_main))
    def _():
        wb_s[...] = w_ref[...].astype(BF16)

    @pl.when(jnp.logical_and(first_row_tile, j == n_main))
    def _():
        wb_s[...] = wt_ref[...]

    o_ref[...] = _dot_nt(h_ref[...], wb_s[...]).astype(BF16)


def _in_proj(h, w_t, seq):
    t, d = h.shape
    tn = 1024
    n_main = IN_COLS // tn
    w_tail = jnp.pad(w_t[n_main * tn:], ((0, (n_main + 1) * tn - IN_COLS), (0, 0))).astype(BF16)
    tm = min(1024, t)
    return pl.pallas_call(
        functools.partial(_in_kernel, n_main=n_main),
        grid=(IN_COLS_PAD // tn, t // tm),
        in_specs=[pl.BlockSpec((tm, d), lambda j, i: (i, 0)),
                  pl.BlockSpec((tn, d), lambda j, i: (jnp.minimum(j, n_main - 1), 0)),
                  pl.BlockSpec((tn, d), lambda j, i: (0, 0))],
        out_specs=pl.BlockSpec((tm, tn), lambda j, i: (i, j)),
        out_shape=jax.ShapeDtypeStruct((t, IN_COLS_PAD), BF16),
        scratch_shapes=[pltpu.VMEM((tn, d), BF16)],
        compiler_params=_cparams(("arbitrary", "arbitrary")),
        name="in_proj",
    )(h, w_t, w_tail)


HG_ROWS = 256


def _software_pipeline(stages, n_blocks):
    depth = len(stages)

    def step(i, static):
        for k in reversed(range(depth)):
            if static and not 0 <= i - k < n_blocks:
                continue
            stages[k](i - k)

    if n_blocks < depth:
        for i in range(n_blocks + depth - 1):
            step(i, True)
        return
    for i in range(depth - 1):
        step(i, True)

    def steady(i, carry):
        step(i, False)
        return carry

    lax.fori_loop(depth - 1, n_blocks, steady, 0)
    for i in range(n_blocks, n_blocks + depth - 1):
        step(i, True)


def _chunk_mask(n):
    row = lax.broadcasted_iota(jnp.int32, (n, n), 0)
    col = lax.broadcasted_iota(jnp.int32, (n, n), 1)
    return jnp.logical_and(row // HGRN_CHUNK == col // HGRN_CHUNK, col <= row)


def _hgrn_kernel(q_ref, f_ref, i_ref, g_ref, lbl_ref, og_ref, o_ref,
                 qb_s, u_s, sp_s, dec_s, oi_s, b_s, k_s, qin_s, kin_s, ku_s, a_s, st_s, *, layer):
    seq = q_ref.shape[1]
    cs = HGRN_CHUNK
    rb = min(HG_ROWS, seq)
    nc = rb // cs
    dk = HGRN_DK
    lg = lbl_ref[...]
    ex = jnp.exp(lg - jnp.max(lg, axis=0, keepdims=True))
    sm = ex / jnp.sum(ex, axis=0, keepdims=True)
    lb = jnp.sum(sm[0:layer + 1], axis=0, keepdims=True)
    mask = _chunk_mask(rb)
    tri = mask.astype(BF16)
    row_chunk = lax.broadcasted_iota(jnp.int32, (rb, dk), 0) // cs

    chunk_sel = [(row_chunk == c).astype(BF16) for c in range(nc)]

    def rows(blk):
        return pl.ds(pl.multiple_of(blk * rb, rb), rb)

    def stage1(blk):
        r = rows(blk)
        f = lb + (1.0 - lb) * jax.nn.sigmoid(f_ref[0, r, :].astype(F32))
        lf = jnp.log(f)
        k_s[r, :] = 1.0 - f
        hi = lf.astype(BF16)
        r1 = lf - hi.astype(F32)
        mid = r1.astype(BF16)
        lo = (r1 - mid.astype(F32)).astype(BF16)
        bhm = _dot(tri, jnp.concatenate([hi, mid], axis=1))
        b_s[r, :] = bhm[:, 0:dk] + bhm[:, dk:] + _dot(tri, lo)

    def stage2(blk):
        r = rows(blk)
        b3 = b_s[r, :].reshape(nc, cs, dk)
        bmid = b3[:, cs // 2 - 1:cs // 2, :]
        blast = b3[:, cs - 1:cs, :]
        q3 = (q_ref[0, r, :].astype(F32) * dk ** -0.5).reshape(nc, cs, dk)
        k3 = k_s[r, :].reshape(nc, cs, dk)
        qin_s[r, :] = (q3 * jnp.exp(b3 - bmid)).reshape(rb, dk).astype(BF16)
        kin_s[r, :] = (k3 * jnp.exp(bmid - b3)).reshape(rb, dk).astype(BF16)
        ku_s[r, :] = (k3 * jnp.exp(blast - b3)).reshape(rb, dk).astype(BF16)
        qb_s[r, :] = (q3 * jnp.exp(b3)).reshape(rb, dk).astype(BF16)
        dec_s[pl.ds(blk * nc, nc)] = jnp.exp(blast)

    def stage3(blk):
        r = rows(blk)
        a_s[r, :] = jnp.where(mask, _dot_nt(qin_s[r, :], kin_s[r, :]), 0.0).astype(BF16)
        vt = i_ref[0, r, :].astype(F32).T.astype(BF16)
        ku = ku_s[r, :]
        ut = _dot(vt, jnp.concatenate([ku * sel for sel in chunk_sel], axis=1))
        for c in range(nc):
            u_s[blk * nc + c] = ut[:, c * dk:(c + 1) * dk]

    def stage4(blk):
        r = rows(blk)
        oi_s[r, :] = _dot(a_s[r, :], i_ref[0, r, :])
        st = st_s[...]
        for c in range(nc):
            sp_s[blk * nc + c] = st.astype(BF16)
            st = st * dec_s[blk * nc + c] + u_s[blk * nc + c]
        st_s[...] = st

    def stage5(blk):
        r = rows(blk)
        inter = [_dot_nt(qb_s[pl.ds(pl.multiple_of(blk * rb + c * cs, cs), cs), :], sp_s[blk * nc + c])
                 for c in range(nc)]
        o = oi_s[r, :] + jnp.concatenate(inter, axis=0)
        o = _rms(o, og_ref[...]) * _silu(g_ref[0, r, :].astype(F32))
        o_ref[0, r, :] = o.astype(BF16)

    st_s[...] = jnp.zeros_like(st_s)
    _software_pipeline([stage1, stage2, stage3, stage4, stage5], seq // rb)


def _hgrn(proj3, lb_logits, onorm_g, layer):
    bsz, seq, _ = proj3.shape
    nh = HGRN_HEADS

    def col(off):
        return pl.BlockSpec((1, seq, HGRN_DK), lambda b, h: (b, 0, off * nh + h))

    nl = lb_logits.shape[0]
    return pl.pallas_call(
        functools.partial(_hgrn_kernel, layer=layer),
        grid=(bsz, nh),
        in_specs=[col(0), col(1), col(2), col(3),
                  pl.BlockSpec((nl, HGRN_DK), lambda b, h: (0, h)),
                  pl.BlockSpec((1, HGRN_DK), lambda b, h: (0, 0))],
        out_specs=pl.BlockSpec((1, seq, HGRN_DK), lambda b, h: (b, 0, h)),
        out_shape=jax.ShapeDtypeStruct((bsz, seq, HGRN_WIDTH), BF16),
        scratch_shapes=[pltpu.VMEM((seq, HGRN_DK), BF16),
                        pltpu.VMEM((seq // HGRN_CHUNK, HGRN_DK, HGRN_DK), F32),
                        pltpu.VMEM((seq // HGRN_CHUNK, HGRN_DK, HGRN_DK), BF16),
                        pltpu.VMEM((seq // HGRN_CHUNK, 1, HGRN_DK), F32),
                        pltpu.VMEM((seq, HGRN_DK), F32),
                        pltpu.VMEM((seq, HGRN_DK), F32), pltpu.VMEM((seq, HGRN_DK), F32),
                        pltpu.VMEM((seq, HGRN_DK), BF16), pltpu.VMEM((seq, HGRN_DK), BF16),
                        pltpu.VMEM((seq, HGRN_DK), BF16), pltpu.VMEM((seq, min(HG_ROWS, seq)), BF16),
                        pltpu.VMEM((HGRN_DK, HGRN_DK), F32)],
        compiler_params=_cparams(("arbitrary", "arbitrary")),
        name="hgrn",
    )(proj3, proj3, proj3, proj3, lb_logits, onorm_g)


def _rope(x, cos, sin_signed, lane):
    half = MLA_ROPE // 2
    swapped = jnp.where(lane < half, pltpu.roll(x, LANES - half, 1), pltpu.roll(x, half, 1))
    return x * cos + swapped * sin_signed


UP_ROWS = 1024


def _up_kernel(p_ref, pos_ref, wq_ref, wkv_ref, qag_ref, kvag_ref, qg_ref, kg_ref, freq_ref,
               q_ref, k_ref, v_ref):
    rows = min(UP_ROWS, p_ref.shape[0])
    lane = lax.broadcasted_iota(jnp.int32, (rows, LANES), 1)
    half = MLA_ROPE // 2
    valid = lane < MLA_ROPE
    qg = qg_ref[...] * (MLA_QK ** -0.5 * LOG2E)
    kg = kg_ref[...]

    def chunk(c, carry):
        r = pl.ds(pl.multiple_of(c * rows, rows), rows)
        p = p_ref[r, :].astype(F32)
        q_a = p[:, 0:Q_LORA]
        kv_a = p[:, Q_LORA:Q_LORA + KV_LORA]
        k_pe = p[:, Q_LORA + KV_LORA:Q_LORA + KV_LORA + LANES]
        qa_n = _rms(q_a, qag_ref[...]).astype(BF16)
        kva_n = _rms(kv_a, kvag_ref[...]).astype(BF16)
        sin_all = jnp.sin(pos_ref[r, :].astype(F32) * freq_ref[0:1, :] + freq_ref[1:2, :])
        cos = jnp.where(valid, pltpu.roll(sin_all, MLA_ROPE, 1), 0.0)
        sin_signed = jnp.where(valid, jnp.where(lane < half, -sin_all, sin_all), 0.0)
        kpe_ss = jnp.sum(k_pe * k_pe, axis=-1, keepdims=True)
        kpe_rot = _rope(k_pe * kg[:, MLA_NOPE:], cos, sin_signed, lane)
        for h in range(MLA_HEADS):
            cols = slice(h * MLA_QPAD, (h + 1) * MLA_QPAD)
            qh = _dot(qa_n, wq_ref[:, cols])
            qn = qh * lax.rsqrt(jnp.sum(qh * qh, axis=-1, keepdims=True) / MLA_QK + EPS) * qg
            q_ref[0, h, r, 0:MLA_NOPE] = qn[:, 0:MLA_NOPE].astype(BF16)
            q_ref[0, h, r, MLA_NOPE:] = _rope(qn[:, MLA_NOPE:], cos, sin_signed, lane).astype(BF16)
            kvh = _dot(kva_n, wkv_ref[:, cols])
            kn = kvh[:, 0:MLA_NOPE]
            rk = lax.rsqrt((jnp.sum(kn * kn, axis=-1, keepdims=True) + kpe_ss) / MLA_QK + EPS)
            k_ref[0, h, r, 0:MLA_NOPE] = (kn * rk * kg[:, 0:MLA_NOPE]).astype(BF16)
            k_ref[0, h, r, MLA_NOPE:] = (kpe_rot * rk).astype(BF16)
            v_ref[0, h, :, r] = kvh[:, MLA_NOPE:].T.astype(BF16)
        return carry

    lax.fori_loop(0, p_ref.shape[0] // rows, chunk, 0)


def _mla_up(proj3, pos3, wq, wkv, qag, kvag, qg, kg, freq):
    bsz, seq, _ = proj3.shape
    tm = min(UP_ROWS, seq)
    nh = MLA_HEADS
    mla_block = 4 * HGRN_WIDTH // 1024

    def const(shape):
        return pl.BlockSpec(shape, lambda b, i: (0,) * len(shape))

    return pl.pallas_call(
        _up_kernel,
        grid=(bsz, seq // tm),
        in_specs=[pl.BlockSpec((None, tm, 1024), lambda b, i: (b, i, mla_block)),
                  pl.BlockSpec((None, tm, 1), lambda b, i: (b, i, 0)),
                  const(wq.shape), const(wkv.shape), const(qag.shape), const(kvag.shape),
                  const(qg.shape), const(kg.shape), const(freq.shape)],
        out_specs=[pl.BlockSpec((1, nh, tm, MLA_QPAD), lambda b, i: (b, 0, i, 0)),
                   pl.BlockSpec((1, nh, tm, MLA_QPAD), lambda b, i: (b, 0, i, 0)),
                   pl.BlockSpec((1, nh, MLA_V, tm), lambda b, i: (b, 0, 0, i))],
        out_shape=[jax.ShapeDtypeStruct((bsz, nh, seq, MLA_QPAD), BF16),
                   jax.ShapeDtypeStruct((bsz, nh, seq, MLA_QPAD), BF16),
                   jax.ShapeDtypeStruct((bsz, nh, MLA_V, seq), BF16)],
        compiler_params=_cparams(("arbitrary", "arbitrary")),
        name="mla_up",
    )(proj3, pos3, wq, wkv, qag, kvag, qg, kg, freq)


ATT_T = 1024


def _attn_kernel(q_ref, k_ref, vt_ref, g_ref, o_ref):
    seq = q_ref.shape[2]
    t = min(ATT_T, seq)
    key = lax.broadcasted_iota(jnp.int32, (t, t), 0)
    qry = lax.broadcasted_iota(jnp.int32, (t, t), 1)
    causal = key <= qry
    neg = jnp.finfo(F32).min
    for qi in range(seq // t):
        off = qi * t
        q = q_ref[0, 0, off:off + t, :]
        sd = jnp.where(causal, _dot_nt(k_ref[0, 0, off:off + t, :], q), neg)
        m = jnp.max(sd, axis=0, keepdims=True)
        if qi > 0:
            so = _dot_nt(k_ref[0, 0, 0:off, :], q)
            m = jnp.maximum(m, jnp.max(so, axis=0, keepdims=True))
        pd = jnp.exp2(sd - m)
        l = jnp.sum(pd, axis=0, keepdims=True)
        ot = _dot(vt_ref[0, 0, :, off:off + t], pd.astype(BF16))
        if qi > 0:
            po = jnp.exp2(so - m)
            l = l + jnp.sum(po, axis=0, keepdims=True)
            ot = ot + _dot(vt_ref[0, 0, :, 0:off], po.astype(BF16))
        ot = ot * (1.0 / l)
        ot = ot * lax.rsqrt(jnp.mean(ot * ot, axis=0, keepdims=True) + EPS) * g_ref[...]
        o_ref[0, off:off + t, :] = ot.T.astype(BF16)


def _attention(q, k, v, g):
    bsz, nh, seq, _ = q.shape
    return pl.pallas_call(
        _attn_kernel,
        grid=(bsz, nh),
        in_specs=[pl.BlockSpec((1, 1, seq, MLA_QPAD), lambda b, h: (b, h, 0, 0)),
                  pl.BlockSpec((1, 1, seq, MLA_QPAD), lambda b, h: (b, h, 0, 0)),
                  pl.BlockSpec((1, 1, MLA_V, seq), lambda b, h: (b, h, 0, 0)),
                  pl.BlockSpec((MLA_V, 1), lambda b, h: (0, 0))],
        out_specs=pl.BlockSpec((1, seq, MLA_V), lambda b, h: (b, 0, h)),
        out_shape=jax.ShapeDtypeStruct((bsz, seq, nh * MLA_V), BF16),
        compiler_params=_cparams(("arbitrary", "arbitrary")),
        name="attn",
    )(q, k, v, g)


def _out_kernel(oa_ref, ob_ref, x_ref, wa_ref, wb_ref, g1_ref, n2g_ref, sh2_ref, sc2_ref,
                wr_ref, br_ref, x1_ref, h2_ref, lg_ref):
    mix = _dot(oa_ref[...], wa_ref[...]) + _dot(ob_ref[...], wb_ref[...])
    x1 = x_ref[...] + g1_ref[0] * mix
    x1_ref[...] = x1
    h2 = _rms(x1, n2g_ref[...]) * (1.0 + sc2_ref[0]) + sh2_ref[0]
    tm = h2.shape[0]
    for s in range(TOK_SUBLANES):
        h2_ref[pl.ds(s, tm, stride=TOK_SUBLANES), :] = _pack_pair(
            h2[:, s * LANES:(s + 1) * LANES], h2[:, (s + TOK_SUBLANES) * LANES:(s + TOK_SUBLANES + 1) * LANES])
    lg = _dot(h2.astype(BF16), wr_ref[...]) + br_ref[...]
    lg_ref[...] = lg.T[0:ROUTE_ROWS, :]


def _out_proj(oa, ob, x2, wa, wb, g1, n2g, sh2, sc2, wr, br, seq):
    t, d = x2.shape
    tm = min(512, seq)

    def const(shape):
        return pl.BlockSpec(shape, lambda i: (0,) * len(shape))

    def per_batch():
        return pl.BlockSpec((1, 1, d), lambda i: (i * tm // seq, 0, 0))

    return pl.pallas_call(
        _out_kernel,
        grid=(t // tm,),
        in_specs=[pl.BlockSpec((tm, HGRN_WIDTH), lambda i: (i, 0)),
                  pl.BlockSpec((tm, HGRN_WIDTH), lambda i: (i, 0)),
                  pl.BlockSpec((tm, d), lambda i: (i, 0)),
                  const(wa.shape), const(wb.shape), per_batch(), const(n2g.shape),
                  per_batch(), per_batch(), const(wr.shape), const(br.shape)],
        out_specs=[pl.BlockSpec((tm, d), lambda i: (i, 0)),
                   pl.BlockSpec((tm * TOK_SUBLANES, LANES), lambda i: (i, 0)),
                   pl.BlockSpec((ROUTE_ROWS, tm), lambda i: (0, i))],
        out_shape=[jax.ShapeDtypeStruct((t, d), F32),
                   jax.ShapeDtypeStruct((t * TOK_SUBLANES, LANES), jnp.uint32),
                   jax.ShapeDtypeStruct((ROUTE_ROWS, t), F32)],
        compiler_params=_cparams(("arbitrary",)),
        name="out_proj",
    )(oa, ob, x2, wa, wb, g1, n2g, sh2, sc2, wr, br)


def _route_kernel(lg_ref, tri_ref, ri_ref, rw_ref, cnt_ref, carry_s):
    step = pl.program_id(0)

    @pl.when(step == 0)
    def _():
        carry_s[...] = jnp.zeros_like(carry_s)

    lg = lg_ref[...]
    tr = lg.shape[1]
    epg = EXPERTS_PER_GROUP
    gl = lg[N_EXPERTS:N_EXPERTS + N_GROUPS, :]
    row_g = lax.broadcasted_iota(jnp.int32, (N_GROUPS, tr), 0)
    gmax = jnp.max(gl, axis=0, keepdims=True)
    g_sel = jnp.min(jnp.where(gl == gmax, row_g, N_GROUPS), axis=0, keepdims=True)
    p_group = 1.0 / jnp.sum(jnp.exp(gl - gmax), axis=0, keepdims=True)

    e_in = lg[0:epg, :]
    for g in range(1, N_GROUPS):
        e_in = jnp.where(g_sel == g, lg[g * epg:(g + 1) * epg, :], e_in)
    row_e = lax.broadcasted_iota(jnp.int32, (epg, tr), 0)
    top1 = jnp.max(e_in, axis=0, keepdims=True)
    i1 = jnp.min(jnp.where(e_in == top1, row_e, epg), axis=0, keepdims=True)
    rest = jnp.where(row_e == i1, -jnp.inf, e_in)
    top2 = jnp.max(rest, axis=0, keepdims=True)
    i2 = jnp.min(jnp.where(rest == top2, row_e, epg), axis=0, keepdims=True)
    e2w = jnp.exp(top2 - top1)
    w1 = p_group / (1.0 + e2w)
    w2 = p_group * e2w / (1.0 + e2w)
    ex1 = g_sel * epg + i1
    ex2 = g_sel * epg + i2

    row_x = lax.broadcasted_iota(jnp.int32, (N_EXPERTS, tr), 0)
    oh1 = row_x == ex1
    oh2 = row_x == ex2
    oh = jnp.logical_or(oh1, oh2)
    before = _dot(oh.astype(BF16), tri_ref[...]) + carry_s[:, 0:1]
    rank1 = jnp.sum(jnp.where(oh1, before, 0.0), axis=0, keepdims=True)
    rank2 = jnp.sum(jnp.where(oh2, before, 0.0), axis=0, keepdims=True)
    carry_s[...] = carry_s[...] + jnp.sum(oh.astype(F32), axis=1, keepdims=True)

    zi = jnp.zeros((4, tr), jnp.int32)
    ri_ref[...] = jnp.concatenate([ex1, ex2, rank1.astype(jnp.int32), rank2.astype(jnp.int32), zi], axis=0)
    rw_ref[...] = jnp.concatenate([w1, w2, jnp.zeros((6, tr), F32)], axis=0)
    cnt_ref[...] = carry_s[...].astype(jnp.int32)


def _route(lg_t, tri):
    t = lg_t.shape[1]
    tr = tri.shape[0]
    return pl.pallas_call(
        _route_kernel,
        grid=(t // tr,),
        in_specs=[pl.BlockSpec((ROUTE_ROWS, tr), lambda i: (0, i)),
                  pl.BlockSpec((tr, tr), lambda i: (0, 0))],
        out_specs=[pl.BlockSpec((8, tr), lambda i: (0, i)),
                   pl.BlockSpec((8, tr), lambda i: (0, i)),
                   pl.BlockSpec((N_EXPERTS, LANES), lambda i: (0, 0))],
        out_shape=[jax.ShapeDtypeStruct((8, t), jnp.int32),
                   jax.ShapeDtypeStruct((8, t), F32),
                   jax.ShapeDtypeStruct((N_EXPERTS, LANES), jnp.int32)],
        scratch_shapes=[pltpu.VMEM((N_EXPERTS, LANES), F32)],
        compiler_params=_cparams(("arbitrary",)),
        name="route",
    )(lg_t, tri)


def _moe_kernel(rd_ref, be_ref, ne_ref, nx_ref, nb_ref, x_ref, wg_ref, wu_ref, wd_ref, y_ref,
                ys, wg_f, wu_f, wd_f, wg_s, wu_s, wd_s, ssem, wsem):
    b = pl.program_id(0)
    last = pl.num_programs(0) - 1
    nb = nb_ref[0]
    tm = MOE_BLOCK
    ts = TOK_SUBLANES
    weights = ((wg_ref, wg_f, wg_s), (wu_ref, wu_f, wu_s), (wd_ref, wd_f, wd_s))

    def fetch(e):
        return [pltpu.make_async_copy(src.at[e], stage, wsem.at[k]) for k, (src, stage, _) in enumerate(weights)]

    def scatter(blk):
        base = blk * tm
        for j in range(tm):
            dst = pl.multiple_of(rd_ref[base + j] * ts, ts)
            pltpu.make_async_copy(ys.at[pl.ds(j * ts, ts), :], y_ref.at[pl.ds(dst, ts), :], ssem).start()

    def wait_scatter():
        pltpu.make_async_copy(ys, y_ref.at[pl.ds(0, tm * ts), :], ssem).wait()

    @pl.when(b == 0)
    def _():
        for copy in fetch(be_ref[0]):
            copy.start()
        ys[...] = jnp.zeros_like(ys)
        spare = pltpu.make_async_copy(ys, y_ref.at[pl.ds(y_ref.shape[0] - tm * ts, tm * ts), :], ssem)
        spare.start()
        spare.wait()

    @pl.when(ne_ref[b] == 1)
    def _():
        for copy in fetch(0):
            copy.wait()
        for _, stage, dst in weights:
            dst[...] = stage[...].astype(BF16)

        @pl.when(nx_ref[b] >= 0)
        def _():
            for copy in fetch(nx_ref[b]):
                copy.start()

    @pl.when(b < nb)
    def _():
        scatter(jnp.maximum(b - 1, 0))
        parts = [_unpack_pair(x_ref[pl.ds(s, tm, stride=ts), :]) for s in range(ts)]
        x = jnp.concatenate([p[0] for p in parts] + [p[1] for p in parts], axis=1).astype(BF16)
        hid = _silu(_dot(x, wg_s[...])) * _dot(x, wu_s[...])
        y = _dot(hid.astype(BF16), wd_s[...])
        wait_scatter()
        for s in range(ts):
            ys[pl.ds(s, tm, stride=ts), :] = _pack_pair(y[:, s * LANES:(s + 1) * LANES],
                                                        y[:, (s + ts) * LANES:(s + ts + 1) * LANES])

        @pl.when(b == last)
        def _():
            scatter(b)
            wait_scatter()

    @pl.when(b == nb)
    def _():
        scatter(b - 1)
        wait_scatter()


def _moe(row_dst, blk_e, new_e, next_e, nb_real, x_sorted, wg, wu, wd, n_out_rows):
    n_blocks = blk_e.shape[0]
    d, de = wg.shape[1], wg.shape[2]
    tm = MOE_BLOCK
    hbm = pl.BlockSpec(memory_space=pl.ANY)
    return pl.pallas_call(
        _moe_kernel,
        grid_spec=pltpu.PrefetchScalarGridSpec(
            num_scalar_prefetch=5,
            grid=(n_blocks,),
            in_specs=[pl.BlockSpec((tm * TOK_SUBLANES, LANES),
                                   lambda b, rd, be, ne, nx, nb: (jnp.minimum(b, nb[0] - 1), 0)),
                      hbm, hbm, hbm],
            out_specs=hbm,
            scratch_shapes=[pltpu.VMEM((tm * TOK_SUBLANES, LANES), jnp.uint32),
                            pltpu.VMEM((d, de), F32), pltpu.VMEM((d, de), F32), pltpu.VMEM((de, d), F32),
                            pltpu.VMEM((d, de), BF16), pltpu.VMEM((d, de), BF16), pltpu.VMEM((de, d), BF16),
                            pltpu.SemaphoreType.DMA(()), pltpu.SemaphoreType.DMA((3,))]),
        out_shape=jax.ShapeDtypeStruct((n_out_rows * TOK_SUBLANES, LANES), jnp.uint32),
        compiler_params=_cparams(("arbitrary",)),
        name="moe",
    )(row_dst, blk_e, new_e, next_e, nb_real, x_sorted, wg, wu, wd)


def _dispatch_kernel(e1_ref, e2_ref, r1_ref, r2_ref, ps_ref, zf_ref, nz_ref, h_ref, x_ref, rd_ref, zbuf, sem, zsem):
    i = pl.program_id(0)
    ts = TOK_SUBLANES
    td = h_ref.shape[0] // ts
    n_tok = e1_ref.shape[0]
    base = i * td
    blk_rows = MOE_BLOCK * ts

    def zero_fill(blk):
        return pltpu.make_async_copy(
            zbuf, x_ref.at[pl.ds(pl.multiple_of(blk * blk_rows, blk_rows), blk_rows), :], zsem)

    @pl.when(i == 0)
    def _():
        zbuf[...] = jnp.zeros_like(zbuf)

        def fill(blk, carry):
            @pl.when(zf_ref[blk] == 1)
            def _():
                zero_fill(blk).start()
            return carry

        lax.fori_loop(0, zf_ref.shape[0], fill, 0)

        def init(blk, carry):
            for j in range(MOE_BLOCK):
                rd_ref[blk * MOE_BLOCK + j] = 2 * n_tok + j
            return carry

        lax.fori_loop(0, rd_ref.shape[0] // MOE_BLOCK, init, 0)

        def drain(k, carry):
            zero_fill(0).wait()
            return carry

        lax.fori_loop(0, nz_ref[0], drain, 0)

    def start(t, carry):
        src = h_ref.at[pl.ds(pl.multiple_of(t * ts, ts), ts), :]
        tok = base + t
        for slot, (e_ref, r_ref) in enumerate(((e1_ref, r1_ref), (e2_ref, r2_ref))):
            row = ps_ref[e_ref[tok]] + r_ref[tok]
            pltpu.make_async_copy(src, x_ref.at[pl.ds(pl.multiple_of(row * ts, ts), ts), :], sem).start()
            rd_ref[row] = 2 * tok + slot
        return carry

    lax.fori_loop(0, td, start, 0, unroll=8)
    for _ in range(2):
        pltpu.make_async_copy(h_ref, x_ref.at[pl.ds(0, td * ts), :], sem).wait()


def _dispatch(ri, pad_start, zero_blk, h2p, n_rows):
    t = ri.shape[1]
    td = min(512, t)
    ts = TOK_SUBLANES
    n_zero = jnp.sum(zero_blk, keepdims=True)
    return pl.pallas_call(
        _dispatch_kernel,
        grid_spec=pltpu.PrefetchScalarGridSpec(
            num_scalar_prefetch=7,
            grid=(t // td,),
            in_specs=[pl.BlockSpec((td * ts, LANES), lambda i, *_: (i, 0))],
            out_specs=[pl.BlockSpec(memory_space=pl.ANY), pl.BlockSpec(memory_space=pltpu.SMEM)],
            scratch_shapes=[pltpu.VMEM((MOE_BLOCK * ts, LANES), jnp.uint32),
                            pltpu.SemaphoreType.DMA(()), pltpu.SemaphoreType.DMA(())]),
        out_shape=[jax.ShapeDtypeStruct((n_rows * ts, LANES), jnp.uint32),
                   jax.ShapeDtypeStruct((n_rows,), jnp.int32)],
        compiler_params=_cparams(("arbitrary",)),
        name="dispatch",
    )(ri[0], ri[1], ri[2], ri[3], pad_start, zero_blk, n_zero, h2p)


def _moe_plan(counts, t):
    tm = MOE_BLOCK
    n_blocks = 2 * t // tm + N_EXPERTS
    padded = ((counts + tm - 1) // tm) * tm
    pad_end = jnp.cumsum(padded)
    pad_start = pad_end - padded
    blk = jnp.arange(n_blocks, dtype=jnp.int32)
    blk_e = jnp.minimum(jnp.sum(pad_end[None, :] <= (blk * tm)[:, None], axis=1), N_EXPERTS - 1).astype(jnp.int32)
    nb_real = (pad_end[-1:] // tm).astype(jnp.int32)
    changed = jnp.concatenate([jnp.ones((1,), bool), blk_e[1:] != blk_e[:-1]])
    new_e = jnp.logical_and(changed, blk < nb_real[0]).astype(jnp.int32)
    ex = jnp.arange(N_EXPERTS, dtype=jnp.int32)
    later = jnp.where((counts > 0)[None, :] & (ex[None, :] > ex[:, None]), ex[None, :], N_EXPERTS).min(axis=1)
    next_e = jnp.where(later < N_EXPERTS, later, -1).astype(jnp.int32)[blk_e]
    last_partial = jnp.any((blk[:, None] == (pad_end // tm - 1)[None, :]) & (counts % tm != 0)[None, :], axis=1)
    zero_blk = jnp.logical_or(last_partial, blk >= nb_real[0]).astype(jnp.int32)
    return pad_start, blk_e, new_e, next_e, nb_real, zero_blk, n_blocks * tm


COMBINE_ROWS = 64


def _combine_kernel(x1_ref, g2_ref, w1_ref, w2_ref, y_ref, o_ref):
    ts = TOK_SUBLANES
    rows = min(COMBINE_ROWS, x1_ref.shape[0])

    def chunk(c, carry):
        r = pl.ds(pl.multiple_of(c * rows, rows), rows)
        w1 = w1_ref[r, :]
        w2 = w2_ref[r, :]
        tile0 = pl.multiple_of(c * rows * 2 * ts, rows * 2 * ts)
        for s in range(ts):
            a_lo, a_hi = _unpack_pair(y_ref[pl.ds(tile0 + s, rows, stride=2 * ts), :])
            b_lo, b_hi = _unpack_pair(y_ref[pl.ds(tile0 + ts + s, rows, stride=2 * ts), :])
            for col, ya, yb in ((s, a_lo, b_lo), (s + ts, a_hi, b_hi)):
                cols = slice(col * LANES, (col + 1) * LANES)
                o_ref[r, cols] = x1_ref[r, cols] + g2_ref[0][:, cols] * (w1 * ya + w2 * yb)
        return carry

    lax.fori_loop(0, x1_ref.shape[0] // rows, chunk, 0)


def _combine(x1, g2, w1, w2, y2, seq):
    t, d = x1.shape
    tc = min(512, seq)
    return pl.pallas_call(
        _combine_kernel,
        grid=(t // tc,),
        in_specs=[pl.BlockSpec((tc, d), lambda i: (i, 0)),
                  pl.BlockSpec((1, 1, d), lambda i: (i * tc // seq, 0, 0)),
                  pl.BlockSpec((tc, 1), lambda i: (i, 0)),
                  pl.BlockSpec((tc, 1), lambda i: (i, 0)),
                  pl.BlockSpec((tc * 2 * TOK_SUBLANES, LANES), lambda i: (i, 0))],
        out_specs=pl.BlockSpec((tc, d), lambda i: (i, 0)),
        out_shape=jax.ShapeDtypeStruct((t, d), F32),
        compiler_params=_cparams(("arbitrary",)),
        name="combine",
    )(x1, g2, w1, w2, y2)


def _q_up_layout(w_q_up):
    w = w_q_up.reshape(Q_LORA, MLA_HEADS, MLA_QK)
    w = jnp.pad(w, ((0, 0), (0, 0), (0, MLA_QPAD - MLA_QK)))
    return w.reshape(Q_LORA, MLA_HEADS * MLA_QPAD).astype(BF16)


def _pad_lanes(g, width):
    return jnp.pad(g, (0, width - g.shape[0])).reshape(1, width)


def kernel(x, c, positions, w_ada, b_ada, norm1_g, w_in, hgrn_lb_logits, hgrn_onorm_g, q_a_norm_g, w_q_up,
           kv_a_norm_g, w_kv_up, q_norm_g, k_norm_g, attn_onorm_g, w_out, norm2_g, w_group, b_group,
           w_router, b_router, w_gate, w_up, w_down):
    bsz, seq, d = x.shape
    t = bsz * seq
    depth = w_ada.shape[0]
    half = MLA_ROPE // 2
    inv_freq = ROPE_BASE ** (-jnp.arange(0, MLA_ROPE, 2, dtype=F32) / MLA_ROPE)
    freq = jnp.stack([jnp.tile(inv_freq, LANES // half),
                      jnp.where(jnp.arange(LANES) < MLA_ROPE, 0.0, jnp.pi / 2).astype(F32)])
    pos3 = positions.reshape(bsz, seq, 1)
    tr = min(512, t)
    tri = jnp.triu(jnp.ones((tr, tr), BF16), 1)

    x2 = x.reshape(t, d)
    for l in range(depth):
        mod = _ada(c, w_ada[l], b_ada[l]).reshape(bsz, 6, 1, d)
        sh1, sc1, g1, sh2, sc2, g2 = (mod[:, i] for i in range(6))

        proj = _in_proj(_norm_mod(x2, norm1_g[l].reshape(1, d), sh1, sc1, seq), w_in[l].T, seq)
        proj3 = proj.reshape(bsz, seq, IN_COLS_PAD)

        o_a = _hgrn(proj3, hgrn_lb_logits, hgrn_onorm_g[l].reshape(1, HGRN_DK), l)

        q, k, v = _mla_up(proj3, pos3, _q_up_layout(w_q_up[l]), w_kv_up[l].astype(BF16),
                          q_a_norm_g[l].reshape(1, Q_LORA), kv_a_norm_g[l].reshape(1, KV_LORA),
                          _pad_lanes(q_norm_g[l], MLA_QPAD), _pad_lanes(k_norm_g[l], MLA_QPAD), freq)
        o_b = _attention(q, k, v, attn_onorm_g[l].reshape(MLA_V, 1))

        w_o = w_out[l].astype(BF16)
        wr = jnp.pad(jnp.concatenate([w_router[l], w_group[l]], axis=1),
                     ((0, 0), (0, LANES - N_EXPERTS - N_GROUPS))).astype(BF16)
        br = _pad_lanes(jnp.concatenate([b_router[l], b_group[l]]), LANES)
        x1, h2, lg_t = _out_proj(o_a.reshape(t, HGRN_WIDTH), o_b.reshape(t, HGRN_WIDTH), x2,
                                 w_o[:HGRN_WIDTH], w_o[HGRN_WIDTH:], g1, norm2_g[l].reshape(1, d),
                                 sh2, sc2, wr, br, seq)

        ri, rw, cnt = _route(lg_t, tri)
        counts = cnt[:, 0]
        pad_start, blk_e, new_e, next_e, nb_real, zero_blk, n_rows = _moe_plan(counts, t)
        x_sorted, row_dst = _dispatch(ri, pad_start, zero_blk, h2, n_rows)
        y2 = _moe(row_dst, blk_e, new_e, next_e, nb_real, x_sorted, w_gate[l], w_up[l], w_down[l], 2 * t + MOE_BLOCK)
        x2 = _combine(x1, g2, rw[0].reshape(t, 1), rw[1].reshape(t, 1), y2, seq)
    return x2.reshape(bsz, seq, d)
```

```python
import functools

import jax
import jax.numpy as jnp
from jax import lax
from jax.experimental import pallas as pl
from jax.experimental.pallas import tpu as pltpu

F32 = jnp.float32
BF16 = jnp.bfloat16
EPS = 1e-6
LOG2E = 1.4426950408889634

D_MODEL = 2048
HGRN_WIDTH = 1024
HGRN_DK = 128
HGRN_HEADS = 8
HGRN_CHUNK = 64
MLA_HEADS = 8
MLA_NOPE = 128
MLA_ROPE = 64
MLA_QK = MLA_NOPE + MLA_ROPE
MLA_V = 128
MLA_QPAD = 256
Q_LORA = 512
KV_LORA = 256
ROPE_BASE = 10000.0
IN_COLS = 4 * HGRN_WIDTH + Q_LORA + KV_LORA + MLA_ROPE
IN_COLS_PAD = 5120
N_GROUPS = 4
EXPERTS_PER_GROUP = 8
N_EXPERTS = 32
D_EXPERT = 512
ROUTE_ROWS = 40
MOE_BLOCK = 256
LANES = 128
TOK_SUBLANES = 8
VMEM_LIMIT = 56 * 1024 * 1024


def _cparams(sem):
    return pltpu.CompilerParams(dimension_semantics=sem, vmem_limit_bytes=VMEM_LIMIT)


def _dot(a, b):
    return jnp.dot(a, b, preferred_element_type=F32)


def _dot_nt(a, b):
    return lax.dot_general(a, b, (((1,), (1,)), ((), ())), preferred_element_type=F32)


def _rms(x, g):
    return x * lax.rsqrt(jnp.mean(x * x, axis=-1, keepdims=True) + EPS) * g


def _silu(x):
    return x * jax.nn.sigmoid(x)


def _pack_pair(lo, hi):
    lo_b = lax.bitcast_convert_type(lo.astype(BF16).astype(F32), jnp.uint32)
    hi_b = lax.bitcast_convert_type(hi.astype(BF16).astype(F32), jnp.uint32)
    return hi_b | (lo_b >> 16)


def _unpack_pair(w):
    lo = lax.bitcast_convert_type(w << 16, F32)
    hi = lax.bitcast_convert_type(w & jnp.uint32(0xFFFF0000), F32)
    return lo, hi


def _ada_kernel(c_ref, w_ref, b_ref, o_ref):
    ca = _silu(c_ref[...]).astype(BF16)
    o_ref[...] = _dot(ca, w_ref[...].astype(BF16)) + b_ref[...]


def _ada(c, w, b):
    bsz, d = c.shape
    n = w.shape[1]
    tn = 1024
    return pl.pallas_call(
        _ada_kernel,
        grid=(n // tn,),
        in_specs=[pl.BlockSpec((bsz, d), lambda j: (0, 0)),
                  pl.BlockSpec((d, tn), lambda j: (0, j)),
                  pl.BlockSpec((1, tn), lambda j: (0, j))],
        out_specs=pl.BlockSpec((bsz, tn), lambda j: (0, j)),
        out_shape=jax.ShapeDtypeStruct((bsz, n), F32),
        compiler_params=_cparams(("arbitrary",)),
        name="ada",
    )(c, w, b.reshape(1, n))


NORM_ROWS = 32


def _norm_kernel(x_ref, g_ref, sh_ref, sc_ref, h_ref):
    def body(c, carry):
        r = pl.ds(pl.multiple_of(c * NORM_ROWS, NORM_ROWS), NORM_ROWS)
        h = _rms(x_ref[r, :], g_ref[...]) * (1.0 + sc_ref[0]) + sh_ref[0]
        h_ref[r, :] = h.astype(BF16)
        return carry
    lax.fori_loop(0, x_ref.shape[0] // NORM_ROWS, body, 0, unroll=2)


def _norm_mod(x2, g, sh, sc, seq):
    t, d = x2.shape
    tm = min(512, seq)
    return pl.pallas_call(
        _norm_kernel,
        grid=(t // tm,),
        in_specs=[pl.BlockSpec((tm, d), lambda i: (i, 0)),
                  pl.BlockSpec((1, d), lambda i: (0, 0)),
                  pl.BlockSpec((1, 1, d), lambda i: (i * tm // seq, 0, 0)),
                  pl.BlockSpec((1, 1, d), lambda i: (i * tm // seq, 0, 0))],
        out_specs=pl.BlockSpec((tm, d), lambda i: (i, 0)),
        out_shape=jax.ShapeDtypeStruct((t, d), BF16),
        compiler_params=_cparams(("arbitrary",)),
        name="norm_mod",
    )(x2, g, sh, sc)


def _in_kernel(h_ref, w_ref, wt_ref, o_ref, wb_s, *, n_main):
    j = pl.program_id(0)
    first_row_tile = pl.program_id(1) == 0

    @pl.when(jnp.logical_and(first_row_tile, j < n_main))
    def _():
        wb_s[...] = w_ref[...].astype(BF16)

    @pl.when(jnp.logical_and(first_row_tile, j == n_main))
    def _():
        wb_s[...] = wt_ref[...]

    o_ref[...] = _dot_nt(h_ref[...], wb_s[...]).astype(BF16)


def _in_proj(h, w_t, seq):
    t, d = h.shape
    tn = 1024
    n_main = IN_COLS // tn
    w_tail = jnp.pad(w_t[n_main * tn:], ((0, (n_main + 1) * tn - IN_COLS), (0, 0))).astype(BF16)
    tm = min(1024, t)
    return pl.pallas_call(
        functools.partial(_in_kernel, n_main=n_main),
        grid=(IN_COLS_PAD // tn, t // tm),
        in_specs=[pl.BlockSpec((tm, d), lambda j, i: (i, 0)),
                  pl.BlockSpec((tn, d), lambda j, i: (jnp.minimum(j, n_main - 1), 0)),
                  pl.BlockSpec((tn, d), lambda j, i: (0, 0))],
        out_specs=pl.BlockSpec((tm, tn), lambda j, i: (i, j)),
        out_shape=jax.ShapeDtypeStruct((t, IN_COLS_PAD), BF16),
        scratch_shapes=[pltpu.VMEM((tn, d), BF16)],
        compiler_params=_cparams(("arbitrary", "arbitrary")),
        name="in_proj",
    )(h, w_t, w_tail)


HG_ROWS = 256


PIPELINE_STATIC_BLOCKS = 8


def _software_pipeline(stages, n_blocks):
    depth = len(stages)

    def step(i, static):
        for k in reversed(range(depth)):
            if static and not 0 <= i - k < n_blocks:
                continue
            stages[k](i - k)

    if n_blocks <= PIPELINE_STATIC_BLOCKS:
        for i in range(n_blocks + depth - 1):
            step(i, True)
        return
    for i in range(depth - 1):
        step(i, True)

    def steady(i, carry):
        step(i, False)
        return carry

    lax.fori_loop(depth - 1, n_blocks, steady, 0)
    for i in range(n_blocks, n_blocks + depth - 1):
        step(i, True)


def _chunk_mask(n):
    row = lax.broadcasted_iota(jnp.int32, (n, n), 0)
    col = lax.broadcasted_iota(jnp.int32, (n, n), 1)
    return jnp.logical_and(row // HGRN_CHUNK == col // HGRN_CHUNK, col <= row)


def _hgrn_kernel(q_ref, f_ref, i_ref, g_ref, lbl_ref, og_ref, o_ref,
                 qb_s, u_s, sp_s, dec_s, oi_s, b_s, k_s, qin_s, kin_s, ku_s, a_s, st_s, *, layer):
    seq = q_ref.shape[1]
    cs = HGRN_CHUNK
    rb = min(HG_ROWS, seq)
    nc = rb // cs
    dk = HGRN_DK
    lg = lbl_ref[...]
    ex = jnp.exp(lg - jnp.max(lg, axis=0, keepdims=True))
    sm = ex / jnp.sum(ex, axis=0, keepdims=True)
    lb = jnp.sum(sm[0:layer + 1], axis=0, keepdims=True)
    mask = _chunk_mask(rb)
    tri = mask.astype(BF16)
    row_chunk = lax.broadcasted_iota(jnp.int32, (rb, dk), 0) // cs

    chunk_sel = [(row_chunk == c).astype(BF16) for c in range(nc)]

    def rows(blk):
        return pl.ds(pl.multiple_of(blk * rb, rb), rb)

    def stage1(blk):
        r = rows(blk)
        f = lb + (1.0 - lb) * jax.nn.sigmoid(f_ref[0, r, :].astype(F32))
        lf = jnp.log(f)
        k_s[r, :] = 1.0 - f
        hi = lf.astype(BF16)
        r1 = lf - hi.astype(F32)
        mid = r1.astype(BF16)
        lo = (r1 - mid.astype(F32)).astype(BF16)
        bhm = _dot(tri, jnp.concatenate([hi, mid], axis=1))
        b_s[r, :] = bhm[:, 0:dk] + bhm[:, dk:] + _dot(tri, lo)

    def stage2(blk):
        r = rows(blk)
        b3 = b_s[r, :].reshape(nc, cs, dk)
        bmid = b3[:, cs // 2 - 1:cs // 2, :]
        blast = b3[:, cs - 1:cs, :]
        q3 = (q_ref[0, r, :].astype(F32) * dk ** -0.5).reshape(nc, cs, dk)
        k3 = k_s[r, :].reshape(nc, cs, dk)
        qin_s[r, :] = (q3 * jnp.exp(b3 - bmid)).reshape(rb, dk).astype(BF16)
        kin_s[r, :] = (k3 * jnp.exp(bmid - b3)).reshape(rb, dk).astype(BF16)
        ku_s[r, :] = (k3 * jnp.exp(blast - b3)).reshape(rb, dk).astype(BF16)
        qb_s[r, :] = (q3 * jnp.exp(b3)).reshape(rb, dk).astype(BF16)
        dec_s[pl.ds(blk * nc, nc)] = jnp.exp(blast)

    def stage3(blk):
        r = rows(blk)
        a_s[r, :] = jnp.where(mask, _dot_nt(qin_s[r, :], kin_s[r, :]), 0.0).astype(BF16)
        vt = i_ref[0, r, :].astype(F32).T.astype(BF16)
        ku = ku_s[r, :]
        ut = _dot(vt, jnp.concatenate([ku * sel for sel in chunk_sel], axis=1))
        for c in range(nc):
            u_s[blk * nc + c] = ut[:, c * dk:(c + 1) * dk]

    def stage4(blk):
        r = rows(blk)
        oi_s[r, :] = _dot(a_s[r, :], i_ref[0, r, :])
        st = st_s[...]
        for c in range(nc):
            sp_s[blk * nc + c] = st.astype(BF16)
            st = st * dec_s[blk * nc + c] + u_s[blk * nc + c]
        st_s[...] = st

    def stage5(blk):
        r = rows(blk)
        inter = [_dot_nt(qb_s[pl.ds(pl.multiple_of(blk * rb + c * cs, cs), cs), :], sp_s[blk * nc + c])
                 for c in range(nc)]
        o = oi_s[r, :] + jnp.concatenate(inter, axis=0)
        o = _rms(o, og_ref[...]) * _silu(g_ref[0, r, :].astype(F32))
        o_ref[0, r, :] = o.astype(BF16)

    st_s[...] = jnp.zeros_like(st_s)
    _software_pipeline([stage1, stage2, stage3, stage4, stage5], seq // rb)


def _hgrn(proj3, lb_logits, onorm_g, layer):
    bsz, seq, _ = proj3.shape
    nh = HGRN_HEADS

    def col(off):
        return pl.BlockSpec((1, seq, HGRN_DK), lambda b, h: (b, 0, off * nh + h))

    nl = lb_logits.shape[0]
    return pl.pallas_call(
        functools.partial(_hgrn_kernel, layer=layer),
        grid=(bsz, nh),
        in_specs=[col(0), col(1), col(2), col(3),
                  pl.BlockSpec((nl, HGRN_DK), lambda b, h: (0, h)),
                  pl.BlockSpec((1, HGRN_DK), lambda b, h: (0, 0))],
        out_specs=pl.BlockSpec((1, seq, HGRN_DK), lambda b, h: (b, 0, h)),
        out_shape=jax.ShapeDtypeStruct((bsz, seq, HGRN_WIDTH), BF16),
        scratch_shapes=[pltpu.VMEM((seq, HGRN_DK), BF16),
                        pltpu.VMEM((seq // HGRN_CHUNK, HGRN_DK, HGRN_DK), F32),
                        pltpu.VMEM((seq // HGRN_CHUNK, HGRN_DK, HGRN_DK), BF16),
                        pltpu.VMEM((seq // HGRN_CHUNK, 1, HGRN_DK), F32),
                        pltpu.VMEM((seq, HGRN_DK), F32),
                        pltpu.VMEM((seq, HGRN_DK), F32), pltpu.VMEM((seq, HGRN_DK), F32),
                        pltpu.VMEM((seq, HGRN_DK), BF16), pltpu.VMEM((seq, HGRN_DK), BF16),
                        pltpu.VMEM((seq, HGRN_DK), BF16), pltpu.VMEM((seq, min(HG_ROWS, seq)), BF16),
                        pltpu.VMEM((HGRN_DK, HGRN_DK), F32)],
        compiler_params=_cparams(("arbitrary", "arbitrary")),
        name="hgrn",
    )(proj3, proj3, proj3, proj3, lb_logits, onorm_g)


def _rope(x, cos, sin_signed, lane):
    half = MLA_ROPE // 2
    swapped = jnp.where(lane < half, pltpu.roll(x, LANES - half, 1), pltpu.roll(x, half, 1))
    return x * cos + swapped * sin_signed


UP_ROWS = 1024


def _up_kernel(p_ref, pos_ref, wq_ref, wkv_ref, qag_ref, kvag_ref, qg_ref, kg_ref, freq_ref,
               q_ref, k_ref, v_ref):
    rows = min(UP_ROWS, p_ref.shape[0])
    lane = lax.broadcasted_iota(jnp.int32, (rows, LANES), 1)
    half = MLA_ROPE // 2
    valid = lane < MLA_ROPE
    qg = qg_ref[...] * (MLA_QK ** -0.5 * LOG2E)
    kg = kg_ref[...]

    def chunk(c, carry):
        r = pl.ds(pl.multiple_of(c * rows, rows), rows)
        p = p_ref[r, :].astype(F32)
        q_a = p[:, 0:Q_LORA]
        kv_a = p[:, Q_LORA:Q_LORA + KV_LORA]
        k_pe = p[:, Q_LORA + KV_LORA:Q_LORA + KV_LORA + LANES]
        qa_n = _rms(q_a, qag_ref[...]).astype(BF16)
        kva_n = _rms(kv_a, kvag_ref[...]).astype(BF16)
        sin_all = jnp.sin(pos_ref[r, :].astype(F32) * freq_ref[0:1, :] + freq_ref[1:2, :])
        cos = jnp.where(valid, pltpu.roll(sin_all, MLA_ROPE, 1), 0.0)
        sin_signed = jnp.where(valid, jnp.where(lane < half, -sin_all, sin_all), 0.0)
        kpe_ss = jnp.sum(k_pe * k_pe, axis=-1, keepdims=True)
        kpe_rot = _rope(k_pe * kg[:, MLA_NOPE:], cos, sin_signed, lane)
        for h in range(MLA_HEADS):
            cols = slice(h * MLA_QPAD, (h + 1) * MLA_QPAD)
            qh = _dot(qa_n, wq_ref[:, cols])
            qn = qh * lax.rsqrt(jnp.sum(qh * qh, axis=-1, keepdims=True) / MLA_QK + EPS) * qg
            q_ref[0, h, r, 0:MLA_NOPE] = qn[:, 0:MLA_NOPE].astype(BF16)
            q_ref[0, h, r, MLA_NOPE:] = _rope(qn[:, MLA_NOPE:], cos, sin_signed, lane).astype(BF16)
            kvh = _dot(kva_n, wkv_ref[:, cols])
            kn = kvh[:, 0:MLA_NOPE]
            rk = lax.rsqrt((jnp.sum(kn * kn, axis=-1, keepdims=True) + kpe_ss) / MLA_QK + EPS)
            k_ref[0, h, r, 0:MLA_NOPE] = (kn * rk * kg[:, 0:MLA_NOPE]).astype(BF16)
            k_ref[0, h, r, MLA_NOPE:] = (kpe_rot * rk).astype(BF16)
            v_ref[0, h, :, r] = kvh[:, MLA_NOPE:].T.astype(BF16)
        return carry

    lax.fori_loop(0, p_ref.shape[0] // rows, chunk, 0)


def _mla_up(proj3, pos3, wq, wkv, qag, kvag, qg, kg, freq):
    bsz, seq, _ = proj3.shape
    tm = min(UP_ROWS, seq)
    nh = MLA_HEADS
    mla_block = 4 * HGRN_WIDTH // 1024

    def const(shape):
        return pl.BlockSpec(shape, lambda b, i: (0,) * len(shape))

    return pl.pallas_call(
        _up_kernel,
        grid=(bsz, seq // tm),
        in_specs=[pl.BlockSpec((None, tm, 1024), lambda b, i: (b, i, mla_block)),
                  pl.BlockSpec((None, tm, 1), lambda b, i: (b, i, 0)),
                  const(wq.shape), const(wkv.shape), const(qag.shape), const(kvag.shape),
                  const(qg.shape), const(kg.shape), const(freq.shape)],
        out_specs=[pl.BlockSpec((1, nh, tm, MLA_QPAD), lambda b, i: (b, 0, i, 0)),
                   pl.BlockSpec((1, nh, tm, MLA_QPAD), lambda b, i: (b, 0, i, 0)),
                   pl.BlockSpec((1, nh, MLA_V, tm), lambda b, i: (b, 0, 0, i))],
        out_shape=[jax.ShapeDtypeStruct((bsz, nh, seq, MLA_QPAD), BF16),
                   jax.ShapeDtypeStruct((bsz, nh, seq, MLA_QPAD), BF16),
                   jax.ShapeDtypeStruct((bsz, nh, MLA_V, seq), BF16)],
        compiler_params=_cparams(("arbitrary", "arbitrary")),
        name="mla_up",
    )(proj3, pos3, wq, wkv, qag, kvag, qg, kg, freq)


ATT_T = 1024


def _attn_kernel(q_ref, k_ref, vt_ref, g_ref, o_ref):
    seq = q_ref.shape[2]
    t = min(ATT_T, seq)
    key = lax.broadcasted_iota(jnp.int32, (t, t), 0)
    qry = lax.broadcasted_iota(jnp.int32, (t, t), 1)
    causal = key <= qry
    neg = jnp.finfo(F32).min
    for qi in range(seq // t):
        off = qi * t
        q = q_ref[0, 0, off:off + t, :]
        sd = jnp.where(causal, _dot_nt(k_ref[0, 0, off:off + t, :], q), neg)
        m = jnp.max(sd, axis=0, keepdims=True)
        if qi > 0:
            so = _dot_nt(k_ref[0, 0, 0:off, :], q)
            m = jnp.maximum(m, jnp.max(so, axis=0, keepdims=True))
        pd = jnp.exp2(sd - m)
        l = jnp.sum(pd, axis=0, keepdims=True)
        ot = _dot(vt_ref[0, 0, :, off:off + t], pd.astype(BF16))
        if qi > 0:
            po = jnp.exp2(so - m)
            l = l + jnp.sum(po, axis=0, keepdims=True)
            ot = ot + _dot(vt_ref[0, 0, :, 0:off], po.astype(BF16))
        ot = ot * (1.0 / l)
        ot = ot * lax.rsqrt(jnp.mean(ot * ot, axis=0, keepdims=True) + EPS) * g_ref[...]
        o_ref[0, off:off + t, :] = ot.T.astype(BF16)


def _attention(q, k, v, g):
    bsz, nh, seq, _ = q.shape
    return pl.pallas_call(
        _attn_kernel,
        grid=(bsz, nh),
        in_specs=[pl.BlockSpec((1, 1, seq, MLA_QPAD), lambda b, h: (b, h, 0, 0)),
                  pl.BlockSpec((1, 1, seq, MLA_QPAD), lambda b, h: (b, h, 0, 0)),
                  pl.BlockSpec((1, 1, MLA_V, seq), lambda b, h: (b, h, 0, 0)),
                  pl.BlockSpec((MLA_V, 1), lambda b, h: (0, 0))],
        out_specs=pl.BlockSpec((1, seq, MLA_V), lambda b, h: (b, 0, h)),
        out_shape=jax.ShapeDtypeStruct((bsz, seq, nh * MLA_V), BF16),
        compiler_params=_cparams(("arbitrary", "arbitrary")),
        name="attn",
    )(q, k, v, g)


def _out_kernel(oa_ref, ob_ref, x_ref, wa_ref, wb_ref, g1_ref, n2g_ref, sh2_ref, sc2_ref,
                wr_ref, br_ref, x1_ref, h2_ref, lg_ref, mix_s):
    @pl.when(pl.program_id(0) == 0)
    def _():
        mix_s[...] = jnp.zeros_like(mix_s)

    mix_next = _dot(oa_ref[...], wa_ref[...]) + _dot(ob_ref[...], wb_ref[...])
    x1 = x_ref[...] + g1_ref[0] * mix_s[...]
    x1_ref[...] = x1
    h2 = _rms(x1, n2g_ref[...]) * (1.0 + sc2_ref[0]) + sh2_ref[0]
    tm = h2.shape[0]
    for s in range(TOK_SUBLANES):
        h2_ref[pl.ds(s, tm, stride=TOK_SUBLANES), :] = _pack_pair(
            h2[:, s * LANES:(s + 1) * LANES], h2[:, (s + TOK_SUBLANES) * LANES:(s + TOK_SUBLANES + 1) * LANES])
    lg = _dot(h2.astype(BF16), wr_ref[...]) + br_ref[...]
    lg_ref[...] = lg.T[0:ROUTE_ROWS, :]
    mix_s[...] = mix_next


def _out_proj(oa, ob, x2, wa, wb, g1, n2g, sh2, sc2, wr, br, seq):
    t, d = x2.shape
    tm = min(512, seq)
    n_tiles = t // tm

    def const(shape):
        return pl.BlockSpec(shape, lambda i: (0,) * len(shape))

    def prev(i):
        return jnp.maximum(i - 1, 0)

    def per_batch():
        return pl.BlockSpec((1, 1, d), lambda i: (prev(i) * tm // seq, 0, 0))

    return pl.pallas_call(
        _out_kernel,
        grid=(n_tiles + 1,),
        in_specs=[pl.BlockSpec((tm, HGRN_WIDTH), lambda i: (jnp.minimum(i, n_tiles - 1), 0)),
                  pl.BlockSpec((tm, HGRN_WIDTH), lambda i: (jnp.minimum(i, n_tiles - 1), 0)),
                  pl.BlockSpec((tm, d), lambda i: (prev(i), 0)),
                  const(wa.shape), const(wb.shape), per_batch(), const(n2g.shape),
                  per_batch(), per_batch(), const(wr.shape), const(br.shape)],
        out_specs=[pl.BlockSpec((tm, d), lambda i: (prev(i), 0)),
                   pl.BlockSpec((tm * TOK_SUBLANES, LANES), lambda i: (prev(i), 0)),
                   pl.BlockSpec((ROUTE_ROWS, tm), lambda i: (0, prev(i)))],
        out_shape=[jax.ShapeDtypeStruct((t, d), F32),
                   jax.ShapeDtypeStruct((t * TOK_SUBLANES, LANES), jnp.uint32),
                   jax.ShapeDtypeStruct((ROUTE_ROWS, t), F32)],
        scratch_shapes=[pltpu.VMEM((tm, d), F32)],
        compiler_params=_cparams(("arbitrary",)),
        name="out_proj",
    )(oa, ob, x2, wa, wb, g1, n2g, sh2, sc2, wr, br)


def _route_kernel(lg_ref, tri_ref, ri_ref, rw_ref, cnt_ref, carry_s):
    step = pl.program_id(0)

    @pl.when(step == 0)
    def _():
        carry_s[...] = jnp.zeros_like(carry_s)

    lg = lg_ref[...]
    tr = lg.shape[1]
    epg = EXPERTS_PER_GROUP
    gl = lg[N_EXPERTS:N_EXPERTS + N_GROUPS, :]
    row_g = lax.broadcasted_iota(jnp.int32, (N_GROUPS, tr), 0)
    gmax = jnp.max(gl, axis=0, keepdims=True)
    g_sel = jnp.min(jnp.where(gl == gmax, row_g, N_GROUPS), axis=0, keepdims=True)
    p_group = 1.0 / jnp.sum(jnp.exp(gl - gmax), axis=0, keepdims=True)

    e_in = lg[0:epg, :]
    for g in range(1, N_GROUPS):
        e_in = jnp.where(g_sel == g, lg[g * epg:(g + 1) * epg, :], e_in)
    row_e = lax.broadcasted_iota(jnp.int32, (epg, tr), 0)
    top1 = jnp.max(e_in, axis=0, keepdims=True)
    i1 = jnp.min(jnp.where(e_in == top1, row_e, epg), axis=0, keepdims=True)
    rest = jnp.where(row_e == i1, -jnp.inf, e_in)
    top2 = jnp.max(rest, axis=0, keepdims=True)
    i2 = jnp.min(jnp.where(rest == top2, row_e, epg), axis=0, keepdims=True)
    e2w = jnp.exp(top2 - top1)
    w1 = p_group / (1.0 + e2w)
    w2 = p_group * e2w / (1.0 + e2w)
    ex1 = g_sel * epg + i1
    ex2 = g_sel * epg + i2

    row_x = lax.broadcasted_iota(jnp.int32, (N_EXPERTS, tr), 0)
    oh1 = row_x == ex1
    oh2 = row_x == ex2
    oh = jnp.logical_or(oh1, oh2)
    before = _dot(oh.astype(BF16), tri_ref[...]) + carry_s[:, 0:1]
    rank1 = jnp.sum(jnp.where(oh1, before, 0.0), axis=0, keepdims=True)
    rank2 = jnp.sum(jnp.where(oh2, before, 0.0), axis=0, keepdims=True)
    carry_s[...] = carry_s[...] + jnp.sum(oh.astype(F32), axis=1, keepdims=True)

    zi = jnp.zeros((4, tr), jnp.int32)
    ri_ref[...] = jnp.concatenate([ex1, ex2, rank1.astype(jnp.int32), rank2.astype(jnp.int32), zi], axis=0)
    rw_ref[...] = jnp.concatenate([w1, w2, jnp.zeros((6, tr), F32)], axis=0)
    cnt_ref[...] = carry_s[...].astype(jnp.int32)


def _route(lg_t, tri):
    t = lg_t.shape[1]
    tr = tri.shape[0]
    return pl.pallas_call(
        _route_kernel,
        grid=(t // tr,),
        in_specs=[pl.BlockSpec((ROUTE_ROWS, tr), lambda i: (0, i)),
                  pl.BlockSpec((tr, tr), lambda i: (0, 0))],
        out_specs=[pl.BlockSpec((8, tr), lambda i: (0, i)),
                   pl.BlockSpec((8, tr), lambda i: (0, i)),
                   pl.BlockSpec((N_EXPERTS, LANES), lambda i: (0, 0))],
        out_shape=[jax.ShapeDtypeStruct((8, t), jnp.int32),
                   jax.ShapeDtypeStruct((8, t), F32),
                   jax.ShapeDtypeStruct((N_EXPERTS, LANES), jnp.int32)],
        scratch_shapes=[pltpu.VMEM((N_EXPERTS, LANES), F32)],
        compiler_params=_cparams(("arbitrary",)),
        name="route",
    )(lg_t, tri)


def _moe_kernel(rd_ref, be_ref, ne_ref, nx_ref, nb_ref, x_ref, wg_ref, wu_ref, wd_ref, y_ref,
                ys, wg_f, wu_f, wd_f, wg_s, wu_s, wd_s, ssem, wsem):
    b = pl.program_id(0)
    last = pl.num_programs(0) - 1
    nb = nb_ref[0]
    tm = MOE_BLOCK
    ts = TOK_SUBLANES
    weights = ((wg_ref, wg_f, wg_s), (wu_ref, wu_f, wu_s), (wd_ref, wd_f, wd_s))

    def fetch(e):
        return [pltpu.make_async_copy(src.at[e], stage, wsem.at[k]) for k, (src, stage, _) in enumerate(weights)]

    def scatter(blk):
        base = blk * tm
        for j in range(tm):
            dst = pl.multiple_of(rd_ref[base + j] * ts, ts)
            pltpu.make_async_copy(ys.at[pl.ds(j * ts, ts), :], y_ref.at[pl.ds(dst, ts), :], ssem).start()

    def wait_scatter():
        pltpu.make_async_copy(ys, y_ref.at[pl.ds(0, tm * ts), :], ssem).wait()

    @pl.when(b == 0)
    def _():
        for copy in fetch(be_ref[0]):
            copy.start()
        ys[...] = jnp.zeros_like(ys)
        spare = pltpu.make_async_copy(ys, y_ref.at[pl.ds(y_ref.shape[0] - tm * ts, tm * ts), :], ssem)
        spare.start()
        spare.wait()

    @pl.when(ne_ref[b] == 1)
    def _():
        for copy in fetch(0):
            copy.wait()
        for _, stage, dst in weights:
            dst[...] = stage[...].astype(BF16)

        @pl.when(nx_ref[b] >= 0)
        def _():
            for copy in fetch(nx_ref[b]):
                copy.start()

    @pl.when(b < nb)
    def _():
        scatter(jnp.maximum(b - 1, 0))
        parts = [_unpack_pair(x_ref[pl.ds(s, tm, stride=ts), :]) for s in range(ts)]
        x = jnp.concatenate([p[0] for p in parts] + [p[1] for p in parts], axis=1).astype(BF16)
        hid = _silu(_dot(x, wg_s[...])) * _dot(x, wu_s[...])
        y = _dot(hid.astype(BF16), wd_s[...])
        wait_scatter()
        for s in range(ts):
            ys[pl.ds(s, tm, stride=ts), :] = _pack_pair(y[:, s * LANES:(s + 1) * LANES],
                                                        y[:, (s + ts) * LANES:(s + ts + 1) * LANES])

        @pl.when(b == last)
        def _():
            scatter(b)
            wait_scatter()

    @pl.when(b == nb)
    def _():
        scatter(b - 1)
        wait_scatter()


def _moe(row_dst, blk_e, new_e, next_e, nb_real, x_sorted, wg, wu, wd, n_out_rows):
    n_blocks = blk_e.shape[0]
    d, de = wg.shape[1], wg.shape[2]
    tm = MOE_BLOCK
    hbm = pl.BlockSpec(memory_space=pl.ANY)
    return pl.pallas_call(
        _moe_kernel,
        grid_spec=pltpu.PrefetchScalarGridSpec(
            num_scalar_prefetch=5,
            grid=(n_blocks,),
            in_specs=[pl.BlockSpec((tm * TOK_SUBLANES, LANES),
                                   lambda b, rd, be, ne, nx, nb: (jnp.minimum(b, nb[0] - 1), 0)),
                      hbm, hbm, hbm],
            out_specs=hbm,
            scratch_shapes=[pltpu.VMEM((tm * TOK_SUBLANES, LANES), jnp.uint32),
                            pltpu.VMEM((d, de), F32), pltpu.VMEM((d, de), F32), pltpu.VMEM((de, d), F32),
                            pltpu.VMEM((d, de), BF16), pltpu.VMEM((d, de), BF16), pltpu.VMEM((de, d), BF16),
                            pltpu.SemaphoreType.DMA(()), pltpu.SemaphoreType.DMA((3,))]),
        out_shape=jax.ShapeDtypeStruct((n_out_rows * TOK_SUBLANES, LANES), jnp.uint32),
        compiler_params=_cparams(("arbitrary",)),
        name="moe",
    )(row_dst, blk_e, new_e, next_e, nb_real, x_sorted, wg, wu, wd)


def _dispatch_kernel(e1_ref, e2_ref, r1_ref, r2_ref, ps_ref, zf_ref, nz_ref, h_ref, x_ref, rd_ref, zbuf, sem, zsem):
    i = pl.program_id(0)
    ts = TOK_SUBLANES
    td = h_ref.shape[0] // ts
    n_tok = e1_ref.shape[0]
    base = i * td
    blk_rows = MOE_BLOCK * ts

    def zero_fill(blk):
        return pltpu.make_async_copy(
            zbuf, x_ref.at[pl.ds(pl.multiple_of(blk * blk_rows, blk_rows), blk_rows), :], zsem)

    @pl.when(i == 0)
    def _():
        zbuf[...] = jnp.zeros_like(zbuf)

        def fill(blk, carry):
            @pl.when(zf_ref[blk] == 1)
            def _():
                zero_fill(blk).start()
            return carry

        lax.fori_loop(0, zf_ref.shape[0], fill, 0)

        def init(blk, carry):
            for j in range(MOE_BLOCK):
                rd_ref[blk * MOE_BLOCK + j] = 2 * n_tok + j
            return carry

        lax.fori_loop(0, rd_ref.shape[0] // MOE_BLOCK, init, 0)

        def drain(k, carry):
            zero_fill(0).wait()
            return carry

        lax.fori_loop(0, nz_ref[0], drain, 0)

    def start(t, carry):
        src = h_ref.at[pl.ds(pl.multiple_of(t * ts, ts), ts), :]
        tok = base + t
        for slot, (e_ref, r_ref) in enumerate(((e1_ref, r1_ref), (e2_ref, r2_ref))):
            row = ps_ref[e_ref[tok]] + r_ref[tok]
            pltpu.make_async_copy(src, x_ref.at[pl.ds(pl.multiple_of(row * ts, ts), ts), :], sem).start()
            rd_ref[row] = 2 * tok + slot
        return carry

    lax.fori_loop(0, td, start, 0, unroll=8)
    for _ in range(2):
        pltpu.make_async_copy(h_ref, x_ref.at[pl.ds(0, td * ts), :], sem).wait()


def _dispatch(ri, pad_start, zero_blk, h2p, n_rows):
    t = ri.shape[1]
    td = min(512, t)
    ts = TOK_SUBLANES
    n_zero = jnp.sum(zero_blk, keepdims=True)
    return pl.pallas_call(
        _dispatch_kernel,
        grid_spec=pltpu.PrefetchScalarGridSpec(
            num_scalar_prefetch=7,
            grid=(t // td,),
            in_specs=[pl.BlockSpec((td * ts, LANES), lambda i, *_: (i, 0))],
            out_specs=[pl.BlockSpec(memory_space=pl.ANY), pl.BlockSpec(memory_space=pltpu.SMEM)],
            scratch_shapes=[pltpu.VMEM((MOE_BLOCK * ts, LANES), jnp.uint32),
                            pltpu.SemaphoreType.DMA(()), pltpu.SemaphoreType.DMA(())]),
        out_shape=[jax.ShapeDtypeStruct((n_rows * ts, LANES), jnp.uint32),
                   jax.ShapeDtypeStruct((n_rows,), jnp.int32)],
        compiler_params=_cparams(("arbitrary",)),
        name="dispatch",
    )(ri[0], ri[1], ri[2], ri[3], pad_start, zero_blk, n_zero, h2p)


def _moe_plan(counts, t):
    tm = MOE_BLOCK
    n_blocks = 2 * t // tm + N_EXPERTS
    padded = ((counts + tm - 1) // tm) * tm
    pad_end = jnp.cumsum(padded)
    pad_start = pad_end - padded
    blk = jnp.arange(n_blocks, dtype=jnp.int32)
    blk_e = jnp.minimum(jnp.sum(pad_end[None, :] <= (blk * tm)[:, None], axis=1), N_EXPERTS - 1).astype(jnp.int32)
    nb_real = (pad_end[-1:] // tm).astype(jnp.int32)
    changed = jnp.concatenate([jnp.ones((1,), bool), blk_e[1:] != blk_e[:-1]])
    new_e = jnp.logical_and(changed, blk < nb_real[0]).astype(jnp.int32)
    ex = jnp.arange(N_EXPERTS, dtype=jnp.int32)
    later = jnp.where((counts > 0)[None, :] & (ex[None, :] > ex[:, None]), ex[None, :], N_EXPERTS).min(axis=1)
    next_e = jnp.where(later < N_EXPERTS, later, -1).astype(jnp.int32)[blk_e]
    last_partial = jnp.any((blk[:, None] == (pad_end // tm - 1)[None, :]) & (counts % tm != 0)[None, :], axis=1)
    zero_blk = jnp.logical_or(last_partial, blk >= nb_real[0]).astype(jnp.int32)
    return pad_start, blk_e, new_e, next_e, nb_real, zero_blk, n_blocks * tm


COMBINE_ROWS = 64


def _combine_kernel(x1_ref, g2_ref, w1_ref, w2_ref, y_ref, o_ref):
    ts = TOK_SUBLANES
    rows = min(COMBINE_ROWS, x1_ref.shape[0])

    def chunk(c, carry):
        r = pl.ds(pl.multiple_of(c * rows, rows), rows)
        w1 = w1_ref[r, :]
        w2 = w2_ref[r, :]
        tile0 = pl.multiple_of(c * rows * 2 * ts, rows * 2 * ts)
        for s in range(ts):
            a_lo, a_hi = _unpack_pair(y_ref[pl.ds(tile0 + s, rows, stride=2 * ts), :])
            b_lo, b_hi = _unpack_pair(y_ref[pl.ds(tile0 + ts + s, rows, stride=2 * ts), :])
            for col, ya, yb in ((s, a_lo, b_lo), (s + ts, a_hi, b_hi)):
                cols = slice(col * LANES, (col + 1) * LANES)
                o_ref[r, cols] = x1_ref[r, cols] + g2_ref[0][:, cols] * (w1 * ya + w2 * yb)
        return carry

    lax.fori_loop(0, x1_ref.shape[0] // rows, chunk, 0)


def _combine(x1, g2, w1, w2, y2, seq):
    t, d = x1.shape
    tc = min(512, seq)
    return pl.pallas_call(
        _combine_kernel,
        grid=(t // tc,),
        in_specs=[pl.BlockSpec((tc, d), lambda i: (i, 0)),
                  pl.BlockSpec((1, 1, d), lambda i: (i * tc // seq, 0, 0)),
                  pl.BlockSpec((tc, 1), lambda i: (i, 0)),
                  pl.BlockSpec((tc, 1), lambda i: (i, 0)),
                  pl.BlockSpec((tc * 2 * TOK_SUBLANES, LANES), lambda i: (i, 0))],
        out_specs=pl.BlockSpec((tc, d), lambda i: (i, 0)),
        out_shape=jax.ShapeDtypeStruct((t, d), F32),
        compiler_params=_cparams(("arbitrary",)),
        name="combine",
    )(x1, g2, w1, w2, y2)


def _q_up_layout(w_q_up):
    w = w_q_up.reshape(Q_LORA, MLA_HEADS, MLA_QK)
    w = jnp.pad(w, ((0, 0), (0, 0), (0, MLA_QPAD - MLA_QK)))
    return w.reshape(Q_LORA, MLA_HEADS * MLA_QPAD).astype(BF16)


def _pad_lanes(g, width):
    return jnp.pad(g, (0, width - g.shape[0])).reshape(1, width)


def kernel(x, c, positions, w_ada, b_ada, norm1_g, w_in, hgrn_lb_logits, hgrn_onorm_g, q_a_norm_g, w_q_up,
           kv_a_norm_g, w_kv_up, q_norm_g, k_norm_g, attn_onorm_g, w_out, norm2_g, w_group, b_group,
           w_router, b_router, w_gate, w_up, w_down):
    bsz, seq, d = x.shape
    t = bsz * seq
    depth = w_ada.shape[0]
    half = MLA_ROPE // 2
    inv_freq = ROPE_BASE ** (-jnp.arange(0, MLA_ROPE, 2, dtype=F32) / MLA_ROPE)
    freq = jnp.stack([jnp.tile(inv_freq, LANES // half),
                      jnp.where(jnp.arange(LANES) < MLA_ROPE, 0.0, jnp.pi / 2).astype(F32)])
    pos3 = positions.reshape(bsz, seq, 1)
    tr = min(512, t)
    tri = jnp.triu(jnp.ones((tr, tr), BF16), 1)

    x2 = x.reshape(t, d)
    for l in range(depth):
        mod = _ada(c, w_ada[l], b_ada[l]).reshape(bsz, 6, 1, d)
        sh1, sc1, g1, sh2, sc2, g2 = (mod[:, i] for i in range(6))

        proj = _in_proj(_norm_mod(x2, norm1_g[l].reshape(1, d), sh1, sc1, seq), w_in[l].T, seq)
        proj3 = proj.reshape(bsz, seq, IN_COLS_PAD)

        o_a = _hgrn(proj3, hgrn_lb_logits, hgrn_onorm_g[l].reshape(1, HGRN_DK), l)

        q, k, v = _mla_up(proj3, pos3, _q_up_layout(w_q_up[l]), w_kv_up[l].astype(BF16),
                          q_a_norm_g[l].reshape(1, Q_LORA), kv_a_norm_g[l].reshape(1, KV_LORA),
                          _pad_lanes(q_norm_g[l], MLA_QPAD), _pad_lanes(k_norm_g[l], MLA_QPAD), freq)
        o_b = _attention(q, k, v, attn_onorm_g[l].reshape(MLA_V, 1))

        w_o = w_out[l].astype(BF16)
        wr = jnp.pad(jnp.concatenate([w_router[l], w_group[l]], axis=1),
                     ((0, 0), (0, LANES - N_EXPERTS - N_GROUPS))).astype(BF16)
        br = _pad_lanes(jnp.concatenate([b_router[l], b_group[l]]), LANES)
        x1, h2, lg_t = _out_proj(o_a.reshape(t, HGRN_WIDTH), o_b.reshape(t, HGRN_WIDTH), x2,
                                 w_o[:HGRN_WIDTH], w_o[HGRN_WIDTH:], g1, norm2_g[l].reshape(1, d),
                                 sh2, sc2, wr, br, seq)

        ri, rw, cnt = _route(lg_t, tri)
        counts = cnt[:, 0]
        pad_start, blk_e, new_e, next_e, nb_real, zero_blk, n_rows = _moe_plan(counts, t)
        x_sorted, row_dst = _dispatch(ri, pad_start, zero_blk, h2, n_rows)
        y2 = _moe(row_dst, blk_e, new_e, next_e, nb_real, x_sorted, w_gate[l], w_up[l], w_down[l], 2 * t + MOE_BLOCK)
        x2 = _combine(x1, g2, rw[0].reshape(t, 1), rw[1].reshape(t, 1), y2, seq)
    return x2.reshape(bsz, seq, d)
```

```python
import functools

import jax
import jax.numpy as jnp
from jax import lax
from jax.experimental import pallas as pl
from jax.experimental.pallas import tpu as pltpu

F32 = jnp.float32
BF16 = jnp.bfloat16
EPS = 1e-6
LOG2E = 1.4426950408889634

D_MODEL = 2048
HGRN_WIDTH = 1024
HGRN_DK = 128
HGRN_HEADS = 8
HGRN_CHUNK = 64
MLA_HEADS = 8
MLA_NOPE = 128
MLA_ROPE = 64
MLA_QK = MLA_NOPE + MLA_ROPE
MLA_V = 128
MLA_QPAD = 256
Q_LORA = 512
KV_LORA = 256
ROPE_BASE = 10000.0
IN_COLS = 4 * HGRN_WIDTH + Q_LORA + KV_LORA + MLA_ROPE
IN_COLS_PAD = 5120
N_GROUPS = 4
EXPERTS_PER_GROUP = 8
N_EXPERTS = 32
D_EXPERT = 512
ROUTE_ROWS = 40
MOE_BLOCK = 256
LANES = 128
TOK_SUBLANES = 8
VMEM_LIMIT = 56 * 1024 * 1024


def _cparams(sem):
    return pltpu.CompilerParams(dimension_semantics=sem, vmem_limit_bytes=VMEM_LIMIT)


def _dot(a, b):
    return jnp.dot(a, b, preferred_element_type=F32)


def _dot_nt(a, b):
    return lax.dot_general(a, b, (((1,), (1,)), ((), ())), preferred_element_type=F32)


def _rms(x, g):
    return x * lax.rsqrt(jnp.mean(x * x, axis=-1, keepdims=True) + EPS) * g


def _silu(x):
    return x * jax.nn.sigmoid(x)


def _pack_pair(lo, hi):
    lo_b = lax.bitcast_convert_type(lo.astype(BF16).astype(F32), jnp.uint32)
    hi_b = lax.bitcast_convert_type(hi.astype(BF16).astype(F32), jnp.uint32)
    return hi_b | (lo_b >> 16)


def _unpack_pair(w):
    lo = lax.bitcast_convert_type(w << 16, F32)
    hi = lax.bitcast_convert_type(w & jnp.uint32(0xFFFF0000), F32)
    return lo, hi


def _ada_kernel(c_ref, w_ref, b_ref, o_ref):
    ca = _silu(c_ref[...]).astype(BF16)
    o_ref[...] = _dot(ca, w_ref[...].astype(BF16)) + b_ref[...]


def _ada(c, w, b):
    bsz, d = c.shape
    n = w.shape[1]
    tn = 1024
    return pl.pallas_call(
        _ada_kernel,
        grid=(n // tn,),
        in_specs=[pl.BlockSpec((bsz, d), lambda j: (0, 0)),
                  pl.BlockSpec((d, tn), lambda j: (0, j)),
                  pl.BlockSpec((1, tn), lambda j: (0, j))],
        out_specs=pl.BlockSpec((bsz, tn), lambda j: (0, j)),
        out_shape=jax.ShapeDtypeStruct((bsz, n), F32),
        compiler_params=_cparams(("arbitrary",)),
        name="ada",
    )(c, w, b.reshape(1, n))


NORM_ROWS = 32


def _norm_kernel(x_ref, g_ref, sh_ref, sc_ref, h_ref):
    def body(c, carry):
        r = pl.ds(pl.multiple_of(c * NORM_ROWS, NORM_ROWS), NORM_ROWS)
        h = _rms(x_ref[r, :], g_ref[...]) * (1.0 + sc_ref[0]) + sh_ref[0]
        h_ref[r, :] = h.astype(BF16)
        return carry
    lax.fori_loop(0, x_ref.shape[0] // NORM_ROWS, body, 0, unroll=2)


def _norm_mod(x2, g, sh, sc, seq):
    t, d = x2.shape
    tm = min(1024, seq)
    return pl.pallas_call(
        _norm_kernel,
        grid=(t // tm,),
        in_specs=[pl.BlockSpec((tm, d), lambda i: (i, 0)),
                  pl.BlockSpec((1, d), lambda i: (0, 0)),
                  pl.BlockSpec((1, 1, d), lambda i: (i * tm // seq, 0, 0)),
                  pl.BlockSpec((1, 1, d), lambda i: (i * tm // seq, 0, 0))],
        out_specs=pl.BlockSpec((tm, d), lambda i: (i, 0)),
        out_shape=jax.ShapeDtypeStruct((t, d), BF16),
        compiler_params=_cparams(("arbitrary",)),
        name="norm_mod",
    )(x2, g, sh, sc)


def _in_kernel(h_ref, w_ref, wt_ref, o_ref, wb_s, *, n_main):
    j = pl.program_id(0)
    first_row_tile = pl.program_id(1) == 0

    @pl.when(jnp.logical_and(first_row_tile, j < n_main))
    def _():
        wb_s[...] = w_ref[...].astype(BF16)

    @pl.when(jnp.logical_and(first_row_tile, j == n_main))
    def _():
        wb_s[...] = wt_ref[...]

    o_ref[...] = _dot_nt(h_ref[...], wb_s[...]).astype(BF16)


def _in_proj(h, w_t, seq):
    t, d = h.shape
    tn = 1024
    n_main = IN_COLS // tn
    w_tail = jnp.pad(w_t[n_main * tn:], ((0, (n_main + 1) * tn - IN_COLS), (0, 0))).astype(BF16)
    tm = min(1024, t)
    return pl.pallas_call(
        functools.partial(_in_kernel, n_main=n_main),
        grid=(IN_COLS_PAD // tn, t // tm),
        in_specs=[pl.BlockSpec((tm, d), lambda j, i: (i, 0)),
                  pl.BlockSpec((tn, d), lambda j, i: (jnp.minimum(j, n_main - 1), 0)),
                  pl.BlockSpec((tn, d), lambda j, i: (0, 0))],
        out_specs=pl.BlockSpec((tm, tn), lambda j, i: (i, j)),
        out_shape=jax.ShapeDtypeStruct((t, IN_COLS_PAD), BF16),
        scratch_shapes=[pltpu.VMEM((tn, d), BF16)],
        compiler_params=_cparams(("arbitrary", "arbitrary")),
        name="in_proj",
    )(h, w_t, w_tail)


HG_ROWS = 256


PIPELINE_STATIC_BLOCKS = 8


def _software_pipeline(stages, n_blocks):
    depth = len(stages)

    def step(i, static):
        for k in reversed(range(depth)):
            if static and not 0 <= i - k < n_blocks:
                continue
            stages[k](i - k)

    if n_blocks <= PIPELINE_STATIC_BLOCKS:
        for i in range(n_blocks + depth - 1):
            step(i, True)
        return
    for i in range(depth - 1):
        step(i, True)

    def steady(i, carry):
        step(i, False)
        return carry

    lax.fori_loop(depth - 1, n_blocks, steady, 0)
    for i in range(n_blocks, n_blocks + depth - 1):
        step(i, True)


def _chunk_mask(n):
    row = lax.broadcasted_iota(jnp.int32, (n, n), 0)
    col = lax.broadcasted_iota(jnp.int32, (n, n), 1)
    return jnp.logical_and(row // HGRN_CHUNK == col // HGRN_CHUNK, col <= row)


def _hgrn_kernel(q_ref, f_ref, i_ref, g_ref, lbl_ref, og_ref, o_ref,
                 qb_s, u_s, sp_s, dec_s, oi_s, b_s, k_s, qin_s, kin_s, ku_s, a_s, st_s, *, layer):
    seq = q_ref.shape[1]
    cs = HGRN_CHUNK
    rb = min(HG_ROWS, seq)
    nc = rb // cs
    dk = HGRN_DK
    lg = lbl_ref[...]
    ex = jnp.exp(lg - jnp.max(lg, axis=0, keepdims=True))
    sm = ex / jnp.sum(ex, axis=0, keepdims=True)
    lb = jnp.sum(sm[0:layer + 1], axis=0, keepdims=True)
    mask = _chunk_mask(rb)
    tri = mask.astype(BF16)
    row_chunk = lax.broadcasted_iota(jnp.int32, (rb, dk), 0) // cs

    chunk_sel = [(row_chunk == c).astype(BF16) for c in range(nc)]

    def rows(blk):
        return pl.ds(pl.multiple_of(blk * rb, rb), rb)

    def stage1(blk):
        r = rows(blk)
        f = lb + (1.0 - lb) * jax.nn.sigmoid(f_ref[0, r, :].astype(F32))
        lf = jnp.log(f)
        k_s[r, :] = 1.0 - f
        hi = lf.astype(BF16)
        r1 = lf - hi.astype(F32)
        mid = r1.astype(BF16)
        lo = (r1 - mid.astype(F32)).astype(BF16)
        bhm = _dot(tri, jnp.concatenate([hi, mid], axis=1))
        b_s[r, :] = bhm[:, 0:dk] + bhm[:, dk:] + _dot(tri, lo)

    def stage2(blk):
        r = rows(blk)
        b3 = b_s[r, :].reshape(nc, cs, dk)
        bmid = b3[:, cs // 2 - 1:cs // 2, :]
        blast = b3[:, cs - 1:cs, :]
        q3 = (q_ref[0, r, :].astype(F32) * dk ** -0.5).reshape(nc, cs, dk)
        k3 = k_s[r, :].reshape(nc, cs, dk)
        qin_s[r, :] = (q3 * jnp.exp(b3 - bmid)).reshape(rb, dk).astype(BF16)
        kin_s[r, :] = (k3 * jnp.exp(bmid - b3)).reshape(rb, dk).astype(BF16)
        ku_s[r, :] = (k3 * jnp.exp(blast - b3)).reshape(rb, dk).astype(BF16)
        qb_s[r, :] = (q3 * jnp.exp(b3)).reshape(rb, dk).astype(BF16)
        dec_s[pl.ds(blk * nc, nc)] = jnp.exp(blast)

    def stage3(blk):
        r = rows(blk)
        a_s[r, :] = jnp.where(mask, _dot_nt(qin_s[r, :], kin_s[r, :]), 0.0).astype(BF16)
        vt = i_ref[0, r, :].astype(F32).T.astype(BF16)
        ku = ku_s[r, :]
        ut = _dot(vt, jnp.concatenate([ku * sel for sel in chunk_sel], axis=1))
        for c in range(nc):
            u_s[blk * nc + c] = ut[:, c * dk:(c + 1) * dk]

    def stage4(blk):
        r = rows(blk)
        oi_s[r, :] = _dot(a_s[r, :], i_ref[0, r, :])
        st = st_s[...]
        for c in range(nc):
            sp_s[blk * nc + c] = st.astype(BF16)
            st = st * dec_s[blk * nc + c] + u_s[blk * nc + c]
        st_s[...] = st

    def stage5(blk):
        r = rows(blk)
        inter = [_dot_nt(qb_s[pl.ds(pl.multiple_of(blk * rb + c * cs, cs), cs), :], sp_s[blk * nc + c])
                 for c in range(nc)]
        o = oi_s[r, :] + jnp.concatenate(inter, axis=0)
        o = _rms(o, og_ref[...]) * _silu(g_ref[0, r, :].astype(F32))
        o_ref[0, r, :] = o.astype(BF16)

    st_s[...] = jnp.zeros_like(st_s)
    _software_pipeline([stage1, stage2, stage3, stage4, stage5], seq // rb)


def _hgrn(proj3, lb_logits, onorm_g, layer):
    bsz, seq, _ = proj3.shape
    nh = HGRN_HEADS

    def col(off):
        return pl.BlockSpec((1, seq, HGRN_DK), lambda b, h: (b, 0, off * nh + h))

    nl = lb_logits.shape[0]
    return pl.pallas_call(
        functools.partial(_hgrn_kernel, layer=layer),
        grid=(bsz, nh),
        in_specs=[col(0), col(1), col(2), col(3),
                  pl.BlockSpec((nl, HGRN_DK), lambda b, h: (0, h)),
                  pl.BlockSpec((1, HGRN_DK), lambda b, h: (0, 0))],
        out_specs=pl.BlockSpec((1, seq, HGRN_DK), lambda b, h: (b, 0, h)),
        out_shape=jax.ShapeDtypeStruct((bsz, seq, HGRN_WIDTH), BF16),
        scratch_shapes=[pltpu.VMEM((seq, HGRN_DK), BF16),
                        pltpu.VMEM((seq // HGRN_CHUNK, HGRN_DK, HGRN_DK), F32),
                        pltpu.VMEM((seq // HGRN_CHUNK, HGRN_DK, HGRN_DK), BF16),
                        pltpu.VMEM((seq // HGRN_CHUNK, 1, HGRN_DK), F32),
                        pltpu.VMEM((seq, HGRN_DK), F32),
                        pltpu.VMEM((seq, HGRN_DK), F32), pltpu.VMEM((seq, HGRN_DK), F32),
                        pltpu.VMEM((seq, HGRN_DK), BF16), pltpu.VMEM((seq, HGRN_DK), BF16),
                        pltpu.VMEM((seq, HGRN_DK), BF16), pltpu.VMEM((seq, min(HG_ROWS, seq)), BF16),
                        pltpu.VMEM((HGRN_DK, HGRN_DK), F32)],
        compiler_params=_cparams(("arbitrary", "arbitrary")),
        name="hgrn",
    )(proj3, proj3, proj3, proj3, lb_logits, onorm_g)


def _rope(x, cos, sin_signed, lane):
    half = MLA_ROPE // 2
    swapped = jnp.where(lane < half, pltpu.roll(x, LANES - half, 1), pltpu.roll(x, half, 1))
    return x * cos + swapped * sin_signed


UP_ROWS = 1024


def _up_kernel(p_ref, pos_ref, wq_ref, wkv_ref, qag_ref, kvag_ref, qg_ref, kg_ref, freq_ref,
               q_ref, k_ref, v_ref):
    rows = min(UP_ROWS, p_ref.shape[0])
    lane = lax.broadcasted_iota(jnp.int32, (rows, LANES), 1)
    half = MLA_ROPE // 2
    valid = lane < MLA_ROPE
    qg = qg_ref[...] * (MLA_QK ** -0.5 * LOG2E)
    kg = kg_ref[...]

    def chunk(c, carry):
        r = pl.ds(pl.multiple_of(c * rows, rows), rows)
        p = p_ref[r, :].astype(F32)
        q_a = p[:, 0:Q_LORA]
        kv_a = p[:, Q_LORA:Q_LORA + KV_LORA]
        k_pe = p[:, Q_LORA + KV_LORA:Q_LORA + KV_LORA + LANES]
        qa_n = _rms(q_a, qag_ref[...]).astype(BF16)
        kva_n = _rms(kv_a, kvag_ref[...]).astype(BF16)
        sin_all = jnp.sin(pos_ref[r, :].astype(F32) * freq_ref[0:1, :] + freq_ref[1:2, :])
        cos = jnp.where(valid, pltpu.roll(sin_all, MLA_ROPE, 1), 0.0)
        sin_signed = jnp.where(valid, jnp.where(lane < half, -sin_all, sin_all), 0.0)
        kpe_ss = jnp.sum(k_pe * k_pe, axis=-1, keepdims=True)
        kpe_rot = _rope(k_pe * kg[:, MLA_NOPE:], cos, sin_signed, lane)
        for h in range(MLA_HEADS):
            cols = slice(h * MLA_QPAD, (h + 1) * MLA_QPAD)
            qh = _dot(qa_n, wq_ref[:, cols])
            qn = qh * lax.rsqrt(jnp.sum(qh * qh, axis=-1, keepdims=True) / MLA_QK + EPS) * qg
            q_ref[0, h, r, 0:MLA_NOPE] = qn[:, 0:MLA_NOPE].astype(BF16)
            q_ref[0, h, r, MLA_NOPE:] = _rope(qn[:, MLA_NOPE:], cos, sin_signed, lane).astype(BF16)
            kvh = _dot(kva_n, wkv_ref[:, cols])
            kn = kvh[:, 0:MLA_NOPE]
            rk = lax.rsqrt((jnp.sum(kn * kn, axis=-1, keepdims=True) + kpe_ss) / MLA_QK + EPS)
            k_ref[0, h, r, 0:MLA_NOPE] = (kn * rk * kg[:, 0:MLA_NOPE]).astype(BF16)
            k_ref[0, h, r, MLA_NOPE:] = (kpe_rot * rk).astype(BF16)
            v_ref[0, h, :, r] = kvh[:, MLA_NOPE:].T.astype(BF16)
        return carry

    lax.fori_loop(0, p_ref.shape[0] // rows, chunk, 0)


def _mla_up(proj3, pos3, wq, wkv, qag, kvag, qg, kg, freq):
    bsz, seq, _ = proj3.shape
    tm = min(UP_ROWS, seq)
    nh = MLA_HEADS
    mla_block = 4 * HGRN_WIDTH // 1024

    def const(shape):
        return pl.BlockSpec(shape, lambda b, i: (0,) * len(shape))

    return pl.pallas_call(
        _up_kernel,
        grid=(bsz, seq // tm),
        in_specs=[pl.BlockSpec((None, tm, 1024), lambda b, i: (b, i, mla_block)),
                  pl.BlockSpec((None, tm, 1), lambda b, i: (b, i, 0)),
                  const(wq.shape), const(wkv.shape), const(qag.shape), const(kvag.shape),
                  const(qg.shape), const(kg.shape), const(freq.shape)],
        out_specs=[pl.BlockSpec((1, nh, tm, MLA_QPAD), lambda b, i: (b, 0, i, 0)),
                   pl.BlockSpec((1, nh, tm, MLA_QPAD), lambda b, i: (b, 0, i, 0)),
                   pl.BlockSpec((1, nh, MLA_V, tm), lambda b, i: (b, 0, 0, i))],
        out_shape=[jax.ShapeDtypeStruct((bsz, nh, seq, MLA_QPAD), BF16),
                   jax.ShapeDtypeStruct((bsz, nh, seq, MLA_QPAD), BF16),
                   jax.ShapeDtypeStruct((bsz, nh, MLA_V, seq), BF16)],
        compiler_params=_cparams(("arbitrary", "arbitrary")),
        name="mla_up",
    )(proj3, pos3, wq, wkv, qag, kvag, qg, kg, freq)


ATT_T = 256


def _attn_kernel(q_ref, k_ref, vt_ref, g_ref, o_ref):
    seq = q_ref.shape[2]
    t = min(ATT_T, seq)
    key = lax.broadcasted_iota(jnp.int32, (t, t), 0)
    qry = lax.broadcasted_iota(jnp.int32, (t, t), 1)
    causal = key <= qry
    neg = jnp.finfo(F32).min

    def widen(x, off, fill):
        return x if off == 0 else jnp.concatenate([jnp.full((x.shape[0], off), fill, x.dtype), x], axis=1)

    scores = []
    m = None
    for off in range(0, seq, t):
        s = _dot_nt(k_ref[0, 0, off:off + t, :], q_ref[0, 0, off:, :])
        diag = jnp.where(causal, s[:, 0:t], neg)
        s = diag if off + t == seq else jnp.concatenate([diag, s[:, t:]], axis=1)
        scores.append(s)
        blk_max = widen(jnp.max(s, axis=0, keepdims=True), off, neg)
        m = blk_max if m is None else jnp.maximum(m, blk_max)
    l = None
    ot = None
    for off, s in zip(range(0, seq, t), scores):
        p = jnp.exp2(s - m[:, off:])
        p_sum = widen(jnp.sum(p, axis=0, keepdims=True), off, 0.0)
        pv = widen(_dot(vt_ref[0, 0, :, off:off + t], p.astype(BF16)), off, 0.0)
        l = p_sum if l is None else l + p_sum
        ot = pv if ot is None else ot + pv
    ot = ot * (1.0 / l)
    ot = ot * lax.rsqrt(jnp.mean(ot * ot, axis=0, keepdims=True) + EPS) * g_ref[...]
    o_ref[0, :, :] = ot.T.astype(BF16)


def _attention(q, k, v, g):
    bsz, nh, seq, _ = q.shape
    return pl.pallas_call(
        _attn_kernel,
        grid=(bsz, nh),
        in_specs=[pl.BlockSpec((1, 1, seq, MLA_QPAD), lambda b, h: (b, h, 0, 0)),
                  pl.BlockSpec((1, 1, seq, MLA_QPAD), lambda b, h: (b, h, 0, 0)),
                  pl.BlockSpec((1, 1, MLA_V, seq), lambda b, h: (b, h, 0, 0)),
                  pl.BlockSpec((MLA_V, 1), lambda b, h: (0, 0))],
        out_specs=pl.BlockSpec((1, seq, MLA_V), lambda b, h: (b, 0, h)),
        out_shape=jax.ShapeDtypeStruct((bsz, seq, nh * MLA_V), BF16),
        compiler_params=_cparams(("arbitrary", "arbitrary")),
        name="attn",
    )(q, k, v, g)


def _out_kernel(oa_ref, ob_ref, x_ref, wa_ref, wb_ref, g1_ref, n2g_ref, sh2_ref, sc2_ref,
                wr_ref, br_ref, x1_ref, h2_ref, lg_ref, mix_s):
    @pl.when(pl.program_id(0) == 0)
    def _():
        mix_s[...] = jnp.zeros_like(mix_s)

    mix_next = _dot(oa_ref[...], wa_ref[...]) + _dot(ob_ref[...], wb_ref[...])
    x1 = x_ref[...] + g1_ref[0] * mix_s[...]
    x1_ref[...] = x1
    h2 = _rms(x1, n2g_ref[...]) * (1.0 + sc2_ref[0]) + sh2_ref[0]
    tm = h2.shape[0]
    for s in range(TOK_SUBLANES):
        h2_ref[pl.ds(s, tm, stride=TOK_SUBLANES), :] = _pack_pair(
            h2[:, s * LANES:(s + 1) * LANES], h2[:, (s + TOK_SUBLANES) * LANES:(s + TOK_SUBLANES + 1) * LANES])
    lg = _dot(h2.astype(BF16), wr_ref[...]) + br_ref[...]
    lg_ref[...] = lg.T[0:ROUTE_ROWS, :]
    mix_s[...] = mix_next


def _out_proj(oa, ob, x2, wa, wb, g1, n2g, sh2, sc2, wr, br, seq):
    t, d = x2.shape
    tm = min(512, seq)
    n_tiles = t // tm

    def const(shape):
        return pl.BlockSpec(shape, lambda i: (0,) * len(shape))

    def prev(i):
        return jnp.maximum(i - 1, 0)

    def per_batch():
        return pl.BlockSpec((1, 1, d), lambda i: (prev(i) * tm // seq, 0, 0))

    return pl.pallas_call(
        _out_kernel,
        grid=(n_tiles + 1,),
        in_specs=[pl.BlockSpec((tm, HGRN_WIDTH), lambda i: (jnp.minimum(i, n_tiles - 1), 0)),
                  pl.BlockSpec((tm, HGRN_WIDTH), lambda i: (jnp.minimum(i, n_tiles - 1), 0)),
                  pl.BlockSpec((tm, d), lambda i: (prev(i), 0)),
                  const(wa.shape), const(wb.shape), per_batch(), const(n2g.shape),
                  per_batch(), per_batch(), const(wr.shape), const(br.shape)],
        out_specs=[pl.BlockSpec((tm, d), lambda i: (prev(i), 0)),
                   pl.BlockSpec((tm * TOK_SUBLANES, LANES), lambda i: (prev(i), 0)),
                   pl.BlockSpec((ROUTE_ROWS, tm), lambda i: (0, prev(i)))],
        out_shape=[jax.ShapeDtypeStruct((t, d), F32),
                   jax.ShapeDtypeStruct((t * TOK_SUBLANES, LANES), jnp.uint32),
                   jax.ShapeDtypeStruct((ROUTE_ROWS, t), F32)],
        scratch_shapes=[pltpu.VMEM((tm, d), F32)],
        compiler_params=_cparams(("arbitrary",)),
        name="out_proj",
    )(oa, ob, x2, wa, wb, g1, n2g, sh2, sc2, wr, br)


def _route_kernel(lg_ref, tri_ref, ri_ref, rw_ref, cnt_ref, carry_s):
    step = pl.program_id(0)

    @pl.when(step == 0)
    def _():
        carry_s[...] = jnp.zeros_like(carry_s)

    lg = lg_ref[...]
    tr = lg.shape[1]
    epg = EXPERTS_PER_GROUP
    gl = lg[N_EXPERTS:N_EXPERTS + N_GROUPS, :]
    row_g = lax.broadcasted_iota(jnp.int32, (N_GROUPS, tr), 0)
    gmax = jnp.max(gl, axis=0, keepdims=True)
    g_sel = jnp.min(jnp.where(gl == gmax, row_g, N_GROUPS), axis=0, keepdims=True)
    p_group = 1.0 / jnp.sum(jnp.exp(gl - gmax), axis=0, keepdims=True)

    e_in = lg[0:epg, :]
    for g in range(1, N_GROUPS):
        e_in = jnp.where(g_sel == g, lg[g * epg:(g + 1) * epg, :], e_in)
    row_e = lax.broadcasted_iota(jnp.int32, (epg, tr), 0)
    top1 = jnp.max(e_in, axis=0, keepdims=True)
    i1 = jnp.min(jnp.where(e_in == top1, row_e, epg), axis=0, keepdims=True)
    rest = jnp.where(row_e == i1, -jnp.inf, e_in)
    top2 = jnp.max(rest, axis=0, keepdims=True)
    i2 = jnp.min(jnp.where(rest == top2, row_e, epg), axis=0, keepdims=True)
    e2w = jnp.exp(top2 - top1)
    w1 = p_group / (1.0 + e2w)
    w2 = p_group * e2w / (1.0 + e2w)
    ex1 = g_sel * epg + i1
    ex2 = g_sel * epg + i2

    row_x = lax.broadcasted_iota(jnp.int32, (N_EXPERTS, tr), 0)
    oh1 = row_x == ex1
    oh2 = row_x == ex2
    oh = jnp.logical_or(oh1, oh2)
    before = _dot(oh.astype(BF16), tri_ref[...]) + carry_s[:, 0:1]
    rank1 = jnp.sum(jnp.where(oh1, before, 0.0), axis=0, keepdims=True)
    rank2 = jnp.sum(jnp.where(oh2, before, 0.0), axis=0, keepdims=True)
    carry_s[...] = carry_s[...] + jnp.sum(oh.astype(F32), axis=1, keepdims=True)

    zi = jnp.zeros((4, tr), jnp.int32)
    ri_ref[...] = jnp.concatenate([ex1, ex2, rank1.astype(jnp.int32), rank2.astype(jnp.int32), zi], axis=0)
    rw_ref[...] = jnp.concatenate([w1, w2, jnp.zeros((6, tr), F32)], axis=0)
    cnt_ref[...] = carry_s[...].astype(jnp.int32)


def _route(lg_t, tri):
    t = lg_t.shape[1]
    tr = tri.shape[0]
    return pl.pallas_call(
        _route_kernel,
        grid=(t // tr,),
        in_specs=[pl.BlockSpec((ROUTE_ROWS, tr), lambda i: (0, i)),
                  pl.BlockSpec((tr, tr), lambda i: (0, 0))],
        out_specs=[pl.BlockSpec((8, tr), lambda i: (0, i)),
                   pl.BlockSpec((8, tr), lambda i: (0, i)),
                   pl.BlockSpec((N_EXPERTS, LANES), lambda i: (0, 0))],
        out_shape=[jax.ShapeDtypeStruct((8, t), jnp.int32),
                   jax.ShapeDtypeStruct((8, t), F32),
                   jax.ShapeDtypeStruct((N_EXPERTS, LANES), jnp.int32)],
        scratch_shapes=[pltpu.VMEM((N_EXPERTS, LANES), F32)],
        compiler_params=_cparams(("arbitrary",)),
        name="route",
    )(lg_t, tri)


def _moe_kernel(rd_ref, be_ref, pe_ref, nb_ref, x_ref, wg_ref, wu_ref, wd_ref, y_ref,
                ys, wg_f, wu_f, wd_f, wg_s, wu_s, wd_s, ssem, wsem):
    b = pl.program_id(0)
    last = pl.num_programs(0) - 1
    nb = nb_ref[0]
    tm = MOE_BLOCK
    ts = TOK_SUBLANES
    weights = ((wg_ref, wg_f, wg_s), (wu_ref, wu_f, wu_s), (wd_ref, wd_f, wd_s))

    def fetch(e):
        return [pltpu.make_async_copy(src.at[e], stage, wsem.at[k]) for k, (src, stage, _) in enumerate(weights)]

    def scatter(blk):
        base = blk * tm
        for j in range(tm):
            dst = pl.multiple_of(rd_ref[base + j] * ts, ts)
            pltpu.make_async_copy(ys.at[pl.ds(j * ts, ts), :], y_ref.at[pl.ds(dst, ts), :], ssem).start()

    def wait_scatter():
        pltpu.make_async_copy(ys, y_ref.at[pl.ds(0, tm * ts), :], ssem).wait()

    @pl.when(b == 0)
    def _():
        for copy in fetch(be_ref[0]):
            copy.start()
        ys[...] = jnp.zeros_like(ys)
        spare = pltpu.make_async_copy(ys, y_ref.at[pl.ds(y_ref.shape[0] - tm * ts, tm * ts), :], ssem)
        spare.start()
        spare.wait()

    expert = be_ref[b]
    first_of_expert = jnp.logical_or(b == 0, expert != be_ref[jnp.maximum(b - 1, 0)])

    @pl.when(jnp.logical_and(b < nb, first_of_expert))
    def _():
        for copy in fetch(0):
            copy.wait()
        for _, stage, dst in weights:
            dst[...] = stage[...].astype(BF16)
        next_blk = pe_ref[expert] // tm

        @pl.when(next_blk < nb)
        def _():
            for copy in fetch(be_ref[next_blk]):
                copy.start()

    @pl.when(b < nb)
    def _():
        scatter(jnp.maximum(b - 1, 0))
        parts = [_unpack_pair(x_ref[pl.ds(s, tm, stride=ts), :]) for s in range(ts)]
        x = jnp.concatenate([p[0] for p in parts] + [p[1] for p in parts], axis=1).astype(BF16)
        hid = _silu(_dot(x, wg_s[...])) * _dot(x, wu_s[...])
        y = _dot(hid.astype(BF16), wd_s[...])
        wait_scatter()
        for s in range(ts):
            ys[pl.ds(s, tm, stride=ts), :] = _pack_pair(y[:, s * LANES:(s + 1) * LANES],
                                                        y[:, (s + ts) * LANES:(s + ts + 1) * LANES])

        @pl.when(b == last)
        def _():
            scatter(b)
            wait_scatter()

    @pl.when(b == nb)
    def _():
        scatter(b - 1)
        wait_scatter()


def _moe(row_dst, blk_e, pad_end, nb_real, x_sorted, wg, wu, wd, n_out_rows):
    n_blocks = blk_e.shape[0]
    d, de = wg.shape[1], wg.shape[2]
    tm = MOE_BLOCK
    hbm = pl.BlockSpec(memory_space=pl.ANY)
    return pl.pallas_call(
        _moe_kernel,
        grid_spec=pltpu.PrefetchScalarGridSpec(
            num_scalar_prefetch=4,
            grid=(n_blocks,),
            in_specs=[pl.BlockSpec((tm * TOK_SUBLANES, LANES),
                                   lambda b, rd, be, pe, nb: (jnp.minimum(b, nb[0] - 1), 0)),
                      hbm, hbm, hbm],
            out_specs=hbm,
            scratch_shapes=[pltpu.VMEM((tm * TOK_SUBLANES, LANES), jnp.uint32),
                            pltpu.VMEM((d, de), F32), pltpu.VMEM((d, de), F32), pltpu.VMEM((de, d), F32),
                            pltpu.VMEM((d, de), BF16), pltpu.VMEM((d, de), BF16), pltpu.VMEM((de, d), BF16),
                            pltpu.SemaphoreType.DMA(()), pltpu.SemaphoreType.DMA((3,))]),
        out_shape=jax.ShapeDtypeStruct((n_out_rows * TOK_SUBLANES, LANES), jnp.uint32),
        compiler_params=_cparams(("arbitrary",)),
        name="moe",
    )(row_dst, blk_e, pad_end, nb_real, x_sorted, wg, wu, wd)


def _dispatch_kernel(e1_ref, e2_ref, r1_ref, r2_ref, ps_ref, pe_ref, cnt_ref, be_ref, nb_ref,
                     h_ref, x_ref, rd_ref, zbuf, sem, zsem):
    i = pl.program_id(0)
    ts = TOK_SUBLANES
    td = h_ref.shape[0] // ts
    n_tok = e1_ref.shape[0]
    base = i * td
    blk_rows = MOE_BLOCK * ts

    def zero_fill(blk):
        return pltpu.make_async_copy(
            zbuf, x_ref.at[pl.ds(pl.multiple_of(blk * blk_rows, blk_rows), blk_rows), :], zsem)

    @pl.when(i == 0)
    def _():
        zbuf[...] = jnp.zeros_like(zbuf)

        def fill(blk, n_fills):
            e = be_ref[blk]
            partial_last = jnp.logical_and(blk == pe_ref[e] // MOE_BLOCK - 1, cnt_ref[e] % MOE_BLOCK != 0)
            needs_fill = jnp.logical_or(blk >= nb_ref[0], partial_last)

            @pl.when(needs_fill)
            def _():
                zero_fill(blk).start()
            return n_fills + needs_fill.astype(jnp.int32)

        n_fills = lax.fori_loop(0, be_ref.shape[0], fill, 0)

        def init(blk, carry):
            for j in range(MOE_BLOCK):
                rd_ref[blk * MOE_BLOCK + j] = 2 * n_tok + j
            return carry

        lax.fori_loop(0, rd_ref.shape[0] // MOE_BLOCK, init, 0)

        def drain(k, carry):
            zero_fill(0).wait()
            return carry

        lax.fori_loop(0, n_fills, drain, 0)

    def start(t, carry):
        src = h_ref.at[pl.ds(pl.multiple_of(t * ts, ts), ts), :]
        tok = base + t
        for slot, (e_ref, r_ref) in enumerate(((e1_ref, r1_ref), (e2_ref, r2_ref))):
            row = ps_ref[e_ref[tok]] + r_ref[tok]
            pltpu.make_async_copy(src, x_ref.at[pl.ds(pl.multiple_of(row * ts, ts), ts), :], sem).start()
            rd_ref[row] = 2 * tok + slot
        return carry

    lax.fori_loop(0, td, start, 0, unroll=8)
    for _ in range(2):
        pltpu.make_async_copy(h_ref, x_ref.at[pl.ds(0, td * ts), :], sem).wait()


def _dispatch(ri, pad_start, pad_end, counts, blk_e, nb_real, h2p, n_rows):
    t = ri.shape[1]
    td = min(512, t)
    ts = TOK_SUBLANES
    return pl.pallas_call(
        _dispatch_kernel,
        grid_spec=pltpu.PrefetchScalarGridSpec(
            num_scalar_prefetch=9,
            grid=(t // td,),
            in_specs=[pl.BlockSpec((td * ts, LANES), lambda i, *_: (i, 0))],
            out_specs=[pl.BlockSpec(memory_space=pl.ANY), pl.BlockSpec(memory_space=pltpu.SMEM)],
            scratch_shapes=[pltpu.VMEM((MOE_BLOCK * ts, LANES), jnp.uint32),
                            pltpu.SemaphoreType.DMA(()), pltpu.SemaphoreType.DMA(())]),
        out_shape=[jax.ShapeDtypeStruct((n_rows * ts, LANES), jnp.uint32),
                   jax.ShapeDtypeStruct((n_rows,), jnp.int32)],
        compiler_params=_cparams(("arbitrary",)),
        name="dispatch",
    )(ri[0], ri[1], ri[2], ri[3], pad_start, pad_end, counts, blk_e, nb_real, h2p)


def _moe_plan(counts, t):
    tm = MOE_BLOCK
    n_blocks = 2 * t // tm + N_EXPERTS
    padded = ((counts + tm - 1) // tm) * tm
    pad_end = jnp.cumsum(padded)
    pad_start = pad_end - padded
    blk = jnp.arange(n_blocks, dtype=jnp.int32)
    blk_e = jnp.minimum(jnp.sum(pad_end[None, :] <= (blk * tm)[:, None], axis=1), N_EXPERTS - 1).astype(jnp.int32)
    nb_real = (pad_end[-1:] // tm).astype(jnp.int32)
    return pad_start, pad_end, blk_e, nb_real, n_blocks * tm


COMBINE_ROWS = 64


def _combine_kernel(x1_ref, g2_ref, w1_ref, w2_ref, y_ref, o_ref):
    ts = TOK_SUBLANES
    rows = min(COMBINE_ROWS, x1_ref.shape[0])

    def chunk(c, carry):
        r = pl.ds(pl.multiple_of(c * rows, rows), rows)
        w1 = w1_ref[r, :]
        w2 = w2_ref[r, :]
        tile0 = pl.multiple_of(c * rows * 2 * ts, rows * 2 * ts)
        for s in range(ts):
            a_lo, a_hi = _unpack_pair(y_ref[pl.ds(tile0 + s, rows, stride=2 * ts), :])
            b_lo, b_hi = _unpack_pair(y_ref[pl.ds(tile0 + ts + s, rows, stride=2 * ts), :])
            for col, ya, yb in ((s, a_lo, b_lo), (s + ts, a_hi, b_hi)):
                cols = slice(col * LANES, (col + 1) * LANES)
                o_ref[r, cols] = x1_ref[r, cols] + g2_ref[0][:, cols] * (w1 * ya + w2 * yb)
        return carry

    lax.fori_loop(0, x1_ref.shape[0] // rows, chunk, 0)


def _combine(x1, g2, w1, w2, y2, seq):
    t, d = x1.shape
    tc = min(512, seq)
    return pl.pallas_call(
        _combine_kernel,
        grid=(t // tc,),
        in_specs=[pl.BlockSpec((tc, d), lambda i: (i, 0)),
                  pl.BlockSpec((1, 1, d), lambda i: (i * tc // seq, 0, 0)),
                  pl.BlockSpec((tc, 1), lambda i: (i, 0)),
                  pl.BlockSpec((tc, 1), lambda i: (i, 0)),
                  pl.BlockSpec((tc * 2 * TOK_SUBLANES, LANES), lambda i: (i, 0))],
        out_specs=pl.BlockSpec((tc, d), lambda i: (i, 0)),
        out_shape=jax.ShapeDtypeStruct((t, d), F32),
        compiler_params=_cparams(("arbitrary",)),
        name="combine",
    )(x1, g2, w1, w2, y2)


def _q_up_layout(w_q_up):
    w = w_q_up.reshape(Q_LORA, MLA_HEADS, MLA_QK)
    w = jnp.pad(w, ((0, 0), (0, 0), (0, MLA_QPAD - MLA_QK)))
    return w.reshape(Q_LORA, MLA_HEADS * MLA_QPAD).astype(BF16)


def _pad_lanes(g, width):
    return jnp.pad(g, (0, width - g.shape[0])).reshape(1, width)


def kernel(x, c, positions, w_ada, b_ada, norm1_g, w_in, hgrn_lb_logits, hgrn_onorm_g, q_a_norm_g, w_q_up,
           kv_a_norm_g, w_kv_up, q_norm_g, k_norm_g, attn_onorm_g, w_out, norm2_g, w_group, b_group,
           w_router, b_router, w_gate, w_up, w_down):
    bsz, seq, d = x.shape
    t = bsz * seq
    depth = w_ada.shape[0]
    half = MLA_ROPE // 2
    inv_freq = ROPE_BASE ** (-jnp.arange(0, MLA_ROPE, 2, dtype=F32) / MLA_ROPE)
    freq = jnp.stack([jnp.tile(inv_freq, LANES // half),
                      jnp.where(jnp.arange(LANES) < MLA_ROPE, 0.0, jnp.pi / 2).astype(F32)])
    pos3 = positions.reshape(bsz, seq, 1)
    tr = min(512, t)
    tri = jnp.triu(jnp.ones((tr, tr), BF16), 1)

    x2 = x.reshape(t, d)
    for l in range(depth):
        mod = _ada(c, w_ada[l], b_ada[l]).reshape(bsz, 6, 1, d)
        sh1, sc1, g1, sh2, sc2, g2 = (mod[:, i] for i in range(6))

        proj = _in_proj(_norm_mod(x2, norm1_g[l].reshape(1, d), sh1, sc1, seq), w_in[l].T, seq)
        proj3 = proj.reshape(bsz, seq, IN_COLS_PAD)

        o_a = _hgrn(proj3, hgrn_lb_logits, hgrn_onorm_g[l].reshape(1, HGRN_DK), l)

        q, k, v = _mla_up(proj3, pos3, _q_up_layout(w_q_up[l]), w_kv_up[l].astype(BF16),
                          q_a_norm_g[l].reshape(1, Q_LORA), kv_a_norm_g[l].reshape(1, KV_LORA),
                          _pad_lanes(q_norm_g[l], MLA_QPAD), _pad_lanes(k_norm_g[l], MLA_QPAD), freq)
        o_b = _attention(q, k, v, attn_onorm_g[l].reshape(MLA_V, 1))

        w_o = w_out[l].astype(BF16)
        wr = jnp.pad(jnp.concatenate([w_router[l], w_group[l]], axis=1),
                     ((0, 0), (0, LANES - N_EXPERTS - N_GROUPS))).astype(BF16)
        br = _pad_lanes(jnp.concatenate([b_router[l], b_group[l]]), LANES)
        x1, h2, lg_t = _out_proj(o_a.reshape(t, HGRN_WIDTH), o_b.reshape(t, HGRN_WIDTH), x2,
                                 w_o[:HGRN_WIDTH], w_o[HGRN_WIDTH:], g1, norm2_g[l].reshape(1, d),
                                 sh2, sc2, wr, br, seq)

        ri, rw, cnt = _route(lg_t, tri)
        counts = cnt[:, 0]
        pad_start, pad_end, blk_e, nb_real, n_rows = _moe_plan(counts, t)
        x_sorted, row_dst = _dispatch(ri, pad_start, pad_end, counts, blk_e, nb_real, h2, n_rows)
        y2 = _moe(row_dst, blk_e, pad_end, nb_real, x_sorted, w_gate[l], w_up[l], w_down[l], 2 * t + MOE_BLOCK)
        x2 = _combine(x1, g2, rw[0].reshape(t, 1), rw[1].reshape(t, 1), y2, seq)
    return x2.reshape(bsz, seq, d)
```

```python
import functools

import jax
import jax.numpy as jnp
from jax import lax
from jax.experimental import pallas as pl
from jax.experimental.pallas import tpu as pltpu

F32 = jnp.float32
BF16 = jnp.bfloat16
EPS = 1e-6
LOG2E = 1.4426950408889634

D_MODEL = 2048
HGRN_WIDTH = 1024
HGRN_DK = 128
HGRN_HEADS = 8
HGRN_CHUNK = 64
MLA_HEADS = 8
MLA_NOPE = 128
MLA_ROPE = 64
MLA_QK = MLA_NOPE + MLA_ROPE
MLA_V = 128
MLA_QPAD = 256
Q_LORA = 512
KV_LORA = 256
ROPE_BASE = 10000.0
IN_COLS = 4 * HGRN_WIDTH + Q_LORA + KV_LORA + MLA_ROPE
IN_COLS_PAD = 5120
N_GROUPS = 4
EXPERTS_PER_GROUP = 8
N_EXPERTS = 32
D_EXPERT = 512
ROUTE_ROWS = 40
MOE_BLOCK = 256
LANES = 128
TOK_SUBLANES = 8
VMEM_LIMIT = 56 * 1024 * 1024


def _cparams(sem):
    return pltpu.CompilerParams(dimension_semantics=sem, vmem_limit_bytes=VMEM_LIMIT)


def _dot(a, b):
    return jnp.dot(a, b, preferred_element_type=F32)


def _dot_nt(a, b):
    return lax.dot_general(a, b, (((1,), (1,)), ((), ())), preferred_element_type=F32)


def _rms(x, g):
    return x * lax.rsqrt(jnp.mean(x * x, axis=-1, keepdims=True) + EPS) * g


def _silu(x):
    return x * jax.nn.sigmoid(x)


def _pack_pair(lo, hi):
    lo_b = lax.bitcast_convert_type(lo.astype(BF16).astype(F32), jnp.uint32)
    hi_b = lax.bitcast_convert_type(hi.astype(BF16).astype(F32), jnp.uint32)
    return hi_b | (lo_b >> 16)


def _unpack_pair(w):
    lo = lax.bitcast_convert_type(w << 16, F32)
    hi = lax.bitcast_convert_type(w & jnp.uint32(0xFFFF0000), F32)
    return lo, hi


def _ada_kernel(c_ref, w_ref, b_ref, o_ref):
    ca = _silu(c_ref[...]).astype(BF16)
    o_ref[...] = _dot(ca, w_ref[...].astype(BF16)) + b_ref[...]


def _ada(c, w, b):
    bsz, d = c.shape
    n = w.shape[1]
    tn = 1024
    return pl.pallas_call(
        _ada_kernel,
        grid=(n // tn,),
        in_specs=[pl.BlockSpec((bsz, d), lambda j: (0, 0)),
                  pl.BlockSpec((d, tn), lambda j: (0, j)),
                  pl.BlockSpec((1, tn), lambda j: (0, j))],
        out_specs=pl.BlockSpec((bsz, tn), lambda j: (0, j)),
        out_shape=jax.ShapeDtypeStruct((bsz, n), F32),
        compiler_params=_cparams(("arbitrary",)),
        name="ada",
    )(c, w, b.reshape(1, n))


NORM_ROWS = 32


def _norm_kernel(x_ref, g_ref, sh_ref, sc_ref, h_ref):
    def body(c, carry):
        r = pl.ds(pl.multiple_of(c * NORM_ROWS, NORM_ROWS), NORM_ROWS)
        h = _rms(x_ref[r, :], g_ref[...]) * (1.0 + sc_ref[0]) + sh_ref[0]
        h_ref[r, :] = h.astype(BF16)
        return carry
    lax.fori_loop(0, x_ref.shape[0] // NORM_ROWS, body, 0, unroll=2)


def _norm_mod(x2, g, sh, sc, seq):
    t, d = x2.shape
    tm = min(1024, seq)
    return pl.pallas_call(
        _norm_kernel,
        grid=(t // tm,),
        in_specs=[pl.BlockSpec((tm, d), lambda i: (i, 0)),
                  pl.BlockSpec((1, d), lambda i: (0, 0)),
                  pl.BlockSpec((1, 1, d), lambda i: (i * tm // seq, 0, 0)),
                  pl.BlockSpec((1, 1, d), lambda i: (i * tm // seq, 0, 0))],
        out_specs=pl.BlockSpec((tm, d), lambda i: (i, 0)),
        out_shape=jax.ShapeDtypeStruct((t, d), BF16),
        compiler_params=_cparams(("arbitrary",)),
        name="norm_mod",
    )(x2, g, sh, sc)


def _in_kernel(h_ref, w_ref, wt_ref, o_ref, wb_s, *, n_main):
    j = pl.program_id(0)
    first_row_tile = pl.program_id(1) == 0

    @pl.when(jnp.logical_and(first_row_tile, j < n_main))
    def _():
        wb_s[...] = w_ref[...].astype(BF16)

    @pl.when(jnp.logical_and(first_row_tile, j == n_main))
    def _():
        wb_s[...] = wt_ref[...]

    o_ref[...] = _dot_nt(h_ref[...], wb_s[...]).astype(BF16)


def _in_proj(h, w_t, seq):
    t, d = h.shape
    tn = 1024
    n_main = IN_COLS // tn
    w_tail = jnp.pad(w_t[n_main * tn:], ((0, (n_main + 1) * tn - IN_COLS), (0, 0))).astype(BF16)
    tm = min(1024, t)
    return pl.pallas_call(
        functools.partial(_in_kernel, n_main=n_main),
        grid=(IN_COLS_PAD // tn, t // tm),
        in_specs=[pl.BlockSpec((tm, d), lambda j, i: (i, 0)),
                  pl.BlockSpec((tn, d), lambda j, i: (jnp.minimum(j, n_main - 1), 0)),
                  pl.BlockSpec((tn, d), lambda j, i: (0, 0))],
        out_specs=pl.BlockSpec((tm, tn), lambda j, i: (i, j)),
        out_shape=jax.ShapeDtypeStruct((t, IN_COLS_PAD), BF16),
        scratch_shapes=[pltpu.VMEM((tn, d), BF16)],
        compiler_params=_cparams(("arbitrary", "arbitrary")),
        name="in_proj",
    )(h, w_t, w_tail)


HG_ROWS = 256


PIPELINE_STATIC_BLOCKS = 8


def _software_pipeline(stages, n_blocks):
    depth = len(stages)

    def step(i, static):
        for k in reversed(range(depth)):
            if static and not 0 <= i - k < n_blocks:
                continue
            stages[k](i - k)

    if n_blocks <= PIPELINE_STATIC_BLOCKS:
        for i in range(n_blocks + depth - 1):
            step(i, True)
        return
    for i in range(depth - 1):
        step(i, True)

    def steady(i, carry):
        step(i, False)
        return carry

    lax.fori_loop(depth - 1, n_blocks, steady, 0)
    for i in range(n_blocks, n_blocks + depth - 1):
        step(i, True)


def _chunk_mask(n):
    row = lax.broadcasted_iota(jnp.int32, (n, n), 0)
    col = lax.broadcasted_iota(jnp.int32, (n, n), 1)
    return jnp.logical_and(row // HGRN_CHUNK == col // HGRN_CHUNK, col <= row)


def _hgrn_kernel(q_ref, f_ref, i_ref, g_ref, lbl_ref, og_ref, o_ref,
                 qb_s, u_s, sp_s, dec_s, oi_s, b_s, k_s, qin_s, kin_s, ku_s, a_s, st_s, *, layer):
    seq = q_ref.shape[1]
    cs = HGRN_CHUNK
    rb = min(HG_ROWS, seq)
    nc = rb // cs
    dk = HGRN_DK
    lg = lbl_ref[...]
    ex = jnp.exp(lg - jnp.max(lg, axis=0, keepdims=True))
    sm = ex / jnp.sum(ex, axis=0, keepdims=True)
    lb = jnp.sum(sm[0:layer + 1], axis=0, keepdims=True)
    mask = _chunk_mask(rb)
    tri = mask.astype(BF16)
    row_chunk = lax.broadcasted_iota(jnp.int32, (rb, dk), 0) // cs

    chunk_sel = [(row_chunk == c).astype(BF16) for c in range(nc)]

    def rows(blk):
        return pl.ds(pl.multiple_of(blk * rb, rb), rb)

    def stage1(blk):
        r = rows(blk)
        f = lb + (1.0 - lb) * jax.nn.sigmoid(f_ref[0, r, :].astype(F32))
        lf = jnp.log2(f)
        k_s[r, :] = 1.0 - f
        hi = lf.astype(BF16)
        r1 = lf - hi.astype(F32)
        mid = r1.astype(BF16)
        lo = (r1 - mid.astype(F32)).astype(BF16)
        bhm = _dot(tri, jnp.concatenate([hi, mid], axis=1))
        b_s[r, :] = bhm[:, 0:dk] + bhm[:, dk:] + _dot(tri, lo)

    def stage2(blk):
        r = rows(blk)
        b3 = b_s[r, :].reshape(nc, cs, dk)
        bmid = b3[:, cs // 2 - 1:cs // 2, :]
        blast = b3[:, cs - 1:cs, :]
        q3 = (q_ref[0, r, :].astype(F32) * dk ** -0.5).reshape(nc, cs, dk)
        k3 = k_s[r, :].reshape(nc, cs, dk)
        qin_s[r, :] = (q3 * jnp.exp2(b3 - bmid)).reshape(rb, dk).astype(BF16)
        kin_s[r, :] = (k3 * jnp.exp2(bmid - b3)).reshape(rb, dk).astype(BF16)
        ku_s[r, :] = (k3 * jnp.exp2(blast - b3)).reshape(rb, dk).astype(BF16)
        qb_s[r, :] = (q3 * jnp.exp2(b3)).reshape(rb, dk).astype(BF16)
        dec_s[pl.ds(blk * nc, nc)] = jnp.exp2(blast)

    def stage3(blk):
        r = rows(blk)
        a_s[r, :] = jnp.where(mask, _dot_nt(qin_s[r, :], kin_s[r, :]), 0.0).astype(BF16)
        vt = i_ref[0, r, :].astype(F32).T.astype(BF16)
        ku = ku_s[r, :]
        ut = _dot(vt, jnp.concatenate([ku * sel for sel in chunk_sel], axis=1))
        for c in range(nc):
            u_s[blk * nc + c] = ut[:, c * dk:(c + 1) * dk]

    def stage4(blk):
        r = rows(blk)
        oi_s[r, :] = _dot(a_s[r, :], i_ref[0, r, :])
        st = st_s[...]
        for c in range(nc):
            sp_s[blk * nc + c] = st.astype(BF16)
            st = st * dec_s[blk * nc + c] + u_s[blk * nc + c]
        st_s[...] = st

    def stage5(blk):
        r = rows(blk)
        inter = [_dot_nt(qb_s[pl.ds(pl.multiple_of(blk * rb + c * cs, cs), cs), :], sp_s[blk * nc + c])
                 for c in range(nc)]
        o = oi_s[r, :] + jnp.concatenate(inter, axis=0)
        o = _rms(o, og_ref[...]) * _silu(g_ref[0, r, :].astype(F32))
        o_ref[0, r, :] = o.astype(BF16)

    st_s[...] = jnp.zeros_like(st_s)
    _software_pipeline([stage1, stage2, stage3, stage4, stage5], seq // rb)


def _hgrn(proj3, lb_logits, onorm_g, layer):
    bsz, seq, _ = proj3.shape
    nh = HGRN_HEADS

    def col(off):
        return pl.BlockSpec((1, seq, HGRN_DK), lambda b, h: (b, 0, off * nh + h))

    nl = lb_logits.shape[0]
    return pl.pallas_call(
        functools.partial(_hgrn_kernel, layer=layer),
        grid=(bsz, nh),
        in_specs=[col(0), col(1), col(2), col(3),
                  pl.BlockSpec((nl, HGRN_DK), lambda b, h: (0, h)),
                  pl.BlockSpec((1, HGRN_DK), lambda b, h: (0, 0))],
        out_specs=pl.BlockSpec((1, seq, HGRN_DK), lambda b, h: (b, 0, h)),
        out_shape=jax.ShapeDtypeStruct((bsz, seq, HGRN_WIDTH), BF16),
        scratch_shapes=[pltpu.VMEM((seq, HGRN_DK), BF16),
                        pltpu.VMEM((seq // HGRN_CHUNK, HGRN_DK, HGRN_DK), F32),
                        pltpu.VMEM((seq // HGRN_CHUNK, HGRN_DK, HGRN_DK), BF16),
                        pltpu.VMEM((seq // HGRN_CHUNK, 1, HGRN_DK), F32),
                        pltpu.VMEM((seq, HGRN_DK), F32),
                        pltpu.VMEM((seq, HGRN_DK), F32), pltpu.VMEM((seq, HGRN_DK), F32),
                        pltpu.VMEM((seq, HGRN_DK), BF16), pltpu.VMEM((seq, HGRN_DK), BF16),
                        pltpu.VMEM((seq, HGRN_DK), BF16), pltpu.VMEM((seq, min(HG_ROWS, seq)), BF16),
                        pltpu.VMEM((HGRN_DK, HGRN_DK), F32)],
        compiler_params=_cparams(("arbitrary", "arbitrary")),
        name="hgrn",
    )(proj3, proj3, proj3, proj3, lb_logits, onorm_g)


UP_ROWS = 1024


def _up_kernel(p_ref, pos_ref, wq_ref, wkv_ref, qag_ref, kvag_ref, qg_ref, kg_ref, freq_ref,
               q_ref, k_ref, v_ref):
    half = MLA_ROPE // 2
    p = p_ref[...].astype(F32)
    qa_n = _rms(p[:, 0:Q_LORA], qag_ref[...]).astype(BF16)
    kva_n = _rms(p[:, Q_LORA:Q_LORA + KV_LORA], kvag_ref[...]).astype(BF16)
    kpe_t = p[:, Q_LORA + KV_LORA:Q_LORA + KV_LORA + LANES].T[0:MLA_ROPE]
    ang = freq_ref[...] * pos_ref[...].astype(F32)
    cos = jnp.cos(ang)
    sin = jnp.sin(ang)

    def rope(x):
        x1, x2 = x[0:half], x[half:]
        return jnp.concatenate([x1 * cos - x2 * sin, x2 * cos + x1 * sin], axis=0)

    kg = kg_ref[...]
    kpe_ss = jnp.sum(kpe_t * kpe_t, axis=0, keepdims=True)
    kpe_rot = rope(kpe_t * kg[MLA_NOPE:])
    qg = qg_ref[...] * (MLA_QK ** -0.5 * LOG2E)
    zpad = jnp.zeros((MLA_QPAD - MLA_QK, p.shape[0]), F32)
    for h in range(MLA_HEADS):
        rows = slice(h * MLA_QPAD, (h + 1) * MLA_QPAD)
        qt = _dot_nt(wq_ref[rows, :], qa_n)
        qn = qt[0:MLA_QK] * lax.rsqrt(jnp.sum(qt * qt, axis=0, keepdims=True) / MLA_QK + EPS) * qg
        q_ref[0, h] = jnp.concatenate([qn[0:MLA_NOPE], rope(qn[MLA_NOPE:]), zpad], axis=0).astype(BF16)
        kvt = _dot_nt(wkv_ref[rows, :], kva_n)
        kn = kvt[0:MLA_NOPE]
        rk = lax.rsqrt((jnp.sum(kn * kn, axis=0, keepdims=True) + kpe_ss) / MLA_QK + EPS)
        kt = jnp.concatenate([kn * rk * kg[0:MLA_NOPE], kpe_rot * rk, zpad], axis=0)
        k_ref[0, h] = kt.T.astype(BF16)
        v_ref[0, h] = kvt[MLA_NOPE:].astype(BF16)


def _mla_up(proj3, pos3, wq_t, wkv_t, qag, kvag, qg, kg, freq):
    bsz, seq, _ = proj3.shape
    tm = min(UP_ROWS, seq)
    nh = MLA_HEADS
    mla_block = 4 * HGRN_WIDTH // 1024

    def const(shape):
        return pl.BlockSpec(shape, lambda b, i: (0,) * len(shape))

    return pl.pallas_call(
        _up_kernel,
        grid=(bsz, seq // tm),
        in_specs=[pl.BlockSpec((None, tm, 1024), lambda b, i: (b, i, mla_block)),
                  pl.BlockSpec((None, 1, tm), lambda b, i: (b, 0, i)),
                  const(wq_t.shape), const(wkv_t.shape), const(qag.shape), const(kvag.shape),
                  const(qg.shape), const(kg.shape), const(freq.shape)],
        out_specs=[pl.BlockSpec((1, nh, MLA_QPAD, tm), lambda b, i: (b, 0, 0, i)),
                   pl.BlockSpec((1, nh, tm, MLA_QPAD), lambda b, i: (b, 0, i, 0)),
                   pl.BlockSpec((1, nh, MLA_V, tm), lambda b, i: (b, 0, 0, i))],
        out_shape=[jax.ShapeDtypeStruct((bsz, nh, MLA_QPAD, seq), BF16),
                   jax.ShapeDtypeStruct((bsz, nh, seq, MLA_QPAD), BF16),
                   jax.ShapeDtypeStruct((bsz, nh, MLA_V, seq), BF16)],
        compiler_params=_cparams(("arbitrary", "arbitrary")),
        name="mla_up",
    )(proj3, pos3, wq_t, wkv_t, qag, kvag, qg, kg, freq)


ATT_T = 256


def _attn_kernel(qt_ref, k_ref, vt_ref, g_ref, o_ref):
    seq = k_ref.shape[2]
    t = min(ATT_T, seq)
    key = lax.broadcasted_iota(jnp.int32, (t, t), 0)
    qry = lax.broadcasted_iota(jnp.int32, (t, t), 1)
    causal = key <= qry
    neg = jnp.finfo(F32).min

    def widen(x, off, fill):
        return x if off == 0 else jnp.concatenate([jnp.full((x.shape[0], off), fill, x.dtype), x], axis=1)

    scores = []
    m = None
    for off in range(0, seq, t):
        s = _dot(k_ref[0, 0, off:off + t, :], qt_ref[0, 0, :, off:])
        diag = jnp.where(causal, s[:, 0:t], neg)
        s = diag if off + t == seq else jnp.concatenate([diag, s[:, t:]], axis=1)
        scores.append(s)
        blk_max = widen(jnp.max(s, axis=0, keepdims=True), off, neg)
        m = blk_max if m is None else jnp.maximum(m, blk_max)
    l = None
    ot = None
    for off, s in zip(range(0, seq, t), scores):
        p = jnp.exp2(s - m[:, off:])
        p_sum = widen(jnp.sum(p, axis=0, keepdims=True), off, 0.0)
        pv = widen(_dot(vt_ref[0, 0, :, off:off + t], p.astype(BF16)), off, 0.0)
        l = p_sum if l is None else l + p_sum
        ot = pv if ot is None else ot + pv
    ot = ot * (1.0 / l)
    ot = ot * lax.rsqrt(jnp.mean(ot * ot, axis=0, keepdims=True) + EPS) * g_ref[...]
    o_ref[0, :, :] = ot.T.astype(BF16)


def _attention(q, k, v, g):
    bsz, nh, seq, _ = k.shape
    return pl.pallas_call(
        _attn_kernel,
        grid=(bsz, nh),
        in_specs=[pl.BlockSpec((1, 1, MLA_QPAD, seq), lambda b, h: (b, h, 0, 0)),
                  pl.BlockSpec((1, 1, seq, MLA_QPAD), lambda b, h: (b, h, 0, 0)),
                  pl.BlockSpec((1, 1, MLA_V, seq), lambda b, h: (b, h, 0, 0)),
                  pl.BlockSpec((MLA_V, 1), lambda b, h: (0, 0))],
        out_specs=pl.BlockSpec((1, seq, MLA_V), lambda b, h: (b, 0, h)),
        out_shape=jax.ShapeDtypeStruct((bsz, seq, nh * MLA_V), BF16),
        compiler_params=_cparams(("arbitrary", "arbitrary")),
        name="attn",
    )(q, k, v, g)


def _out_kernel(oa_ref, ob_ref, x_ref, wa_ref, wb_ref, g1_ref, n2g_ref, sh2_ref, sc2_ref,
                wr_ref, br_ref, x1_ref, h2_ref, lg_ref, mix_s):
    @pl.when(pl.program_id(0) == 0)
    def _():
        mix_s[...] = jnp.zeros_like(mix_s)

    mix_next = _dot(oa_ref[...], wa_ref[...]) + _dot(ob_ref[...], wb_ref[...])
    x1 = x_ref[...] + g1_ref[0] * mix_s[...]
    x1_ref[...] = x1
    h2 = _rms(x1, n2g_ref[...]) * (1.0 + sc2_ref[0]) + sh2_ref[0]
    tm = h2.shape[0]
    for s in range(TOK_SUBLANES):
        h2_ref[pl.ds(s, tm, stride=TOK_SUBLANES), :] = _pack_pair(
            h2[:, s * LANES:(s + 1) * LANES], h2[:, (s + TOK_SUBLANES) * LANES:(s + TOK_SUBLANES + 1) * LANES])
    lg = _dot(h2.astype(BF16), wr_ref[...]) + br_ref[...]
    lg_ref[...] = lg.T[0:ROUTE_ROWS, :]
    mix_s[...] = mix_next


def _out_proj(oa, ob, x2, wa, wb, g1, n2g, sh2, sc2, wr, br, seq):
    t, d = x2.shape
    tm = min(512, seq)
    n_tiles = t // tm

    def const(shape):
        return pl.BlockSpec(shape, lambda i: (0,) * len(shape))

    def prev(i):
        return jnp.maximum(i - 1, 0)

    def per_batch():
        return pl.BlockSpec((1, 1, d), lambda i: (prev(i) * tm // seq, 0, 0))

    return pl.pallas_call(
        _out_kernel,
        grid=(n_tiles + 1,),
        in_specs=[pl.BlockSpec((tm, HGRN_WIDTH), lambda i: (jnp.minimum(i, n_tiles - 1), 0)),
                  pl.BlockSpec((tm, HGRN_WIDTH), lambda i: (jnp.minimum(i, n_tiles - 1), 0)),
                  pl.BlockSpec((tm, d), lambda i: (prev(i), 0)),
                  const(wa.shape), const(wb.shape), per_batch(), const(n2g.shape),
                  per_batch(), per_batch(), const(wr.shape), const(br.shape)],
        out_specs=[pl.BlockSpec((tm, d), lambda i: (prev(i), 0)),
                   pl.BlockSpec((tm * TOK_SUBLANES, LANES), lambda i: (prev(i), 0)),
                   pl.BlockSpec((ROUTE_ROWS, tm), lambda i: (0, prev(i)))],
        out_shape=[jax.ShapeDtypeStruct((t, d), F32),
                   jax.ShapeDtypeStruct((t * TOK_SUBLANES, LANES), jnp.uint32),
                   jax.ShapeDtypeStruct((ROUTE_ROWS, t), F32)],
        scratch_shapes=[pltpu.VMEM((tm, d), F32)],
        compiler_params=_cparams(("arbitrary",)),
        name="out_proj",
    )(oa, ob, x2, wa, wb, g1, n2g, sh2, sc2, wr, br)


def _route_kernel(lg_ref, tri_ref, ri_ref, rw_ref, cnt_ref, carry_s):
    step = pl.program_id(0)

    @pl.when(step == 0)
    def _():
        carry_s[...] = jnp.zeros_like(carry_s)

    lg = lg_ref[...]
    tr = lg.shape[1]
    epg = EXPERTS_PER_GROUP
    gl = lg[N_EXPERTS:N_EXPERTS + N_GROUPS, :]
    row_g = lax.broadcasted_iota(jnp.int32, (N_GROUPS, tr), 0)
    gmax = jnp.max(gl, axis=0, keepdims=True)
    g_sel = jnp.min(jnp.where(gl == gmax, row_g, N_GROUPS), axis=0, keepdims=True)
    p_group = 1.0 / jnp.sum(jnp.exp(gl - gmax), axis=0, keepdims=True)

    e_in = lg[0:epg, :]
    for g in range(1, N_GROUPS):
        e_in = jnp.where(g_sel == g, lg[g * epg:(g + 1) * epg, :], e_in)
    row_e = lax.broadcasted_iota(jnp.int32, (epg, tr), 0)
    top1 = jnp.max(e_in, axis=0, keepdims=True)
    i1 = jnp.min(jnp.where(e_in == top1, row_e, epg), axis=0, keepdims=True)
    rest = jnp.where(row_e == i1, -jnp.inf, e_in)
    top2 = jnp.max(rest, axis=0, keepdims=True)
    i2 = jnp.min(jnp.where(rest == top2, row_e, epg), axis=0, keepdims=True)
    e2w = jnp.exp(top2 - top1)
    w1 = p_group / (1.0 + e2w)
    w2 = p_group * e2w / (1.0 + e2w)
    ex1 = g_sel * epg + i1
    ex2 = g_sel * epg + i2

    row_x = lax.broadcasted_iota(jnp.int32, (N_EXPERTS, tr), 0)
    oh1 = row_x == ex1
    oh2 = row_x == ex2
    oh = jnp.logical_or(oh1, oh2)
    before = _dot(oh.astype(BF16), tri_ref[...]) + carry_s[:, 0:1]
    rank1 = jnp.sum(jnp.where(oh1, before, 0.0), axis=0, keepdims=True)
    rank2 = jnp.sum(jnp.where(oh2, before, 0.0), axis=0, keepdims=True)
    carry_s[...] = carry_s[...] + jnp.sum(oh.astype(F32), axis=1, keepdims=True)

    zi = jnp.zeros((4, tr), jnp.int32)
    ri_ref[...] = jnp.concatenate([ex1, ex2, rank1.astype(jnp.int32), rank2.astype(jnp.int32), zi], axis=0)
    rw_ref[...] = jnp.concatenate([w1, w2, jnp.zeros((6, tr), F32)], axis=0)
    cnt_ref[...] = carry_s[...].astype(jnp.int32)


def _route(lg_t, tri):
    t = lg_t.shape[1]
    tr = tri.shape[0]
    return pl.pallas_call(
        _route_kernel,
        grid=(t // tr,),
        in_specs=[pl.BlockSpec((ROUTE_ROWS, tr), lambda i: (0, i)),
                  pl.BlockSpec((tr, tr), lambda i: (0, 0))],
        out_specs=[pl.BlockSpec((8, tr), lambda i: (0, i)),
                   pl.BlockSpec((8, tr), lambda i: (0, i)),
                   pl.BlockSpec((N_EXPERTS, LANES), lambda i: (0, 0))],
        out_shape=[jax.ShapeDtypeStruct((8, t), jnp.int32),
                   jax.ShapeDtypeStruct((8, t), F32),
                   jax.ShapeDtypeStruct((N_EXPERTS, LANES), jnp.int32)],
        scratch_shapes=[pltpu.VMEM((N_EXPERTS, LANES), F32)],
        compiler_params=_cparams(("arbitrary",)),
        name="route",
    )(lg_t, tri)


def _moe_kernel(rd_ref, be_ref, pe_ref, nb_ref, x_ref, wg_ref, wu_ref, wd_ref, y_ref,
                ys, wg_f, wu_f, wd_f, wg_s, wu_s, wd_s, ssem, wsem):
    b = pl.program_id(0)
    last = pl.num_programs(0) - 1
    nb = nb_ref[0]
    tm = MOE_BLOCK
    ts = TOK_SUBLANES
    weights = ((wg_ref, wg_f, wg_s), (wu_ref, wu_f, wu_s), (wd_ref, wd_f, wd_s))

    def fetch(e):
        return [pltpu.make_async_copy(src.at[e], stage, wsem.at[k]) for k, (src, stage, _) in enumerate(weights)]

    def scatter(blk):
        base = blk * tm
        for j in range(tm):
            dst = pl.multiple_of(rd_ref[base + j] * ts, ts)
            pltpu.make_async_copy(ys.at[pl.ds(j * ts, ts), :], y_ref.at[pl.ds(dst, ts), :], ssem).start()

    def wait_scatter():
        pltpu.make_async_copy(ys, y_ref.at[pl.ds(0, tm * ts), :], ssem).wait()

    @pl.when(b == 0)
    def _():
        for copy in fetch(be_ref[0]):
            copy.start()
        ys[...] = jnp.zeros_like(ys)
        spare = pltpu.make_async_copy(ys, y_ref.at[pl.ds(y_ref.shape[0] - tm * ts, tm * ts), :], ssem)
        spare.start()
        spare.wait()

    expert = be_ref[b]
    first_of_expert = jnp.logical_or(b == 0, expert != be_ref[jnp.maximum(b - 1, 0)])

    @pl.when(jnp.logical_and(b < nb, first_of_expert))
    def _():
        for copy in fetch(0):
            copy.wait()
        for _, stage, dst in weights:
            dst[...] = stage[...].astype(BF16)
        next_blk = pe_ref[expert] // tm

        @pl.when(next_blk < nb)
        def _():
            for copy in fetch(be_ref[next_blk]):
                copy.start()

    @pl.when(b < nb)
    def _():
        scatter(jnp.maximum(b - 1, 0))
        parts = [_unpack_pair(x_ref[pl.ds(s, tm, stride=ts), :]) for s in range(ts)]
        x = jnp.concatenate([p[0] for p in parts] + [p[1] for p in parts], axis=1).astype(BF16)
        hid = _silu(_dot(x, wg_s[...])) * _dot(x, wu_s[...])
        y = _dot(hid.astype(BF16), wd_s[...])
        wait_scatter()
        for s in range(ts):
            ys[pl.ds(s, tm, stride=ts), :] = _pack_pair(y[:, s * LANES:(s + 1) * LANES],
                                                        y[:, (s + ts) * LANES:(s + ts + 1) * LANES])

        @pl.when(b == last)
        def _():
            scatter(b)
            wait_scatter()

    @pl.when(b == nb)
    def _():
        scatter(b - 1)
        wait_scatter()


def _moe(row_dst, blk_e, pad_end, nb_real, x_sorted, wg, wu, wd, n_out_rows):
    n_blocks = blk_e.shape[0]
    d, de = wg.shape[1], wg.shape[2]
    tm = MOE_BLOCK
    hbm = pl.BlockSpec(memory_space=pl.ANY)
    return pl.pallas_call(
        _moe_kernel,
        grid_spec=pltpu.PrefetchScalarGridSpec(
            num_scalar_prefetch=4,
            grid=(n_blocks,),
            in_specs=[pl.BlockSpec((tm * TOK_SUBLANES, LANES),
                                   lambda b, rd, be, pe, nb: (jnp.minimum(b, nb[0] - 1), 0)),
                      hbm, hbm, hbm],
            out_specs=hbm,
            scratch_shapes=[pltpu.VMEM((tm * TOK_SUBLANES, LANES), jnp.uint32),
                            pltpu.VMEM((d, de), F32), pltpu.VMEM((d, de), F32), pltpu.VMEM((de, d), F32),
                            pltpu.VMEM((d, de), BF16), pltpu.VMEM((d, de), BF16), pltpu.VMEM((de, d), BF16),
                            pltpu.SemaphoreType.DMA(()), pltpu.SemaphoreType.DMA((3,))]),
        out_shape=jax.ShapeDtypeStruct((n_out_rows * TOK_SUBLANES, LANES), jnp.uint32),
        compiler_params=_cparams(("arbitrary",)),
        name="moe",
    )(row_dst, blk_e, pad_end, nb_real, x_sorted, wg, wu, wd)


def _dispatch_kernel(e1_ref, e2_ref, r1_ref, r2_ref, ps_ref, pe_ref, cnt_ref, be_ref, nb_ref,
                     h_ref, x_ref, rd_ref, zbuf, sem, zsem):
    i = pl.program_id(0)
    ts = TOK_SUBLANES
    td = h_ref.shape[0] // ts
    n_tok = e1_ref.shape[0]
    base = i * td
    blk_rows = MOE_BLOCK * ts

    def zero_fill(blk):
        return pltpu.make_async_copy(
            zbuf, x_ref.at[pl.ds(pl.multiple_of(blk * blk_rows, blk_rows), blk_rows), :], zsem)

    @pl.when(i == 0)
    def _():
        zbuf[...] = jnp.zeros_like(zbuf)

        def fill(blk, n_fills):
            e = be_ref[blk]
            partial_last = jnp.logical_and(blk == pe_ref[e] // MOE_BLOCK - 1, cnt_ref[e] % MOE_BLOCK != 0)
            needs_fill = jnp.logical_or(blk >= nb_ref[0], partial_last)

            @pl.when(needs_fill)
            def _():
                zero_fill(blk).start()
            return n_fills + needs_fill.astype(jnp.int32)

        n_fills = lax.fori_loop(0, be_ref.shape[0], fill, 0)

        def init(blk, carry):
            for j in range(MOE_BLOCK):
                rd_ref[blk * MOE_BLOCK + j] = 2 * n_tok + j
            return carry

        lax.fori_loop(0, rd_ref.shape[0] // MOE_BLOCK, init, 0)

        def drain(k, carry):
            zero_fill(0).wait()
            return carry

        lax.fori_loop(0, n_fills, drain, 0)

    def start(t, carry):
        src = h_ref.at[pl.ds(pl.multiple_of(t * ts, ts), ts), :]
        tok = base + t
        for slot, (e_ref, r_ref) in enumerate(((e1_ref, r1_ref), (e2_ref, r2_ref))):
            row = ps_ref[e_ref[tok]] + r_ref[tok]
            pltpu.make_async_copy(src, x_ref.at[pl.ds(pl.multiple_of(row * ts, ts), ts), :], sem).start()
            rd_ref[row] = 2 * tok + slot
        return carry

    lax.fori_loop(0, td, start, 0, unroll=8)
    for _ in range(2):
        pltpu.make_async_copy(h_ref, x_ref.at[pl.ds(0, td * ts), :], sem).wait()


def _dispatch(ri, pad_start, pad_end, counts, blk_e, nb_real, h2p, n_rows):
    t = ri.shape[1]
    td = min(512, t)
    ts = TOK_SUBLANES
    return pl.pallas_call(
        _dispatch_kernel,
        grid_spec=pltpu.PrefetchScalarGridSpec(
            num_scalar_prefetch=9,
            grid=(t // td,),
            in_specs=[pl.BlockSpec((td * ts, LANES), lambda i, *_: (i, 0))],
            out_specs=[pl.BlockSpec(memory_space=pl.ANY), pl.BlockSpec(memory_space=pltpu.SMEM)],
            scratch_shapes=[pltpu.VMEM((MOE_BLOCK * ts, LANES), jnp.uint32),
                            pltpu.SemaphoreType.DMA(()), pltpu.SemaphoreType.DMA(())]),
        out_shape=[jax.ShapeDtypeStruct((n_rows * ts, LANES), jnp.uint32),
                   jax.ShapeDtypeStruct((n_rows,), jnp.int32)],
        compiler_params=_cparams(("arbitrary",)),
        name="dispatch",
    )(ri[0], ri[1], ri[2], ri[3], pad_start, pad_end, counts, blk_e, nb_real, h2p)


def _moe_plan(counts, t):
    tm = MOE_BLOCK
    n_blocks = 2 * t // tm + N_EXPERTS
    padded = ((counts + tm - 1) // tm) * tm
    pad_end = jnp.cumsum(padded)
    pad_start = pad_end - padded
    blk = jnp.arange(n_blocks, dtype=jnp.int32)
    blk_e = jnp.minimum(jnp.sum(pad_end[None, :] <= (blk * tm)[:, None], axis=1), N_EXPERTS - 1).astype(jnp.int32)
    nb_real = (pad_end[-1:] // tm).astype(jnp.int32)
    return pad_start, pad_end, blk_e, nb_real, n_blocks * tm


COMBINE_ROWS = 64


def _combine_kernel(x1_ref, g2_ref, w1_ref, w2_ref, y_ref, o_ref):
    ts = TOK_SUBLANES
    rows = min(COMBINE_ROWS, x1_ref.shape[0])

    def chunk(c, carry):
        r = pl.ds(pl.multiple_of(c * rows, rows), rows)
        w1 = w1_ref[r, :]
        w2 = w2_ref[r, :]
        tile0 = pl.multiple_of(c * rows * 2 * ts, rows * 2 * ts)
        for s in range(ts):
            a_lo, a_hi = _unpack_pair(y_ref[pl.ds(tile0 + s, rows, stride=2 * ts), :])
            b_lo, b_hi = _unpack_pair(y_ref[pl.ds(tile0 + ts + s, rows, stride=2 * ts), :])
            for col, ya, yb in ((s, a_lo, b_lo), (s + ts, a_hi, b_hi)):
                cols = slice(col * LANES, (col + 1) * LANES)
                o_ref[r, cols] = x1_ref[r, cols] + g2_ref[0][:, cols] * (w1 * ya + w2 * yb)
        return carry

    lax.fori_loop(0, x1_ref.shape[0] // rows, chunk, 0)


def _combine(x1, g2, w1, w2, y2, seq):
    t, d = x1.shape
    tc = min(512, seq)
    return pl.pallas_call(
        _combine_kernel,
        grid=(t // tc,),
        in_specs=[pl.BlockSpec((tc, d), lambda i: (i, 0)),
                  pl.BlockSpec((1, 1, d), lambda i: (i * tc // seq, 0, 0)),
                  pl.BlockSpec((tc, 1), lambda i: (i, 0)),
                  pl.BlockSpec((tc, 1), lambda i: (i, 0)),
                  pl.BlockSpec((tc * 2 * TOK_SUBLANES, LANES), lambda i: (i, 0))],
        out_specs=pl.BlockSpec((tc, d), lambda i: (i, 0)),
        out_shape=jax.ShapeDtypeStruct((t, d), F32),
        compiler_params=_cparams(("arbitrary",)),
        name="combine",
    )(x1, g2, w1, w2, y2)


def _q_up_layout(w_q_up):
    w = w_q_up.reshape(Q_LORA, MLA_HEADS, MLA_QK)
    w = jnp.pad(w, ((0, 0), (0, 0), (0, MLA_QPAD - MLA_QK)))
    return w.reshape(Q_LORA, MLA_HEADS * MLA_QPAD).T.astype(BF16)


def _pad_lanes(g, width):
    return jnp.pad(g, (0, width - g.shape[0])).reshape(1, width)


def kernel(x, c, positions, w_ada, b_ada, norm1_g, w_in, hgrn_lb_logits, hgrn_onorm_g, q_a_norm_g, w_q_up,
           kv_a_norm_g, w_kv_up, q_norm_g, k_norm_g, attn_onorm_g, w_out, norm2_g, w_group, b_group,
           w_router, b_router, w_gate, w_up, w_down):
    bsz, seq, d = x.shape
    t = bsz * seq
    depth = w_ada.shape[0]
    half = MLA_ROPE // 2
    inv_freq = ROPE_BASE ** (-jnp.arange(0, MLA_ROPE, 2, dtype=F32) / MLA_ROPE)
    freq = inv_freq.reshape(half, 1)
    pos3 = positions.reshape(bsz, 1, seq)
    tr = min(512, t)
    tri = jnp.triu(jnp.ones((tr, tr), BF16), 1)

    x2 = x.reshape(t, d)
    for l in range(depth):
        mod = _ada(c, w_ada[l], b_ada[l]).reshape(bsz, 6, 1, d)
        sh1, sc1, g1, sh2, sc2, g2 = (mod[:, i] for i in range(6))

        proj = _in_proj(_norm_mod(x2, norm1_g[l].reshape(1, d), sh1, sc1, seq), w_in[l].T, seq)
        proj3 = proj.reshape(bsz, seq, IN_COLS_PAD)

        o_a = _hgrn(proj3, hgrn_lb_logits, hgrn_onorm_g[l].reshape(1, HGRN_DK), l)

        q, k, v = _mla_up(proj3, pos3, _q_up_layout(w_q_up[l]), w_kv_up[l].T.astype(BF16),
                          q_a_norm_g[l].reshape(1, Q_LORA), kv_a_norm_g[l].reshape(1, KV_LORA),
                          q_norm_g[l].reshape(MLA_QK, 1), k_norm_g[l].reshape(MLA_QK, 1), freq)
        o_b = _attention(q, k, v, attn_onorm_g[l].reshape(MLA_V, 1))

        w_o = w_out[l].astype(BF16)
        wr = jnp.pad(jnp.concatenate([w_router[l], w_group[l]], axis=1),
                     ((0, 0), (0, LANES - N_EXPERTS - N_GROUPS))).astype(BF16)
        br = _pad_lanes(jnp.concatenate([b_router[l], b_group[l]]), LANES)
        x1, h2, lg_t = _out_proj(o_a.reshape(t, HGRN_WIDTH), o_b.reshape(t, HGRN_WIDTH), x2,
                                 w_o[:HGRN_WIDTH], w_o[HGRN_WIDTH:], g1, norm2_g[l].reshape(1, d),
                                 sh2, sc2, wr, br, seq)

        ri, rw, cnt = _route(lg_t, tri)
        counts = cnt[:, 0]
        pad_start, pad_end, blk_e, nb_real, n_rows = _moe_plan(counts, t)
        x_sorted, row_dst = _dispatch(ri, pad_start, pad_end, counts, blk_e, nb_real, h2, n_rows)
        y2 = _moe(row_dst, blk_e, pad_end, nb_real, x_sorted, w_gate[l], w_up[l], w_down[l], 2 * t + MOE_BLOCK)
        x2 = _combine(x1, g2, rw[0].reshape(t, 1), rw[1].reshape(t, 1), y2, seq)
    return x2.reshape(bsz, seq, d)
```

```python
import functools

import jax
import jax.numpy as jnp
from jax import lax
from jax.experimental import pallas as pl
from jax.experimental.pallas import tpu as pltpu

F32 = jnp.float32
BF16 = jnp.bfloat16
EPS = 1e-6
LOG2E = 1.4426950408889634

D_MODEL = 2048
HGRN_WIDTH = 1024
HGRN_DK = 128
HGRN_HEADS = 8
HGRN_CHUNK = 64
MLA_HEADS = 8
MLA_NOPE = 128
MLA_ROPE = 64
MLA_QK = MLA_NOPE + MLA_ROPE
MLA_V = 128
MLA_QPAD = 256
Q_LORA = 512
KV_LORA = 256
ROPE_BASE = 10000.0
IN_COLS = 4 * HGRN_WIDTH + Q_LORA + KV_LORA + MLA_ROPE
IN_COLS_PAD = 5120
N_GROUPS = 4
EXPERTS_PER_GROUP = 8
N_EXPERTS = 32
D_EXPERT = 512
ROUTE_ROWS = 40
MOE_BLOCK = 256
LANES = 128
TOK_SUBLANES = 8
VMEM_LIMIT = 56 * 1024 * 1024


def _cparams(sem):
    return pltpu.CompilerParams(dimension_semantics=sem, vmem_limit_bytes=VMEM_LIMIT)


def _dot(a, b):
    return jnp.dot(a, b, preferred_element_type=F32)


def _dot_nt(a, b):
    return lax.dot_general(a, b, (((1,), (1,)), ((), ())), preferred_element_type=F32)


def _rms(x, g):
    return x * lax.rsqrt(jnp.mean(x * x, axis=-1, keepdims=True) + EPS) * g


def _silu(x):
    return x * jax.nn.sigmoid(x)


def _pack_pair(lo, hi):
    lo_b = lax.bitcast_convert_type(lo.astype(BF16).astype(F32), jnp.uint32)
    hi_b = lax.bitcast_convert_type(hi.astype(BF16).astype(F32), jnp.uint32)
    return hi_b | (lo_b >> 16)


def _unpack_pair(w):
    lo = lax.bitcast_convert_type(w << 16, F32)
    hi = lax.bitcast_convert_type(w & jnp.uint32(0xFFFF0000), F32)
    return lo, hi


def _ada_kernel(c_ref, w_ref, b_ref, o_ref):
    ca = _silu(c_ref[...]).astype(BF16)
    o_ref[...] = _dot(ca, w_ref[...].astype(BF16)) + b_ref[...]


def _ada(c, w, b):
    bsz, d = c.shape
    n = w.shape[1]
    tn = 1024
    return pl.pallas_call(
        _ada_kernel,
        grid=(n // tn,),
        in_specs=[pl.BlockSpec((bsz, d), lambda j: (0, 0)),
                  pl.BlockSpec((d, tn), lambda j: (0, j)),
                  pl.BlockSpec((1, tn), lambda j: (0, j))],
        out_specs=pl.BlockSpec((bsz, tn), lambda j: (0, j)),
        out_shape=jax.ShapeDtypeStruct((bsz, n), F32),
        compiler_params=_cparams(("arbitrary",)),
        name="ada",
    )(c, w, b.reshape(1, n))


NORM_ROWS = 32


def _norm_kernel(x_ref, g_ref, sh_ref, sc_ref, h_ref):
    def body(c, carry):
        r = pl.ds(pl.multiple_of(c * NORM_ROWS, NORM_ROWS), NORM_ROWS)
        h = _rms(x_ref[r, :], g_ref[...]) * (1.0 + sc_ref[0]) + sh_ref[0]
        h_ref[r, :] = h.astype(BF16)
        return carry
    lax.fori_loop(0, x_ref.shape[0] // NORM_ROWS, body, 0, unroll=2)


def _norm_mod(x2, g, sh, sc, seq, n_rows):
    t, d = x2.shape
    tm = min(1024, seq, n_rows)
    return pl.pallas_call(
        _norm_kernel,
        grid=(n_rows // tm,),
        in_specs=[pl.BlockSpec((tm, d), lambda i: (i, 0)),
                  pl.BlockSpec((1, d), lambda i: (0, 0)),
                  pl.BlockSpec((1, 1, d), lambda i: (i * tm // seq, 0, 0)),
                  pl.BlockSpec((1, 1, d), lambda i: (i * tm // seq, 0, 0))],
        out_specs=pl.BlockSpec((tm, d), lambda i: (i, 0)),
        out_shape=jax.ShapeDtypeStruct((n_rows, d), BF16),
        compiler_params=_cparams(("arbitrary",)),
        name="norm_mod",
    )(x2, g, sh, sc)


IN_NORM_CHUNKS = 4


def _in_kernel(h0_ref, x_ref, g_ref, sh_ref, sc_ref, w_ref, o_ref, h_s):
    i = pl.program_id(0)
    j = pl.program_id(1)
    slot = i % 2

    @pl.when(jnp.logical_and(i == 0, j == 0))
    def _():
        h_s[0] = h0_ref[...]

    o_ref[...] = _dot_nt(h_s[slot], w_ref[...]).astype(BF16)
    rows = x_ref.shape[0] // IN_NORM_CHUNKS
    first = jnp.minimum(j, IN_NORM_CHUNKS - 1) * rows
    gain = g_ref[...]
    scale = 1.0 + sc_ref[0]
    shift = sh_ref[0]
    for c in range(rows // NORM_ROWS):
        r = pl.ds(pl.multiple_of(first + c * NORM_ROWS, NORM_ROWS), NORM_ROWS)
        h_s[1 - slot, r, :] = (_rms(x_ref[r, :], gain) * scale + shift).astype(BF16)


def _in_proj(x2, g, sh, sc, w_t, seq):
    t, d = x2.shape
    tn = 1024
    tm = min(1024, seq)
    n_tiles = t // tm
    w_b = jnp.pad(w_t, ((0, IN_COLS_PAD - IN_COLS), (0, 0))).astype(BF16)
    h0 = _norm_mod(x2, g, sh, sc, seq, tm)

    def nxt(i):
        return jnp.minimum(i + 1, n_tiles - 1)

    return pl.pallas_call(
        _in_kernel,
        grid=(n_tiles, IN_COLS_PAD // tn),
        in_specs=[pl.BlockSpec((tm, d), lambda i, j: (0, 0)),
                  pl.BlockSpec((tm, d), lambda i, j: (nxt(i), 0)),
                  pl.BlockSpec((1, d), lambda i, j: (0, 0)),
                  pl.BlockSpec((1, 1, d), lambda i, j: (nxt(i) * tm // seq, 0, 0)),
                  pl.BlockSpec((1, 1, d), lambda i, j: (nxt(i) * tm // seq, 0, 0)),
                  pl.BlockSpec((tn, d), lambda i, j: (j, 0))],
        out_specs=pl.BlockSpec((tm, tn), lambda i, j: (i, j)),
        out_shape=jax.ShapeDtypeStruct((t, IN_COLS_PAD), BF16),
        scratch_shapes=[pltpu.VMEM((2, tm, d), BF16)],
        compiler_params=_cparams(("arbitrary", "arbitrary")),
        name="in_proj",
    )(h0, x2, g, sh, sc, w_b)


HG_ROWS = 256


PIPELINE_STATIC_BLOCKS = 8


def _software_pipeline(stages, n_blocks):
    depth = len(stages)

    def step(i, static):
        for k in reversed(range(depth)):
            if static and not 0 <= i - k < n_blocks:
                continue
            stages[k](i - k)

    if n_blocks <= PIPELINE_STATIC_BLOCKS:
        for i in range(n_blocks + depth - 1):
            step(i, True)
        return
    for i in range(depth - 1):
        step(i, True)

    def steady(i, carry):
        step(i, False)
        return carry

    lax.fori_loop(depth - 1, n_blocks, steady, 0)
    for i in range(n_blocks, n_blocks + depth - 1):
        step(i, True)


def _chunk_mask(n):
    row = lax.broadcasted_iota(jnp.int32, (n, n), 0)
    col = lax.broadcasted_iota(jnp.int32, (n, n), 1)
    return jnp.logical_and(row // HGRN_CHUNK == col // HGRN_CHUNK, col <= row)


def _hgrn_kernel(q_ref, f_ref, i_ref, g_ref, lbl_ref, og_ref, o_ref,
                 qb_s, u_s, sp_s, dec_s, oi_s, b_s, k_s, qin_s, kin_s, ku_s, a_s, st_s, *, layer):
    seq = q_ref.shape[1]
    cs = HGRN_CHUNK
    rb = min(HG_ROWS, seq)
    nc = rb // cs
    dk = HGRN_DK
    lg = lbl_ref[...]
    ex = jnp.exp(lg - jnp.max(lg, axis=0, keepdims=True))
    sm = ex / jnp.sum(ex, axis=0, keepdims=True)
    lb = jnp.sum(sm[0:layer + 1], axis=0, keepdims=True)
    mask = _chunk_mask(rb)
    tri = mask.astype(BF16)
    row_chunk = lax.broadcasted_iota(jnp.int32, (rb, dk), 0) // cs

    chunk_sel = [(row_chunk == c).astype(BF16) for c in range(nc)]

    def rows(blk):
        return pl.ds(pl.multiple_of(blk * rb, rb), rb)

    def stage1(blk):
        r = rows(blk)
        f = lb + (1.0 - lb) * jax.nn.sigmoid(f_ref[0, r, :].astype(F32))
        lf = jnp.log2(f)
        k_s[r, :] = 1.0 - f
        hi = lf.astype(BF16)
        r1 = lf - hi.astype(F32)
        mid = r1.astype(BF16)
        lo = (r1 - mid.astype(F32)).astype(BF16)
        bhm = _dot(tri, jnp.concatenate([hi, mid], axis=1))
        b_s[r, :] = bhm[:, 0:dk] + bhm[:, dk:] + _dot(tri, lo)

    def stage2(blk):
        r = rows(blk)
        b3 = b_s[r, :].reshape(nc, cs, dk)
        bmid = b3[:, cs // 2 - 1:cs // 2, :]
        blast = b3[:, cs - 1:cs, :]
        q3 = (q_ref[0, r, :].astype(F32) * dk ** -0.5).reshape(nc, cs, dk)
        k3 = k_s[r, :].reshape(nc, cs, dk)
        qin_s[r, :] = (q3 * jnp.exp2(b3 - bmid)).reshape(rb, dk).astype(BF16)
        kin_s[r, :] = (k3 * jnp.exp2(bmid - b3)).reshape(rb, dk).astype(BF16)
        ku_s[r, :] = (k3 * jnp.exp2(blast - b3)).reshape(rb, dk).astype(BF16)
        qb_s[r, :] = (q3 * jnp.exp2(b3)).reshape(rb, dk).astype(BF16)
        dec_s[pl.ds(blk * nc, nc)] = jnp.exp2(blast)

    def stage3(blk):
        r = rows(blk)
        a_s[r, :] = jnp.where(mask, _dot_nt(qin_s[r, :], kin_s[r, :]), 0.0).astype(BF16)
        vt = i_ref[0, r, :].astype(F32).T.astype(BF16)
        ku = ku_s[r, :]
        ut = _dot(vt, jnp.concatenate([ku * sel for sel in chunk_sel], axis=1))
        for c in range(nc):
            u_s[blk * nc + c] = ut[:, c * dk:(c + 1) * dk]

    def stage4(blk):
        r = rows(blk)
        oi_s[r, :] = _dot(a_s[r, :], i_ref[0, r, :])
        st = st_s[...]
        for c in range(nc):
            sp_s[blk * nc + c] = st.astype(BF16)
            st = st * dec_s[blk * nc + c] + u_s[blk * nc + c]
        st_s[...] = st

    def stage5(blk):
        r = rows(blk)
        inter = [_dot_nt(qb_s[pl.ds(pl.multiple_of(blk * rb + c * cs, cs), cs), :], sp_s[blk * nc + c])
                 for c in range(nc)]
        o = oi_s[r, :] + jnp.concatenate(inter, axis=0)
        o = _rms(o, og_ref[...]) * _silu(g_ref[0, r, :].astype(F32))
        o_ref[0, r, :] = o.astype(BF16)

    st_s[...] = jnp.zeros_like(st_s)
    _software_pipeline([stage1, stage2, stage3, stage4, stage5], seq // rb)


def _hgrn(proj3, lb_logits, onorm_g, layer):
    bsz, seq, _ = proj3.shape
    nh = HGRN_HEADS

    def col(off):
        return pl.BlockSpec((1, seq, HGRN_DK), lambda b, h: (b, 0, off * nh + h))

    nl = lb_logits.shape[0]
    return pl.pallas_call(
        functools.partial(_hgrn_kernel, layer=layer),
        grid=(bsz, nh),
        in_specs=[col(0), col(1), col(2), col(3),
                  pl.BlockSpec((nl, HGRN_DK), lambda b, h: (0, h)),
                  pl.BlockSpec((1, HGRN_DK), lambda b, h: (0, 0))],
        out_specs=pl.BlockSpec((1, seq, HGRN_DK), lambda b, h: (b, 0, h)),
        out_shape=jax.ShapeDtypeStruct((bsz, seq, HGRN_WIDTH), BF16),
        scratch_shapes=[pltpu.VMEM((seq, HGRN_DK), BF16),
                        pltpu.VMEM((seq // HGRN_CHUNK, HGRN_DK, HGRN_DK), F32),
                        pltpu.VMEM((seq // HGRN_CHUNK, HGRN_DK, HGRN_DK), BF16),
                        pltpu.VMEM((seq // HGRN_CHUNK, 1, HGRN_DK), F32),
                        pltpu.VMEM((seq, HGRN_DK), F32),
                        pltpu.VMEM((seq, HGRN_DK), F32), pltpu.VMEM((seq, HGRN_DK), F32),
                        pltpu.VMEM((seq, HGRN_DK), BF16), pltpu.VMEM((seq, HGRN_DK), BF16),
                        pltpu.VMEM((seq, HGRN_DK), BF16), pltpu.VMEM((seq, min(HG_ROWS, seq)), BF16),
                        pltpu.VMEM((HGRN_DK, HGRN_DK), F32)],
        compiler_params=_cparams(("arbitrary", "arbitrary")),
        name="hgrn",
    )(proj3, proj3, proj3, proj3, lb_logits, onorm_g)


UP_ROWS = 1024


def _up_kernel(p_ref, pos_ref, wq_ref, wkv_ref, qag_ref, kvag_ref, qg_ref, kg_ref, freq_ref,
               q_ref, k_ref, v_ref):
    half = MLA_ROPE // 2
    p = p_ref[...].astype(F32)
    qa_n = _rms(p[:, 0:Q_LORA], qag_ref[...]).astype(BF16)
    kva_n = _rms(p[:, Q_LORA:Q_LORA + KV_LORA], kvag_ref[...]).astype(BF16)
    kpe_t = p[:, Q_LORA + KV_LORA:Q_LORA + KV_LORA + LANES].T[0:MLA_ROPE]
    ang = freq_ref[...] * pos_ref[...].astype(F32)
    cos = jnp.cos(ang)
    sin = jnp.sin(ang)

    def rope(x):
        x1, x2 = x[0:half], x[half:]
        return jnp.concatenate([x1 * cos - x2 * sin, x2 * cos + x1 * sin], axis=0)

    kg = kg_ref[...]
    kpe_ss = jnp.sum(kpe_t * kpe_t, axis=0, keepdims=True)
    kpe_rot = rope(kpe_t * kg[MLA_NOPE:])
    qg = qg_ref[...] * (MLA_QK ** -0.5 * LOG2E)
    zpad = jnp.zeros((MLA_QPAD - MLA_QK, p.shape[0]), F32)
    for h in range(MLA_HEADS):
        rows = slice(h * MLA_QPAD, (h + 1) * MLA_QPAD)
        qt = _dot_nt(wq_ref[rows, :], qa_n)
        qn = qt[0:MLA_QK] * lax.rsqrt(jnp.sum(qt * qt, axis=0, keepdims=True) / MLA_QK + EPS) * qg
        q_ref[0, h] = jnp.concatenate([qn[0:MLA_NOPE], rope(qn[MLA_NOPE:]), zpad], axis=0).astype(BF16)
        kvt = _dot_nt(wkv_ref[rows, :], kva_n)
        kn = kvt[0:MLA_NOPE]
        rk = lax.rsqrt((jnp.sum(kn * kn, axis=0, keepdims=True) + kpe_ss) / MLA_QK + EPS)
        kt = jnp.concatenate([kn * rk * kg[0:MLA_NOPE], kpe_rot * rk, zpad], axis=0)
        k_ref[0, h] = kt.T.astype(BF16)
        v_ref[0, h] = kvt[MLA_NOPE:].astype(BF16)


def _mla_up(proj3, pos3, wq_t, wkv_t, qag, kvag, qg, kg, freq):
    bsz, seq, _ = proj3.shape
    tm = min(UP_ROWS, seq)
    nh = MLA_HEADS
    mla_block = 4 * HGRN_WIDTH // 1024

    def const(shape):
        return pl.BlockSpec(shape, lambda b, i: (0,) * len(shape))

    return pl.pallas_call(
        _up_kernel,
        grid=(bsz, seq // tm),
        in_specs=[pl.BlockSpec((None, tm, 1024), lambda b, i: (b, i, mla_block)),
                  pl.BlockSpec((None, 1, tm), lambda b, i: (b, 0, i)),
                  const(wq_t.shape), const(wkv_t.shape), const(qag.shape), const(kvag.shape),
                  const(qg.shape), const(kg.shape), const(freq.shape)],
        out_specs=[pl.BlockSpec((1, nh, MLA_QPAD, tm), lambda b, i: (b, 0, 0, i)),
                   pl.BlockSpec((1, nh, tm, MLA_QPAD), lambda b, i: (b, 0, i, 0)),
                   pl.BlockSpec((1, nh, MLA_V, tm), lambda b, i: (b, 0, 0, i))],
        out_shape=[jax.ShapeDtypeStruct((bsz, nh, MLA_QPAD, seq), BF16),
                   jax.ShapeDtypeStruct((bsz, nh, seq, MLA_QPAD), BF16),
                   jax.ShapeDtypeStruct((bsz, nh, MLA_V, seq), BF16)],
        compiler_params=_cparams(("arbitrary", "arbitrary")),
        name="mla_up",
    )(proj3, pos3, wq_t, wkv_t, qag, kvag, qg, kg, freq)


ATT_T = 256


def _attn_kernel(qt_ref, k_ref, vt_ref, g_ref, o_ref):
    seq = k_ref.shape[2]
    t = min(ATT_T, seq)
    key = lax.broadcasted_iota(jnp.int32, (t, t), 0)
    qry = lax.broadcasted_iota(jnp.int32, (t, t), 1)
    causal = key <= qry
    neg = jnp.finfo(F32).min

    def widen(x, off, fill):
        return x if off == 0 else jnp.concatenate([jnp.full((x.shape[0], off), fill, x.dtype), x], axis=1)

    scores = []
    m = None
    for off in range(0, seq, t):
        s = _dot(k_ref[0, 0, off:off + t, :], qt_ref[0, 0, :, off:])
        diag = jnp.where(causal, s[:, 0:t], neg)
        s = diag if off + t == seq else jnp.concatenate([diag, s[:, t:]], axis=1)
        scores.append(s)
        blk_max = widen(jnp.max(s, axis=0, keepdims=True), off, neg)
        m = blk_max if m is None else jnp.maximum(m, blk_max)
    l = None
    ot = None
    for off, s in zip(range(0, seq, t), scores):
        p = jnp.exp2(s - m[:, off:])
        p_sum = widen(jnp.sum(p, axis=0, keepdims=True), off, 0.0)
        pv = widen(_dot(vt_ref[0, 0, :, off:off + t], p.astype(BF16)), off, 0.0)
        l = p_sum if l is None else l + p_sum
        ot = pv if ot is None else ot + pv
    ot = ot * (1.0 / l)
    ot = ot * lax.rsqrt(jnp.mean(ot * ot, axis=0, keepdims=True) + EPS) * g_ref[...]
    o_ref[0, :, :] = ot.T.astype(BF16)


def _attention(q, k, v, g):
    bsz, nh, seq, _ = k.shape
    return pl.pallas_call(
        _attn_kernel,
        grid=(bsz, nh),
        in_specs=[pl.BlockSpec((1, 1, MLA_QPAD, seq), lambda b, h: (b, h, 0, 0)),
                  pl.BlockSpec((1, 1, seq, MLA_QPAD), lambda b, h: (b, h, 0, 0)),
                  pl.BlockSpec((1, 1, MLA_V, seq), lambda b, h: (b, h, 0, 0)),
                  pl.BlockSpec((MLA_V, 1), lambda b, h: (0, 0))],
        out_specs=pl.BlockSpec((1, seq, MLA_V), lambda b, h: (b, 0, h)),
        out_shape=jax.ShapeDtypeStruct((bsz, seq, nh * MLA_V), BF16),
        compiler_params=_cparams(("arbitrary", "arbitrary")),
        name="attn",
    )(q, k, v, g)


def _out_kernel(oa_ref, ob_ref, x_ref, wa_ref, wb_ref, g1_ref, n2g_ref, sh2_ref, sc2_ref,
                wr_ref, br_ref, x1_ref, h2_ref, lg_ref, mix_s):
    @pl.when(pl.program_id(0) == 0)
    def _():
        mix_s[...] = jnp.zeros_like(mix_s)

    mix_next = _dot(oa_ref[...], wa_ref[...]) + _dot(ob_ref[...], wb_ref[...])
    x1 = x_ref[...] + g1_ref[0] * mix_s[...]
    x1_ref[...] = x1
    h2 = _rms(x1, n2g_ref[...]) * (1.0 + sc2_ref[0]) + sh2_ref[0]
    tm = h2.shape[0]
    for s in range(TOK_SUBLANES):
        h2_ref[pl.ds(s, tm, stride=TOK_SUBLANES), :] = _pack_pair(
            h2[:, s * LANES:(s + 1) * LANES], h2[:, (s + TOK_SUBLANES) * LANES:(s + TOK_SUBLANES + 1) * LANES])
    lg = _dot(h2.astype(BF16), wr_ref[...]) + br_ref[...]
    lg_ref[...] = lg.T[0:ROUTE_ROWS, :]
    mix_s[...] = mix_next


def _out_proj(oa, ob, x2, wa, wb, g1, n2g, sh2, sc2, wr, br, seq):
    t, d = x2.shape
    tm = min(512, seq)
    n_tiles = t // tm

    def const(shape):
        return pl.BlockSpec(shape, lambda i: (0,) * len(shape))

    def prev(i):
        return jnp.maximum(i - 1, 0)

    def per_batch():
        return pl.BlockSpec((1, 1, d), lambda i: (prev(i) * tm // seq, 0, 0))

    return pl.pallas_call(
        _out_kernel,
        grid=(n_tiles + 1,),
        in_specs=[pl.BlockSpec((tm, HGRN_WIDTH), lambda i: (jnp.minimum(i, n_tiles - 1), 0)),
                  pl.BlockSpec((tm, HGRN_WIDTH), lambda i: (jnp.minimum(i, n_tiles - 1), 0)),
                  pl.BlockSpec((tm, d), lambda i: (prev(i), 0)),
                  const(wa.shape), const(wb.shape), per_batch(), const(n2g.shape),
                  per_batch(), per_batch(), const(wr.shape), const(br.shape)],
        out_specs=[pl.BlockSpec((tm, d), lambda i: (prev(i), 0)),
                   pl.BlockSpec((tm * TOK_SUBLANES, LANES), lambda i: (prev(i), 0)),
                   pl.BlockSpec((ROUTE_ROWS, tm), lambda i: (0, prev(i)))],
        out_shape=[jax.ShapeDtypeStruct((t, d), F32),
                   jax.ShapeDtypeStruct((t * TOK_SUBLANES, LANES), jnp.uint32),
                   jax.ShapeDtypeStruct((ROUTE_ROWS, t), F32)],
        scratch_shapes=[pltpu.VMEM((tm, d), F32)],
        compiler_params=_cparams(("arbitrary",)),
        name="out_proj",
    )(oa, ob, x2, wa, wb, g1, n2g, sh2, sc2, wr, br)


def _route_kernel(lg_ref, tri_ref, ri_ref, rw_ref, cnt_ref, carry_s):
    step = pl.program_id(0)

    @pl.when(step == 0)
    def _():
        carry_s[...] = jnp.zeros_like(carry_s)

    lg = lg_ref[...]
    tr = lg.shape[1]
    epg = EXPERTS_PER_GROUP
    gl = lg[N_EXPERTS:N_EXPERTS + N_GROUPS, :]
    row_g = lax.broadcasted_iota(jnp.int32, (N_GROUPS, tr), 0)
    gmax = jnp.max(gl, axis=0, keepdims=True)
    g_sel = jnp.min(jnp.where(gl == gmax, row_g, N_GROUPS), axis=0, keepdims=True)
    p_group = 1.0 / jnp.sum(jnp.exp(gl - gmax), axis=0, keepdims=True)

    e_in = lg[0:epg, :]
    for g in range(1, N_GROUPS):
        e_in = jnp.where(g_sel == g, lg[g * epg:(g + 1) * epg, :], e_in)
    row_e = lax.broadcasted_iota(jnp.int32, (epg, tr), 0)
    top1 = jnp.max(e_in, axis=0, keepdims=True)
    i1 = jnp.min(jnp.where(e_in == top1, row_e, epg), axis=0, keepdims=True)
    rest = jnp.where(row_e == i1, -jnp.inf, e_in)
    top2 = jnp.max(rest, axis=0, keepdims=True)
    i2 = jnp.min(jnp.where(rest == top2, row_e, epg), axis=0, keepdims=True)
    e2w = jnp.exp(top2 - top1)
    w1 = p_group / (1.0 + e2w)
    w2 = p_group * e2w / (1.0 + e2w)
    ex1 = g_sel * epg + i1
    ex2 = g_sel * epg + i2

    row_x = lax.broadcasted_iota(jnp.int32, (N_EXPERTS, tr), 0)
    oh1 = row_x == ex1
    oh2 = row_x == ex2
    oh = jnp.logical_or(oh1, oh2)
    before = _dot(oh.astype(BF16), tri_ref[...]) + carry_s[:, 0:1]
    rank1 = jnp.sum(jnp.where(oh1, before, 0.0), axis=0, keepdims=True)
    rank2 = jnp.sum(jnp.where(oh2, before, 0.0), axis=0, keepdims=True)
    carry_s[...] = carry_s[...] + jnp.sum(oh.astype(F32), axis=1, keepdims=True)

    zi = jnp.zeros((4, tr), jnp.int32)
    ri_ref[...] = jnp.concatenate([ex1, ex2, rank1.astype(jnp.int32), rank2.astype(jnp.int32), zi], axis=0)
    rw_ref[...] = jnp.concatenate([w1, w2, jnp.zeros((6, tr), F32)], axis=0)
    cnt_ref[...] = carry_s[...].astype(jnp.int32)


def _route(lg_t, tri):
    t = lg_t.shape[1]
    tr = tri.shape[0]
    return pl.pallas_call(
        _route_kernel,
        grid=(t // tr,),
        in_specs=[pl.BlockSpec((ROUTE_ROWS, tr), lambda i: (0, i)),
                  pl.BlockSpec((tr, tr), lambda i: (0, 0))],
        out_specs=[pl.BlockSpec((8, tr), lambda i: (0, i)),
                   pl.BlockSpec((8, tr), lambda i: (0, i)),
                   pl.BlockSpec((N_EXPERTS, LANES), lambda i: (0, 0))],
        out_shape=[jax.ShapeDtypeStruct((8, t), jnp.int32),
                   jax.ShapeDtypeStruct((8, t), F32),
                   jax.ShapeDtypeStruct((N_EXPERTS, LANES), jnp.int32)],
        scratch_shapes=[pltpu.VMEM((N_EXPERTS, LANES), F32)],
        compiler_params=_cparams(("arbitrary",)),
        name="route",
    )(lg_t, tri)


def _moe_kernel(rd_ref, be_ref, pe_ref, nb_ref, x_ref, wg_ref, wu_ref, wd_ref, y_ref,
                ys, wg_f, wu_f, wd_f, wg_s, wu_s, wd_s, ssem, wsem):
    b = pl.program_id(0)
    last = pl.num_programs(0) - 1
    nb = nb_ref[0]
    tm = MOE_BLOCK
    ts = TOK_SUBLANES
    weights = ((wg_ref, wg_f, wg_s), (wu_ref, wu_f, wu_s), (wd_ref, wd_f, wd_s))

    def fetch(e):
        return [pltpu.make_async_copy(src.at[e], stage, wsem.at[k]) for k, (src, stage, _) in enumerate(weights)]

    def scatter(blk):
        base = blk * tm
        for j in range(tm):
            dst = pl.multiple_of(rd_ref[base + j] * ts, ts)
            pltpu.make_async_copy(ys.at[pl.ds(j * ts, ts), :], y_ref.at[pl.ds(dst, ts), :], ssem).start()

    def wait_scatter():
        pltpu.make_async_copy(ys, y_ref.at[pl.ds(0, tm * ts), :], ssem).wait()

    @pl.when(b == 0)
    def _():
        for copy in fetch(be_ref[0]):
            copy.start()
        ys[...] = jnp.zeros_like(ys)
        spare = pltpu.make_async_copy(ys, y_ref.at[pl.ds(y_ref.shape[0] - tm * ts, tm * ts), :], ssem)
        spare.start()
        spare.wait()

    expert = be_ref[b]
    first_of_expert = jnp.logical_or(b == 0, expert != be_ref[jnp.maximum(b - 1, 0)])

    @pl.when(jnp.logical_and(b < nb, first_of_expert))
    def _():
        for copy in fetch(0):
            copy.wait()
        for _, stage, dst in weights:
            dst[...] = stage[...].astype(BF16)
        next_blk = pe_ref[expert] // tm

        @pl.when(next_blk < nb)
        def _():
            for copy in fetch(be_ref[next_blk]):
                copy.start()

    @pl.when(b < nb)
    def _():
        scatter(jnp.maximum(b - 1, 0))
        parts = [_unpack_pair(x_ref[pl.ds(s, tm, stride=ts), :]) for s in range(ts)]
        x = jnp.concatenate([p[0] for p in parts] + [p[1] for p in parts], axis=1).astype(BF16)
        hid = _silu(_dot(x, wg_s[...])) * _dot(x, wu_s[...])
        y = _dot(hid.astype(BF16), wd_s[...])
        wait_scatter()
        for s in range(ts):
            ys[pl.ds(s, tm, stride=ts), :] = _pack_pair(y[:, s * LANES:(s + 1) * LANES],
                                                        y[:, (s + ts) * LANES:(s + ts + 1) * LANES])

        @pl.when(b == last)
        def _():
            scatter(b)
            wait_scatter()

    @pl.when(b == nb)
    def _():
        scatter(b - 1)
        wait_scatter()


def _moe(row_dst, blk_e, pad_end, nb_real, x_sorted, wg, wu, wd, n_out_rows):
    n_blocks = blk_e.shape[0]
    d, de = wg.shape[1], wg.shape[2]
    tm = MOE_BLOCK
    hbm = pl.BlockSpec(memory_space=pl.ANY)
    return pl.pallas_call(
        _moe_kernel,
        grid_spec=pltpu.PrefetchScalarGridSpec(
            num_scalar_prefetch=4,
            grid=(n_blocks,),
            in_specs=[pl.BlockSpec((tm * TOK_SUBLANES, LANES),
                                   lambda b, rd, be, pe, nb: (jnp.minimum(b, nb[0] - 1), 0)),
                      hbm, hbm, hbm],
            out_specs=hbm,
            scratch_shapes=[pltpu.VMEM((tm * TOK_SUBLANES, LANES), jnp.uint32),
                            pltpu.VMEM((d, de), F32), pltpu.VMEM((d, de), F32), pltpu.VMEM((de, d), F32),
                            pltpu.VMEM((d, de), BF16), pltpu.VMEM((d, de), BF16), pltpu.VMEM((de, d), BF16),
                            pltpu.SemaphoreType.DMA(()), pltpu.SemaphoreType.DMA((3,))]),
        out_shape=jax.ShapeDtypeStruct((n_out_rows * TOK_SUBLANES, LANES), jnp.uint32),
        compiler_params=_cparams(("arbitrary",)),
        name="moe",
    )(row_dst, blk_e, pad_end, nb_real, x_sorted, wg, wu, wd)


def _dispatch_kernel(e1_ref, e2_ref, r1_ref, r2_ref, ps_ref, pe_ref, cnt_ref, be_ref, nb_ref,
                     h_ref, x_ref, rd_ref, zbuf, sem, zsem):
    i = pl.program_id(0)
    ts = TOK_SUBLANES
    td = h_ref.shape[0] // ts
    n_tok = e1_ref.shape[0]
    base = i * td
    blk_rows = MOE_BLOCK * ts

    def zero_fill(blk):
        return pltpu.make_async_copy(
            zbuf, x_ref.at[pl.ds(pl.multiple_of(blk * blk_rows, blk_rows), blk_rows), :], zsem)

    @pl.when(i == 0)
    def _():
        zbuf[...] = jnp.zeros_like(zbuf)

        def fill(blk, n_fills):
            e = be_ref[blk]
            partial_last = jnp.logical_and(blk == pe_ref[e] // MOE_BLOCK - 1, cnt_ref[e] % MOE_BLOCK != 0)
            needs_fill = jnp.logical_or(blk >= nb_ref[0], partial_last)

            @pl.when(needs_fill)
            def _():
                zero_fill(blk).start()
            return n_fills + needs_fill.astype(jnp.int32)

        n_fills = lax.fori_loop(0, be_ref.shape[0], fill, 0)

        def init(blk, carry):
            for j in range(MOE_BLOCK):
                rd_ref[blk * MOE_BLOCK + j] = 2 * n_tok + j
            return carry

        lax.fori_loop(0, rd_ref.shape[0] // MOE_BLOCK, init, 0)

        def drain(k, carry):
            zero_fill(0).wait()
            return carry

        lax.fori_loop(0, n_fills, drain, 0)

    def start(t, carry):
        src = h_ref.at[pl.ds(pl.multiple_of(t * ts, ts), ts), :]
        tok = base + t
        for slot, (e_ref, r_ref) in enumerate(((e1_ref, r1_ref), (e2_ref, r2_ref))):
            row = ps_ref[e_ref[tok]] + r_ref[tok]
            pltpu.make_async_copy(src, x_ref.at[pl.ds(pl.multiple_of(row * ts, ts), ts), :], sem).start()
            rd_ref[row] = 2 * tok + slot
        return carry

    lax.fori_loop(0, td, start, 0, unroll=8)
    for _ in range(2):
        pltpu.make_async_copy(h_ref, x_ref.at[pl.ds(0, td * ts), :], sem).wait()


def _dispatch(ri, pad_start, pad_end, counts, blk_e, nb_real, h2p, n_rows):
    t = ri.shape[1]
    td = min(512, t)
    ts = TOK_SUBLANES
    return pl.pallas_call(
        _dispatch_kernel,
        grid_spec=pltpu.PrefetchScalarGridSpec(
            num_scalar_prefetch=9,
            grid=(t // td,),
            in_specs=[pl.BlockSpec((td * ts, LANES), lambda i, *_: (i, 0))],
            out_specs=[pl.BlockSpec(memory_space=pl.ANY), pl.BlockSpec(memory_space=pltpu.SMEM)],
            scratch_shapes=[pltpu.VMEM((MOE_BLOCK * ts, LANES), jnp.uint32),
                            pltpu.SemaphoreType.DMA(()), pltpu.SemaphoreType.DMA(())]),
        out_shape=[jax.ShapeDtypeStruct((n_rows * ts, LANES), jnp.uint32),
                   jax.ShapeDtypeStruct((n_rows,), jnp.int32)],
        compiler_params=_cparams(("arbitrary",)),
        name="dispatch",
    )(ri[0], ri[1], ri[2], ri[3], pad_start, pad_end, counts, blk_e, nb_real, h2p)


def _moe_plan(counts, t):
    tm = MOE_BLOCK
    n_blocks = 2 * t // tm + N_EXPERTS
    padded = ((counts + tm - 1) // tm) * tm
    pad_end = jnp.cumsum(padded)
    pad_start = pad_end - padded
    blk = jnp.arange(n_blocks, dtype=jnp.int32)
    blk_e = jnp.minimum(jnp.sum(pad_end[None, :] <= (blk * tm)[:, None], axis=1), N_EXPERTS - 1).astype(jnp.int32)
    nb_real = (pad_end[-1:] // tm).astype(jnp.int32)
    return pad_start, pad_end, blk_e, nb_real, n_blocks * tm


COMBINE_ROWS = 64


def _combine_kernel(x1_ref, g2_ref, rw_ref, y_ref, o_ref, w_s):
    ts = TOK_SUBLANES
    rows = min(COMBINE_ROWS, x1_ref.shape[0])
    w_s[...] = rw_ref[...].T

    def chunk(c, carry):
        r = pl.ds(pl.multiple_of(c * rows, rows), rows)
        w1 = w_s[r, 0:1]
        w2 = w_s[r, 1:2]
        tile0 = pl.multiple_of(c * rows * 2 * ts, rows * 2 * ts)
        for s in range(ts):
            a_lo, a_hi = _unpack_pair(y_ref[pl.ds(tile0 + s, rows, stride=2 * ts), :])
            b_lo, b_hi = _unpack_pair(y_ref[pl.ds(tile0 + ts + s, rows, stride=2 * ts), :])
            for col, ya, yb in ((s, a_lo, b_lo), (s + ts, a_hi, b_hi)):
                cols = slice(col * LANES, (col + 1) * LANES)
                o_ref[r, cols] = x1_ref[r, cols] + g2_ref[0][:, cols] * (w1 * ya + w2 * yb)
        return carry

    lax.fori_loop(0, x1_ref.shape[0] // rows, chunk, 0)


def _combine(x1, g2, rw, y2, seq):
    t, d = x1.shape
    tc = min(512, seq)
    return pl.pallas_call(
        _combine_kernel,
        grid=(t // tc,),
        in_specs=[pl.BlockSpec((tc, d), lambda i: (i, 0)),
                  pl.BlockSpec((1, 1, d), lambda i: (i * tc // seq, 0, 0)),
                  pl.BlockSpec((rw.shape[0], tc), lambda i: (0, i)),
                  pl.BlockSpec((tc * 2 * TOK_SUBLANES, LANES), lambda i: (i, 0))],
        out_specs=pl.BlockSpec((tc, d), lambda i: (i, 0)),
        out_shape=jax.ShapeDtypeStruct((t, d), F32),
        scratch_shapes=[pltpu.VMEM((tc, rw.shape[0]), F32)],
        compiler_params=_cparams(("arbitrary",)),
        name="combine",
    )(x1, g2, rw, y2)


def _q_up_layout(w_q_up):
    w = w_q_up.reshape(Q_LORA, MLA_HEADS, MLA_QK)
    w = jnp.pad(w, ((0, 0), (0, 0), (0, MLA_QPAD - MLA_QK)))
    return w.reshape(Q_LORA, MLA_HEADS * MLA_QPAD).T.astype(BF16)


def _pad_lanes(g, width):
    return jnp.pad(g, (0, width - g.shape[0])).reshape(1, width)


def kernel(x, c, positions, w_ada, b_ada, norm1_g, w_in, hgrn_lb_logits, hgrn_onorm_g, q_a_norm_g, w_q_up,
           kv_a_norm_g, w_kv_up, q_norm_g, k_norm_g, attn_onorm_g, w_out, norm2_g, w_group, b_group,
           w_router, b_router, w_gate, w_up, w_down):
    bsz, seq, d = x.shape
    t = bsz * seq
    depth = w_ada.shape[0]
    half = MLA_ROPE // 2
    inv_freq = ROPE_BASE ** (-jnp.arange(0, MLA_ROPE, 2, dtype=F32) / MLA_ROPE)
    freq = inv_freq.reshape(half, 1)
    pos3 = positions.reshape(bsz, 1, seq)
    tr = min(1024, t)
    tri = jnp.triu(jnp.ones((tr, tr), BF16), 1)

    x2 = x.reshape(t, d)
    for l in range(depth):
        mod = _ada(c, w_ada[l], b_ada[l]).reshape(bsz, 6, 1, d)
        sh1, sc1, g1, sh2, sc2, g2 = (mod[:, i] for i in range(6))

        proj = _in_proj(x2, norm1_g[l].reshape(1, d), sh1, sc1, w_in[l].T, seq)
        proj3 = proj.reshape(bsz, seq, IN_COLS_PAD)

        o_a = _hgrn(proj3, hgrn_lb_logits, hgrn_onorm_g[l].reshape(1, HGRN_DK), l)

        q, k, v = _mla_up(proj3, pos3, _q_up_layout(w_q_up[l]), w_kv_up[l].T.astype(BF16),
                          q_a_norm_g[l].reshape(1, Q_LORA), kv_a_norm_g[l].reshape(1, KV_LORA),
                          q_norm_g[l].reshape(MLA_QK, 1), k_norm_g[l].reshape(MLA_QK, 1), freq)
        o_b = _attention(q, k, v, attn_onorm_g[l].reshape(MLA_V, 1))

        w_o = w_out[l].astype(BF16)
        wr = jnp.pad(jnp.concatenate([w_router[l], w_group[l]], axis=1),
                     ((0, 0), (0, LANES - N_EXPERTS - N_GROUPS))).astype(BF16)
        br = _pad_lanes(jnp.concatenate([b_router[l], b_group[l]]), LANES)
        x1, h2, lg_t = _out_proj(o_a.reshape(t, HGRN_WIDTH), o_b.reshape(t, HGRN_WIDTH), x2,
                                 w_o[:HGRN_WIDTH], w_o[HGRN_WIDTH:], g1, norm2_g[l].reshape(1, d),
                                 sh2, sc2, wr, br, seq)

        ri, rw, cnt = _route(lg_t, tri)
        counts = cnt[:, 0]
        pad_start, pad_end, blk_e, nb_real, n_rows = _moe_plan(counts, t)
        x_sorted, row_dst = _dispatch(ri, pad_start, pad_end, counts, blk_e, nb_real, h2, n_rows)
        y2 = _moe(row_dst, blk_e, pad_end, nb_real, x_sorted, w_gate[l], w_up[l], w_down[l], 2 * t + MOE_BLOCK)
        x2 = _combine(x1, g2, rw, y2, seq)
    return x2.reshape(bsz, seq, d)
```

```python
import functools

import jax
import jax.numpy as jnp
from jax import lax
from jax.experimental import pallas as pl
from jax.experimental.pallas import tpu as pltpu

F32 = jnp.float32
BF16 = jnp.bfloat16
EPS = 1e-6
LOG2E = 1.4426950408889634

D_MODEL = 2048
HGRN_WIDTH = 1024
HGRN_DK = 128
HGRN_HEADS = 8
HGRN_CHUNK = 64
MLA_HEADS = 8
MLA_NOPE = 128
MLA_ROPE = 64
MLA_QK = MLA_NOPE + MLA_ROPE
MLA_V = 128
MLA_QPAD = 256
Q_LORA = 512
KV_LORA = 256
ROPE_BASE = 10000.0
IN_COLS = 4 * HGRN_WIDTH + Q_LORA + KV_LORA + MLA_ROPE
IN_COLS_PAD = 5120
N_GROUPS = 4
EXPERTS_PER_GROUP = 8
N_EXPERTS = 32
D_EXPERT = 512
ROUTE_ROWS = 40
MOE_BLOCK = 256
LANES = 128
TOK_SUBLANES = 8
VMEM_LIMIT = 56 * 1024 * 1024


def _cparams(sem):
    return pltpu.CompilerParams(dimension_semantics=sem, vmem_limit_bytes=VMEM_LIMIT)


def _dot(a, b):
    return jnp.dot(a, b, preferred_element_type=F32)


def _dot_nt(a, b):
    return lax.dot_general(a, b, (((1,), (1,)), ((), ())), preferred_element_type=F32)


def _rms(x, g):
    return x * lax.rsqrt(jnp.mean(x * x, axis=-1, keepdims=True) + EPS) * g


def _silu(x):
    return x * jax.nn.sigmoid(x)


def _pack_pair(lo, hi):
    lo_b = lax.bitcast_convert_type(lo.astype(BF16).astype(F32), jnp.uint32)
    hi_b = lax.bitcast_convert_type(hi.astype(BF16).astype(F32), jnp.uint32)
    return hi_b | (lo_b >> 16)


def _unpack_pair(w):
    lo = lax.bitcast_convert_type(w << 16, F32)
    hi = lax.bitcast_convert_type(w & jnp.uint32(0xFFFF0000), F32)
    return lo, hi


def _ada_kernel(c_ref, w_ref, b_ref, o_ref):
    ca = _silu(c_ref[...]).astype(BF16)
    o_ref[...] = _dot(ca, w_ref[...].astype(BF16)) + b_ref[...]


def _ada(c, w, b):
    bsz, d = c.shape
    n = w.shape[1]
    tn = 1024
    return pl.pallas_call(
        _ada_kernel,
        grid=(n // tn,),
        in_specs=[pl.BlockSpec((bsz, d), lambda j: (0, 0)),
                  pl.BlockSpec((d, tn), lambda j: (0, j)),
                  pl.BlockSpec((1, tn), lambda j: (0, j))],
        out_specs=pl.BlockSpec((bsz, tn), lambda j: (0, j)),
        out_shape=jax.ShapeDtypeStruct((bsz, n), F32),
        compiler_params=_cparams(("arbitrary",)),
        name="ada",
    )(c, w, b.reshape(1, n))


NORM_ROWS = 32


def _norm_kernel(x_ref, g_ref, sh_ref, sc_ref, h_ref):
    def body(c, carry):
        r = pl.ds(pl.multiple_of(c * NORM_ROWS, NORM_ROWS), NORM_ROWS)
        h = _rms(x_ref[r, :], g_ref[...]) * (1.0 + sc_ref[0]) + sh_ref[0]
        h_ref[r, :] = h.astype(BF16)
        return carry
    lax.fori_loop(0, x_ref.shape[0] // NORM_ROWS, body, 0, unroll=2)


def _norm_mod(x2, g, sh, sc, seq, n_rows):
    t, d = x2.shape
    tm = min(1024, seq, n_rows)
    return pl.pallas_call(
        _norm_kernel,
        grid=(n_rows // tm,),
        in_specs=[pl.BlockSpec((tm, d), lambda i: (i, 0)),
                  pl.BlockSpec((1, d), lambda i: (0, 0)),
                  pl.BlockSpec((1, 1, d), lambda i: (i * tm // seq, 0, 0)),
                  pl.BlockSpec((1, 1, d), lambda i: (i * tm // seq, 0, 0))],
        out_specs=pl.BlockSpec((tm, d), lambda i: (i, 0)),
        out_shape=jax.ShapeDtypeStruct((n_rows, d), BF16),
        compiler_params=_cparams(("arbitrary",)),
        name="norm_mod",
    )(x2, g, sh, sc)


IN_NORM_CHUNKS = 4


def _in_kernel(h0_ref, x_ref, g_ref, sh_ref, sc_ref, w_ref, o_ref, h_s):
    i = pl.program_id(0)
    j = pl.program_id(1)
    slot = i % 2

    @pl.when(jnp.logical_and(i == 0, j == 0))
    def _():
        h_s[0] = h0_ref[...]

    o_ref[...] = _dot_nt(h_s[slot], w_ref[...]).astype(BF16)
    rows = x_ref.shape[0] // IN_NORM_CHUNKS
    first = jnp.minimum(j, IN_NORM_CHUNKS - 1) * rows
    gain = g_ref[...]
    scale = 1.0 + sc_ref[0]
    shift = sh_ref[0]
    for c in range(rows // NORM_ROWS):
        r = pl.ds(pl.multiple_of(first + c * NORM_ROWS, NORM_ROWS), NORM_ROWS)
        h_s[1 - slot, r, :] = (_rms(x_ref[r, :], gain) * scale + shift).astype(BF16)


def _in_proj(x2, g, sh, sc, w_t, seq):
    t, d = x2.shape
    tn = 1024
    tm = min(1024, seq)
    n_tiles = t // tm
    w_b = jnp.concatenate([w_t.astype(BF16), jnp.zeros((IN_COLS_PAD - IN_COLS, d), BF16)], axis=0)
    h0 = _norm_mod(x2, g, sh, sc, seq, tm)

    def nxt(i):
        return jnp.minimum(i + 1, n_tiles - 1)

    return pl.pallas_call(
        _in_kernel,
        grid=(n_tiles, IN_COLS_PAD // tn),
        in_specs=[pl.BlockSpec((tm, d), lambda i, j: (0, 0)),
                  pl.BlockSpec((tm, d), lambda i, j: (nxt(i), 0)),
                  pl.BlockSpec((1, d), lambda i, j: (0, 0)),
                  pl.BlockSpec((1, 1, d), lambda i, j: (nxt(i) * tm // seq, 0, 0)),
                  pl.BlockSpec((1, 1, d), lambda i, j: (nxt(i) * tm // seq, 0, 0)),
                  pl.BlockSpec((tn, d), lambda i, j: (j, 0))],
        out_specs=pl.BlockSpec((tm, tn), lambda i, j: (i, j)),
        out_shape=jax.ShapeDtypeStruct((t, IN_COLS_PAD), BF16),
        scratch_shapes=[pltpu.VMEM((2, tm, d), BF16)],
        compiler_params=_cparams(("arbitrary", "arbitrary")),
        name="in_proj",
    )(h0, x2, g, sh, sc, w_b)


HG_ROWS = 256


PIPELINE_STATIC_BLOCKS = 8


def _software_pipeline(stages, n_blocks):
    depth = len(stages)

    def step(i, static):
        for k in reversed(range(depth)):
            if static and not 0 <= i - k < n_blocks:
                continue
            stages[k](i - k)

    if n_blocks <= PIPELINE_STATIC_BLOCKS:
        for i in range(n_blocks + depth - 1):
            step(i, True)
        return
    for i in range(depth - 1):
        step(i, True)

    def steady(i, carry):
        step(i, False)
        return carry

    lax.fori_loop(depth - 1, n_blocks, steady, 0)
    for i in range(n_blocks, n_blocks + depth - 1):
        step(i, True)


def _chunk_mask(n):
    row = lax.broadcasted_iota(jnp.int32, (n, n), 0)
    col = lax.broadcasted_iota(jnp.int32, (n, n), 1)
    return jnp.logical_and(row // HGRN_CHUNK == col // HGRN_CHUNK, col <= row)


def _hgrn_kernel(q_ref, f_ref, i_ref, g_ref, lbl_ref, og_ref, o_ref,
                 qb_s, u_s, sp_s, dec_s, oi_s, b_s, k_s, qin_s, kin_s, ku_s, a_s, st_s, *, layer):
    seq = q_ref.shape[1]
    cs = HGRN_CHUNK
    rb = min(HG_ROWS, seq)
    nc = rb // cs
    dk = HGRN_DK
    lg = lbl_ref[...]
    ex = jnp.exp(lg - jnp.max(lg, axis=0, keepdims=True))
    sm = ex / jnp.sum(ex, axis=0, keepdims=True)
    lb = jnp.sum(sm[0:layer + 1], axis=0, keepdims=True)
    mask = _chunk_mask(rb)
    tri = mask.astype(BF16)
    row_chunk = lax.broadcasted_iota(jnp.int32, (rb, dk), 0) // cs

    chunk_sel = [(row_chunk == c).astype(BF16) for c in range(nc)]

    def rows(blk):
        return pl.ds(pl.multiple_of(blk * rb, rb), rb)

    def stage1(blk):
        r = rows(blk)
        f = lb + (1.0 - lb) * jax.nn.sigmoid(f_ref[0, r, :].astype(F32))
        lf = jnp.log2(f)
        k_s[r, :] = 1.0 - f
        hi = lf.astype(BF16)
        r1 = lf - hi.astype(F32)
        mid = r1.astype(BF16)
        lo = (r1 - mid.astype(F32)).astype(BF16)
        bhm = _dot(tri, jnp.concatenate([hi, mid], axis=1))
        b_s[r, :] = bhm[:, 0:dk] + bhm[:, dk:] + _dot(tri, lo)

    def stage2(blk):
        r = rows(blk)
        b3 = b_s[r, :].reshape(nc, cs, dk)
        bmid = b3[:, cs // 2 - 1:cs // 2, :]
        blast = b3[:, cs - 1:cs, :]
        q3 = (q_ref[0, r, :].astype(F32) * dk ** -0.5).reshape(nc, cs, dk)
        k3 = k_s[r, :].reshape(nc, cs, dk)
        qin_s[r, :] = (q3 * jnp.exp2(b3 - bmid)).reshape(rb, dk).astype(BF16)
        kin_s[r, :] = (k3 * jnp.exp2(bmid - b3)).reshape(rb, dk).astype(BF16)
        ku_s[r, :] = (k3 * jnp.exp2(blast - b3)).reshape(rb, dk).astype(BF16)
        qb_s[r, :] = (q3 * jnp.exp2(b3)).reshape(rb, dk).astype(BF16)
        dec_s[pl.ds(blk * nc, nc)] = jnp.exp2(blast)

    def stage3(blk):
        r = rows(blk)
        a_s[r, :] = jnp.where(mask, _dot_nt(qin_s[r, :], kin_s[r, :]), 0.0).astype(BF16)
        vt = i_ref[0, r, :].astype(F32).T.astype(BF16)
        ku = ku_s[r, :]
        ut = _dot(vt, jnp.concatenate([ku * sel for sel in chunk_sel], axis=1))
        for c in range(nc):
            u_s[blk * nc + c] = ut[:, c * dk:(c + 1) * dk]

    def stage4(blk):
        r = rows(blk)
        oi_s[r, :] = _dot(a_s[r, :], i_ref[0, r, :])
        st = st_s[...]
        for c in range(nc):
            sp_s[blk * nc + c] = st.astype(BF16)
            st = st * dec_s[blk * nc + c] + u_s[blk * nc + c]
        st_s[...] = st

    def stage5(blk):
        r = rows(blk)
        inter = [_dot_nt(qb_s[pl.ds(pl.multiple_of(blk * rb + c * cs, cs), cs), :], sp_s[blk * nc + c])
                 for c in range(nc)]
        o = oi_s[r, :] + jnp.concatenate(inter, axis=0)
        o = _rms(o, og_ref[...]) * _silu(g_ref[0, r, :].astype(F32))
        o_ref[0, r, :] = o.astype(BF16)

    st_s[...] = jnp.zeros_like(st_s)
    _software_pipeline([stage1, stage2, stage3, stage4, stage5], seq // rb)


def _hgrn(proj3, lb_logits, onorm_g, layer):
    bsz, seq, _ = proj3.shape
    nh = HGRN_HEADS

    def col(off):
        return pl.BlockSpec((1, seq, HGRN_DK), lambda b, h: (b, 0, off * nh + h))

    nl = lb_logits.shape[0]
    return pl.pallas_call(
        functools.partial(_hgrn_kernel, layer=layer),
        grid=(bsz, nh),
        in_specs=[col(0), col(1), col(2), col(3),
                  pl.BlockSpec((nl, HGRN_DK), lambda b, h: (0, h)),
                  pl.BlockSpec((1, HGRN_DK), lambda b, h: (0, 0))],
        out_specs=pl.BlockSpec((1, seq, HGRN_DK), lambda b, h: (b, 0, h)),
        out_shape=jax.ShapeDtypeStruct((bsz, seq, HGRN_WIDTH), BF16),
        scratch_shapes=[pltpu.VMEM((seq, HGRN_DK), BF16),
                        pltpu.VMEM((seq // HGRN_CHUNK, HGRN_DK, HGRN_DK), F32),
                        pltpu.VMEM((seq // HGRN_CHUNK, HGRN_DK, HGRN_DK), BF16),
                        pltpu.VMEM((seq // HGRN_CHUNK, 1, HGRN_DK), F32),
                        pltpu.VMEM((seq, HGRN_DK), F32),
                        pltpu.VMEM((seq, HGRN_DK), F32), pltpu.VMEM((seq, HGRN_DK), F32),
                        pltpu.VMEM((seq, HGRN_DK), BF16), pltpu.VMEM((seq, HGRN_DK), BF16),
                        pltpu.VMEM((seq, HGRN_DK), BF16), pltpu.VMEM((seq, min(HG_ROWS, seq)), BF16),
                        pltpu.VMEM((HGRN_DK, HGRN_DK), F32)],
        compiler_params=_cparams(("arbitrary", "arbitrary")),
        name="hgrn",
    )(proj3, proj3, proj3, proj3, lb_logits, onorm_g)


UP_ROWS = 1024


def _up_kernel(p_ref, pos_ref, wq_ref, wkv_ref, qag_ref, kvag_ref, qg_ref, kg_ref, freq_ref,
               q_ref, k_ref, v_ref):
    half = MLA_ROPE // 2
    p = p_ref[...].astype(F32)
    qa_n = _rms(p[:, 0:Q_LORA], qag_ref[...]).astype(BF16)
    kva_n = _rms(p[:, Q_LORA:Q_LORA + KV_LORA], kvag_ref[...]).astype(BF16)
    kpe_t = p[:, Q_LORA + KV_LORA:Q_LORA + KV_LORA + LANES].T[0:MLA_ROPE]
    ang = freq_ref[...] * pos_ref[...].astype(F32)
    cos = jnp.cos(ang)
    sin = jnp.sin(ang)

    def rope(x):
        x1, x2 = x[0:half], x[half:]
        return jnp.concatenate([x1 * cos - x2 * sin, x2 * cos + x1 * sin], axis=0)

    kg = kg_ref[...]
    kpe_ss = jnp.sum(kpe_t * kpe_t, axis=0, keepdims=True)
    kpe_rot = rope(kpe_t * kg[MLA_NOPE:])
    qg = qg_ref[...] * (MLA_QK ** -0.5 * LOG2E)
    zpad = jnp.zeros((MLA_QPAD - MLA_QK, p.shape[0]), F32)
    for h in range(MLA_HEADS):
        rows = slice(h * MLA_QPAD, (h + 1) * MLA_QPAD)
        qt = _dot_nt(wq_ref[rows, :], qa_n)
        qn = qt[0:MLA_QK] * lax.rsqrt(jnp.sum(qt * qt, axis=0, keepdims=True) / MLA_QK + EPS) * qg
        q_ref[0, h] = jnp.concatenate([qn[0:MLA_NOPE], rope(qn[MLA_NOPE:]), zpad], axis=0).astype(BF16)
        kvt = _dot_nt(wkv_ref[rows, :], kva_n)
        kn = kvt[0:MLA_NOPE]
        rk = lax.rsqrt((jnp.sum(kn * kn, axis=0, keepdims=True) + kpe_ss) / MLA_QK + EPS)
        kt = jnp.concatenate([kn * rk * kg[0:MLA_NOPE], kpe_rot * rk, zpad], axis=0)
        k_ref[0, h] = kt.T.astype(BF16)
        v_ref[0, h] = kvt[MLA_NOPE:].astype(BF16)


def _mla_up(proj3, pos3, wq_t, wkv_t, qag, kvag, qg, kg, freq):
    bsz, seq, _ = proj3.shape
    tm = min(UP_ROWS, seq)
    nh = MLA_HEADS
    mla_block = 4 * HGRN_WIDTH // 1024

    def const(shape):
        return pl.BlockSpec(shape, lambda b, i: (0,) * len(shape))

    return pl.pallas_call(
        _up_kernel,
        grid=(bsz, seq // tm),
        in_specs=[pl.BlockSpec((None, tm, 1024), lambda b, i: (b, i, mla_block)),
                  pl.BlockSpec((None, 1, tm), lambda b, i: (b, 0, i)),
                  const(wq_t.shape), const(wkv_t.shape), const(qag.shape), const(kvag.shape),
                  const(qg.shape), const(kg.shape), const(freq.shape)],
        out_specs=[pl.BlockSpec((1, nh, MLA_QPAD, tm), lambda b, i: (b, 0, 0, i)),
                   pl.BlockSpec((1, nh, tm, MLA_QPAD), lambda b, i: (b, 0, i, 0)),
                   pl.BlockSpec((1, nh, MLA_V, tm), lambda b, i: (b, 0, 0, i))],
        out_shape=[jax.ShapeDtypeStruct((bsz, nh, MLA_QPAD, seq), BF16),
                   jax.ShapeDtypeStruct((bsz, nh, seq, MLA_QPAD), BF16),
                   jax.ShapeDtypeStruct((bsz, nh, MLA_V, seq), BF16)],
        compiler_params=_cparams(("arbitrary", "arbitrary")),
        name="mla_up",
    )(proj3, pos3, wq_t, wkv_t, qag, kvag, qg, kg, freq)


ATT_T = 256


def _attn_kernel(qt_ref, k_ref, vt_ref, g_ref, o_ref):
    seq = k_ref.shape[2]
    t = min(ATT_T, seq)
    key = lax.broadcasted_iota(jnp.int32, (t, t), 0)
    qry = lax.broadcasted_iota(jnp.int32, (t, t), 1)
    causal = key <= qry
    neg = jnp.finfo(F32).min

    def widen(x, off, fill):
        return x if off == 0 else jnp.concatenate([jnp.full((x.shape[0], off), fill, x.dtype), x], axis=1)

    scores = []
    m = None
    for off in range(0, seq, t):
        s = _dot(k_ref[0, 0, off:off + t, :], qt_ref[0, 0, :, off:])
        diag = jnp.where(causal, s[:, 0:t], neg)
        s = diag if off + t == seq else jnp.concatenate([diag, s[:, t:]], axis=1)
        scores.append(s)
        blk_max = widen(jnp.max(s, axis=0, keepdims=True), off, neg)
        m = blk_max if m is None else jnp.maximum(m, blk_max)
    l = None
    ot = None
    for off, s in zip(range(0, seq, t), scores):
        p = jnp.exp2(s - m[:, off:])
        p_sum = widen(jnp.sum(p, axis=0, keepdims=True), off, 0.0)
        pv = widen(_dot(vt_ref[0, 0, :, off:off + t], p.astype(BF16)), off, 0.0)
        l = p_sum if l is None else l + p_sum
        ot = pv if ot is None else ot + pv
    ot = ot * (1.0 / l)
    ot = ot * lax.rsqrt(jnp.mean(ot * ot, axis=0, keepdims=True) + EPS) * g_ref[...]
    o_ref[0, :, :] = ot.T.astype(BF16)


def _attention(q, k, v, g):
    bsz, nh, seq, _ = k.shape
    return pl.pallas_call(
        _attn_kernel,
        grid=(bsz, nh),
        in_specs=[pl.BlockSpec((1, 1, MLA_QPAD, seq), lambda b, h: (b, h, 0, 0)),
                  pl.BlockSpec((1, 1, seq, MLA_QPAD), lambda b, h: (b, h, 0, 0)),
                  pl.BlockSpec((1, 1, MLA_V, seq), lambda b, h: (b, h, 0, 0)),
                  pl.BlockSpec((MLA_V, 1), lambda b, h: (0, 0))],
        out_specs=pl.BlockSpec((1, seq, MLA_V), lambda b, h: (b, 0, h)),
        out_shape=jax.ShapeDtypeStruct((bsz, seq, nh * MLA_V), BF16),
        compiler_params=_cparams(("arbitrary", "arbitrary")),
        name="attn",
    )(q, k, v, g)


def _out_kernel(oa_ref, ob_ref, x_ref, wa_ref, wb_ref, g1_ref, n2g_ref, sh2_ref, sc2_ref,
                wr_ref, br_ref, x1_ref, h2_ref, lg_ref, mix_s):
    @pl.when(pl.program_id(0) == 0)
    def _():
        mix_s[...] = jnp.zeros_like(mix_s)

    mix_next = _dot(oa_ref[...], wa_ref[...]) + _dot(ob_ref[...], wb_ref[...])
    x1 = x_ref[...] + g1_ref[0] * mix_s[...]
    x1_ref[...] = x1
    h2 = _rms(x1, n2g_ref[...]) * (1.0 + sc2_ref[0]) + sh2_ref[0]
    tm = h2.shape[0]
    for s in range(TOK_SUBLANES):
        h2_ref[pl.ds(s, tm, stride=TOK_SUBLANES), :] = _pack_pair(
            h2[:, s * LANES:(s + 1) * LANES], h2[:, (s + TOK_SUBLANES) * LANES:(s + TOK_SUBLANES + 1) * LANES])
    lg = _dot(h2.astype(BF16), wr_ref[...]) + br_ref[...]
    lg_ref[...] = lg.T[0:ROUTE_ROWS, :]
    mix_s[...] = mix_next


def _out_proj(oa, ob, x2, wa, wb, g1, n2g, sh2, sc2, wr, br, seq):
    t, d = x2.shape
    tm = min(512, seq)
    n_tiles = t // tm

    def const(shape):
        return pl.BlockSpec(shape, lambda i: (0,) * len(shape))

    def prev(i):
        return jnp.maximum(i - 1, 0)

    def per_batch():
        return pl.BlockSpec((1, 1, d), lambda i: (prev(i) * tm // seq, 0, 0))

    return pl.pallas_call(
        _out_kernel,
        grid=(n_tiles + 1,),
        in_specs=[pl.BlockSpec((tm, HGRN_WIDTH), lambda i: (jnp.minimum(i, n_tiles - 1), 0)),
                  pl.BlockSpec((tm, HGRN_WIDTH), lambda i: (jnp.minimum(i, n_tiles - 1), 0)),
                  pl.BlockSpec((tm, d), lambda i: (prev(i), 0)),
                  const(wa.shape), const(wb.shape), per_batch(), const(n2g.shape),
                  per_batch(), per_batch(), const(wr.shape), const(br.shape)],
        out_specs=[pl.BlockSpec((tm, d), lambda i: (prev(i), 0)),
                   pl.BlockSpec((tm * TOK_SUBLANES, LANES), lambda i: (prev(i), 0)),
                   pl.BlockSpec((ROUTE_ROWS, tm), lambda i: (0, prev(i)))],
        out_shape=[jax.ShapeDtypeStruct((t, d), F32),
                   jax.ShapeDtypeStruct((t * TOK_SUBLANES, LANES), jnp.uint32),
                   jax.ShapeDtypeStruct((ROUTE_ROWS, t), F32)],
        scratch_shapes=[pltpu.VMEM((tm, d), F32)],
        compiler_params=_cparams(("arbitrary",)),
        name="out_proj",
    )(oa, ob, x2, wa, wb, g1, n2g, sh2, sc2, wr, br)


def _route_kernel(lg_ref, tri_ref, ri_ref, rw_ref, cnt_ref, carry_s):
    step = pl.program_id(0)

    @pl.when(step == 0)
    def _():
        carry_s[...] = jnp.zeros_like(carry_s)

    lg = lg_ref[...]
    tr = lg.shape[1]
    epg = EXPERTS_PER_GROUP
    gl = lg[N_EXPERTS:N_EXPERTS + N_GROUPS, :]
    row_g = lax.broadcasted_iota(jnp.int32, (N_GROUPS, tr), 0)
    gmax = jnp.max(gl, axis=0, keepdims=True)
    g_sel = jnp.min(jnp.where(gl == gmax, row_g, N_GROUPS), axis=0, keepdims=True)
    p_group = 1.0 / jnp.sum(jnp.exp(gl - gmax), axis=0, keepdims=True)

    e_in = lg[0:epg, :]
    for g in range(1, N_GROUPS):
        e_in = jnp.where(g_sel == g, lg[g * epg:(g + 1) * epg, :], e_in)
    row_e = lax.broadcasted_iota(jnp.int32, (epg, tr), 0)
    top1 = jnp.max(e_in, axis=0, keepdims=True)
    i1 = jnp.min(jnp.where(e_in == top1, row_e, epg), axis=0, keepdims=True)
    rest = jnp.where(row_e == i1, -jnp.inf, e_in)
    top2 = jnp.max(rest, axis=0, keepdims=True)
    i2 = jnp.min(jnp.where(rest == top2, row_e, epg), axis=0, keepdims=True)
    e2w = jnp.exp(top2 - top1)
    w1 = p_group / (1.0 + e2w)
    w2 = p_group * e2w / (1.0 + e2w)
    ex1 = g_sel * epg + i1
    ex2 = g_sel * epg + i2

    row_x = lax.broadcasted_iota(jnp.int32, (N_EXPERTS, tr), 0)
    oh1 = row_x == ex1
    oh2 = row_x == ex2
    oh = jnp.logical_or(oh1, oh2)
    before = _dot(oh.astype(BF16), tri_ref[...]) + carry_s[:, 0:1]
    rank1 = jnp.sum(jnp.where(oh1, before, 0.0), axis=0, keepdims=True)
    rank2 = jnp.sum(jnp.where(oh2, before, 0.0), axis=0, keepdims=True)
    carry_s[...] = carry_s[...] + jnp.sum(oh.astype(F32), axis=1, keepdims=True)

    zi = jnp.zeros((4, tr), jnp.int32)
    ri_ref[...] = jnp.concatenate([ex1, ex2, rank1.astype(jnp.int32), rank2.astype(jnp.int32), zi], axis=0)
    rw_ref[...] = jnp.concatenate([w1, w2, jnp.zeros((6, tr), F32)], axis=0)
    cnt_ref[...] = carry_s[...].astype(jnp.int32)


def _route(lg_t, tri):
    t = lg_t.shape[1]
    tr = tri.shape[0]
    return pl.pallas_call(
        _route_kernel,
        grid=(t // tr,),
        in_specs=[pl.BlockSpec((ROUTE_ROWS, tr), lambda i: (0, i)),
                  pl.BlockSpec((tr, tr), lambda i: (0, 0))],
        out_specs=[pl.BlockSpec((8, tr), lambda i: (0, i)),
                   pl.BlockSpec((8, tr), lambda i: (0, i)),
                   pl.BlockSpec((N_EXPERTS, LANES), lambda i: (0, 0))],
        out_shape=[jax.ShapeDtypeStruct((8, t), jnp.int32),
                   jax.ShapeDtypeStruct((8, t), F32),
                   jax.ShapeDtypeStruct((N_EXPERTS, LANES), jnp.int32)],
        scratch_shapes=[pltpu.VMEM((N_EXPERTS, LANES), F32)],
        compiler_params=_cparams(("arbitrary",)),
        name="route",
    )(lg_t, tri)


def _moe_kernel(rd_ref, be_ref, pe_ref, nb_ref, x_ref, wg_ref, wu_ref, wd_ref, y_ref,
                ys, wg_f, wu_f, wd_f, wg_s, wu_s, wd_s, ssem, wsem):
    b = pl.program_id(0)
    last = pl.num_programs(0) - 1
    nb = nb_ref[0]
    tm = MOE_BLOCK
    ts = TOK_SUBLANES
    weights = ((wg_ref, wg_f, wg_s), (wu_ref, wu_f, wu_s), (wd_ref, wd_f, wd_s))

    def fetch(e):
        return [pltpu.make_async_copy(src.at[e], stage, wsem.at[k]) for k, (src, stage, _) in enumerate(weights)]

    def scatter(blk):
        base = blk * tm
        for j in range(tm):
            dst = pl.multiple_of(rd_ref[base + j] * ts, ts)
            pltpu.make_async_copy(ys.at[pl.ds(j * ts, ts), :], y_ref.at[pl.ds(dst, ts), :], ssem).start()

    def wait_scatter():
        pltpu.make_async_copy(ys, y_ref.at[pl.ds(0, tm * ts), :], ssem).wait()

    @pl.when(b == 0)
    def _():
        for copy in fetch(be_ref[0]):
            copy.start()
        ys[...] = jnp.zeros_like(ys)
        spare = pltpu.make_async_copy(ys, y_ref.at[pl.ds(y_ref.shape[0] - tm * ts, tm * ts), :], ssem)
        spare.start()
        spare.wait()

    expert = be_ref[b]
    first_of_expert = jnp.logical_or(b == 0, expert != be_ref[jnp.maximum(b - 1, 0)])

    def block(fresh):
        scatter(jnp.maximum(b - 1, 0))
        parts = [_unpack_pair(x_ref[pl.ds(s, tm, stride=ts), :]) for s in range(ts)]
        x = jnp.concatenate([p[0] for p in parts] + [p[1] for p in parts], axis=1).astype(BF16)
        if fresh:
            for copy in fetch(0):
                copy.wait()
            k_half = x.shape[1] // 2
            g = u = None
            for k0 in (0, k_half):
                ks = slice(k0, k0 + k_half)
                wgk = wg_f[ks, :].astype(BF16)
                wuk = wu_f[ks, :].astype(BF16)
                wg_s[ks, :] = wgk
                wu_s[ks, :] = wuk
                gk = _dot(x[:, ks], wgk)
                uk = _dot(x[:, ks], wuk)
                g = gk if g is None else g + gk
                u = uk if u is None else u + uk
            wdb = wd_f[...].astype(BF16)
            wd_s[...] = wdb
            y = _dot((_silu(g) * u).astype(BF16), wdb)
            next_blk = pe_ref[expert] // tm

            @pl.when(next_blk < nb)
            def _():
                for copy in fetch(be_ref[next_blk]):
                    copy.start()
        else:
            hid = _silu(_dot(x, wg_s[...])) * _dot(x, wu_s[...])
            y = _dot(hid.astype(BF16), wd_s[...])
        wait_scatter()
        for s in range(ts):
            ys[pl.ds(s, tm, stride=ts), :] = _pack_pair(y[:, s * LANES:(s + 1) * LANES],
                                                        y[:, (s + ts) * LANES:(s + ts + 1) * LANES])

        @pl.when(b == last)
        def _():
            scatter(b)
            wait_scatter()

    real = b < nb
    pl.when(jnp.logical_and(real, first_of_expert))(lambda: block(True))
    pl.when(jnp.logical_and(real, jnp.logical_not(first_of_expert)))(lambda: block(False))

    @pl.when(b == nb)
    def _():
        scatter(b - 1)
        wait_scatter()


def _moe(row_dst, blk_e, pad_end, nb_real, x_sorted, wg, wu, wd, n_out_rows):
    n_blocks = blk_e.shape[0]
    d, de = wg.shape[1], wg.shape[2]
    tm = MOE_BLOCK
    hbm = pl.BlockSpec(memory_space=pl.ANY)
    return pl.pallas_call(
        _moe_kernel,
        grid_spec=pltpu.PrefetchScalarGridSpec(
            num_scalar_prefetch=4,
            grid=(n_blocks,),
            in_specs=[pl.BlockSpec((tm * TOK_SUBLANES, LANES),
                                   lambda b, rd, be, pe, nb: (jnp.minimum(b, nb[0] - 1), 0)),
                      hbm, hbm, hbm],
            out_specs=hbm,
            scratch_shapes=[pltpu.VMEM((tm * TOK_SUBLANES, LANES), jnp.uint32),
                            pltpu.VMEM((d, de), F32), pltpu.VMEM((d, de), F32), pltpu.VMEM((de, d), F32),
                            pltpu.VMEM((d, de), BF16), pltpu.VMEM((d, de), BF16), pltpu.VMEM((de, d), BF16),
                            pltpu.SemaphoreType.DMA(()), pltpu.SemaphoreType.DMA((3,))]),
        out_shape=jax.ShapeDtypeStruct((n_out_rows * TOK_SUBLANES, LANES), jnp.uint32),
        compiler_params=_cparams(("arbitrary",)),
        name="moe",
    )(row_dst, blk_e, pad_end, nb_real, x_sorted, wg, wu, wd)


def _dispatch_kernel(e1_ref, e2_ref, r1_ref, r2_ref, ps_ref, pe_ref, cnt_ref, be_ref, nb_ref,
                     h_ref, x_ref, rd_ref, zbuf, sem, zsem):
    i = pl.program_id(0)
    ts = TOK_SUBLANES
    td = h_ref.shape[0] // ts
    n_tok = e1_ref.shape[0]
    base = i * td
    blk_rows = MOE_BLOCK * ts

    def zero_fill(blk):
        return pltpu.make_async_copy(
            zbuf, x_ref.at[pl.ds(pl.multiple_of(blk * blk_rows, blk_rows), blk_rows), :], zsem)

    @pl.when(i == 0)
    def _():
        zbuf[...] = jnp.zeros_like(zbuf)

        def fill(blk, n_fills):
            e = be_ref[blk]
            partial_last = jnp.logical_and(blk == pe_ref[e] // MOE_BLOCK - 1, cnt_ref[e] % MOE_BLOCK != 0)
            needs_fill = jnp.logical_or(blk >= nb_ref[0], partial_last)

            @pl.when(needs_fill)
            def _():
                zero_fill(blk).start()
            return n_fills + needs_fill.astype(jnp.int32)

        n_fills = lax.fori_loop(0, be_ref.shape[0], fill, 0)

        def init(blk, carry):
            for j in range(MOE_BLOCK):
                rd_ref[blk * MOE_BLOCK + j] = 2 * n_tok + j
            return carry

        lax.fori_loop(0, rd_ref.shape[0] // MOE_BLOCK, init, 0)

        def drain(k, carry):
            zero_fill(0).wait()
            return carry

        lax.fori_loop(0, n_fills, drain, 0)

    def start(t, carry):
        src = h_ref.at[pl.ds(pl.multiple_of(t * ts, ts), ts), :]
        tok = base + t
        for slot, (e_ref, r_ref) in enumerate(((e1_ref, r1_ref), (e2_ref, r2_ref))):
            row = ps_ref[e_ref[tok]] + r_ref[tok]
            pltpu.make_async_copy(src, x_ref.at[pl.ds(pl.multiple_of(row * ts, ts), ts), :], sem).start()
            rd_ref[row] = 2 * tok + slot
        return carry

    lax.fori_loop(0, td, start, 0, unroll=8)
    for _ in range(2):
        pltpu.make_async_copy(h_ref, x_ref.at[pl.ds(0, td * ts), :], sem).wait()


def _dispatch(ri, pad_start, pad_end, counts, blk_e, nb_real, h2p, n_rows):
    t = ri.shape[1]
    td = min(512, t)
    ts = TOK_SUBLANES
    return pl.pallas_call(
        _dispatch_kernel,
        grid_spec=pltpu.PrefetchScalarGridSpec(
            num_scalar_prefetch=9,
            grid=(t // td,),
            in_specs=[pl.BlockSpec((td * ts, LANES), lambda i, *_: (i, 0))],
            out_specs=[pl.BlockSpec(memory_space=pl.ANY), pl.BlockSpec(memory_space=pltpu.SMEM)],
            scratch_shapes=[pltpu.VMEM((MOE_BLOCK * ts, LANES), jnp.uint32),
                            pltpu.SemaphoreType.DMA(()), pltpu.SemaphoreType.DMA(())]),
        out_shape=[jax.ShapeDtypeStruct((n_rows * ts, LANES), jnp.uint32),
                   jax.ShapeDtypeStruct((n_rows,), jnp.int32)],
        compiler_params=_cparams(("arbitrary",)),
        name="dispatch",
    )(ri[0], ri[1], ri[2], ri[3], pad_start, pad_end, counts, blk_e, nb_real, h2p)


def _moe_plan(counts, t):
    tm = MOE_BLOCK
    n_blocks = 2 * t // tm + N_EXPERTS
    padded = ((counts + tm - 1) // tm) * tm
    pad_end = jnp.cumsum(padded)
    pad_start = pad_end - padded
    blk = jnp.arange(n_blocks, dtype=jnp.int32)
    blk_e = jnp.minimum(jnp.sum(pad_end[None, :] <= (blk * tm)[:, None], axis=1), N_EXPERTS - 1).astype(jnp.int32)
    nb_real = (pad_end[-1:] // tm).astype(jnp.int32)
    return pad_start, pad_end, blk_e, nb_real, n_blocks * tm


COMBINE_ROWS = 64


def _combine_kernel(x1_ref, g2_ref, rw_ref, y_ref, o_ref, w_s):
    ts = TOK_SUBLANES
    rows = min(COMBINE_ROWS, x1_ref.shape[0])
    w_s[...] = rw_ref[...].T

    def chunk(c, carry):
        r = pl.ds(pl.multiple_of(c * rows, rows), rows)
        w1 = w_s[r, 0:1]
        w2 = w_s[r, 1:2]
        tile0 = pl.multiple_of(c * rows * 2 * ts, rows * 2 * ts)
        for s in range(ts):
            a_lo, a_hi = _unpack_pair(y_ref[pl.ds(tile0 + s, rows, stride=2 * ts), :])
            b_lo, b_hi = _unpack_pair(y_ref[pl.ds(tile0 + ts + s, rows, stride=2 * ts), :])
            for col, ya, yb in ((s, a_lo, b_lo), (s + ts, a_hi, b_hi)):
                cols = slice(col * LANES, (col + 1) * LANES)
                o_ref[r, cols] = x1_ref[r, cols] + g2_ref[0][:, cols] * (w1 * ya + w2 * yb)
        return carry

    lax.fori_loop(0, x1_ref.shape[0] // rows, chunk, 0)


def _combine(x1, g2, rw, y2, seq):
    t, d = x1.shape
    tc = min(512, seq)
    return pl.pallas_call(
        _combine_kernel,
        grid=(t // tc,),
        in_specs=[pl.BlockSpec((tc, d), lambda i: (i, 0)),
                  pl.BlockSpec((1, 1, d), lambda i: (i * tc // seq, 0, 0)),
                  pl.BlockSpec((rw.shape[0], tc), lambda i: (0, i)),
                  pl.BlockSpec((tc * 2 * TOK_SUBLANES, LANES), lambda i: (i, 0))],
        out_specs=pl.BlockSpec((tc, d), lambda i: (i, 0)),
        out_shape=jax.ShapeDtypeStruct((t, d), F32),
        scratch_shapes=[pltpu.VMEM((tc, rw.shape[0]), F32)],
        compiler_params=_cparams(("arbitrary",)),
        name="combine",
    )(x1, g2, rw, y2)


def _q_up_layout(w_q_up):
    w = w_q_up.reshape(Q_LORA, MLA_HEADS, MLA_QK)
    w = jnp.pad(w, ((0, 0), (0, 0), (0, MLA_QPAD - MLA_QK)))
    return w.reshape(Q_LORA, MLA_HEADS * MLA_QPAD).T.astype(BF16)


def _pad_lanes(g, width):
    return jnp.pad(g, (0, width - g.shape[0])).reshape(1, width)


def kernel(x, c, positions, w_ada, b_ada, norm1_g, w_in, hgrn_lb_logits, hgrn_onorm_g, q_a_norm_g, w_q_up,
           kv_a_norm_g, w_kv_up, q_norm_g, k_norm_g, attn_onorm_g, w_out, norm2_g, w_group, b_group,
           w_router, b_router, w_gate, w_up, w_down):
    bsz, seq, d = x.shape
    t = bsz * seq
    depth = w_ada.shape[0]
    half = MLA_ROPE // 2
    inv_freq = ROPE_BASE ** (-jnp.arange(0, MLA_ROPE, 2, dtype=F32) / MLA_ROPE)
    freq = inv_freq.reshape(half, 1)
    pos3 = positions.reshape(bsz, 1, seq)
    tr = min(1024, t)
    tri = jnp.triu(jnp.ones((tr, tr), BF16), 1)

    x2 = x.reshape(t, d)
    for l in range(depth):
        mod = _ada(c, w_ada[l], b_ada[l]).reshape(bsz, 6, 1, d)
        sh1, sc1, g1, sh2, sc2, g2 = (mod[:, i] for i in range(6))

        proj = _in_proj(x2, norm1_g[l].reshape(1, d), sh1, sc1, w_in[l].T, seq)
        proj3 = proj.reshape(bsz, seq, IN_COLS_PAD)

        o_a = _hgrn(proj3, hgrn_lb_logits, hgrn_onorm_g[l].reshape(1, HGRN_DK), l)

        q, k, v = _mla_up(proj3, pos3, _q_up_layout(w_q_up[l]), w_kv_up[l].T.astype(BF16),
                          q_a_norm_g[l].reshape(1, Q_LORA), kv_a_norm_g[l].reshape(1, KV_LORA),
                          q_norm_g[l].reshape(MLA_QK, 1), k_norm_g[l].reshape(MLA_QK, 1), freq)
        o_b = _attention(q, k, v, attn_onorm_g[l].reshape(MLA_V, 1))

        w_o = w_out[l].astype(BF16)
        wr = jnp.pad(jnp.concatenate([w_router[l], w_group[l]], axis=1),
                     ((0, 0), (0, LANES - N_EXPERTS - N_GROUPS))).astype(BF16)
        br = _pad_lanes(jnp.concatenate([b_router[l], b_group[l]]), LANES)
        x1, h2, lg_t = _out_proj(o_a.reshape(t, HGRN_WIDTH), o_b.reshape(t, HGRN_WIDTH), x2,
                                 w_o[:HGRN_WIDTH], w_o[HGRN_WIDTH:], g1, norm2_g[l].reshape(1, d),
                                 sh2, sc2, wr, br, seq)

        ri, rw, cnt = _route(lg_t, tri)
        counts = cnt[:, 0]
        pad_start, pad_end, blk_e, nb_real, n_rows = _moe_plan(counts, t)
        x_sorted, row_dst = _dispatch(ri, pad_start, pad_end, counts, blk_e, nb_real, h2, n_rows)
        y2 = _moe(row_dst, blk_e, pad_end, nb_real, x_sorted, w_gate[l], w_up[l], w_down[l], 2 * t + MOE_BLOCK)
        x2 = _combine(x1, g2, rw, y2, seq)
    return x2.reshape(bsz, seq, d)
```

```python
import functools

import jax
import jax.numpy as jnp
from jax import lax
from jax.experimental import pallas as pl
from jax.experimental.pallas import tpu as pltpu

F32 = jnp.float32
BF16 = jnp.bfloat16
EPS = 1e-6
LOG2E = 1.4426950408889634

D_MODEL = 2048
HGRN_WIDTH = 1024
HGRN_DK = 128
HGRN_HEADS = 8
HGRN_CHUNK = 64
MLA_HEADS = 8
MLA_NOPE = 128
MLA_ROPE = 64
MLA_QK = MLA_NOPE + MLA_ROPE
MLA_V = 128
MLA_QPAD = 256
Q_LORA = 512
KV_LORA = 256
ROPE_BASE = 10000.0
IN_COLS = 4 * HGRN_WIDTH + Q_LORA + KV_LORA + MLA_ROPE
IN_COLS_PAD = 5120
N_GROUPS = 4
EXPERTS_PER_GROUP = 8
N_EXPERTS = 32
D_EXPERT = 512
ROUTE_ROWS = 40
MOE_BLOCK = 256
LANES = 128
TOK_SUBLANES = 8
VMEM_LIMIT = 56 * 1024 * 1024


def _cparams(sem):
    return pltpu.CompilerParams(dimension_semantics=sem, vmem_limit_bytes=VMEM_LIMIT)


def _dot(a, b):
    return jnp.dot(a, b, preferred_element_type=F32)


def _dot_nt(a, b):
    return lax.dot_general(a, b, (((1,), (1,)), ((), ())), preferred_element_type=F32)


def _rms(x, g):
    return x * lax.rsqrt(jnp.mean(x * x, axis=-1, keepdims=True) + EPS) * g


def _silu(x):
    return x * jax.nn.sigmoid(x)


def _pack_pair(lo, hi):
    lo_b = lax.bitcast_convert_type(lo.astype(BF16).astype(F32), jnp.uint32)
    hi_b = lax.bitcast_convert_type(hi.astype(BF16).astype(F32), jnp.uint32)
    return hi_b | (lo_b >> 16)


def _unpack_pair(w):
    lo = lax.bitcast_convert_type(w << 16, F32)
    hi = lax.bitcast_convert_type(w & jnp.uint32(0xFFFF0000), F32)
    return lo, hi


def _ada_kernel(c_ref, w_ref, b_ref, o_ref):
    ca = _silu(c_ref[...]).astype(BF16)
    o_ref[...] = _dot(ca, w_ref[...].astype(BF16)) + b_ref[...]


def _ada(c, w, b):
    bsz, d = c.shape
    n = w.shape[1]
    tn = 1024
    return pl.pallas_call(
        _ada_kernel,
        grid=(n // tn,),
        in_specs=[pl.BlockSpec((bsz, d), lambda j: (0, 0)),
                  pl.BlockSpec((d, tn), lambda j: (0, j)),
                  pl.BlockSpec((1, tn), lambda j: (0, j))],
        out_specs=pl.BlockSpec((bsz, tn), lambda j: (0, j)),
        out_shape=jax.ShapeDtypeStruct((bsz, n), F32),
        compiler_params=_cparams(("arbitrary",)),
        name="ada",
    )(c, w, b.reshape(1, n))


NORM_ROWS = 32


def _norm_kernel(x_ref, g_ref, sh_ref, sc_ref, h_ref):
    def body(c, carry):
        r = pl.ds(pl.multiple_of(c * NORM_ROWS, NORM_ROWS), NORM_ROWS)
        h = _rms(x_ref[r, :], g_ref[...]) * (1.0 + sc_ref[0]) + sh_ref[0]
        h_ref[r, :] = h.astype(BF16)
        return carry
    lax.fori_loop(0, x_ref.shape[0] // NORM_ROWS, body, 0, unroll=2)


def _norm_mod(x2, g, sh, sc, seq, n_rows):
    t, d = x2.shape
    tm = min(1024, seq, n_rows)
    return pl.pallas_call(
        _norm_kernel,
        grid=(n_rows // tm,),
        in_specs=[pl.BlockSpec((tm, d), lambda i: (i, 0)),
                  pl.BlockSpec((1, d), lambda i: (0, 0)),
                  pl.BlockSpec((1, 1, d), lambda i: (i * tm // seq, 0, 0)),
                  pl.BlockSpec((1, 1, d), lambda i: (i * tm // seq, 0, 0))],
        out_specs=pl.BlockSpec((tm, d), lambda i: (i, 0)),
        out_shape=jax.ShapeDtypeStruct((n_rows, d), BF16),
        compiler_params=_cparams(("arbitrary",)),
        name="norm_mod",
    )(x2, g, sh, sc)


IN_NORM_CHUNKS = 4


def _in_kernel(h0_ref, x_ref, g_ref, sh_ref, sc_ref, w_ref, o_ref, h_s):
    i = pl.program_id(0)
    j = pl.program_id(1)
    slot = i % 2

    @pl.when(jnp.logical_and(i == 0, j == 0))
    def _():
        h_s[0] = h0_ref[...]

    o_ref[...] = _dot_nt(h_s[slot], w_ref[...]).astype(BF16)
    rows = x_ref.shape[0] // IN_NORM_CHUNKS
    first = jnp.minimum(j, IN_NORM_CHUNKS - 1) * rows
    gain = g_ref[...]
    scale = 1.0 + sc_ref[0]
    shift = sh_ref[0]
    for c in range(rows // NORM_ROWS):
        r = pl.ds(pl.multiple_of(first + c * NORM_ROWS, NORM_ROWS), NORM_ROWS)
        h_s[1 - slot, r, :] = (_rms(x_ref[r, :], gain) * scale + shift).astype(BF16)


def _in_proj(x2, g, sh, sc, w_t, seq):
    t, d = x2.shape
    tn = 1024
    tm = min(1024, seq)
    n_tiles = t // tm
    w_b = jnp.pad(w_t, ((0, IN_COLS_PAD - IN_COLS), (0, 0))).astype(BF16)
    h0 = _norm_mod(x2, g, sh, sc, seq, tm)

    def nxt(i):
        return jnp.minimum(i + 1, n_tiles - 1)

    return pl.pallas_call(
        _in_kernel,
        grid=(n_tiles, IN_COLS_PAD // tn),
        in_specs=[pl.BlockSpec((tm, d), lambda i, j: (0, 0)),
                  pl.BlockSpec((tm, d), lambda i, j: (nxt(i), 0)),
                  pl.BlockSpec((1, d), lambda i, j: (0, 0)),
                  pl.BlockSpec((1, 1, d), lambda i, j: (nxt(i) * tm // seq, 0, 0)),
                  pl.BlockSpec((1, 1, d), lambda i, j: (nxt(i) * tm // seq, 0, 0)),
                  pl.BlockSpec((tn, d), lambda i, j: (j, 0))],
        out_specs=pl.BlockSpec((tm, tn), lambda i, j: (i, j)),
        out_shape=jax.ShapeDtypeStruct((t, IN_COLS_PAD), BF16),
        scratch_shapes=[pltpu.VMEM((2, tm, d), BF16)],
        compiler_params=_cparams(("arbitrary", "arbitrary")),
        name="in_proj",
    )(h0, x2, g, sh, sc, w_b)


HG_ROWS = 256
HGRN_HEADS_PER_STEP = 4


PIPELINE_STATIC_BLOCKS = 8


def _software_pipeline(stages, n_blocks):
    depth = len(stages)

    def step(i, static):
        for k in reversed(range(depth)):
            if static and not 0 <= i - k < n_blocks:
                continue
            stages[k](i - k)

    if n_blocks <= PIPELINE_STATIC_BLOCKS:
        for i in range(n_blocks + depth - 1):
            step(i, True)
        return
    for i in range(depth - 1):
        step(i, True)

    def steady(i, carry):
        step(i, False)
        return carry

    lax.fori_loop(depth - 1, n_blocks, steady, 0)
    for i in range(n_blocks, n_blocks + depth - 1):
        step(i, True)


def _chunk_mask(n):
    row = lax.broadcasted_iota(jnp.int32, (n, n), 0)
    col = lax.broadcasted_iota(jnp.int32, (n, n), 1)
    return jnp.logical_and(row // HGRN_CHUNK == col // HGRN_CHUNK, col <= row)


def _hgrn_kernel(q_ref, f_ref, i_ref, g_ref, lbl_ref, og_ref, o_ref, *scratch, layer):
    for hh in range(q_ref.shape[2] // HGRN_DK):
        ls = pl.ds(hh * HGRN_DK, HGRN_DK)
        _hgrn_head(q_ref.at[:, :, ls], f_ref.at[:, :, ls], i_ref.at[:, :, ls], g_ref.at[:, :, ls],
                   lbl_ref.at[:, ls], og_ref, o_ref.at[:, :, ls], *scratch, layer=layer)


def _hgrn_head(q_ref, f_ref, i_ref, g_ref, lbl_ref, og_ref, o_ref,
               qb_s, u_s, sp_s, dec_s, oi_s, b_s, k_s, qin_s, kin_s, ku_s, a_s, st_s, *, layer):
    seq = q_ref.shape[1]
    cs = HGRN_CHUNK
    rb = min(HG_ROWS, seq)
    nc = rb // cs
    dk = HGRN_DK
    lg = lbl_ref[...]
    ex = jnp.exp(lg - jnp.max(lg, axis=0, keepdims=True))
    sm = ex / jnp.sum(ex, axis=0, keepdims=True)
    lb = jnp.sum(sm[0:layer + 1], axis=0, keepdims=True)
    mask = _chunk_mask(rb)
    tri = mask.astype(BF16)
    row_chunk = lax.broadcasted_iota(jnp.int32, (rb, dk), 0) // cs

    chunk_sel = [(row_chunk == c).astype(BF16) for c in range(nc)]

    def rows(blk):
        return pl.ds(pl.multiple_of(blk * rb, rb), rb)

    def stage1(blk):
        r = rows(blk)
        f = lb + (1.0 - lb) * jax.nn.sigmoid(f_ref[0, r, :].astype(F32))
        lf = jnp.log2(f)
        k_s[r, :] = 1.0 - f
        hi = lf.astype(BF16)
        r1 = lf - hi.astype(F32)
        mid = r1.astype(BF16)
        lo = (r1 - mid.astype(F32)).astype(BF16)
        bhm = _dot(tri, jnp.concatenate([hi, mid], axis=1))
        b_s[r, :] = bhm[:, 0:dk] + bhm[:, dk:] + _dot(tri, lo)

    def stage2(blk):
        r = rows(blk)
        b3 = b_s[r, :].reshape(nc, cs, dk)
        bmid = b3[:, cs // 2 - 1:cs // 2, :]
        blast = b3[:, cs - 1:cs, :]
        q3 = (q_ref[0, r, :].astype(F32) * dk ** -0.5).reshape(nc, cs, dk)
        k3 = k_s[r, :].reshape(nc, cs, dk)
        qin_s[r, :] = (q3 * jnp.exp2(b3 - bmid)).reshape(rb, dk).astype(BF16)
        kin_s[r, :] = (k3 * jnp.exp2(bmid - b3)).reshape(rb, dk).astype(BF16)
        ku_s[r, :] = (k3 * jnp.exp2(blast - b3)).reshape(rb, dk).astype(BF16)
        qb_s[r, :] = (q3 * jnp.exp2(b3)).reshape(rb, dk).astype(BF16)
        dec_s[pl.ds(blk * nc, nc)] = jnp.exp2(blast)

    def stage3(blk):
        r = rows(blk)
        a_s[r, :] = jnp.where(mask, _dot_nt(qin_s[r, :], kin_s[r, :]), 0.0).astype(BF16)
        vt = i_ref[0, r, :].astype(F32).T.astype(BF16)
        ku = ku_s[r, :]
        ut = _dot(vt, jnp.concatenate([ku * sel for sel in chunk_sel], axis=1))
        for c in range(nc):
            u_s[blk * nc + c] = ut[:, c * dk:(c + 1) * dk]

    def stage4(blk):
        r = rows(blk)
        oi_s[r, :] = _dot(a_s[r, :], i_ref[0, r, :])
        st = st_s[...]
        for c in range(nc):
            sp_s[blk * nc + c] = st.astype(BF16)
            st = st * dec_s[blk * nc + c] + u_s[blk * nc + c]
        st_s[...] = st

    def stage5(blk):
        r = rows(blk)
        inter = [_dot_nt(qb_s[pl.ds(pl.multiple_of(blk * rb + c * cs, cs), cs), :], sp_s[blk * nc + c])
                 for c in range(nc)]
        o = oi_s[r, :] + jnp.concatenate(inter, axis=0)
        o = _rms(o, og_ref[...]) * _silu(g_ref[0, r, :].astype(F32))
        o_ref[0, r, :] = o.astype(BF16)

    st_s[...] = jnp.zeros_like(st_s)
    _software_pipeline([stage1, stage2, stage3, stage4, stage5], seq // rb)


def _hgrn(proj3, lb_logits, onorm_g, layer):
    bsz, seq, _ = proj3.shape
    nh = HGRN_HEADS

    hps = HGRN_HEADS_PER_STEP
    width = hps * HGRN_DK

    def col(off):
        return pl.BlockSpec((1, seq, width), lambda b, h: (b, 0, off * (nh // hps) + h))

    nl = lb_logits.shape[0]
    return pl.pallas_call(
        functools.partial(_hgrn_kernel, layer=layer),
        grid=(bsz, nh // hps),
        in_specs=[col(0), col(1), col(2), col(3),
                  pl.BlockSpec((nl, width), lambda b, h: (0, h)),
                  pl.BlockSpec((1, HGRN_DK), lambda b, h: (0, 0))],
        out_specs=pl.BlockSpec((1, seq, width), lambda b, h: (b, 0, h)),
        out_shape=jax.ShapeDtypeStruct((bsz, seq, HGRN_WIDTH), BF16),
        scratch_shapes=[pltpu.VMEM((seq, HGRN_DK), BF16),
                        pltpu.VMEM((seq // HGRN_CHUNK, HGRN_DK, HGRN_DK), F32),
                        pltpu.VMEM((seq // HGRN_CHUNK, HGRN_DK, HGRN_DK), BF16),
                        pltpu.VMEM((seq // HGRN_CHUNK, 1, HGRN_DK), F32),
                        pltpu.VMEM((seq, HGRN_DK), F32),
                        pltpu.VMEM((seq, HGRN_DK), F32), pltpu.VMEM((seq, HGRN_DK), F32),
                        pltpu.VMEM((seq, HGRN_DK), BF16), pltpu.VMEM((seq, HGRN_DK), BF16),
                        pltpu.VMEM((seq, HGRN_DK), BF16), pltpu.VMEM((seq, min(HG_ROWS, seq)), BF16),
                        pltpu.VMEM((HGRN_DK, HGRN_DK), F32)],
        compiler_params=_cparams(("arbitrary", "arbitrary")),
        name="hgrn",
    )(proj3, proj3, proj3, proj3, lb_logits, onorm_g)


UP_ROWS = 1024


def _up_kernel(p_ref, pos_ref, wq_ref, wkv_ref, qag_ref, kvag_ref, qg_ref, kg_ref, freq_ref,
               q_ref, k_ref, v_ref):
    half = MLA_ROPE // 2
    p = p_ref[...].astype(F32)
    qa_n = _rms(p[:, 0:Q_LORA], qag_ref[...]).astype(BF16)
    kva_n = _rms(p[:, Q_LORA:Q_LORA + KV_LORA], kvag_ref[...]).astype(BF16)
    kpe_t = p[:, Q_LORA + KV_LORA:Q_LORA + KV_LORA + LANES].T[0:MLA_ROPE]
    ang = freq_ref[...] * pos_ref[...].astype(F32)
    cos = jnp.cos(ang)
    sin = jnp.sin(ang)

    def rope(x):
        x1, x2 = x[0:half], x[half:]
        return jnp.concatenate([x1 * cos - x2 * sin, x2 * cos + x1 * sin], axis=0)

    kg = kg_ref[...]
    kpe_ss = jnp.sum(kpe_t * kpe_t, axis=0, keepdims=True)
    kpe_rot = rope(kpe_t * kg[MLA_NOPE:])
    qg = qg_ref[...] * (MLA_QK ** -0.5 * LOG2E)
    zpad = jnp.zeros((MLA_QPAD - MLA_QK, p.shape[0]), F32)
    for h in range(MLA_HEADS):
        rows = slice(h * MLA_QPAD, (h + 1) * MLA_QPAD)
        qt = _dot_nt(wq_ref[rows, :], qa_n)
        qn = qt[0:MLA_QK] * lax.rsqrt(jnp.sum(qt * qt, axis=0, keepdims=True) / MLA_QK + EPS) * qg
        q_ref[0, h] = jnp.concatenate([qn[0:MLA_NOPE], rope(qn[MLA_NOPE:]), zpad], axis=0).astype(BF16)
        kvt = _dot_nt(wkv_ref[rows, :], kva_n)
        kn = kvt[0:MLA_NOPE]
        rk = lax.rsqrt((jnp.sum(kn * kn, axis=0, keepdims=True) + kpe_ss) / MLA_QK + EPS)
        kt = jnp.concatenate([kn * rk * kg[0:MLA_NOPE], kpe_rot * rk, zpad], axis=0)
        k_ref[0, h] = kt.T.astype(BF16)
        v_ref[0, h] = kvt[MLA_NOPE:].astype(BF16)


def _mla_up(proj3, pos3, wq_t, wkv_t, qag, kvag, qg, kg, freq):
    bsz, seq, _ = proj3.shape
    tm = min(UP_ROWS, seq)
    nh = MLA_HEADS
    mla_block = 4 * HGRN_WIDTH // 1024

    def const(shape):
        return pl.BlockSpec(shape, lambda b, i: (0,) * len(shape))

    return pl.pallas_call(
        _up_kernel,
        grid=(bsz, seq // tm),
        in_specs=[pl.BlockSpec((None, tm, 1024), lambda b, i: (b, i, mla_block)),
                  pl.BlockSpec((None, 1, tm), lambda b, i: (b, 0, i)),
                  const(wq_t.shape), const(wkv_t.shape), const(qag.shape), const(kvag.shape),
                  const(qg.shape), const(kg.shape), const(freq.shape)],
        out_specs=[pl.BlockSpec((1, nh, MLA_QPAD, tm), lambda b, i: (b, 0, 0, i)),
                   pl.BlockSpec((1, nh, tm, MLA_QPAD), lambda b, i: (b, 0, i, 0)),
                   pl.BlockSpec((1, nh, MLA_V, tm), lambda b, i: (b, 0, 0, i))],
        out_shape=[jax.ShapeDtypeStruct((bsz, nh, MLA_QPAD, seq), BF16),
                   jax.ShapeDtypeStruct((bsz, nh, seq, MLA_QPAD), BF16),
                   jax.ShapeDtypeStruct((bsz, nh, MLA_V, seq), BF16)],
        compiler_params=_cparams(("arbitrary", "arbitrary")),
        name="mla_up",
    )(proj3, pos3, wq_t, wkv_t, qag, kvag, qg, kg, freq)


ATT_T = 256
ATT_HEADS_PER_STEP = 4


def _attn_kernel(qt_ref, k_ref, vt_ref, g_ref, o_ref):
    for hh in range(k_ref.shape[1]):
        _attn_head(qt_ref.at[:, pl.ds(hh, 1)], k_ref.at[:, pl.ds(hh, 1)], vt_ref.at[:, pl.ds(hh, 1)], g_ref,
                   o_ref.at[:, :, pl.ds(hh * MLA_V, MLA_V)])


def _attn_head(qt_ref, k_ref, vt_ref, g_ref, o_ref):
    seq = k_ref.shape[2]
    t = min(ATT_T, seq)
    key = lax.broadcasted_iota(jnp.int32, (t, t), 0)
    qry = lax.broadcasted_iota(jnp.int32, (t, t), 1)
    causal = key <= qry
    neg = jnp.finfo(F32).min

    def widen(x, off, fill):
        return x if off == 0 else jnp.concatenate([jnp.full((x.shape[0], off), fill, x.dtype), x], axis=1)

    scores = []
    m = None
    for off in range(0, seq, t):
        s = _dot(k_ref[0, 0, off:off + t, :], qt_ref[0, 0, :, off:])
        diag = jnp.where(causal, s[:, 0:t], neg)
        s = diag if off + t == seq else jnp.concatenate([diag, s[:, t:]], axis=1)
        scores.append(s)
        blk_max = widen(jnp.max(s, axis=0, keepdims=True), off, neg)
        m = blk_max if m is None else jnp.maximum(m, blk_max)
    l = None
    ot = None
    for off, s in zip(range(0, seq, t), scores):
        p = jnp.exp2(s - m[:, off:])
        p_sum = widen(jnp.sum(p, axis=0, keepdims=True), off, 0.0)
        pv = widen(_dot(vt_ref[0, 0, :, off:off + t], p.astype(BF16)), off, 0.0)
        l = p_sum if l is None else l + p_sum
        ot = pv if ot is None else ot + pv
    ot = ot * (1.0 / l)
    ot = ot * lax.rsqrt(jnp.mean(ot * ot, axis=0, keepdims=True) + EPS) * g_ref[...]
    o_ref[0, :, :] = ot.T.astype(BF16)


def _attention(q, k, v, g):
    bsz, nh, seq, _ = k.shape
    hps = ATT_HEADS_PER_STEP
    return pl.pallas_call(
        _attn_kernel,
        grid=(bsz, nh // hps),
        in_specs=[pl.BlockSpec((1, hps, MLA_QPAD, seq), lambda b, h: (b, h, 0, 0)),
                  pl.BlockSpec((1, hps, seq, MLA_QPAD), lambda b, h: (b, h, 0, 0)),
                  pl.BlockSpec((1, hps, MLA_V, seq), lambda b, h: (b, h, 0, 0)),
                  pl.BlockSpec((MLA_V, 1), lambda b, h: (0, 0))],
        out_specs=pl.BlockSpec((1, seq, hps * MLA_V), lambda b, h: (b, 0, h)),
        out_shape=jax.ShapeDtypeStruct((bsz, seq, nh * MLA_V), BF16),
        compiler_params=_cparams(("arbitrary", "arbitrary")),
        name="attn",
    )(q, k, v, g)


def _out_kernel(oa_ref, ob_ref, x_ref, wa_ref, wb_ref, g1_ref, n2g_ref, sh2_ref, sc2_ref,
                wr_ref, br_ref, x1_ref, h2_ref, lg_ref, mix_s):
    @pl.when(pl.program_id(0) == 0)
    def _():
        mix_s[...] = jnp.zeros_like(mix_s)

    mix_next = _dot(oa_ref[...], wa_ref[...]) + _dot(ob_ref[...], wb_ref[...])
    x1 = x_ref[...] + g1_ref[0] * mix_s[...]
    x1_ref[...] = x1
    h2 = _rms(x1, n2g_ref[...]) * (1.0 + sc2_ref[0]) + sh2_ref[0]
    tm = h2.shape[0]
    for s in range(TOK_SUBLANES):
        h2_ref[pl.ds(s, tm, stride=TOK_SUBLANES), :] = _pack_pair(
            h2[:, s * LANES:(s + 1) * LANES], h2[:, (s + TOK_SUBLANES) * LANES:(s + TOK_SUBLANES + 1) * LANES])
    lg = _dot(h2.astype(BF16), wr_ref[...]) + br_ref[...]
    lg_ref[...] = lg.T[0:ROUTE_ROWS, :]
    mix_s[...] = mix_next


def _out_proj(oa, ob, x2, wa, wb, g1, n2g, sh2, sc2, wr, br, seq):
    t, d = x2.shape
    tm = min(512, seq)
    n_tiles = t // tm

    def const(shape):
        return pl.BlockSpec(shape, lambda i: (0,) * len(shape))

    def prev(i):
        return jnp.maximum(i - 1, 0)

    def per_batch():
        return pl.BlockSpec((1, 1, d), lambda i: (prev(i) * tm // seq, 0, 0))

    return pl.pallas_call(
        _out_kernel,
        grid=(n_tiles + 1,),
        in_specs=[pl.BlockSpec((tm, HGRN_WIDTH), lambda i: (jnp.minimum(i, n_tiles - 1), 0)),
                  pl.BlockSpec((tm, HGRN_WIDTH), lambda i: (jnp.minimum(i, n_tiles - 1), 0)),
                  pl.BlockSpec((tm, d), lambda i: (prev(i), 0)),
                  const(wa.shape), const(wb.shape), per_batch(), const(n2g.shape),
                  per_batch(), per_batch(), const(wr.shape), const(br.shape)],
        out_specs=[pl.BlockSpec((tm, d), lambda i: (prev(i), 0)),
                   pl.BlockSpec((tm * TOK_SUBLANES, LANES), lambda i: (prev(i), 0)),
                   pl.BlockSpec((ROUTE_ROWS, tm), lambda i: (0, prev(i)))],
        out_shape=[jax.ShapeDtypeStruct((t, d), F32),
                   jax.ShapeDtypeStruct((t * TOK_SUBLANES, LANES), jnp.uint32),
                   jax.ShapeDtypeStruct((ROUTE_ROWS, t), F32)],
        scratch_shapes=[pltpu.VMEM((tm, d), F32)],
        compiler_params=_cparams(("arbitrary",)),
        name="out_proj",
    )(oa, ob, x2, wa, wb, g1, n2g, sh2, sc2, wr, br)


def _route_kernel(lg_ref, tri_ref, ri_ref, rw_ref, cnt_ref, carry_s):
    step = pl.program_id(0)

    @pl.when(step == 0)
    def _():
        carry_s[...] = jnp.zeros_like(carry_s)

    lg = lg_ref[...]
    tr = lg.shape[1]
    epg = EXPERTS_PER_GROUP
    gl = lg[N_EXPERTS:N_EXPERTS + N_GROUPS, :]
    row_g = lax.broadcasted_iota(jnp.int32, (N_GROUPS, tr), 0)
    gmax = jnp.max(gl, axis=0, keepdims=True)
    g_sel = jnp.min(jnp.where(gl == gmax, row_g, N_GROUPS), axis=0, keepdims=True)
    p_group = 1.0 / jnp.sum(jnp.exp(gl - gmax), axis=0, keepdims=True)

    e_in = lg[0:epg, :]
    for g in range(1, N_GROUPS):
        e_in = jnp.where(g_sel == g, lg[g * epg:(g + 1) * epg, :], e_in)
    row_e = lax.broadcasted_iota(jnp.int32, (epg, tr), 0)
    top1 = jnp.max(e_in, axis=0, keepdims=True)
    i1 = jnp.min(jnp.where(e_in == top1, row_e, epg), axis=0, keepdims=True)
    rest = jnp.where(row_e == i1, -jnp.inf, e_in)
    top2 = jnp.max(rest, axis=0, keepdims=True)
    i2 = jnp.min(jnp.where(rest == top2, row_e, epg), axis=0, keepdims=True)
    e2w = jnp.exp(top2 - top1)
    w1 = p_group / (1.0 + e2w)
    w2 = p_group * e2w / (1.0 + e2w)
    ex1 = g_sel * epg + i1
    ex2 = g_sel * epg + i2

    row_x = lax.broadcasted_iota(jnp.int32, (N_EXPERTS, tr), 0)
    oh1 = row_x == ex1
    oh2 = row_x == ex2
    oh = jnp.logical_or(oh1, oh2)
    before = _dot(oh.astype(BF16), tri_ref[...]) + carry_s[:, 0:1]
    rank1 = jnp.sum(jnp.where(oh1, before, 0.0), axis=0, keepdims=True)
    rank2 = jnp.sum(jnp.where(oh2, before, 0.0), axis=0, keepdims=True)
    carry_s[...] = carry_s[...] + jnp.sum(oh.astype(F32), axis=1, keepdims=True)

    zi = jnp.zeros((4, tr), jnp.int32)
    ri_ref[...] = jnp.concatenate([ex1, ex2, rank1.astype(jnp.int32), rank2.astype(jnp.int32), zi], axis=0)
    rw_ref[...] = jnp.concatenate([w1, w2, jnp.zeros((6, tr), F32)], axis=0)
    cnt_ref[...] = carry_s[...].astype(jnp.int32)


def _route(lg_t, tri):
    t = lg_t.shape[1]
    tr = tri.shape[0]
    return pl.pallas_call(
        _route_kernel,
        grid=(t // tr,),
        in_specs=[pl.BlockSpec((ROUTE_ROWS, tr), lambda i: (0, i)),
                  pl.BlockSpec((tr, tr), lambda i: (0, 0))],
        out_specs=[pl.BlockSpec((8, tr), lambda i: (0, i)),
                   pl.BlockSpec((8, tr), lambda i: (0, i)),
                   pl.BlockSpec((N_EXPERTS, LANES), lambda i: (0, 0))],
        out_shape=[jax.ShapeDtypeStruct((8, t), jnp.int32),
                   jax.ShapeDtypeStruct((8, t), F32),
                   jax.ShapeDtypeStruct((N_EXPERTS, LANES), jnp.int32)],
        scratch_shapes=[pltpu.VMEM((N_EXPERTS, LANES), F32)],
        compiler_params=_cparams(("arbitrary",)),
        name="route",
    )(lg_t, tri)


def _moe_kernel(rd_ref, be_ref, pe_ref, nb_ref, x_ref, wg_ref, wu_ref, wd_ref, y_ref,
                ys, wg_f, wu_f, wd_f, wg_s, wu_s, wd_s, ssem, wsem):
    b = pl.program_id(0)
    last = pl.num_programs(0) - 1
    nb = nb_ref[0]
    tm = MOE_BLOCK
    ts = TOK_SUBLANES
    weights = ((wg_ref, wg_f, wg_s), (wu_ref, wu_f, wu_s), (wd_ref, wd_f, wd_s))

    def fetch(e):
        return [pltpu.make_async_copy(src.at[e], stage, wsem.at[k]) for k, (src, stage, _) in enumerate(weights)]

    def scatter(blk):
        base = blk * tm
        for j in range(tm):
            dst = pl.multiple_of(rd_ref[base + j] * ts, ts)
            pltpu.make_async_copy(ys.at[pl.ds(j * ts, ts), :], y_ref.at[pl.ds(dst, ts), :], ssem).start()

    def wait_scatter():
        pltpu.make_async_copy(ys, y_ref.at[pl.ds(0, tm * ts), :], ssem).wait()

    @pl.when(b == 0)
    def _():
        for copy in fetch(be_ref[0]):
            copy.start()
        ys[...] = jnp.zeros_like(ys)
        spare = pltpu.make_async_copy(ys, y_ref.at[pl.ds(y_ref.shape[0] - tm * ts, tm * ts), :], ssem)
        spare.start()
        spare.wait()

    expert = be_ref[b]
    first_of_expert = jnp.logical_or(b == 0, expert != be_ref[jnp.maximum(b - 1, 0)])

    @pl.when(jnp.logical_and(b < nb, first_of_expert))
    def _():
        for copy in fetch(0):
            copy.wait()
        for _, stage, dst in weights:
            dst[...] = stage[...].astype(BF16)
        next_blk = pe_ref[expert] // tm

        @pl.when(next_blk < nb)
        def _():
            for copy in fetch(be_ref[next_blk]):
                copy.start()

    @pl.when(b < nb)
    def _():
        scatter(jnp.maximum(b - 1, 0))
        parts = [_unpack_pair(x_ref[pl.ds(s, tm, stride=ts), :]) for s in range(ts)]
        x = jnp.concatenate([p[0] for p in parts] + [p[1] for p in parts], axis=1).astype(BF16)
        hid = _silu(_dot(x, wg_s[...])) * _dot(x, wu_s[...])
        y = _dot(hid.astype(BF16), wd_s[...])
        wait_scatter()
        for s in range(ts):
            ys[pl.ds(s, tm, stride=ts), :] = _pack_pair(y[:, s * LANES:(s + 1) * LANES],
                                                        y[:, (s + ts) * LANES:(s + ts + 1) * LANES])

        @pl.when(b == last)
        def _():
            scatter(b)
            wait_scatter()

    @pl.when(b == nb)
    def _():
        scatter(b - 1)
        wait_scatter()


def _moe(row_dst, blk_e, pad_end, nb_real, x_sorted, wg, wu, wd, n_out_rows):
    n_blocks = blk_e.shape[0]
    d, de = wg.shape[1], wg.shape[2]
    tm = MOE_BLOCK
    hbm = pl.BlockSpec(memory_space=pl.ANY)
    return pl.pallas_call(
        _moe_kernel,
        grid_spec=pltpu.PrefetchScalarGridSpec(
            num_scalar_prefetch=4,
            grid=(n_blocks,),
            in_specs=[pl.BlockSpec((tm * TOK_SUBLANES, LANES),
                                   lambda b, rd, be, pe, nb: (jnp.minimum(b, nb[0] - 1), 0)),
                      hbm, hbm, hbm],
            out_specs=hbm,
            scratch_shapes=[pltpu.VMEM((tm * TOK_SUBLANES, LANES), jnp.uint32),
                            pltpu.VMEM((d, de), F32), pltpu.VMEM((d, de), F32), pltpu.VMEM((de, d), F32),
                            pltpu.VMEM((d, de), BF16), pltpu.VMEM((d, de), BF16), pltpu.VMEM((de, d), BF16),
                            pltpu.SemaphoreType.DMA(()), pltpu.SemaphoreType.DMA((3,))]),
        out_shape=jax.ShapeDtypeStruct((n_out_rows * TOK_SUBLANES, LANES), jnp.uint32),
        compiler_params=_cparams(("arbitrary",)),
        name="moe",
    )(row_dst, blk_e, pad_end, nb_real, x_sorted, wg, wu, wd)


def _dispatch_kernel(e1_ref, e2_ref, r1_ref, r2_ref, ps_ref, pe_ref, cnt_ref, be_ref, nb_ref,
                     h_ref, x_ref, rd_ref, zbuf, sem, zsem):
    i = pl.program_id(0)
    ts = TOK_SUBLANES
    td = h_ref.shape[0] // ts
    n_tok = e1_ref.shape[0]
    base = i * td
    blk_rows = MOE_BLOCK * ts

    def zero_fill(blk):
        return pltpu.make_async_copy(
            zbuf, x_ref.at[pl.ds(pl.multiple_of(blk * blk_rows, blk_rows), blk_rows), :], zsem)

    @pl.when(i == 0)
    def _():
        zbuf[...] = jnp.zeros_like(zbuf)

        def fill(blk, n_fills):
            e = be_ref[blk]
            partial_last = jnp.logical_and(blk == pe_ref[e] // MOE_BLOCK - 1, cnt_ref[e] % MOE_BLOCK != 0)
            needs_fill = jnp.logical_or(blk >= nb_ref[0], partial_last)

            @pl.when(needs_fill)
            def _():
                zero_fill(blk).start()
            return n_fills + needs_fill.astype(jnp.int32)

        n_fills = lax.fori_loop(0, be_ref.shape[0], fill, 0)

        def init(blk, carry):
            for j in range(MOE_BLOCK):
                rd_ref[blk * MOE_BLOCK + j] = 2 * n_tok + j
            return carry

        lax.fori_loop(0, rd_ref.shape[0] // MOE_BLOCK, init, 0)

        def drain(k, carry):
            zero_fill(0).wait()
            return carry

        lax.fori_loop(0, n_fills, drain, 0)

    def start(t, carry):
        src = h_ref.at[pl.ds(pl.multiple_of(t * ts, ts), ts), :]
        tok = base + t
        for slot, (e_ref, r_ref) in enumerate(((e1_ref, r1_ref), (e2_ref, r2_ref))):
            row = ps_ref[e_ref[tok]] + r_ref[tok]
            pltpu.make_async_copy(src, x_ref.at[pl.ds(pl.multiple_of(row * ts, ts), ts), :], sem).start()
            rd_ref[row] = 2 * tok + slot
        return carry

    lax.fori_loop(0, td, start, 0, unroll=8)
    for _ in range(2):
        pltpu.make_async_copy(h_ref, x_ref.at[pl.ds(0, td * ts), :], sem).wait()


def _dispatch(ri, pad_start, pad_end, counts, blk_e, nb_real, h2p, n_rows):
    t = ri.shape[1]
    td = min(512, t)
    ts = TOK_SUBLANES
    return pl.pallas_call(
        _dispatch_kernel,
        grid_spec=pltpu.PrefetchScalarGridSpec(
            num_scalar_prefetch=9,
            grid=(t // td,),
            in_specs=[pl.BlockSpec((td * ts, LANES), lambda i, *_: (i, 0))],
            out_specs=[pl.BlockSpec(memory_space=pl.ANY), pl.BlockSpec(memory_space=pltpu.SMEM)],
            scratch_shapes=[pltpu.VMEM((MOE_BLOCK * ts, LANES), jnp.uint32),
                            pltpu.SemaphoreType.DMA(()), pltpu.SemaphoreType.DMA(())]),
        out_shape=[jax.ShapeDtypeStruct((n_rows * ts, LANES), jnp.uint32),
                   jax.ShapeDtypeStruct((n_rows,), jnp.int32)],
        compiler_params=_cparams(("arbitrary",)),
        name="dispatch",
    )(ri[0], ri[1], ri[2], ri[3], pad_start, pad_end, counts, blk_e, nb_real, h2p)


def _moe_plan(counts, t):
    tm = MOE_BLOCK
    n_blocks = 2 * t // tm + N_EXPERTS
    padded = ((counts + tm - 1) // tm) * tm
    pad_end = jnp.cumsum(padded)
    pad_start = pad_end - padded
    blk = jnp.arange(n_blocks, dtype=jnp.int32)
    blk_e = jnp.minimum(jnp.sum(pad_end[None, :] <= (blk * tm)[:, None], axis=1), N_EXPERTS - 1).astype(jnp.int32)
    nb_real = (pad_end[-1:] // tm).astype(jnp.int32)
    return pad_start, pad_end, blk_e, nb_real, n_blocks * tm


COMBINE_ROWS = 64


def _combine_kernel(x1_ref, g2_ref, rw_ref, y_ref, o_ref, w_s):
    ts = TOK_SUBLANES
    rows = min(COMBINE_ROWS, x1_ref.shape[0])
    w_s[...] = rw_ref[...].T

    def chunk(c, carry):
        r = pl.ds(pl.multiple_of(c * rows, rows), rows)
        w1 = w_s[r, 0:1]
        w2 = w_s[r, 1:2]
        tile0 = pl.multiple_of(c * rows * 2 * ts, rows * 2 * ts)
        for s in range(ts):
            a_lo, a_hi = _unpack_pair(y_ref[pl.ds(tile0 + s, rows, stride=2 * ts), :])
            b_lo, b_hi = _unpack_pair(y_ref[pl.ds(tile0 + ts + s, rows, stride=2 * ts), :])
            for col, ya, yb in ((s, a_lo, b_lo), (s + ts, a_hi, b_hi)):
                cols = slice(col * LANES, (col + 1) * LANES)
                o_ref[r, cols] = x1_ref[r, cols] + g2_ref[0][:, cols] * (w1 * ya + w2 * yb)
        return carry

    lax.fori_loop(0, x1_ref.shape[0] // rows, chunk, 0)


def _combine(x1, g2, rw, y2, seq):
    t, d = x1.shape
    tc = min(512, seq)
    return pl.pallas_call(
        _combine_kernel,
        grid=(t // tc,),
        in_specs=[pl.BlockSpec((tc, d), lambda i: (i, 0)),
                  pl.BlockSpec((1, 1, d), lambda i: (i * tc // seq, 0, 0)),
                  pl.BlockSpec((rw.shape[0], tc), lambda i: (0, i)),
                  pl.BlockSpec((tc * 2 * TOK_SUBLANES, LANES), lambda i: (i, 0))],
        out_specs=pl.BlockSpec((tc, d), lambda i: (i, 0)),
        out_shape=jax.ShapeDtypeStruct((t, d), F32),
        scratch_shapes=[pltpu.VMEM((tc, rw.shape[0]), F32)],
        compiler_params=_cparams(("arbitrary",)),
        name="combine",
    )(x1, g2, rw, y2)


def _q_up_layout(w_q_up):
    w = w_q_up.reshape(Q_LORA, MLA_HEADS, MLA_QK)
    w = jnp.pad(w, ((0, 0), (0, 0), (0, MLA_QPAD - MLA_QK)))
    return w.reshape(Q_LORA, MLA_HEADS * MLA_QPAD).T.astype(BF16)


def _pad_lanes(g, width):
    return jnp.pad(g, (0, width - g.shape[0])).reshape(1, width)


def kernel(x, c, positions, w_ada, b_ada, norm1_g, w_in, hgrn_lb_logits, hgrn_onorm_g, q_a_norm_g, w_q_up,
           kv_a_norm_g, w_kv_up, q_norm_g, k_norm_g, attn_onorm_g, w_out, norm2_g, w_group, b_group,
           w_router, b_router, w_gate, w_up, w_down):
    bsz, seq, d = x.shape
    t = bsz * seq
    depth = w_ada.shape[0]
    half = MLA_ROPE // 2
    inv_freq = ROPE_BASE ** (-jnp.arange(0, MLA_ROPE, 2, dtype=F32) / MLA_ROPE)
    freq = inv_freq.reshape(half, 1)
    pos3 = positions.reshape(bsz, 1, seq)
    tr = min(1024, t)
    tri = jnp.triu(jnp.ones((tr, tr), BF16), 1)

    x2 = x.reshape(t, d)
    for l in range(depth):
        mod = _ada(c, w_ada[l], b_ada[l]).reshape(bsz, 6, 1, d)
        sh1, sc1, g1, sh2, sc2, g2 = (mod[:, i] for i in range(6))

        proj = _in_proj(x2, norm1_g[l].reshape(1, d), sh1, sc1, w_in[l].T, seq)
        proj3 = proj.reshape(bsz, seq, IN_COLS_PAD)

        o_a = _hgrn(proj3, hgrn_lb_logits, hgrn_onorm_g[l].reshape(1, HGRN_DK), l)

        q, k, v = _mla_up(proj3, pos3, _q_up_layout(w_q_up[l]), w_kv_up[l].T.astype(BF16),
                          q_a_norm_g[l].reshape(1, Q_LORA), kv_a_norm_g[l].reshape(1, KV_LORA),
                          q_norm_g[l].reshape(MLA_QK, 1), k_norm_g[l].reshape(MLA_QK, 1), freq)
        o_b = _attention(q, k, v, attn_onorm_g[l].reshape(MLA_V, 1))

        w_o = w_out[l].astype(BF16)
        wr = jnp.pad(jnp.concatenate([w_router[l], w_group[l]], axis=1),
                     ((0, 0), (0, LANES - N_EXPERTS - N_GROUPS))).astype(BF16)
        br = _pad_lanes(jnp.concatenate([b_router[l], b_group[l]]), LANES)
        x1, h2, lg_t = _out_proj(o_a.reshape(t, HGRN_WIDTH), o_b.reshape(t, HGRN_WIDTH), x2,
                                 w_o[:HGRN_WIDTH], w_o[HGRN_WIDTH:], g1, norm2_g[l].reshape(1, d),
                                 sh2, sc2, wr, br, seq)

        ri, rw, cnt = _route(lg_t, tri)
        counts = cnt[:, 0]
        pad_start, pad_end, blk_e, nb_real, n_rows = _moe_plan(counts, t)
        x_sorted, row_dst = _dispatch(ri, pad_start, pad_end, counts, blk_e, nb_real, h2, n_rows)
        y2 = _moe(row_dst, blk_e, pad_end, nb_real, x_sorted, w_gate[l], w_up[l], w_down[l], 2 * t + MOE_BLOCK)
        x2 = _combine(x1, g2, rw, y2, seq)
    return x2.reshape(bsz, seq, d)
```

```python
import functools

import jax
import jax.numpy as jnp
from jax import lax
from jax.experimental import pallas as pl
from jax.experimental.pallas import tpu as pltpu

F32 = jnp.float32
BF16 = jnp.bfloat16
EPS = 1e-6
LOG2E = 1.4426950408889634

D_MODEL = 2048
HGRN_WIDTH = 1024
HGRN_DK = 128
HGRN_HEADS = 8
HGRN_CHUNK = 64
MLA_HEADS = 8
MLA_NOPE = 128
MLA_ROPE = 64
MLA_QK = MLA_NOPE + MLA_ROPE
MLA_V = 128
MLA_QPAD = 256
Q_LORA = 512
KV_LORA = 256
ROPE_BASE = 10000.0
IN_COLS = 4 * HGRN_WIDTH + Q_LORA + KV_LORA + MLA_ROPE
IN_COLS_PAD = 5120
N_GROUPS = 4
EXPERTS_PER_GROUP = 8
N_EXPERTS = 32
D_EXPERT = 512
ROUTE_ROWS = 40
MOE_BLOCK = 256
LANES = 128
TOK_SUBLANES = 8
VMEM_LIMIT = 56 * 1024 * 1024


def _cparams(sem):
    return pltpu.CompilerParams(dimension_semantics=sem, vmem_limit_bytes=VMEM_LIMIT)


def _dot(a, b):
    return jnp.dot(a, b, preferred_element_type=F32)


def _dot_nt(a, b):
    return lax.dot_general(a, b, (((1,), (1,)), ((), ())), preferred_element_type=F32)


def _rms(x, g):
    return x * lax.rsqrt(jnp.mean(x * x, axis=-1, keepdims=True) + EPS) * g


def _silu(x):
    return x * jax.nn.sigmoid(x)


def _pack_pair(lo, hi):
    lo_b = lax.bitcast_convert_type(lo.astype(BF16).astype(F32), jnp.uint32)
    hi_b = lax.bitcast_convert_type(hi.astype(BF16).astype(F32), jnp.uint32)
    return hi_b | (lo_b >> 16)


def _unpack_pair(w):
    lo = lax.bitcast_convert_type(w << 16, F32)
    hi = lax.bitcast_convert_type(w & jnp.uint32(0xFFFF0000), F32)
    return lo, hi


def _ada_kernel(c_ref, w_ref, b_ref, o_ref):
    ca = _silu(c_ref[...]).astype(BF16)
    o_ref[...] = _dot(ca, w_ref[...].astype(BF16)) + b_ref[...]


def _ada(c, w, b):
    bsz, d = c.shape
    n = w.shape[1]
    tn = 1024
    return pl.pallas_call(
        _ada_kernel,
        grid=(n // tn,),
        in_specs=[pl.BlockSpec((bsz, d), lambda j: (0, 0)),
                  pl.BlockSpec((d, tn), lambda j: (0, j)),
                  pl.BlockSpec((1, tn), lambda j: (0, j))],
        out_specs=pl.BlockSpec((bsz, tn), lambda j: (0, j)),
        out_shape=jax.ShapeDtypeStruct((bsz, n), F32),
        compiler_params=_cparams(("arbitrary",)),
        name="ada",
    )(c, w, b.reshape(1, n))


NORM_ROWS = 32


def _norm_kernel(x_ref, g_ref, sh_ref, sc_ref, h_ref):
    def body(c, carry):
        r = pl.ds(pl.multiple_of(c * NORM_ROWS, NORM_ROWS), NORM_ROWS)
        h = _rms(x_ref[r, :], g_ref[...]) * (1.0 + sc_ref[0]) + sh_ref[0]
        h_ref[r, :] = h.astype(BF16)
        return carry
    lax.fori_loop(0, x_ref.shape[0] // NORM_ROWS, body, 0, unroll=2)


def _norm_mod(x2, g, sh, sc, seq, n_rows):
    t, d = x2.shape
    tm = min(1024, seq, n_rows)
    return pl.pallas_call(
        _norm_kernel,
        grid=(n_rows // tm,),
        in_specs=[pl.BlockSpec((tm, d), lambda i: (i, 0)),
                  pl.BlockSpec((1, d), lambda i: (0, 0)),
                  pl.BlockSpec((1, 1, d), lambda i: (i * tm // seq, 0, 0)),
                  pl.BlockSpec((1, 1, d), lambda i: (i * tm // seq, 0, 0))],
        out_specs=pl.BlockSpec((tm, d), lambda i: (i, 0)),
        out_shape=jax.ShapeDtypeStruct((n_rows, d), BF16),
        compiler_params=_cparams(("arbitrary",)),
        name="norm_mod",
    )(x2, g, sh, sc)


IN_NORM_CHUNKS = 4


def _in_kernel(h0_ref, x_ref, g_ref, sh_ref, sc_ref, w_ref, o_ref, h_s):
    i = pl.program_id(0)
    j = pl.program_id(1)
    slot = i % 2

    @pl.when(jnp.logical_and(i == 0, j == 0))
    def _():
        h_s[0] = h0_ref[...]

    o_ref[...] = _dot_nt(h_s[slot], w_ref[...]).astype(BF16)
    rows = x_ref.shape[0] // IN_NORM_CHUNKS
    first = jnp.minimum(j, IN_NORM_CHUNKS - 1) * rows
    gain = g_ref[...]
    scale = 1.0 + sc_ref[0]
    shift = sh_ref[0]
    for c in range(rows // NORM_ROWS):
        r = pl.ds(pl.multiple_of(first + c * NORM_ROWS, NORM_ROWS), NORM_ROWS)
        h_s[1 - slot, r, :] = (_rms(x_ref[r, :], gain) * scale + shift).astype(BF16)


def _in_proj(x2, g, sh, sc, w_t, seq):
    t, d = x2.shape
    tn = 1024
    tm = min(1024, seq)
    n_tiles = t // tm
    w_b = jnp.pad(w_t, ((0, IN_COLS_PAD - IN_COLS), (0, 0))).astype(BF16)
    h0 = _norm_mod(x2, g, sh, sc, seq, tm)

    def nxt(i):
        return jnp.minimum(i + 1, n_tiles - 1)

    return pl.pallas_call(
        _in_kernel,
        grid=(n_tiles, IN_COLS_PAD // tn),
        in_specs=[pl.BlockSpec((tm, d), lambda i, j: (0, 0)),
                  pl.BlockSpec((tm, d), lambda i, j: (nxt(i), 0)),
                  pl.BlockSpec((1, d), lambda i, j: (0, 0)),
                  pl.BlockSpec((1, 1, d), lambda i, j: (nxt(i) * tm // seq, 0, 0)),
                  pl.BlockSpec((1, 1, d), lambda i, j: (nxt(i) * tm // seq, 0, 0)),
                  pl.BlockSpec((tn, d), lambda i, j: (j, 0))],
        out_specs=pl.BlockSpec((tm, tn), lambda i, j: (i, j)),
        out_shape=jax.ShapeDtypeStruct((t, IN_COLS_PAD), BF16),
        scratch_shapes=[pltpu.VMEM((2, tm, d), BF16)],
        compiler_params=_cparams(("arbitrary", "arbitrary")),
        name="in_proj",
    )(h0, x2, g, sh, sc, w_b)


HG_ROWS = 256
HGRN_HEADS_PER_STEP = 2


PIPELINE_STATIC_BLOCKS = 8


def _software_pipeline(stages, n_blocks):
    depth = len(stages)

    def step(i, static):
        for k in reversed(range(depth)):
            if static and not 0 <= i - k < n_blocks:
                continue
            stages[k](i - k)

    if n_blocks <= PIPELINE_STATIC_BLOCKS:
        for i in range(n_blocks + depth - 1):
            step(i, True)
        return
    for i in range(depth - 1):
        step(i, True)

    def steady(i, carry):
        step(i, False)
        return carry

    lax.fori_loop(depth - 1, n_blocks, steady, 0)
    for i in range(n_blocks, n_blocks + depth - 1):
        step(i, True)


def _chunk_mask(n):
    row = lax.broadcasted_iota(jnp.int32, (n, n), 0)
    col = lax.broadcasted_iota(jnp.int32, (n, n), 1)
    return jnp.logical_and(row // HGRN_CHUNK == col // HGRN_CHUNK, col <= row)


def _hgrn_kernel(q_ref, f_ref, i_ref, g_ref, lbl_ref, og_ref, o_ref, *scratch, layer):
    for hh in range(q_ref.shape[2] // HGRN_DK):
        ls = pl.ds(hh * HGRN_DK, HGRN_DK)
        _hgrn_head(q_ref.at[:, :, ls], f_ref.at[:, :, ls], i_ref.at[:, :, ls], g_ref.at[:, :, ls],
                   lbl_ref.at[:, ls], og_ref, o_ref.at[:, :, ls], *scratch, layer=layer)


def _hgrn_head(q_ref, f_ref, i_ref, g_ref, lbl_ref, og_ref, o_ref,
               qb_s, u_s, sp_s, dec_s, oi_s, b_s, k_s, qin_s, kin_s, ku_s, a_s, st_s, *, layer):
    seq = q_ref.shape[1]
    cs = HGRN_CHUNK
    rb = min(HG_ROWS, seq)
    nc = rb // cs
    dk = HGRN_DK
    lg = lbl_ref[...]
    ex = jnp.exp(lg - jnp.max(lg, axis=0, keepdims=True))
    sm = ex / jnp.sum(ex, axis=0, keepdims=True)
    lb = jnp.sum(sm[0:layer + 1], axis=0, keepdims=True)
    mask = _chunk_mask(rb)
    tri = mask.astype(BF16)
    row_chunk = lax.broadcasted_iota(jnp.int32, (rb, dk), 0) // cs

    chunk_sel = [(row_chunk == c).astype(BF16) for c in range(nc)]

    def rows(blk):
        return pl.ds(pl.multiple_of(blk * rb, rb), rb)

    def stage1(blk):
        r = rows(blk)
        f = lb + (1.0 - lb) * jax.nn.sigmoid(f_ref[0, r, :].astype(F32))
        lf = jnp.log2(f)
        k_s[r, :] = 1.0 - f
        hi = lf.astype(BF16)
        r1 = lf - hi.astype(F32)
        mid = r1.astype(BF16)
        lo = (r1 - mid.astype(F32)).astype(BF16)
        bhm = _dot(tri, jnp.concatenate([hi, mid], axis=1))
        b_s[r, :] = bhm[:, 0:dk] + bhm[:, dk:] + _dot(tri, lo)

    def stage2(blk):
        r = rows(blk)
        b3 = b_s[r, :].reshape(nc, cs, dk)
        bmid = b3[:, cs // 2 - 1:cs // 2, :]
        blast = b3[:, cs - 1:cs, :]
        q3 = (q_ref[0, r, :].astype(F32) * dk ** -0.5).reshape(nc, cs, dk)
        k3 = k_s[r, :].reshape(nc, cs, dk)
        qin_s[r, :] = (q3 * jnp.exp2(b3 - bmid)).reshape(rb, dk).astype(BF16)
        kin_s[r, :] = (k3 * jnp.exp2(bmid - b3)).reshape(rb, dk).astype(BF16)
        ku_s[r, :] = (k3 * jnp.exp2(blast - b3)).reshape(rb, dk).astype(BF16)
        qb_s[r, :] = (q3 * jnp.exp2(b3)).reshape(rb, dk).astype(BF16)
        dec_s[pl.ds(blk * nc, nc)] = jnp.exp2(blast)

    def stage3(blk):
        r = rows(blk)
        a_s[r, :] = jnp.where(mask, _dot_nt(qin_s[r, :], kin_s[r, :]), 0.0).astype(BF16)
        vt = i_ref[0, r, :].astype(F32).T.astype(BF16)
        ku = ku_s[r, :]
        ut = _dot(vt, jnp.concatenate([ku * sel for sel in chunk_sel], axis=1))
        for c in range(nc):
            u_s[blk * nc + c] = ut[:, c * dk:(c + 1) * dk]

    def stage4(blk):
        r = rows(blk)
        oi_s[r, :] = _dot(a_s[r, :], i_ref[0, r, :])
        st = st_s[...]
        for c in range(nc):
            sp_s[blk * nc + c] = st.astype(BF16)
            st = st * dec_s[blk * nc + c] + u_s[blk * nc + c]
        st_s[...] = st

    def stage5(blk):
        r = rows(blk)
        inter = [_dot_nt(qb_s[pl.ds(pl.multiple_of(blk * rb + c * cs, cs), cs), :], sp_s[blk * nc + c])
                 for c in range(nc)]
        o = oi_s[r, :] + jnp.concatenate(inter, axis=0)
        o = _rms(o, og_ref[...]) * _silu(g_ref[0, r, :].astype(F32))
        o_ref[0, r, :] = o.astype(BF16)

    st_s[...] = jnp.zeros_like(st_s)
    _software_pipeline([stage1, stage2, stage3, stage4, stage5], seq // rb)


def _hgrn(proj3, lb_logits, onorm_g, layer):
    bsz, seq, _ = proj3.shape
    nh = HGRN_HEADS

    hps = HGRN_HEADS_PER_STEP
    width = hps * HGRN_DK

    def col(off):
        return pl.BlockSpec((1, seq, width), lambda b, h: (b, 0, off * (nh // hps) + h))

    nl = lb_logits.shape[0]
    return pl.pallas_call(
        functools.partial(_hgrn_kernel, layer=layer),
        grid=(bsz, nh // hps),
        in_specs=[col(0), col(1), col(2), col(3),
                  pl.BlockSpec((nl, width), lambda b, h: (0, h)),
                  pl.BlockSpec((1, HGRN_DK), lambda b, h: (0, 0))],
        out_specs=pl.BlockSpec((1, seq, width), lambda b, h: (b, 0, h)),
        out_shape=jax.ShapeDtypeStruct((bsz, seq, HGRN_WIDTH), BF16),
        scratch_shapes=[pltpu.VMEM((seq, HGRN_DK), BF16),
                        pltpu.VMEM((seq // HGRN_CHUNK, HGRN_DK, HGRN_DK), F32),
                        pltpu.VMEM((seq // HGRN_CHUNK, HGRN_DK, HGRN_DK), BF16),
                        pltpu.VMEM((seq // HGRN_CHUNK, 1, HGRN_DK), F32),
                        pltpu.VMEM((seq, HGRN_DK), F32),
                        pltpu.VMEM((seq, HGRN_DK), F32), pltpu.VMEM((seq, HGRN_DK), F32),
                        pltpu.VMEM((seq, HGRN_DK), BF16), pltpu.VMEM((seq, HGRN_DK), BF16),
                        pltpu.VMEM((seq, HGRN_DK), BF16), pltpu.VMEM((seq, min(HG_ROWS, seq)), BF16),
                        pltpu.VMEM((HGRN_DK, HGRN_DK), F32)],
        compiler_params=_cparams(("arbitrary", "arbitrary")),
        name="hgrn",
    )(proj3, proj3, proj3, proj3, lb_logits, onorm_g)


UP_ROWS = 1024


def _up_kernel(p_ref, pos_ref, wq_ref, wkv_ref, qag_ref, kvag_ref, qg_ref, kg_ref, freq_ref,
               q_ref, k_ref, v_ref):
    half = MLA_ROPE // 2
    p = p_ref[...].astype(F32)
    qa_n = _rms(p[:, 0:Q_LORA], qag_ref[...]).astype(BF16)
    kva_n = _rms(p[:, Q_LORA:Q_LORA + KV_LORA], kvag_ref[...]).astype(BF16)
    kpe_t = p[:, Q_LORA + KV_LORA:Q_LORA + KV_LORA + LANES].T[0:MLA_ROPE]
    ang = freq_ref[...] * pos_ref[...].astype(F32)
    cos = jnp.cos(ang)
    sin = jnp.sin(ang)

    def rope(x):
        x1, x2 = x[0:half], x[half:]
        return jnp.concatenate([x1 * cos - x2 * sin, x2 * cos + x1 * sin], axis=0)

    kg = kg_ref[...]
    kpe_ss = jnp.sum(kpe_t * kpe_t, axis=0, keepdims=True)
    kpe_rot = rope(kpe_t * kg[MLA_NOPE:])
    qg = qg_ref[...] * (MLA_QK ** -0.5 * LOG2E)
    zpad = jnp.zeros((MLA_QPAD - MLA_QK, p.shape[0]), F32)
    for h in range(MLA_HEADS):
        rows = slice(h * MLA_QPAD, (h + 1) * MLA_QPAD)
        qt = _dot_nt(wq_ref[rows, :], qa_n)
        qn = qt[0:MLA_QK] * lax.rsqrt(jnp.sum(qt * qt, axis=0, keepdims=True) / MLA_QK + EPS) * qg
        q_ref[0, h] = jnp.concatenate([qn[0:MLA_NOPE], rope(qn[MLA_NOPE:]), zpad], axis=0).astype(BF16)
        kvt = _dot_nt(wkv_ref[rows, :], kva_n)
        kn = kvt[0:MLA_NOPE]
        rk = lax.rsqrt((jnp.sum(kn * kn, axis=0, keepdims=True) + kpe_ss) / MLA_QK + EPS)
        kt = jnp.concatenate([kn * rk * kg[0:MLA_NOPE], kpe_rot * rk, zpad], axis=0)
        k_ref[0, h] = kt.T.astype(BF16)
        v_ref[0, h] = kvt[MLA_NOPE:].astype(BF16)


def _mla_up(proj3, pos3, wq_t, wkv_t, qag, kvag, qg, kg, freq):
    bsz, seq, _ = proj3.shape
    tm = min(UP_ROWS, seq)
    nh = MLA_HEADS
    mla_block = 4 * HGRN_WIDTH // 1024

    def const(shape):
        return pl.BlockSpec(shape, lambda b, i: (0,) * len(shape))

    return pl.pallas_call(
        _up_kernel,
        grid=(bsz, seq // tm),
        in_specs=[pl.BlockSpec((None, tm, 1024), lambda b, i: (b, i, mla_block)),
                  pl.BlockSpec((None, 1, tm), lambda b, i: (b, 0, i)),
                  const(wq_t.shape), const(wkv_t.shape), const(qag.shape), const(kvag.shape),
                  const(qg.shape), const(kg.shape), const(freq.shape)],
        out_specs=[pl.BlockSpec((1, nh, MLA_QPAD, tm), lambda b, i: (b, 0, 0, i)),
                   pl.BlockSpec((1, nh, tm, MLA_QPAD), lambda b, i: (b, 0, i, 0)),
                   pl.BlockSpec((1, nh, MLA_V, tm), lambda b, i: (b, 0, 0, i))],
        out_shape=[jax.ShapeDtypeStruct((bsz, nh, MLA_QPAD, seq), BF16),
                   jax.ShapeDtypeStruct((bsz, nh, seq, MLA_QPAD), BF16),
                   jax.ShapeDtypeStruct((bsz, nh, MLA_V, seq), BF16)],
        compiler_params=_cparams(("arbitrary", "arbitrary")),
        name="mla_up",
    )(proj3, pos3, wq_t, wkv_t, qag, kvag, qg, kg, freq)


ATT_T = 256
ATT_HEADS_PER_STEP = 2


def _attn_kernel(qt_ref, k_ref, vt_ref, g_ref, o_ref):
    for hh in range(k_ref.shape[1]):
        _attn_head(qt_ref.at[:, pl.ds(hh, 1)], k_ref.at[:, pl.ds(hh, 1)], vt_ref.at[:, pl.ds(hh, 1)], g_ref,
                   o_ref.at[:, :, pl.ds(hh * MLA_V, MLA_V)])


def _attn_head(qt_ref, k_ref, vt_ref, g_ref, o_ref):
    seq = k_ref.shape[2]
    t = min(ATT_T, seq)
    key = lax.broadcasted_iota(jnp.int32, (t, t), 0)
    qry = lax.broadcasted_iota(jnp.int32, (t, t), 1)
    causal = key <= qry
    neg = jnp.finfo(F32).min

    def widen(x, off, fill):
        return x if off == 0 else jnp.concatenate([jnp.full((x.shape[0], off), fill, x.dtype), x], axis=1)

    scores = []
    m = None
    for off in range(0, seq, t):
        s = _dot(k_ref[0, 0, off:off + t, :], qt_ref[0, 0, :, off:])
        diag = jnp.where(causal, s[:, 0:t], neg)
        s = diag if off + t == seq else jnp.concatenate([diag, s[:, t:]], axis=1)
        scores.append(s)
        blk_max = widen(jnp.max(s, axis=0, keepdims=True), off, neg)
        m = blk_max if m is None else jnp.maximum(m, blk_max)
    l = None
    ot = None
    for off, s in zip(range(0, seq, t), scores):
        p = jnp.exp2(s - m[:, off:])
        p_sum = widen(jnp.sum(p, axis=0, keepdims=True), off, 0.0)
        pv = widen(_dot(vt_ref[0, 0, :, off:off + t], p.astype(BF16)), off, 0.0)
        l = p_sum if l is None else l + p_sum
        ot = pv if ot is None else ot + pv
    ot = ot * (1.0 / l)
    ot = ot * lax.rsqrt(jnp.mean(ot * ot, axis=0, keepdims=True) + EPS) * g_ref[...]
    o_ref[0, :, :] = ot.T.astype(BF16)


def _attention(q, k, v, g):
    bsz, nh, seq, _ = k.shape
    hps = ATT_HEADS_PER_STEP
    return pl.pallas_call(
        _attn_kernel,
        grid=(bsz, nh // hps),
        in_specs=[pl.BlockSpec((1, hps, MLA_QPAD, seq), lambda b, h: (b, h, 0, 0)),
                  pl.BlockSpec((1, hps, seq, MLA_QPAD), lambda b, h: (b, h, 0, 0)),
                  pl.BlockSpec((1, hps, MLA_V, seq), lambda b, h: (b, h, 0, 0)),
                  pl.BlockSpec((MLA_V, 1), lambda b, h: (0, 0))],
        out_specs=pl.BlockSpec((1, seq, hps * MLA_V), lambda b, h: (b, 0, h)),
        out_shape=jax.ShapeDtypeStruct((bsz, seq, nh * MLA_V), BF16),
        compiler_params=_cparams(("arbitrary", "arbitrary")),
        name="attn",
    )(q, k, v, g)


def _out_kernel(oa_ref, ob_ref, x_ref, wa_ref, wb_ref, g1_ref, n2g_ref, sh2_ref, sc2_ref,
                wr_ref, br_ref, x1_ref, h2_ref, lg_ref, mix_s):
    @pl.when(pl.program_id(0) == 0)
    def _():
        mix_s[...] = jnp.zeros_like(mix_s)

    mix_next = _dot(oa_ref[...], wa_ref[...]) + _dot(ob_ref[...], wb_ref[...])
    x1 = x_ref[...] + g1_ref[0] * mix_s[...]
    x1_ref[...] = x1
    h2 = _rms(x1, n2g_ref[...]) * (1.0 + sc2_ref[0]) + sh2_ref[0]
    tm = h2.shape[0]
    for s in range(TOK_SUBLANES):
        h2_ref[pl.ds(s, tm, stride=TOK_SUBLANES), :] = _pack_pair(
            h2[:, s * LANES:(s + 1) * LANES], h2[:, (s + TOK_SUBLANES) * LANES:(s + TOK_SUBLANES + 1) * LANES])
    lg = _dot(h2.astype(BF16), wr_ref[...]) + br_ref[...]
    lg_ref[...] = lg.T[0:ROUTE_ROWS, :]
    mix_s[...] = mix_next


def _out_proj(oa, ob, x2, wa, wb, g1, n2g, sh2, sc2, wr, br, seq):
    t, d = x2.shape
    tm = min(512, seq)
    n_tiles = t // tm

    def const(shape):
        return pl.BlockSpec(shape, lambda i: (0,) * len(shape))

    def prev(i):
        return jnp.maximum(i - 1, 0)

    def per_batch():
        return pl.BlockSpec((1, 1, d), lambda i: (prev(i) * tm // seq, 0, 0))

    return pl.pallas_call(
        _out_kernel,
        grid=(n_tiles + 1,),
        in_specs=[pl.BlockSpec((tm, HGRN_WIDTH), lambda i: (jnp.minimum(i, n_tiles - 1), 0)),
                  pl.BlockSpec((tm, HGRN_WIDTH), lambda i: (jnp.minimum(i, n_tiles - 1), 0)),
                  pl.BlockSpec((tm, d), lambda i: (prev(i), 0)),
                  const(wa.shape), const(wb.shape), per_batch(), const(n2g.shape),
                  per_batch(), per_batch(), const(wr.shape), const(br.shape)],
        out_specs=[pl.BlockSpec((tm, d), lambda i: (prev(i), 0)),
                   pl.BlockSpec((tm * TOK_SUBLANES, LANES), lambda i: (prev(i), 0)),
                   pl.BlockSpec((ROUTE_ROWS, tm), lambda i: (0, prev(i)))],
        out_shape=[jax.ShapeDtypeStruct((t, d), F32),
                   jax.ShapeDtypeStruct((t * TOK_SUBLANES, LANES), jnp.uint32),
                   jax.ShapeDtypeStruct((ROUTE_ROWS, t), F32)],
        scratch_shapes=[pltpu.VMEM((tm, d), F32)],
        compiler_params=_cparams(("arbitrary",)),
        name="out_proj",
    )(oa, ob, x2, wa, wb, g1, n2g, sh2, sc2, wr, br)


def _route_kernel(lg_ref, tri_ref, ri_ref, rw_ref, cnt_ref, carry_s):
    step = pl.program_id(0)

    @pl.when(step == 0)
    def _():
        carry_s[...] = jnp.zeros_like(carry_s)

    lg = lg_ref[...]
    tr = lg.shape[1]
    epg = EXPERTS_PER_GROUP
    gl = lg[N_EXPERTS:N_EXPERTS + N_GROUPS, :]
    row_g = lax.broadcasted_iota(jnp.int32, (N_GROUPS, tr), 0)
    gmax = jnp.max(gl, axis=0, keepdims=True)
    g_sel = jnp.min(jnp.where(gl == gmax, row_g, N_GROUPS), axis=0, keepdims=True)
    p_group = 1.0 / jnp.sum(jnp.exp(gl - gmax), axis=0, keepdims=True)

    e_in = lg[0:epg, :]
    for g in range(1, N_GROUPS):
        e_in = jnp.where(g_sel == g, lg[g * epg:(g + 1) * epg, :], e_in)
    row_e = lax.broadcasted_iota(jnp.int32, (epg, tr), 0)
    top1 = jnp.max(e_in, axis=0, keepdims=True)
    i1 = jnp.min(jnp.where(e_in == top1, row_e, epg), axis=0, keepdims=True)
    rest = jnp.where(row_e == i1, -jnp.inf, e_in)
    top2 = jnp.max(rest, axis=0, keepdims=True)
    i2 = jnp.min(jnp.where(rest == top2, row_e, epg), axis=0, keepdims=True)
    e2w = jnp.exp(top2 - top1)
    w1 = p_group / (1.0 + e2w)
    w2 = p_group * e2w / (1.0 + e2w)
    ex1 = g_sel * epg + i1
    ex2 = g_sel * epg + i2

    row_x = lax.broadcasted_iota(jnp.int32, (N_EXPERTS, tr), 0)
    oh1 = row_x == ex1
    oh2 = row_x == ex2
    oh = jnp.logical_or(oh1, oh2)
    before = _dot(oh.astype(BF16), tri_ref[...]) + carry_s[:, 0:1]
    rank1 = jnp.sum(jnp.where(oh1, before, 0.0), axis=0, keepdims=True)
    rank2 = jnp.sum(jnp.where(oh2, before, 0.0), axis=0, keepdims=True)
    carry_s[...] = carry_s[...] + jnp.sum(oh.astype(F32), axis=1, keepdims=True)

    zi = jnp.zeros((4, tr), jnp.int32)
    ri_ref[...] = jnp.concatenate([ex1, ex2, rank1.astype(jnp.int32), rank2.astype(jnp.int32), zi], axis=0)
    rw_ref[...] = jnp.concatenate([w1, w2, jnp.zeros((6, tr), F32)], axis=0)
    cnt_ref[...] = carry_s[...].astype(jnp.int32)


def _route(lg_t, tri):
    t = lg_t.shape[1]
    tr = tri.shape[0]
    return pl.pallas_call(
        _route_kernel,
        grid=(t // tr,),
        in_specs=[pl.BlockSpec((ROUTE_ROWS, tr), lambda i: (0, i)),
                  pl.BlockSpec((tr, tr), lambda i: (0, 0))],
        out_specs=[pl.BlockSpec((8, tr), lambda i: (0, i)),
                   pl.BlockSpec((8, tr), lambda i: (0, i)),
                   pl.BlockSpec((N_EXPERTS, LANES), lambda i: (0, 0))],
        out_shape=[jax.ShapeDtypeStruct((8, t), jnp.int32),
                   jax.ShapeDtypeStruct((8, t), F32),
                   jax.ShapeDtypeStruct((N_EXPERTS, LANES), jnp.int32)],
        scratch_shapes=[pltpu.VMEM((N_EXPERTS, LANES), F32)],
        compiler_params=_cparams(("arbitrary",)),
        name="route",
    )(lg_t, tri)


def _moe_kernel(rd_ref, be_ref, pe_ref, nb_ref, x_ref, wg_ref, wu_ref, wd_ref, y_ref,
                ys, wg_f, wu_f, wd_f, wg_s, wu_s, wd_s, ssem, wsem):
    b = pl.program_id(0)
    last = pl.num_programs(0) - 1
    nb = nb_ref[0]
    tm = MOE_BLOCK
    ts = TOK_SUBLANES
    weights = ((wg_ref, wg_f, wg_s), (wu_ref, wu_f, wu_s), (wd_ref, wd_f, wd_s))

    def fetch(e):
        return [pltpu.make_async_copy(src.at[e], stage, wsem.at[k]) for k, (src, stage, _) in enumerate(weights)]

    def scatter(blk):
        base = blk * tm
        for j in range(tm):
            dst = pl.multiple_of(rd_ref[base + j] * ts, ts)
            pltpu.make_async_copy(ys.at[pl.ds(j * ts, ts), :], y_ref.at[pl.ds(dst, ts), :], ssem).start(priority=j % 2)

    def wait_scatter():
        pltpu.make_async_copy(ys, y_ref.at[pl.ds(0, tm * ts), :], ssem).wait()

    @pl.when(b == 0)
    def _():
        for copy in fetch(be_ref[0]):
            copy.start()
        ys[...] = jnp.zeros_like(ys)
        spare = pltpu.make_async_copy(ys, y_ref.at[pl.ds(y_ref.shape[0] - tm * ts, tm * ts), :], ssem)
        spare.start()
        spare.wait()

    expert = be_ref[b]
    first_of_expert = jnp.logical_or(b == 0, expert != be_ref[jnp.maximum(b - 1, 0)])

    @pl.when(jnp.logical_and(b < nb, first_of_expert))
    def _():
        for copy in fetch(0):
            copy.wait()
        for _, stage, dst in weights:
            dst[...] = stage[...].astype(BF16)
        next_blk = pe_ref[expert] // tm

        @pl.when(next_blk < nb)
        def _():
            for copy in fetch(be_ref[next_blk]):
                copy.start()

    @pl.when(b < nb)
    def _():
        scatter(jnp.maximum(b - 1, 0))
        parts = [_unpack_pair(x_ref[pl.ds(s, tm, stride=ts), :]) for s in range(ts)]
        x = jnp.concatenate([p[0] for p in parts] + [p[1] for p in parts], axis=1).astype(BF16)
        hid = _silu(_dot(x, wg_s[...])) * _dot(x, wu_s[...])
        y = _dot(hid.astype(BF16), wd_s[...])
        wait_scatter()
        for s in range(ts):
            ys[pl.ds(s, tm, stride=ts), :] = _pack_pair(y[:, s * LANES:(s + 1) * LANES],
                                                        y[:, (s + ts) * LANES:(s + ts + 1) * LANES])

        @pl.when(b == last)
        def _():
            scatter(b)
            wait_scatter()

    @pl.when(b == nb)
    def _():
        scatter(b - 1)
        wait_scatter()


def _moe(row_dst, blk_e, pad_end, nb_real, x_sorted, wg, wu, wd, n_out_rows):
    n_blocks = blk_e.shape[0]
    d, de = wg.shape[1], wg.shape[2]
    tm = MOE_BLOCK
    hbm = pl.BlockSpec(memory_space=pl.ANY)
    return pl.pallas_call(
        _moe_kernel,
        grid_spec=pltpu.PrefetchScalarGridSpec(
            num_scalar_prefetch=4,
            grid=(n_blocks,),
            in_specs=[pl.BlockSpec((tm * TOK_SUBLANES, LANES),
                                   lambda b, rd, be, pe, nb: (jnp.minimum(b, nb[0] - 1), 0)),
                      hbm, hbm, hbm],
            out_specs=hbm,
            scratch_shapes=[pltpu.VMEM((tm * TOK_SUBLANES, LANES), jnp.uint32),
                            pltpu.VMEM((d, de), F32), pltpu.VMEM((d, de), F32), pltpu.VMEM((de, d), F32),
                            pltpu.VMEM((d, de), BF16), pltpu.VMEM((d, de), BF16), pltpu.VMEM((de, d), BF16),
                            pltpu.SemaphoreType.DMA(()), pltpu.SemaphoreType.DMA((3,))]),
        out_shape=jax.ShapeDtypeStruct((n_out_rows * TOK_SUBLANES, LANES), jnp.uint32),
        compiler_params=_cparams(("arbitrary",)),
        name="moe",
    )(row_dst, blk_e, pad_end, nb_real, x_sorted, wg, wu, wd)


def _dispatch_kernel(e1_ref, e2_ref, r1_ref, r2_ref, ps_ref, pe_ref, cnt_ref, be_ref, nb_ref,
                     h_ref, x_ref, rd_ref, zbuf, sem, zsem):
    i = pl.program_id(0)
    ts = TOK_SUBLANES
    td = h_ref.shape[0] // ts
    n_tok = e1_ref.shape[0]
    base = i * td
    blk_rows = MOE_BLOCK * ts

    def zero_fill(blk):
        return pltpu.make_async_copy(
            zbuf, x_ref.at[pl.ds(pl.multiple_of(blk * blk_rows, blk_rows), blk_rows), :], zsem)

    @pl.when(i == 0)
    def _():
        zbuf[...] = jnp.zeros_like(zbuf)

        def fill(blk, n_fills):
            e = be_ref[blk]
            partial_last = jnp.logical_and(blk == pe_ref[e] // MOE_BLOCK - 1, cnt_ref[e] % MOE_BLOCK != 0)
            needs_fill = jnp.logical_or(blk >= nb_ref[0], partial_last)

            @pl.when(needs_fill)
            def _():
                zero_fill(blk).start()
            return n_fills + needs_fill.astype(jnp.int32)

        n_fills = lax.fori_loop(0, be_ref.shape[0], fill, 0)

        def init(blk, carry):
            for j in range(MOE_BLOCK):
                rd_ref[blk * MOE_BLOCK + j] = 2 * n_tok + j
            return carry

        lax.fori_loop(0, rd_ref.shape[0] // MOE_BLOCK, init, 0)

        def drain(k, carry):
            zero_fill(0).wait()
            return carry

        lax.fori_loop(0, n_fills, drain, 0)

    def start(t, carry):
        src = h_ref.at[pl.ds(pl.multiple_of(t * ts, ts), ts), :]
        tok = base + t
        for slot, (e_ref, r_ref) in enumerate(((e1_ref, r1_ref), (e2_ref, r2_ref))):
            row = ps_ref[e_ref[tok]] + r_ref[tok]
            pltpu.make_async_copy(src, x_ref.at[pl.ds(pl.multiple_of(row * ts, ts), ts), :], sem).start(priority=slot)
            rd_ref[row] = 2 * tok + slot
        return carry

    lax.fori_loop(0, td, start, 0, unroll=8)
    for _ in range(2):
        pltpu.make_async_copy(h_ref, x_ref.at[pl.ds(0, td * ts), :], sem).wait()


def _dispatch(ri, pad_start, pad_end, counts, blk_e, nb_real, h2p, n_rows):
    t = ri.shape[1]
    td = min(512, t)
    ts = TOK_SUBLANES
    return pl.pallas_call(
        _dispatch_kernel,
        grid_spec=pltpu.PrefetchScalarGridSpec(
            num_scalar_prefetch=9,
            grid=(t // td,),
            in_specs=[pl.BlockSpec((td * ts, LANES), lambda i, *_: (i, 0))],
            out_specs=[pl.BlockSpec(memory_space=pl.ANY), pl.BlockSpec(memory_space=pltpu.SMEM)],
            scratch_shapes=[pltpu.VMEM((MOE_BLOCK * ts, LANES), jnp.uint32),
                            pltpu.SemaphoreType.DMA(()), pltpu.SemaphoreType.DMA(())]),
        out_shape=[jax.ShapeDtypeStruct((n_rows * ts, LANES), jnp.uint32),
                   jax.ShapeDtypeStruct((n_rows,), jnp.int32)],
        compiler_params=_cparams(("arbitrary",)),
        name="dispatch",
    )(ri[0], ri[1], ri[2], ri[3], pad_start, pad_end, counts, blk_e, nb_real, h2p)


def _moe_plan(counts, t):
    tm = MOE_BLOCK
    n_blocks = 2 * t // tm + N_EXPERTS
    padded = ((counts + tm - 1) // tm) * tm
    pad_end = jnp.cumsum(padded)
    pad_start = pad_end - padded
    blk = jnp.arange(n_blocks, dtype=jnp.int32)
    blk_e = jnp.minimum(jnp.sum(pad_end[None, :] <= (blk * tm)[:, None], axis=1), N_EXPERTS - 1).astype(jnp.int32)
    nb_real = (pad_end[-1:] // tm).astype(jnp.int32)
    return pad_start, pad_end, blk_e, nb_real, n_blocks * tm


COMBINE_ROWS = 64


def _combine_kernel(x1_ref, g2_ref, rw_ref, y_ref, o_ref, w_s):
    ts = TOK_SUBLANES
    rows = min(COMBINE_ROWS, x1_ref.shape[0])
    w_s[...] = rw_ref[...].T

    def chunk(c, carry):
        r = pl.ds(pl.multiple_of(c * rows, rows), rows)
        w1 = w_s[r, 0:1]
        w2 = w_s[r, 1:2]
        tile0 = pl.multiple_of(c * rows * 2 * ts, rows * 2 * ts)
        for s in range(ts):
            a_lo, a_hi = _unpack_pair(y_ref[pl.ds(tile0 + s, rows, stride=2 * ts), :])
            b_lo, b_hi = _unpack_pair(y_ref[pl.ds(tile0 + ts + s, rows, stride=2 * ts), :])
            for col, ya, yb in ((s, a_lo, b_lo), (s + ts, a_hi, b_hi)):
                cols = slice(col * LANES, (col + 1) * LANES)
                o_ref[r, cols] = x1_ref[r, cols] + g2_ref[0][:, cols] * (w1 * ya + w2 * yb)
        return carry

    lax.fori_loop(0, x1_ref.shape[0] // rows, chunk, 0)


def _combine(x1, g2, rw, y2, seq):
    t, d = x1.shape
    tc = min(512, seq)
    return pl.pallas_call(
        _combine_kernel,
        grid=(t // tc,),
        in_specs=[pl.BlockSpec((tc, d), lambda i: (i, 0)),
                  pl.BlockSpec((1, 1, d), lambda i: (i * tc // seq, 0, 0)),
                  pl.BlockSpec((rw.shape[0], tc), lambda i: (0, i)),
                  pl.BlockSpec((tc * 2 * TOK_SUBLANES, LANES), lambda i: (i, 0))],
        out_specs=pl.BlockSpec((tc, d), lambda i: (i, 0)),
        out_shape=jax.ShapeDtypeStruct((t, d), F32),
        scratch_shapes=[pltpu.VMEM((tc, rw.shape[0]), F32)],
        compiler_params=_cparams(("arbitrary",)),
        name="combine",
    )(x1, g2, rw, y2)


def _q_up_layout(w_q_up):
    w = w_q_up.reshape(Q_LORA, MLA_HEADS, MLA_QK)
    w = jnp.pad(w, ((0, 0), (0, 0), (0, MLA_QPAD - MLA_QK)))
    return w.reshape(Q_LORA, MLA_HEADS * MLA_QPAD).T.astype(BF16)


def _pad_lanes(g, width):
    return jnp.pad(g, (0, width - g.shape[0])).reshape(1, width)


def kernel(x, c, positions, w_ada, b_ada, norm1_g, w_in, hgrn_lb_logits, hgrn_onorm_g, q_a_norm_g, w_q_up,
           kv_a_norm_g, w_kv_up, q_norm_g, k_norm_g, attn_onorm_g, w_out, norm2_g, w_group, b_group,
           w_router, b_router, w_gate, w_up, w_down):
    bsz, seq, d = x.shape
    t = bsz * seq
    depth = w_ada.shape[0]
    half = MLA_ROPE // 2
    inv_freq = ROPE_BASE ** (-jnp.arange(0, MLA_ROPE, 2, dtype=F32) / MLA_ROPE)
    freq = inv_freq.reshape(half, 1)
    pos3 = positions.reshape(bsz, 1, seq)
    tr = min(1024, t)
    tri = jnp.triu(jnp.ones((tr, tr), BF16), 1)

    x2 = x.reshape(t, d)
    for l in range(depth):
        mod = _ada(c, w_ada[l], b_ada[l]).reshape(bsz, 6, 1, d)
        sh1, sc1, g1, sh2, sc2, g2 = (mod[:, i] for i in range(6))

        proj = _in_proj(x2, norm1_g[l].reshape(1, d), sh1, sc1, w_in[l].T, seq)
        proj3 = proj.reshape(bsz, seq, IN_COLS_PAD)

        o_a = _hgrn(proj3, hgrn_lb_logits, hgrn_onorm_g[l].reshape(1, HGRN_DK), l)

        q, k, v = _mla_up(proj3, pos3, _q_up_layout(w_q_up[l]), w_kv_up[l].T.astype(BF16),
                          q_a_norm_g[l].reshape(1, Q_LORA), kv_a_norm_g[l].reshape(1, KV_LORA),
                          q_norm_g[l].reshape(MLA_QK, 1), k_norm_g[l].reshape(MLA_QK, 1), freq)
        o_b = _attention(q, k, v, attn_onorm_g[l].reshape(MLA_V, 1))

        w_o = w_out[l].astype(BF16)
        wr = jnp.pad(jnp.concatenate([w_router[l], w_group[l]], axis=1),
                     ((0, 0), (0, LANES - N_EXPERTS - N_GROUPS))).astype(BF16)
        br = _pad_lanes(jnp.concatenate([b_router[l], b_group[l]]), LANES)
        x1, h2, lg_t = _out_proj(o_a.reshape(t, HGRN_WIDTH), o_b.reshape(t, HGRN_WIDTH), x2,
                                 w_o[:HGRN_WIDTH], w_o[HGRN_WIDTH:], g1, norm2_g[l].reshape(1, d),
                                 sh2, sc2, wr, br, seq)

        ri, rw, cnt = _route(lg_t, tri)
        counts = cnt[:, 0]
        pad_start, pad_end, blk_e, nb_real, n_rows = _moe_plan(counts, t)
        x_sorted, row_dst = _dispatch(ri, pad_start, pad_end, counts, blk_e, nb_real, h2, n_rows)
        y2 = _moe(row_dst, blk_e, pad_end, nb_real, x_sorted, w_gate[l], w_up[l], w_down[l], 2 * t + MOE_BLOCK)
        x2 = _combine(x1, g2, rw, y2, seq)
    return x2.reshape(bsz, seq, d)
```

```python
import functools

import jax
import jax.numpy as jnp
from jax import lax
from jax.experimental import pallas as pl
from jax.experimental.pallas import tpu as pltpu

F32 = jnp.float32
BF16 = jnp.bfloat16
EPS = 1e-6
LOG2E = 1.4426950408889634

D_MODEL = 2048
HGRN_WIDTH = 1024
HGRN_DK = 128
HGRN_HEADS = 8
HGRN_CHUNK = 64
MLA_HEADS = 8
MLA_NOPE = 128
MLA_ROPE = 64
MLA_QK = MLA_NOPE + MLA_ROPE
MLA_V = 128
MLA_VT_ROWS = MLA_V + 16
MLA_QPAD = 256
Q_LORA = 512
KV_LORA = 256
ROPE_BASE = 10000.0
IN_COLS = 4 * HGRN_WIDTH + Q_LORA + KV_LORA + MLA_ROPE
IN_COLS_PAD = 5120
N_GROUPS = 4
EXPERTS_PER_GROUP = 8
N_EXPERTS = 32
D_EXPERT = 512
ROUTE_ROWS = 40
MOE_BLOCK = 256
LANES = 128
TOK_SUBLANES = 8
VMEM_LIMIT = 56 * 1024 * 1024


def _cparams(sem):
    return pltpu.CompilerParams(dimension_semantics=sem, vmem_limit_bytes=VMEM_LIMIT)


def _dot(a, b):
    return jnp.dot(a, b, preferred_element_type=F32)


def _dot_nt(a, b):
    return lax.dot_general(a, b, (((1,), (1,)), ((), ())), preferred_element_type=F32)


def _rms(x, g):
    return x * lax.rsqrt(jnp.mean(x * x, axis=-1, keepdims=True) + EPS) * g


def _silu(x):
    return x * jax.nn.sigmoid(x)


def _pack_pair(lo, hi):
    lo_b = lax.bitcast_convert_type(lo.astype(BF16).astype(F32), jnp.uint32)
    hi_b = lax.bitcast_convert_type(hi.astype(BF16).astype(F32), jnp.uint32)
    return hi_b | (lo_b >> 16)


def _unpack_pair(w):
    lo = lax.bitcast_convert_type(w << 16, F32)
    hi = lax.bitcast_convert_type(w & jnp.uint32(0xFFFF0000), F32)
    return lo, hi


def _ada_kernel(c_ref, w_ref, b_ref, o_ref):
    ca = _silu(c_ref[...]).astype(BF16)
    o_ref[...] = _dot(ca, w_ref[...].astype(BF16)) + b_ref[...]


def _ada(c, w, b):
    bsz, d = c.shape
    n = w.shape[1]
    tn = 1024
    return pl.pallas_call(
        _ada_kernel,
        grid=(n // tn,),
        in_specs=[pl.BlockSpec((bsz, d), lambda j: (0, 0)),
                  pl.BlockSpec((d, tn), lambda j: (0, j)),
                  pl.BlockSpec((1, tn), lambda j: (0, j))],
        out_specs=pl.BlockSpec((bsz, tn), lambda j: (0, j)),
        out_shape=jax.ShapeDtypeStruct((bsz, n), F32),
        compiler_params=_cparams(("arbitrary",)),
        name="ada",
    )(c, w, b.reshape(1, n))


NORM_ROWS = 32


def _norm_kernel(x_ref, g_ref, sh_ref, sc_ref, h_ref):
    def body(c, carry):
        r = pl.ds(pl.multiple_of(c * NORM_ROWS, NORM_ROWS), NORM_ROWS)
        h = _rms(x_ref[r, :], g_ref[...]) * (1.0 + sc_ref[0]) + sh_ref[0]
        h_ref[r, :] = h.astype(BF16)
        return carry
    lax.fori_loop(0, x_ref.shape[0] // NORM_ROWS, body, 0, unroll=2)


def _norm_mod(x2, g, sh, sc, seq, n_rows):
    t, d = x2.shape
    tm = min(1024, seq, n_rows)
    return pl.pallas_call(
        _norm_kernel,
        grid=(n_rows // tm,),
        in_specs=[pl.BlockSpec((tm, d), lambda i: (i, 0)),
                  pl.BlockSpec((1, d), lambda i: (0, 0)),
                  pl.BlockSpec((1, 1, d), lambda i: (i * tm // seq, 0, 0)),
                  pl.BlockSpec((1, 1, d), lambda i: (i * tm // seq, 0, 0))],
        out_specs=pl.BlockSpec((tm, d), lambda i: (i, 0)),
        out_shape=jax.ShapeDtypeStruct((n_rows, d), BF16),
        compiler_params=_cparams(("arbitrary",)),
        name="norm_mod",
    )(x2, g, sh, sc)


IN_NORM_CHUNKS = 4


def _in_kernel(h0_ref, x_ref, g_ref, sh_ref, sc_ref, w_ref, o_ref, h_s):
    i = pl.program_id(0)
    j = pl.program_id(1)
    slot = i % 2

    @pl.when(jnp.logical_and(i == 0, j == 0))
    def _():
        h_s[0] = h0_ref[...]

    o_ref[...] = _dot_nt(h_s[slot], w_ref[...]).astype(BF16)
    rows = x_ref.shape[0] // IN_NORM_CHUNKS
    first = jnp.minimum(j, IN_NORM_CHUNKS - 1) * rows
    gain = g_ref[...]
    scale = 1.0 + sc_ref[0]
    shift = sh_ref[0]
    for c in range(rows // NORM_ROWS):
        r = pl.ds(pl.multiple_of(first + c * NORM_ROWS, NORM_ROWS), NORM_ROWS)
        h_s[1 - slot, r, :] = (_rms(x_ref[r, :], gain) * scale + shift).astype(BF16)


def _in_proj(x2, g, sh, sc, w_t, seq):
    t, d = x2.shape
    tn = 1024
    tm = min(1024, seq)
    n_tiles = t // tm
    w_b = jnp.pad(w_t, ((0, IN_COLS_PAD - IN_COLS), (0, 0))).astype(BF16)
    h0 = _norm_mod(x2, g, sh, sc, seq, tm)

    def nxt(i):
        return jnp.minimum(i + 1, n_tiles - 1)

    return pl.pallas_call(
        _in_kernel,
        grid=(n_tiles, IN_COLS_PAD // tn),
        in_specs=[pl.BlockSpec((tm, d), lambda i, j: (0, 0)),
                  pl.BlockSpec((tm, d), lambda i, j: (nxt(i), 0)),
                  pl.BlockSpec((1, d), lambda i, j: (0, 0)),
                  pl.BlockSpec((1, 1, d), lambda i, j: (nxt(i) * tm // seq, 0, 0)),
                  pl.BlockSpec((1, 1, d), lambda i, j: (nxt(i) * tm // seq, 0, 0)),
                  pl.BlockSpec((tn, d), lambda i, j: (j, 0))],
        out_specs=pl.BlockSpec((tm, tn), lambda i, j: (i, j)),
        out_shape=jax.ShapeDtypeStruct((t, IN_COLS_PAD), BF16),
        scratch_shapes=[pltpu.VMEM((2, tm, d), BF16)],
        compiler_params=_cparams(("arbitrary", "arbitrary")),
        name="in_proj",
    )(h0, x2, g, sh, sc, w_b)


HG_ROWS = 256
HGRN_HEADS_PER_STEP = 2


PIPELINE_STATIC_BLOCKS = 8


def _software_pipeline(stages, n_blocks):
    depth = len(stages)

    def step(i, static):
        for k in reversed(range(depth)):
            if static and not 0 <= i - k < n_blocks:
                continue
            stages[k](i - k)

    if n_blocks <= PIPELINE_STATIC_BLOCKS:
        for i in range(n_blocks + depth - 1):
            step(i, True)
        return
    for i in range(depth - 1):
        step(i, True)

    def steady(i, carry):
        step(i, False)
        return carry

    lax.fori_loop(depth - 1, n_blocks, steady, 0)
    for i in range(n_blocks, n_blocks + depth - 1):
        step(i, True)


def _chunk_mask(n):
    row = lax.broadcasted_iota(jnp.int32, (n, n), 0)
    col = lax.broadcasted_iota(jnp.int32, (n, n), 1)
    return jnp.logical_and(row // HGRN_CHUNK == col // HGRN_CHUNK, col <= row)


HGRN_RING = 3


def _hgrn_kernel(proj_ref, lbl_ref, og_ref, o_ref, ring, rsem, *scratch, layer, n_col_steps, n_steps):
    step = pl.program_id(0) * n_col_steps + pl.program_id(1)
    width = ring.shape[-1]

    def fetch(s):
        b, h = s // n_col_steps, s % n_col_steps
        slot = s % HGRN_RING
        cols = [(k * n_col_steps + h) * width for k in range(ring.shape[1])]
        if not isinstance(s, int):
            cols = [pl.multiple_of(c, width) for c in cols]
        return [pltpu.make_async_copy(proj_ref.at[b, :, pl.ds(c, width)], ring.at[slot, k], rsem.at[slot, k])
                for k, c in enumerate(cols)]

    @pl.when(step == 0)
    def _():
        for s in range(min(HGRN_RING - 1, n_steps)):
            for copy in fetch(s):
                copy.start()

    @pl.when(step + HGRN_RING - 1 < n_steps)
    def _():
        for copy in fetch(step + HGRN_RING - 1):
            copy.start()

    for copy in fetch(step):
        copy.wait()
    slot = step % HGRN_RING
    q_ref, f_ref, i_ref, g_ref = (ring.at[slot, pl.ds(k, 1)] for k in range(4))
    for hh in range(width // HGRN_DK):
        ls = pl.ds(hh * HGRN_DK, HGRN_DK)
        _hgrn_head(q_ref.at[:, :, ls], f_ref.at[:, :, ls], i_ref.at[:, :, ls], g_ref.at[:, :, ls],
                   lbl_ref.at[:, ls], og_ref, o_ref.at[:, :, ls], *scratch, layer=layer)


def _hgrn_head(q_ref, f_ref, i_ref, g_ref, lbl_ref, og_ref, o_ref,
               qb_s, u_s, sp_s, dec_s, oi_s, b_s, k_s, qin_s, kin_s, ku_s, a_s, st_s, *, layer):
    seq = q_ref.shape[1]
    cs = HGRN_CHUNK
    rb = min(HG_ROWS, seq)
    nc = rb // cs
    dk = HGRN_DK
    lg = lbl_ref[...]
    ex = jnp.exp(lg - jnp.max(lg, axis=0, keepdims=True))
    sm = ex / jnp.sum(ex, axis=0, keepdims=True)
    lb = jnp.sum(sm[0:layer + 1], axis=0, keepdims=True)
    mask = _chunk_mask(rb)
    tri = mask.astype(BF16)
    row_chunk = lax.broadcasted_iota(jnp.int32, (rb, dk), 0) // cs

    chunk_sel = [(row_chunk == c).astype(BF16) for c in range(nc)]

    def rows(blk):
        return pl.ds(pl.multiple_of(blk * rb, rb), rb)

    def stage1(blk):
        r = rows(blk)
        f = lb + (1.0 - lb) * jax.nn.sigmoid(f_ref[0, r, :].astype(F32))
        lf = jnp.log2(f)
        k_s[r, :] = 1.0 - f
        hi = lf.astype(BF16)
        r1 = lf - hi.astype(F32)
        mid = r1.astype(BF16)
        lo = (r1 - mid.astype(F32)).astype(BF16)
        bhm = _dot(tri, jnp.concatenate([hi, mid], axis=1))
        b_s[r, :] = bhm[:, 0:dk] + bhm[:, dk:] + _dot(tri, lo)

    def stage2(blk):
        r = rows(blk)
        b3 = b_s[r, :].reshape(nc, cs, dk)
        bmid = b3[:, cs // 2 - 1:cs // 2, :]
        blast = b3[:, cs - 1:cs, :]
        q3 = (q_ref[0, r, :].astype(F32) * dk ** -0.5).reshape(nc, cs, dk)
        k3 = k_s[r, :].reshape(nc, cs, dk)
        qin_s[r, :] = (q3 * jnp.exp2(b3 - bmid)).reshape(rb, dk).astype(BF16)
        kin_s[r, :] = (k3 * jnp.exp2(bmid - b3)).reshape(rb, dk).astype(BF16)
        ku_s[r, :] = (k3 * jnp.exp2(blast - b3)).reshape(rb, dk).astype(BF16)
        qb_s[r, :] = (q3 * jnp.exp2(b3)).reshape(rb, dk).astype(BF16)
        dec_s[pl.ds(blk * nc, nc)] = jnp.exp2(blast)

    def stage3(blk):
        r = rows(blk)
        a_s[r, :] = jnp.where(mask, _dot_nt(qin_s[r, :], kin_s[r, :]), 0.0).astype(BF16)
        vt = i_ref[0, r, :].astype(F32).T.astype(BF16)
        ku = ku_s[r, :]
        ut = _dot(vt, jnp.concatenate([ku * sel for sel in chunk_sel], axis=1))
        for c in range(nc):
            u_s[blk * nc + c] = ut[:, c * dk:(c + 1) * dk]

    def stage4(blk):
        r = rows(blk)
        oi_s[r, :] = _dot(a_s[r, :], i_ref[0, r, :])
        st = st_s[...]
        for c in range(nc):
            sp_s[blk * nc + c] = st.astype(BF16)
            st = st * dec_s[blk * nc + c] + u_s[blk * nc + c]
        st_s[...] = st

    def stage5(blk):
        r = rows(blk)
        inter = [_dot_nt(qb_s[pl.ds(pl.multiple_of(blk * rb + c * cs, cs), cs), :], sp_s[blk * nc + c])
                 for c in range(nc)]
        o = oi_s[r, :] + jnp.concatenate(inter, axis=0)
        o = _rms(o, og_ref[...]) * _silu(g_ref[0, r, :].astype(F32))
        o_ref[0, r, :] = o.astype(BF16)

    st_s[...] = jnp.zeros_like(st_s)
    _software_pipeline([stage1, stage2, stage3, stage4, stage5], seq // rb)


def _hgrn(proj3, lb_logits, onorm_g, layer):
    bsz, seq, _ = proj3.shape
    nh = HGRN_HEADS

    hps = HGRN_HEADS_PER_STEP
    width = hps * HGRN_DK

    n_col_steps = nh // hps
    nl = lb_logits.shape[0]
    return pl.pallas_call(
        functools.partial(_hgrn_kernel, layer=layer, n_col_steps=n_col_steps, n_steps=bsz * n_col_steps),
        grid=(bsz, n_col_steps),
        in_specs=[pl.BlockSpec(memory_space=pl.ANY),
                  pl.BlockSpec((nl, width), lambda b, h: (0, h)),
                  pl.BlockSpec((1, HGRN_DK), lambda b, h: (0, 0))],
        out_specs=pl.BlockSpec((1, seq, width), lambda b, h: (b, 0, h)),
        out_shape=jax.ShapeDtypeStruct((bsz, seq, HGRN_WIDTH), BF16),
        scratch_shapes=[pltpu.VMEM((HGRN_RING, 4, seq, width), BF16),
                        pltpu.SemaphoreType.DMA((HGRN_RING, 4)),
                        pltpu.VMEM((seq, HGRN_DK), BF16),
                        pltpu.VMEM((seq // HGRN_CHUNK, HGRN_DK, HGRN_DK), F32),
                        pltpu.VMEM((seq // HGRN_CHUNK, HGRN_DK, HGRN_DK), BF16),
                        pltpu.VMEM((seq // HGRN_CHUNK, 1, HGRN_DK), F32),
                        pltpu.VMEM((seq, HGRN_DK), F32),
                        pltpu.VMEM((seq, HGRN_DK), F32), pltpu.VMEM((seq, HGRN_DK), F32),
                        pltpu.VMEM((seq, HGRN_DK), BF16), pltpu.VMEM((seq, HGRN_DK), BF16),
                        pltpu.VMEM((seq, HGRN_DK), BF16), pltpu.VMEM((seq, min(HG_ROWS, seq)), BF16),
                        pltpu.VMEM((HGRN_DK, HGRN_DK), F32)],
        compiler_params=_cparams(("arbitrary", "arbitrary")),
        name="hgrn",
    )(proj3, lb_logits, onorm_g)


UP_ROWS = 1024


def _up_kernel(p_ref, pos_ref, wq_ref, wkv_ref, qag_ref, kvag_ref, qg_ref, kg_ref, freq_ref,
               q_ref, k_ref, v_ref):
    half = MLA_ROPE // 2
    p = p_ref[...].astype(F32)
    qa_n = _rms(p[:, 0:Q_LORA], qag_ref[...]).astype(BF16)
    kva_n = _rms(p[:, Q_LORA:Q_LORA + KV_LORA], kvag_ref[...]).astype(BF16)
    kpe_t = p[:, Q_LORA + KV_LORA:Q_LORA + KV_LORA + LANES].T[0:MLA_ROPE]
    ang = freq_ref[...] * pos_ref[...].astype(F32)
    cos = jnp.cos(ang)
    sin = jnp.sin(ang)

    def rope(x):
        x1, x2 = x[0:half], x[half:]
        return jnp.concatenate([x1 * cos - x2 * sin, x2 * cos + x1 * sin], axis=0)

    kg = kg_ref[...]
    kpe_ss = jnp.sum(kpe_t * kpe_t, axis=0, keepdims=True)
    kpe_rot = rope(kpe_t * kg[MLA_NOPE:])
    qg = qg_ref[...] * (MLA_QK ** -0.5 * LOG2E)
    zpad = jnp.zeros((MLA_QPAD - MLA_QK, p.shape[0]), F32)
    ones_row = (lax.broadcasted_iota(jnp.int32, (MLA_VT_ROWS - MLA_V, p.shape[0]), 0) == 0).astype(BF16)
    for h in range(MLA_HEADS):
        rows = slice(h * MLA_QPAD, (h + 1) * MLA_QPAD)
        qt = _dot_nt(wq_ref[rows, :], qa_n)
        qn = qt[0:MLA_QK] * lax.rsqrt(jnp.sum(qt * qt, axis=0, keepdims=True) / MLA_QK + EPS) * qg
        q_ref[0, h] = jnp.concatenate([qn[0:MLA_NOPE], rope(qn[MLA_NOPE:]), zpad], axis=0).astype(BF16)
        kvt = _dot_nt(wkv_ref[rows, :], kva_n)
        kn = kvt[0:MLA_NOPE]
        rk = lax.rsqrt((jnp.sum(kn * kn, axis=0, keepdims=True) + kpe_ss) / MLA_QK + EPS)
        kt = jnp.concatenate([kn * rk * kg[0:MLA_NOPE], kpe_rot * rk, zpad], axis=0)
        k_ref[0, h] = kt.T.astype(BF16)
        v_ref[0, h, 0:MLA_V] = kvt[MLA_NOPE:].astype(BF16)
        v_ref[0, h, MLA_V:] = ones_row


def _mla_up(proj3, pos3, wq_t, wkv_t, qag, kvag, qg, kg, freq):
    bsz, seq, _ = proj3.shape
    tm = min(UP_ROWS, seq)
    nh = MLA_HEADS
    mla_block = 4 * HGRN_WIDTH // 1024

    def const(shape):
        return pl.BlockSpec(shape, lambda b, i: (0,) * len(shape))

    return pl.pallas_call(
        _up_kernel,
        grid=(bsz, seq // tm),
        in_specs=[pl.BlockSpec((None, tm, 1024), lambda b, i: (b, i, mla_block)),
                  pl.BlockSpec((None, 1, tm), lambda b, i: (b, 0, i)),
                  const(wq_t.shape), const(wkv_t.shape), const(qag.shape), const(kvag.shape),
                  const(qg.shape), const(kg.shape), const(freq.shape)],
        out_specs=[pl.BlockSpec((1, nh, MLA_QPAD, tm), lambda b, i: (b, 0, 0, i)),
                   pl.BlockSpec((1, nh, tm, MLA_QPAD), lambda b, i: (b, 0, i, 0)),
                   pl.BlockSpec((1, nh, MLA_VT_ROWS, tm), lambda b, i: (b, 0, 0, i))],
        out_shape=[jax.ShapeDtypeStruct((bsz, nh, MLA_QPAD, seq), BF16),
                   jax.ShapeDtypeStruct((bsz, nh, seq, MLA_QPAD), BF16),
                   jax.ShapeDtypeStruct((bsz, nh, MLA_VT_ROWS, seq), BF16)],
        compiler_params=_cparams(("arbitrary", "arbitrary")),
        name="mla_up",
    )(proj3, pos3, wq_t, wkv_t, qag, kvag, qg, kg, freq)


ATT_T = 256
ATT_HEADS_PER_STEP = 2


def _attn_kernel(qt_ref, k_ref, vt_ref, g_ref, o_ref):
    for hh in range(k_ref.shape[1]):
        _attn_head(qt_ref.at[:, pl.ds(hh, 1)], k_ref.at[:, pl.ds(hh, 1)], vt_ref.at[:, pl.ds(hh, 1)], g_ref,
                   o_ref.at[:, :, pl.ds(hh * MLA_V, MLA_V)])


def _attn_head(qt_ref, k_ref, vt_ref, g_ref, o_ref):
    seq = k_ref.shape[2]
    t = min(ATT_T, seq)
    key = lax.broadcasted_iota(jnp.int32, (t, t), 0)
    qry = lax.broadcasted_iota(jnp.int32, (t, t), 1)
    causal = key <= qry
    neg = jnp.finfo(F32).min

    def widen(x, off, fill):
        return x if off == 0 else jnp.concatenate([jnp.full((x.shape[0], off), fill, x.dtype), x], axis=1)

    scores = []
    m = None
    for off in range(0, seq, t):
        s = _dot(k_ref[0, 0, off:off + t, :], qt_ref[0, 0, :, off:])
        diag = jnp.where(causal, s[:, 0:t], neg)
        s = diag if off + t == seq else jnp.concatenate([diag, s[:, t:]], axis=1)
        scores.append(s)
        blk_max = widen(jnp.max(s, axis=0, keepdims=True), off, neg)
        m = blk_max if m is None else jnp.maximum(m, blk_max)
    ot = None
    for off, s in zip(range(0, seq, t), scores):
        p = jnp.exp2(s - m[:, off:])
        pv = widen(_dot(vt_ref[0, 0, :, off:off + t], p.astype(BF16)), off, 0.0)
        ot = pv if ot is None else ot + pv
    ot = ot[0:MLA_V] * (1.0 / ot[MLA_V:MLA_V + 1])
    ot = ot * lax.rsqrt(jnp.mean(ot * ot, axis=0, keepdims=True) + EPS) * g_ref[...]
    o_ref[0, :, :] = ot.T.astype(BF16)


def _attention(q, k, v, g):
    bsz, nh, seq, _ = k.shape
    hps = ATT_HEADS_PER_STEP
    return pl.pallas_call(
        _attn_kernel,
        grid=(bsz, nh // hps),
        in_specs=[pl.BlockSpec((1, hps, MLA_QPAD, seq), lambda b, h: (b, h, 0, 0)),
                  pl.BlockSpec((1, hps, seq, MLA_QPAD), lambda b, h: (b, h, 0, 0)),
                  pl.BlockSpec((1, hps, MLA_VT_ROWS, seq), lambda b, h: (b, h, 0, 0)),
                  pl.BlockSpec((MLA_V, 1), lambda b, h: (0, 0))],
        out_specs=pl.BlockSpec((1, seq, hps * MLA_V), lambda b, h: (b, 0, h)),
        out_shape=jax.ShapeDtypeStruct((bsz, seq, nh * MLA_V), BF16),
        compiler_params=_cparams(("arbitrary", "arbitrary")),
        name="attn",
    )(q, k, v, g)


def _out_kernel(oa_ref, ob_ref, x_ref, wa_ref, wb_ref, g1_ref, n2g_ref, sh2_ref, sc2_ref,
                wr_ref, br_ref, x1_ref, h2_ref, lg_ref, mix_s):
    @pl.when(pl.program_id(0) == 0)
    def _():
        mix_s[...] = jnp.zeros_like(mix_s)

    mix_next = _dot(oa_ref[...], wa_ref[...]) + _dot(ob_ref[...], wb_ref[...])
    x1 = x_ref[...] + g1_ref[0] * mix_s[...]
    x1_ref[...] = x1
    h2 = _rms(x1, n2g_ref[...]) * (1.0 + sc2_ref[0]) + sh2_ref[0]
    tm = h2.shape[0]
    for s in range(TOK_SUBLANES):
        h2_ref[pl.ds(s, tm, stride=TOK_SUBLANES), :] = _pack_pair(
            h2[:, s * LANES:(s + 1) * LANES], h2[:, (s + TOK_SUBLANES) * LANES:(s + TOK_SUBLANES + 1) * LANES])
    lg = _dot(h2.astype(BF16), wr_ref[...]) + br_ref[...]
    lg_ref[...] = lg.T[0:ROUTE_ROWS, :]
    mix_s[...] = mix_next


def _out_proj(oa, ob, x2, wa, wb, g1, n2g, sh2, sc2, wr, br, seq):
    t, d = x2.shape
    tm = min(512, seq)
    n_tiles = t // tm

    def const(shape):
        return pl.BlockSpec(shape, lambda i: (0,) * len(shape))

    def prev(i):
        return jnp.maximum(i - 1, 0)

    def per_batch():
        return pl.BlockSpec((1, 1, d), lambda i: (prev(i) * tm // seq, 0, 0))

    return pl.pallas_call(
        _out_kernel,
        grid=(n_tiles + 1,),
        in_specs=[pl.BlockSpec((tm, HGRN_WIDTH), lambda i: (jnp.minimum(i, n_tiles - 1), 0)),
                  pl.BlockSpec((tm, HGRN_WIDTH), lambda i: (jnp.minimum(i, n_tiles - 1), 0)),
                  pl.BlockSpec((tm, d), lambda i: (prev(i), 0)),
                  const(wa.shape), const(wb.shape), per_batch(), const(n2g.shape),
                  per_batch(), per_batch(), const(wr.shape), const(br.shape)],
        out_specs=[pl.BlockSpec((tm, d), lambda i: (prev(i), 0)),
                   pl.BlockSpec((tm * TOK_SUBLANES, LANES), lambda i: (prev(i), 0)),
                   pl.BlockSpec((ROUTE_ROWS, tm), lambda i: (0, prev(i)))],
        out_shape=[jax.ShapeDtypeStruct((t, d), F32),
                   jax.ShapeDtypeStruct((t * TOK_SUBLANES, LANES), jnp.uint32),
                   jax.ShapeDtypeStruct((ROUTE_ROWS, t), F32)],
        scratch_shapes=[pltpu.VMEM((tm, d), F32)],
        compiler_params=_cparams(("arbitrary",)),
        name="out_proj",
    )(oa, ob, x2, wa, wb, g1, n2g, sh2, sc2, wr, br)


def _route_kernel(lg_ref, tri_ref, ri_ref, rw_ref, cnt_ref, carry_s):
    step = pl.program_id(0)

    @pl.when(step == 0)
    def _():
        carry_s[...] = jnp.zeros_like(carry_s)

    lg = lg_ref[...]
    tr = lg.shape[1]
    epg = EXPERTS_PER_GROUP
    gl = lg[N_EXPERTS:N_EXPERTS + N_GROUPS, :]
    row_g = lax.broadcasted_iota(jnp.int32, (N_GROUPS, tr), 0)
    gmax = jnp.max(gl, axis=0, keepdims=True)
    g_sel = jnp.min(jnp.where(gl == gmax, row_g, N_GROUPS), axis=0, keepdims=True)
    p_group = 1.0 / jnp.sum(jnp.exp(gl - gmax), axis=0, keepdims=True)

    e_in = lg[0:epg, :]
    for g in range(1, N_GROUPS):
        e_in = jnp.where(g_sel == g, lg[g * epg:(g + 1) * epg, :], e_in)
    row_e = lax.broadcasted_iota(jnp.int32, (epg, tr), 0)
    top1 = jnp.max(e_in, axis=0, keepdims=True)
    i1 = jnp.min(jnp.where(e_in == top1, row_e, epg), axis=0, keepdims=True)
    rest = jnp.where(row_e == i1, -jnp.inf, e_in)
    top2 = jnp.max(rest, axis=0, keepdims=True)
    i2 = jnp.min(jnp.where(rest == top2, row_e, epg), axis=0, keepdims=True)
    e2w = jnp.exp(top2 - top1)
    w1 = p_group / (1.0 + e2w)
    w2 = p_group * e2w / (1.0 + e2w)
    ex1 = g_sel * epg + i1
    ex2 = g_sel * epg + i2

    row_x = lax.broadcasted_iota(jnp.int32, (N_EXPERTS, tr), 0)
    oh1 = row_x == ex1
    oh2 = row_x == ex2
    oh = jnp.logical_or(oh1, oh2)
    before = _dot(oh.astype(BF16), tri_ref[...]) + carry_s[:, 0:1]
    rank1 = jnp.sum(jnp.where(oh1, before, 0.0), axis=0, keepdims=True)
    rank2 = jnp.sum(jnp.where(oh2, before, 0.0), axis=0, keepdims=True)
    carry_s[...] = carry_s[...] + jnp.sum(oh.astype(F32), axis=1, keepdims=True)

    zi = jnp.zeros((4, tr), jnp.int32)
    ri_ref[...] = jnp.concatenate([ex1, ex2, rank1.astype(jnp.int32), rank2.astype(jnp.int32), zi], axis=0)
    rw_ref[...] = jnp.concatenate([w1, w2, jnp.zeros((6, tr), F32)], axis=0)
    cnt_ref[...] = carry_s[...].astype(jnp.int32)


def _route(lg_t, tri):
    t = lg_t.shape[1]
    tr = tri.shape[0]
    return pl.pallas_call(
        _route_kernel,
        grid=(t // tr,),
        in_specs=[pl.BlockSpec((ROUTE_ROWS, tr), lambda i: (0, i)),
                  pl.BlockSpec((tr, tr), lambda i: (0, 0))],
        out_specs=[pl.BlockSpec((8, tr), lambda i: (0, i)),
                   pl.BlockSpec((8, tr), lambda i: (0, i)),
                   pl.BlockSpec((N_EXPERTS, LANES), lambda i: (0, 0))],
        out_shape=[jax.ShapeDtypeStruct((8, t), jnp.int32),
                   jax.ShapeDtypeStruct((8, t), F32),
                   jax.ShapeDtypeStruct((N_EXPERTS, LANES), jnp.int32)],
        scratch_shapes=[pltpu.VMEM((N_EXPERTS, LANES), F32)],
        compiler_params=_cparams(("arbitrary",)),
        name="route",
    )(lg_t, tri)


def _moe_kernel(rd_ref, be_ref, pe_ref, nb_ref, x_ref, wg_ref, wu_ref, wd_ref, y_ref,
                ys, wg_f, wu_f, wd_f, wg_s, wu_s, wd_s, ssem, wsem):
    b = pl.program_id(0)
    last = pl.num_programs(0) - 1
    nb = nb_ref[0]
    tm = MOE_BLOCK
    ts = TOK_SUBLANES
    weights = ((wg_ref, wg_f, wg_s), (wu_ref, wu_f, wu_s), (wd_ref, wd_f, wd_s))

    def fetch(e):
        return [pltpu.make_async_copy(src.at[e], stage, wsem.at[k]) for k, (src, stage, _) in enumerate(weights)]

    def scatter(blk):
        base = blk * tm
        for j in range(tm):
            dst = pl.multiple_of(rd_ref[base + j] * ts, ts)
            pltpu.make_async_copy(ys.at[pl.ds(j * ts, ts), :], y_ref.at[pl.ds(dst, ts), :], ssem).start()

    def wait_scatter():
        pltpu.make_async_copy(ys, y_ref.at[pl.ds(0, tm * ts), :], ssem).wait()

    @pl.when(b == 0)
    def _():
        for copy in fetch(be_ref[0]):
            copy.start()
        ys[...] = jnp.zeros_like(ys)
        spare = pltpu.make_async_copy(ys, y_ref.at[pl.ds(y_ref.shape[0] - tm * ts, tm * ts), :], ssem)
        spare.start()
        spare.wait()

    expert = be_ref[b]
    first_of_expert = jnp.logical_or(b == 0, expert != be_ref[jnp.maximum(b - 1, 0)])

    @pl.when(jnp.logical_and(b < nb, first_of_expert))
    def _():
        for copy in fetch(0):
            copy.wait()
        for _, stage, dst in weights:
            dst[...] = stage[...].astype(BF16)
        next_blk = pe_ref[expert] // tm

        @pl.when(next_blk < nb)
        def _():
            for copy in fetch(be_ref[next_blk]):
                copy.start()

    @pl.when(b < nb)
    def _():
        scatter(jnp.maximum(b - 1, 0))
        parts = [_unpack_pair(x_ref[pl.ds(s, tm, stride=ts), :]) for s in range(ts)]
        x = jnp.concatenate([p[0] for p in parts] + [p[1] for p in parts], axis=1).astype(BF16)
        hid = _silu(_dot(x, wg_s[...])) * _dot(x, wu_s[...])
        y = _dot(hid.astype(BF16), wd_s[...])
        wait_scatter()
        for s in range(ts):
            ys[pl.ds(s, tm, stride=ts), :] = _pack_pair(y[:, s * LANES:(s + 1) * LANES],
                                                        y[:, (s + ts) * LANES:(s + ts + 1) * LANES])

        @pl.when(b == last)
        def _():
            scatter(b)
            wait_scatter()

    @pl.when(b == nb)
    def _():
        scatter(b - 1)
        wait_scatter()


def _moe(row_dst, blk_e, pad_end, nb_real, x_sorted, wg, wu, wd, n_out_rows):
    n_blocks = blk_e.shape[0]
    d, de = wg.shape[1], wg.shape[2]
    tm = MOE_BLOCK
    hbm = pl.BlockSpec(memory_space=pl.ANY)
    return pl.pallas_call(
        _moe_kernel,
        grid_spec=pltpu.PrefetchScalarGridSpec(
            num_scalar_prefetch=4,
            grid=(n_blocks,),
            in_specs=[pl.BlockSpec((tm * TOK_SUBLANES, LANES),
                                   lambda b, rd, be, pe, nb: (jnp.minimum(b, nb[0] - 1), 0)),
                      hbm, hbm, hbm],
            out_specs=hbm,
            scratch_shapes=[pltpu.VMEM((tm * TOK_SUBLANES, LANES), jnp.uint32),
                            pltpu.VMEM((d, de), F32), pltpu.VMEM((d, de), F32), pltpu.VMEM((de, d), F32),
                            pltpu.VMEM((d, de), BF16), pltpu.VMEM((d, de), BF16), pltpu.VMEM((de, d), BF16),
                            pltpu.SemaphoreType.DMA(()), pltpu.SemaphoreType.DMA((3,))]),
        out_shape=jax.ShapeDtypeStruct((n_out_rows * TOK_SUBLANES, LANES), jnp.uint32),
        compiler_params=_cparams(("arbitrary",)),
        name="moe",
    )(row_dst, blk_e, pad_end, nb_real, x_sorted, wg, wu, wd)


def _dispatch_kernel(e1_ref, e2_ref, r1_ref, r2_ref, ps_ref, pe_ref, cnt_ref, be_ref, nb_ref,
                     h_ref, x_ref, rd_ref, zbuf, sem, zsem):
    i = pl.program_id(0)
    ts = TOK_SUBLANES
    td = h_ref.shape[0] // ts
    n_tok = e1_ref.shape[0]
    base = i * td
    blk_rows = MOE_BLOCK * ts

    def zero_fill(blk):
        return pltpu.make_async_copy(
            zbuf, x_ref.at[pl.ds(pl.multiple_of(blk * blk_rows, blk_rows), blk_rows), :], zsem)

    @pl.when(i == 0)
    def _():
        zbuf[...] = jnp.zeros_like(zbuf)

        def fill(blk, n_fills):
            e = be_ref[blk]
            partial_last = jnp.logical_and(blk == pe_ref[e] // MOE_BLOCK - 1, cnt_ref[e] % MOE_BLOCK != 0)
            needs_fill = jnp.logical_or(blk >= nb_ref[0], partial_last)

            @pl.when(needs_fill)
            def _():
                zero_fill(blk).start()
            return n_fills + needs_fill.astype(jnp.int32)

        n_fills = lax.fori_loop(0, be_ref.shape[0], fill, 0)

        def init(blk, carry):
            for j in range(MOE_BLOCK):
                rd_ref[blk * MOE_BLOCK + j] = 2 * n_tok + j
            return carry

        lax.fori_loop(0, rd_ref.shape[0] // MOE_BLOCK, init, 0)

        def drain(k, carry):
            zero_fill(0).wait()
            return carry

        lax.fori_loop(0, n_fills, drain, 0)

    def start(t, carry):
        src = h_ref.at[pl.ds(pl.multiple_of(t * ts, ts), ts), :]
        tok = base + t
        for slot, (e_ref, r_ref) in enumerate(((e1_ref, r1_ref), (e2_ref, r2_ref))):
            row = ps_ref[e_ref[tok]] + r_ref[tok]
            pltpu.make_async_copy(src, x_ref.at[pl.ds(pl.multiple_of(row * ts, ts), ts), :], sem).start()
            rd_ref[row] = 2 * tok + slot
        return carry

    lax.fori_loop(0, td, start, 0, unroll=8)
    for _ in range(2):
        pltpu.make_async_copy(h_ref, x_ref.at[pl.ds(0, td * ts), :], sem).wait()


def _dispatch(ri, pad_start, pad_end, counts, blk_e, nb_real, h2p, n_rows):
    t = ri.shape[1]
    td = min(512, t)
    ts = TOK_SUBLANES
    return pl.pallas_call(
        _dispatch_kernel,
        grid_spec=pltpu.PrefetchScalarGridSpec(
            num_scalar_prefetch=9,
            grid=(t // td,),
            in_specs=[pl.BlockSpec((td * ts, LANES), lambda i, *_: (i, 0))],
            out_specs=[pl.BlockSpec(memory_space=pl.ANY), pl.BlockSpec(memory_space=pltpu.SMEM)],
            scratch_shapes=[pltpu.VMEM((MOE_BLOCK * ts, LANES), jnp.uint32),
                            pltpu.SemaphoreType.DMA(()), pltpu.SemaphoreType.DMA(())]),
        out_shape=[jax.ShapeDtypeStruct((n_rows * ts, LANES), jnp.uint32),
                   jax.ShapeDtypeStruct((n_rows,), jnp.int32)],
        compiler_params=_cparams(("arbitrary",)),
        name="dispatch",
    )(ri[0], ri[1], ri[2], ri[3], pad_start, pad_end, counts, blk_e, nb_real, h2p)


def _moe_plan(counts, t):
    tm = MOE_BLOCK
    n_blocks = 2 * t // tm + N_EXPERTS
    padded = ((counts + tm - 1) // tm) * tm
    pad_end = jnp.cumsum(padded)
    pad_start = pad_end - padded
    blk = jnp.arange(n_blocks, dtype=jnp.int32)
    blk_e = jnp.minimum(jnp.sum(pad_end[None, :] <= (blk * tm)[:, None], axis=1), N_EXPERTS - 1).astype(jnp.int32)
    nb_real = (pad_end[-1:] // tm).astype(jnp.int32)
    return pad_start, pad_end, blk_e, nb_real, n_blocks * tm


COMBINE_ROWS = 64


def _combine_kernel(x1_ref, g2_ref, rw_ref, y_ref, o_ref, w_s):
    ts = TOK_SUBLANES
    rows = min(COMBINE_ROWS, x1_ref.shape[0])
    w_s[...] = rw_ref[...].T

    def chunk(c, carry):
        r = pl.ds(pl.multiple_of(c * rows, rows), rows)
        w1 = w_s[r, 0:1]
        w2 = w_s[r, 1:2]
        tile0 = pl.multiple_of(c * rows * 2 * ts, rows * 2 * ts)
        for s in range(ts):
            a_lo, a_hi = _unpack_pair(y_ref[pl.ds(tile0 + s, rows, stride=2 * ts), :])
            b_lo, b_hi = _unpack_pair(y_ref[pl.ds(tile0 + ts + s, rows, stride=2 * ts), :])
            for col, ya, yb in ((s, a_lo, b_lo), (s + ts, a_hi, b_hi)):
                cols = slice(col * LANES, (col + 1) * LANES)
                o_ref[r, cols] = x1_ref[r, cols] + g2_ref[0][:, cols] * (w1 * ya + w2 * yb)
        return carry

    lax.fori_loop(0, x1_ref.shape[0] // rows, chunk, 0)


def _combine(x1, g2, rw, y2, seq):
    t, d = x1.shape
    tc = min(512, seq)
    return pl.pallas_call(
        _combine_kernel,
        grid=(t // tc,),
        in_specs=[pl.BlockSpec((tc, d), lambda i: (i, 0)),
                  pl.BlockSpec((1, 1, d), lambda i: (i * tc // seq, 0, 0)),
                  pl.BlockSpec((rw.shape[0], tc), lambda i: (0, i)),
                  pl.BlockSpec((tc * 2 * TOK_SUBLANES, LANES), lambda i: (i, 0))],
        out_specs=pl.BlockSpec((tc, d), lambda i: (i, 0)),
        out_shape=jax.ShapeDtypeStruct((t, d), F32),
        scratch_shapes=[pltpu.VMEM((tc, rw.shape[0]), F32)],
        compiler_params=_cparams(("arbitrary",)),
        name="combine",
    )(x1, g2, rw, y2)


def _q_up_layout(w_q_up):
    w = w_q_up.reshape(Q_LORA, MLA_HEADS, MLA_QK)
    w = jnp.pad(w, ((0, 0), (0, 0), (0, MLA_QPAD - MLA_QK)))
    return w.reshape(Q_LORA, MLA_HEADS * MLA_QPAD).T.astype(BF16)


def _pad_lanes(g, width):
    return jnp.pad(g, (0, width - g.shape[0])).reshape(1, width)


def kernel(x, c, positions, w_ada, b_ada, norm1_g, w_in, hgrn_lb_logits, hgrn_onorm_g, q_a_norm_g, w_q_up,
           kv_a_norm_g, w_kv_up, q_norm_g, k_norm_g, attn_onorm_g, w_out, norm2_g, w_group, b_group,
           w_router, b_router, w_gate, w_up, w_down):
    bsz, seq, d = x.shape
    t = bsz * seq
    depth = w_ada.shape[0]
    half = MLA_ROPE // 2
    inv_freq = ROPE_BASE ** (-jnp.arange(0, MLA_ROPE, 2, dtype=F32) / MLA_ROPE)
    freq = inv_freq.reshape(half, 1)
    pos3 = positions.reshape(bsz, 1, seq)
    tr = min(1024, t)
    tri = jnp.triu(jnp.ones((tr, tr), BF16), 1)

    x2 = x.reshape(t, d)
    for l in range(depth):
        mod = _ada(c, w_ada[l], b_ada[l]).reshape(bsz, 6, 1, d)
        sh1, sc1, g1, sh2, sc2, g2 = (mod[:, i] for i in range(6))

        proj = _in_proj(x2, norm1_g[l].reshape(1, d), sh1, sc1, w_in[l].T, seq)
        proj3 = proj.reshape(bsz, seq, IN_COLS_PAD)

        o_a = _hgrn(proj3, hgrn_lb_logits, hgrn_onorm_g[l].reshape(1, HGRN_DK), l)

        q, k, v = _mla_up(proj3, pos3, _q_up_layout(w_q_up[l]), w_kv_up[l].T.astype(BF16),
                          q_a_norm_g[l].reshape(1, Q_LORA), kv_a_norm_g[l].reshape(1, KV_LORA),
                          q_norm_g[l].reshape(MLA_QK, 1), k_norm_g[l].reshape(MLA_QK, 1), freq)
        o_b = _attention(q, k, v, attn_onorm_g[l].reshape(MLA_V, 1))

        w_o = w_out[l].astype(BF16)
        wr = jnp.pad(jnp.concatenate([w_router[l], w_group[l]], axis=1),
                     ((0, 0), (0, LANES - N_EXPERTS - N_GROUPS))).astype(BF16)
        br = _pad_lanes(jnp.concatenate([b_router[l], b_group[l]]), LANES)
        x1, h2, lg_t = _out_proj(o_a.reshape(t, HGRN_WIDTH), o_b.reshape(t, HGRN_WIDTH), x2,
                                 w_o[:HGRN_WIDTH], w_o[HGRN_WIDTH:], g1, norm2_g[l].reshape(1, d),
                                 sh2, sc2, wr, br, seq)

        ri, rw, cnt = _route(lg_t, tri)
        counts = cnt[:, 0]
        pad_start, pad_end, blk_e, nb_real, n_rows = _moe_plan(counts, t)
        x_sorted, row_dst = _dispatch(ri, pad_start, pad_end, counts, blk_e, nb_real, h2, n_rows)
        y2 = _moe(row_dst, blk_e, pad_end, nb_real, x_sorted, w_gate[l], w_up[l], w_down[l], 2 * t + MOE_BLOCK)
        x2 = _combine(x1, g2, rw, y2, seq)
    return x2.reshape(bsz, seq, d)
```

```python
import functools

import jax
import jax.numpy as jnp
from jax import lax
from jax.experimental import pallas as pl
from jax.experimental.pallas import tpu as pltpu

F32 = jnp.float32
BF16 = jnp.bfloat16
EPS = 1e-6
LOG2E = 1.4426950408889634

D_MODEL = 2048
HGRN_WIDTH = 1024
HGRN_DK = 128
HGRN_HEADS = 8
HGRN_CHUNK = 64
MLA_HEADS = 8
MLA_NOPE = 128
MLA_ROPE = 64
MLA_QK = MLA_NOPE + MLA_ROPE
MLA_V = 128
MLA_VT_ROWS = MLA_V + 16
MLA_QPAD = 256
Q_LORA = 512
KV_LORA = 256
ROPE_BASE = 10000.0
IN_COLS = 4 * HGRN_WIDTH + Q_LORA + KV_LORA + MLA_ROPE
IN_COLS_PAD = 5120
N_GROUPS = 4
EXPERTS_PER_GROUP = 8
N_EXPERTS = 32
D_EXPERT = 512
ROUTE_ROWS = 40
MOE_BLOCK = 256
LANES = 128
TOK_SUBLANES = 8
VMEM_LIMIT = 56 * 1024 * 1024


def _cparams(sem):
    return pltpu.CompilerParams(dimension_semantics=sem, vmem_limit_bytes=VMEM_LIMIT)


def _dot(a, b):
    return jnp.dot(a, b, preferred_element_type=F32)


def _dot_nt(a, b):
    return lax.dot_general(a, b, (((1,), (1,)), ((), ())), preferred_element_type=F32)


def _rms(x, g):
    return x * lax.rsqrt(jnp.mean(x * x, axis=-1, keepdims=True) + EPS) * g


def _silu(x):
    return x * jax.nn.sigmoid(x)


def _pack_pair(lo, hi):
    lo_b = lax.bitcast_convert_type(lo.astype(BF16).astype(F32), jnp.uint32)
    hi_b = lax.bitcast_convert_type(hi.astype(BF16).astype(F32), jnp.uint32)
    return hi_b | (lo_b >> 16)


def _unpack_pair(w):
    lo = lax.bitcast_convert_type(w << 16, F32)
    hi = lax.bitcast_convert_type(w & jnp.uint32(0xFFFF0000), F32)
    return lo, hi


def _ada_kernel(c_ref, w_ref, b_ref, o_ref):
    ca = _silu(c_ref[...]).astype(BF16)
    o_ref[...] = _dot(ca, w_ref[...].astype(BF16)) + b_ref[...]


def _ada(c, w, b):
    bsz, d = c.shape
    n = w.shape[1]
    tn = 1024
    return pl.pallas_call(
        _ada_kernel,
        grid=(n // tn,),
        in_specs=[pl.BlockSpec((bsz, d), lambda j: (0, 0)),
                  pl.BlockSpec((d, tn), lambda j: (0, j)),
                  pl.BlockSpec((1, tn), lambda j: (0, j))],
        out_specs=pl.BlockSpec((bsz, tn), lambda j: (0, j)),
        out_shape=jax.ShapeDtypeStruct((bsz, n), F32),
        compiler_params=_cparams(("arbitrary",)),
        name="ada",
    )(c, w, b.reshape(1, n))


NORM_ROWS = 32


def _norm_kernel(x_ref, g_ref, sh_ref, sc_ref, h_ref):
    def body(c, carry):
        r = pl.ds(pl.multiple_of(c * NORM_ROWS, NORM_ROWS), NORM_ROWS)
        h = _rms(x_ref[r, :], g_ref[...]) * (1.0 + sc_ref[0]) + sh_ref[0]
        h_ref[r, :] = h.astype(BF16)
        return carry
    lax.fori_loop(0, x_ref.shape[0] // NORM_ROWS, body, 0, unroll=2)


def _norm_mod(x2, g, sh, sc, seq, n_rows):
    t, d = x2.shape
    tm = min(1024, seq, n_rows)
    return pl.pallas_call(
        _norm_kernel,
        grid=(n_rows // tm,),
        in_specs=[pl.BlockSpec((tm, d), lambda i: (i, 0)),
                  pl.BlockSpec((1, d), lambda i: (0, 0)),
                  pl.BlockSpec((1, 1, d), lambda i: (i * tm // seq, 0, 0)),
                  pl.BlockSpec((1, 1, d), lambda i: (i * tm // seq, 0, 0))],
        out_specs=pl.BlockSpec((tm, d), lambda i: (i, 0)),
        out_shape=jax.ShapeDtypeStruct((n_rows, d), BF16),
        compiler_params=_cparams(("arbitrary",)),
        name="norm_mod",
    )(x2, g, sh, sc)


IN_NORM_CHUNKS = 4


def _in_kernel(h0_ref, x_ref, g_ref, sh_ref, sc_ref, w_ref, o_ref, h_s):
    i = pl.program_id(0)
    j = pl.program_id(1)
    slot = i % 2

    @pl.when(jnp.logical_and(i == 0, j == 0))
    def _():
        h_s[0] = h0_ref[...]

    o_ref[...] = _dot_nt(h_s[slot], w_ref[...]).astype(BF16)
    rows = x_ref.shape[0] // IN_NORM_CHUNKS
    first = jnp.minimum(j, IN_NORM_CHUNKS - 1) * rows
    gain = g_ref[...]
    scale = 1.0 + sc_ref[0]
    shift = sh_ref[0]
    for c in range(rows // NORM_ROWS):
        r = pl.ds(pl.multiple_of(first + c * NORM_ROWS, NORM_ROWS), NORM_ROWS)
        h_s[1 - slot, r, :] = (_rms(x_ref[r, :], gain) * scale + shift).astype(BF16)


def _in_proj(x2, g, sh, sc, w_t, seq):
    t, d = x2.shape
    tn = 1024
    tm = min(1024, seq)
    n_tiles = t // tm
    w_b = jnp.pad(w_t, ((0, IN_COLS_PAD - IN_COLS), (0, 0))).astype(BF16)
    h0 = _norm_mod(x2, g, sh, sc, seq, tm)

    def nxt(i):
        return jnp.minimum(i + 1, n_tiles - 1)

    return pl.pallas_call(
        _in_kernel,
        grid=(n_tiles, IN_COLS_PAD // tn),
        in_specs=[pl.BlockSpec((tm, d), lambda i, j: (0, 0)),
                  pl.BlockSpec((tm, d), lambda i, j: (nxt(i), 0)),
                  pl.BlockSpec((1, d), lambda i, j: (0, 0)),
                  pl.BlockSpec((1, 1, d), lambda i, j: (nxt(i) * tm // seq, 0, 0)),
                  pl.BlockSpec((1, 1, d), lambda i, j: (nxt(i) * tm // seq, 0, 0)),
                  pl.BlockSpec((tn, d), lambda i, j: (j, 0))],
        out_specs=pl.BlockSpec((tm, tn), lambda i, j: (i, j)),
        out_shape=jax.ShapeDtypeStruct((t, IN_COLS_PAD), BF16),
        scratch_shapes=[pltpu.VMEM((2, tm, d), BF16)],
        compiler_params=_cparams(("arbitrary", "arbitrary")),
        name="in_proj",
    )(h0, x2, g, sh, sc, w_b)


HG_ROWS = 256
HGRN_HEADS_PER_STEP = 2


PIPELINE_STATIC_BLOCKS = 8


def _software_pipeline(stages, n_blocks):
    depth = len(stages)

    def step(i, static):
        for k in reversed(range(depth)):
            if static and not 0 <= i - k < n_blocks:
                continue
            stages[k](i - k)

    if n_blocks <= PIPELINE_STATIC_BLOCKS:
        for i in range(n_blocks + depth - 1):
            step(i, True)
        return
    for i in range(depth - 1):
        step(i, True)

    def steady(i, carry):
        step(i, False)
        return carry

    lax.fori_loop(depth - 1, n_blocks, steady, 0)
    for i in range(n_blocks, n_blocks + depth - 1):
        step(i, True)


def _chunk_mask(n):
    row = lax.broadcasted_iota(jnp.int32, (n, n), 0)
    col = lax.broadcasted_iota(jnp.int32, (n, n), 1)
    return jnp.logical_and(row // HGRN_CHUNK == col // HGRN_CHUNK, col <= row)


def _hgrn_kernel(q_ref, f_ref, i_ref, g_ref, lbl_ref, og_ref, o_ref, *scratch, layer):
    for hh in range(q_ref.shape[2] // HGRN_DK):
        ls = pl.ds(hh * HGRN_DK, HGRN_DK)
        _hgrn_head(q_ref.at[:, :, ls], f_ref.at[:, :, ls], i_ref.at[:, :, ls], g_ref.at[:, :, ls],
                   lbl_ref.at[:, ls], og_ref, o_ref.at[:, :, ls], *scratch, layer=layer)


def _hgrn_head(q_ref, f_ref, i_ref, g_ref, lbl_ref, og_ref, o_ref,
               qb_s, u_s, sp_s, dec_s, oi_s, b_s, k_s, qin_s, kin_s, ku_s, a_s, st_s, *, layer):
    seq = q_ref.shape[1]
    cs = HGRN_CHUNK
    rb = min(HG_ROWS, seq)
    nc = rb // cs
    dk = HGRN_DK
    lg = lbl_ref[...]
    ex = jnp.exp(lg - jnp.max(lg, axis=0, keepdims=True))
    sm = ex / jnp.sum(ex, axis=0, keepdims=True)
    lb = jnp.sum(sm[0:layer + 1], axis=0, keepdims=True)
    mask = _chunk_mask(rb)
    tri = mask.astype(BF16)
    row_chunk = lax.broadcasted_iota(jnp.int32, (rb, dk), 0) // cs

    chunk_sel = [(row_chunk == c).astype(BF16) for c in range(nc)]

    def rows(blk):
        return pl.ds(pl.multiple_of(blk * rb, rb), rb)

    def stage1(blk):
        r = rows(blk)
        f = lb + (1.0 - lb) * jax.nn.sigmoid(f_ref[0, r, :].astype(F32))
        lf = jnp.log2(f)
        k_s[r, :] = 1.0 - f
        hi = lf.astype(BF16)
        r1 = lf - hi.astype(F32)
        mid = r1.astype(BF16)
        lo = (r1 - mid.astype(F32)).astype(BF16)
        bhm = _dot(tri, jnp.concatenate([hi, mid], axis=1))
        b_s[r, :] = bhm[:, 0:dk] + bhm[:, dk:] + _dot(tri, lo)

    def stage2(blk):
        r = rows(blk)
        b3 = b_s[r, :].reshape(nc, cs, dk)
        bmid = b3[:, cs // 2 - 1:cs // 2, :]
        blast = b3[:, cs - 1:cs, :]
        q3 = (q_ref[0, r, :].astype(F32) * dk ** -0.5).reshape(nc, cs, dk)
        k3 = k_s[r, :].reshape(nc, cs, dk)
        qin_s[r, :] = (q3 * jnp.exp2(b3 - bmid)).reshape(rb, dk).astype(BF16)
        kin_s[r, :] = (k3 * jnp.exp2(bmid - b3)).reshape(rb, dk).astype(BF16)
        ku_s[r, :] = (k3 * jnp.exp2(blast - b3)).reshape(rb, dk).astype(BF16)
        qb_s[r, :] = (q3 * jnp.exp2(b3)).reshape(rb, dk).astype(BF16)
        dec_s[pl.ds(blk * nc, nc)] = jnp.exp2(blast)

    def stage3(blk):
        r = rows(blk)
        a_s[r, :] = jnp.where(mask, _dot_nt(qin_s[r, :], kin_s[r, :]), 0.0).astype(BF16)
        vt = i_ref[0, r, :].astype(F32).T.astype(BF16)
        ku = ku_s[r, :]
        ut = _dot(vt, jnp.concatenate([ku * sel for sel in chunk_sel], axis=1))
        for c in range(nc):
            u_s[blk * nc + c] = ut[:, c * dk:(c + 1) * dk]

    def stage4(blk):
        r = rows(blk)
        oi_s[r, :] = _dot(a_s[r, :], i_ref[0, r, :])
        st = st_s[...]
        for c in range(nc):
            sp_s[blk * nc + c] = st.astype(BF16)
            st = st * dec_s[blk * nc + c] + u_s[blk * nc + c]
        st_s[...] = st

    def stage5(blk):
        r = rows(blk)
        inter = [_dot_nt(qb_s[pl.ds(pl.multiple_of(blk * rb + c * cs, cs), cs), :], sp_s[blk * nc + c])
                 for c in range(nc)]
        o = oi_s[r, :] + jnp.concatenate(inter, axis=0)
        o = _rms(o, og_ref[...]) * _silu(g_ref[0, r, :].astype(F32))
        o_ref[0, r, :] = o.astype(BF16)

    st_s[...] = jnp.zeros_like(st_s)
    _software_pipeline([stage1, stage2, stage3, stage4, stage5], seq // rb)


def _hgrn(proj3, lb_logits, onorm_g, layer):
    bsz, seq, _ = proj3.shape
    nh = HGRN_HEADS

    hps = HGRN_HEADS_PER_STEP
    width = hps * HGRN_DK

    def col(off):
        return pl.BlockSpec((1, seq, width), lambda b, h: (b, 0, off * (nh // hps) + h))

    nl = lb_logits.shape[0]
    return pl.pallas_call(
        functools.partial(_hgrn_kernel, layer=layer),
        grid=(bsz, nh // hps),
        in_specs=[col(0), col(1), col(2), col(3),
                  pl.BlockSpec((nl, width), lambda b, h: (0, h)),
                  pl.BlockSpec((1, HGRN_DK), lambda b, h: (0, 0))],
        out_specs=pl.BlockSpec((1, seq, width), lambda b, h: (b, 0, h)),
        out_shape=jax.ShapeDtypeStruct((bsz, seq, HGRN_WIDTH), BF16),
        scratch_shapes=[pltpu.VMEM((seq, HGRN_DK), BF16),
                        pltpu.VMEM((seq // HGRN_CHUNK, HGRN_DK, HGRN_DK), F32),
                        pltpu.VMEM((seq // HGRN_CHUNK, HGRN_DK, HGRN_DK), BF16),
                        pltpu.VMEM((seq // HGRN_CHUNK, 1, HGRN_DK), F32),
                        pltpu.VMEM((seq, HGRN_DK), F32),
                        pltpu.VMEM((seq, HGRN_DK), F32), pltpu.VMEM((seq, HGRN_DK), F32),
                        pltpu.VMEM((seq, HGRN_DK), BF16), pltpu.VMEM((seq, HGRN_DK), BF16),
                        pltpu.VMEM((seq, HGRN_DK), BF16), pltpu.VMEM((seq, min(HG_ROWS, seq)), BF16),
                        pltpu.VMEM((HGRN_DK, HGRN_DK), F32)],
        compiler_params=_cparams(("arbitrary", "arbitrary")),
        name="hgrn",
    )(proj3, proj3, proj3, proj3, lb_logits, onorm_g)


UP_ROWS = 1024


def _up_kernel(p_ref, pos_ref, wq_ref, wkv_ref, qag_ref, kvag_ref, qg_ref, kg_ref, freq_ref,
               q_ref, k_ref, v_ref):
    half = MLA_ROPE // 2
    p = p_ref[...].astype(F32)
    qa_n = _rms(p[:, 0:Q_LORA], qag_ref[...]).astype(BF16)
    kva_n = _rms(p[:, Q_LORA:Q_LORA + KV_LORA], kvag_ref[...]).astype(BF16)
    kpe_t = p[:, Q_LORA + KV_LORA:Q_LORA + KV_LORA + LANES].T[0:MLA_ROPE]
    ang = freq_ref[...] * pos_ref[...].astype(F32)
    cos = jnp.cos(ang)
    sin = jnp.sin(ang)

    def rope(x):
        x1, x2 = x[0:half], x[half:]
        return jnp.concatenate([x1 * cos - x2 * sin, x2 * cos + x1 * sin], axis=0)

    kg = kg_ref[...]
    kpe_ss = jnp.sum(kpe_t * kpe_t, axis=0, keepdims=True)
    kpe_rot = rope(kpe_t * kg[MLA_NOPE:])
    qg = qg_ref[...] * (MLA_QK ** -0.5 * LOG2E)
    zpad = jnp.zeros((MLA_QPAD - MLA_QK, p.shape[0]), F32)
    ones_row = (lax.broadcasted_iota(jnp.int32, (MLA_VT_ROWS - MLA_V, p.shape[0]), 0) == 0).astype(BF16)
    for h in range(MLA_HEADS):
        rows = slice(h * MLA_QPAD, (h + 1) * MLA_QPAD)
        qt = _dot_nt(wq_ref[rows, :], qa_n)
        qn = qt[0:MLA_QK] * lax.rsqrt(jnp.sum(qt * qt, axis=0, keepdims=True) / MLA_QK + EPS) * qg
        q_ref[0, h] = jnp.concatenate([qn[0:MLA_NOPE], rope(qn[MLA_NOPE:]), zpad], axis=0).astype(BF16)
        kvt = _dot_nt(wkv_ref[rows, :], kva_n)
        kn = kvt[0:MLA_NOPE]
        rk = lax.rsqrt((jnp.sum(kn * kn, axis=0, keepdims=True) + kpe_ss) / MLA_QK + EPS)
        kt = jnp.concatenate([kn * rk * kg[0:MLA_NOPE], kpe_rot * rk, zpad], axis=0)
        k_ref[0, h] = kt.T.astype(BF16)
        v_ref[0, h, 0:MLA_V] = kvt[MLA_NOPE:].astype(BF16)
        v_ref[0, h, MLA_V:] = ones_row


def _mla_up(proj3, pos3, wq_t, wkv_t, qag, kvag, qg, kg, freq):
    bsz, seq, _ = proj3.shape
    tm = min(UP_ROWS, seq)
    nh = MLA_HEADS
    mla_block = 4 * HGRN_WIDTH // 1024

    def const(shape):
        return pl.BlockSpec(shape, lambda b, i: (0,) * len(shape))

    return pl.pallas_call(
        _up_kernel,
        grid=(bsz, seq // tm),
        in_specs=[pl.BlockSpec((None, tm, 1024), lambda b, i: (b, i, mla_block)),
                  pl.BlockSpec((None, 1, tm), lambda b, i: (b, 0, i)),
                  const(wq_t.shape), const(wkv_t.shape), const(qag.shape), const(kvag.shape),
                  const(qg.shape), const(kg.shape), const(freq.shape)],
        out_specs=[pl.BlockSpec((1, nh, MLA_QPAD, tm), lambda b, i: (b, 0, 0, i)),
                   pl.BlockSpec((1, nh, tm, MLA_QPAD), lambda b, i: (b, 0, i, 0)),
                   pl.BlockSpec((1, nh, MLA_VT_ROWS, tm), lambda b, i: (b, 0, 0, i))],
        out_shape=[jax.ShapeDtypeStruct((bsz, nh, MLA_QPAD, seq), BF16),
                   jax.ShapeDtypeStruct((bsz, nh, seq, MLA_QPAD), BF16),
                   jax.ShapeDtypeStruct((bsz, nh, MLA_VT_ROWS, seq), BF16)],
        compiler_params=_cparams(("arbitrary", "arbitrary")),
        name="mla_up",
    )(proj3, pos3, wq_t, wkv_t, qag, kvag, qg, kg, freq)


ATT_T = 256
ATT_HEADS_PER_STEP = 2


def _attn_kernel(qt_ref, k_ref, vt_ref, g_ref, o_ref):
    for hh in range(k_ref.shape[1]):
        _attn_head(qt_ref.at[:, pl.ds(hh, 1)], k_ref.at[:, pl.ds(hh, 1)], vt_ref.at[:, pl.ds(hh, 1)], g_ref,
                   o_ref.at[:, :, pl.ds(hh * MLA_V, MLA_V)])


def _attn_head(qt_ref, k_ref, vt_ref, g_ref, o_ref):
    seq = k_ref.shape[2]
    t = min(ATT_T, seq)
    key = lax.broadcasted_iota(jnp.int32, (t, t), 0)
    qry = lax.broadcasted_iota(jnp.int32, (t, t), 1)
    causal = key <= qry
    neg = jnp.finfo(F32).min

    def widen(x, off, fill):
        return x if off == 0 else jnp.concatenate([jnp.full((x.shape[0], off), fill, x.dtype), x], axis=1)

    scores = []
    m = None
    for off in range(0, seq, t):
        s = _dot(k_ref[0, 0, off:off + t, :], qt_ref[0, 0, :, off:])
        diag = jnp.where(causal, s[:, 0:t], neg)
        s = diag if off + t == seq else jnp.concatenate([diag, s[:, t:]], axis=1)
        scores.append(s)
        blk_max = widen(jnp.max(s, axis=0, keepdims=True), off, neg)
        m = blk_max if m is None else jnp.maximum(m, blk_max)
    ot = None
    for off, s in zip(range(0, seq, t), scores):
        p = jnp.exp2(s - m[:, off:])
        pv = widen(_dot(vt_ref[0, 0, :, off:off + t], p.astype(BF16)), off, 0.0)
        ot = pv if ot is None else ot + pv
    ot = ot[0:MLA_V] * (1.0 / ot[MLA_V:MLA_V + 1])
    ot = ot * lax.rsqrt(jnp.mean(ot * ot, axis=0, keepdims=True) + EPS) * g_ref[...]
    o_ref[0, :, :] = ot.T.astype(BF16)


def _attention(q, k, v, g):
    bsz, nh, seq, _ = k.shape
    hps = ATT_HEADS_PER_STEP
    return pl.pallas_call(
        _attn_kernel,
        grid=(bsz, nh // hps),
        in_specs=[pl.BlockSpec((1, hps, MLA_QPAD, seq), lambda b, h: (b, h, 0, 0)),
                  pl.BlockSpec((1, hps, seq, MLA_QPAD), lambda b, h: (b, h, 0, 0)),
                  pl.BlockSpec((1, hps, MLA_VT_ROWS, seq), lambda b, h: (b, h, 0, 0)),
                  pl.BlockSpec((MLA_V, 1), lambda b, h: (0, 0))],
        out_specs=pl.BlockSpec((1, seq, hps * MLA_V), lambda b, h: (b, 0, h)),
        out_shape=jax.ShapeDtypeStruct((bsz, seq, nh * MLA_V), BF16),
        compiler_params=_cparams(("arbitrary", "arbitrary")),
        name="attn",
    )(q, k, v, g)


def _out_kernel(oa_ref, ob_ref, x_ref, wa_ref, wb_ref, g1_ref, n2g_ref, sh2_ref, sc2_ref,
                wr_ref, br_ref, x1_ref, h2_ref, lg_ref, mix_s):
    @pl.when(pl.program_id(0) == 0)
    def _():
        mix_s[...] = jnp.zeros_like(mix_s)

    mix_next = _dot(oa_ref[...], wa_ref[...]) + _dot(ob_ref[...], wb_ref[...])
    x1 = x_ref[...] + g1_ref[0] * mix_s[...]
    x1_ref[...] = x1
    h2 = _rms(x1, n2g_ref[...]) * (1.0 + sc2_ref[0]) + sh2_ref[0]
    tm = h2.shape[0]
    for s in range(TOK_SUBLANES):
        h2_ref[pl.ds(s, tm, stride=TOK_SUBLANES), :] = _pack_pair(
            h2[:, s * LANES:(s + 1) * LANES], h2[:, (s + TOK_SUBLANES) * LANES:(s + TOK_SUBLANES + 1) * LANES])
    lg = _dot(h2.astype(BF16), wr_ref[...]) + br_ref[...]
    lg_ref[...] = lg.T[0:ROUTE_ROWS, :]
    mix_s[...] = mix_next


def _out_proj(oa, ob, x2, wa, wb, g1, n2g, sh2, sc2, wr, br, seq):
    t, d = x2.shape
    tm = min(512, seq)
    n_tiles = t // tm

    def const(shape):
        return pl.BlockSpec(shape, lambda i: (0,) * len(shape))

    def prev(i):
        return jnp.maximum(i - 1, 0)

    def per_batch():
        return pl.BlockSpec((1, 1, d), lambda i: (prev(i) * tm // seq, 0, 0))

    return pl.pallas_call(
        _out_kernel,
        grid=(n_tiles + 1,),
        in_specs=[pl.BlockSpec((tm, HGRN_WIDTH), lambda i: (jnp.minimum(i, n_tiles - 1), 0)),
                  pl.BlockSpec((tm, HGRN_WIDTH), lambda i: (jnp.minimum(i, n_tiles - 1), 0)),
                  pl.BlockSpec((tm, d), lambda i: (prev(i), 0)),
                  const(wa.shape), const(wb.shape), per_batch(), const(n2g.shape),
                  per_batch(), per_batch(), const(wr.shape), const(br.shape)],
        out_specs=[pl.BlockSpec((tm, d), lambda i: (prev(i), 0)),
                   pl.BlockSpec((tm * TOK_SUBLANES, LANES), lambda i: (prev(i), 0)),
                   pl.BlockSpec((ROUTE_ROWS, tm), lambda i: (0, prev(i)))],
        out_shape=[jax.ShapeDtypeStruct((t, d), F32),
                   jax.ShapeDtypeStruct((t * TOK_SUBLANES, LANES), jnp.uint32),
                   jax.ShapeDtypeStruct((ROUTE_ROWS, t), F32)],
        scratch_shapes=[pltpu.VMEM((tm, d), F32)],
        compiler_params=_cparams(("arbitrary",)),
        name="out_proj",
    )(oa, ob, x2, wa, wb, g1, n2g, sh2, sc2, wr, br)


def _route_kernel(lg_ref, tri_ref, ri_ref, rw_ref, cnt_ref, carry_s):
    step = pl.program_id(0)

    @pl.when(step == 0)
    def _():
        carry_s[...] = jnp.zeros_like(carry_s)

    lg = lg_ref[...]
    tr = lg.shape[1]
    epg = EXPERTS_PER_GROUP
    gl = lg[N_EXPERTS:N_EXPERTS + N_GROUPS, :]
    row_g = lax.broadcasted_iota(jnp.int32, (N_GROUPS, tr), 0)
    gmax = jnp.max(gl, axis=0, keepdims=True)
    g_sel = jnp.min(jnp.where(gl == gmax, row_g, N_GROUPS), axis=0, keepdims=True)
    p_group = 1.0 / jnp.sum(jnp.exp(gl - gmax), axis=0, keepdims=True)

    e_in = lg[0:epg, :]
    for g in range(1, N_GROUPS):
        e_in = jnp.where(g_sel == g, lg[g * epg:(g + 1) * epg, :], e_in)
    row_e = lax.broadcasted_iota(jnp.int32, (epg, tr), 0)
    top1 = jnp.max(e_in, axis=0, keepdims=True)
    i1 = jnp.min(jnp.where(e_in == top1, row_e, epg), axis=0, keepdims=True)
    rest = jnp.where(row_e == i1, -jnp.inf, e_in)
    top2 = jnp.max(rest, axis=0, keepdims=True)
    i2 = jnp.min(jnp.where(rest == top2, row_e, epg), axis=0, keepdims=True)
    e2w = jnp.exp(top2 - top1)
    w1 = p_group / (1.0 + e2w)
    w2 = p_group * e2w / (1.0 + e2w)
    ex1 = g_sel * epg + i1
    ex2 = g_sel * epg + i2

    row_x = lax.broadcasted_iota(jnp.int32, (N_EXPERTS, tr), 0)
    oh1 = row_x == ex1
    oh2 = row_x == ex2
    oh = jnp.logical_or(oh1, oh2)
    before = _dot(oh.astype(BF16), tri_ref[...]) + carry_s[:, 0:1]
    rank1 = jnp.sum(jnp.where(oh1, before, 0.0), axis=0, keepdims=True)
    rank2 = jnp.sum(jnp.where(oh2, before, 0.0), axis=0, keepdims=True)
    carry_s[...] = carry_s[...] + jnp.sum(oh.astype(F32), axis=1, keepdims=True)

    zi = jnp.zeros((4, tr), jnp.int32)
    ri_ref[...] = jnp.concatenate([ex1, ex2, rank1.astype(jnp.int32), rank2.astype(jnp.int32), zi], axis=0)
    rw_ref[...] = jnp.concatenate([w1, w2, jnp.zeros((6, tr), F32)], axis=0)
    cnt_ref[...] = carry_s[...].astype(jnp.int32)


def _route(lg_t, tri):
    t = lg_t.shape[1]
    tr = tri.shape[0]
    return pl.pallas_call(
        _route_kernel,
        grid=(t // tr,),
        in_specs=[pl.BlockSpec((ROUTE_ROWS, tr), lambda i: (0, i)),
                  pl.BlockSpec((tr, tr), lambda i: (0, 0))],
        out_specs=[pl.BlockSpec((8, tr), lambda i: (0, i)),
                   pl.BlockSpec((8, tr), lambda i: (0, i)),
                   pl.BlockSpec((N_EXPERTS, LANES), lambda i: (0, 0))],
        out_shape=[jax.ShapeDtypeStruct((8, t), jnp.int32),
                   jax.ShapeDtypeStruct((8, t), F32),
                   jax.ShapeDtypeStruct((N_EXPERTS, LANES), jnp.int32)],
        scratch_shapes=[pltpu.VMEM((N_EXPERTS, LANES), F32)],
        compiler_params=_cparams(("arbitrary",)),
        name="route",
    )(lg_t, tri)


def _moe_kernel(rd_ref, be_ref, pe_ref, nb_ref, x_ref, wg_ref, wu_ref, wd_ref, y_ref,
                ys, wg_f, wu_f, wd_f, wg_s, wu_s, wd_s, ssem, wsem):
    b = pl.program_id(0)
    last = pl.num_programs(0) - 1
    nb = nb_ref[0]
    tm = MOE_BLOCK
    ts = TOK_SUBLANES
    weights = ((wg_ref, wg_f, wg_s), (wu_ref, wu_f, wu_s), (wd_ref, wd_f, wd_s))

    def fetch(e):
        return [pltpu.make_async_copy(src.at[e], stage, wsem.at[k]) for k, (src, stage, _) in enumerate(weights)]

    def scatter(blk, src):
        base = blk * tm
        for j in range(tm):
            dst = pl.multiple_of(rd_ref[base + j] * ts, ts)
            pltpu.make_async_copy(ys.at[src, pl.ds(j * ts, ts), :], y_ref.at[pl.ds(dst, ts), :], ssem).start()

    def wait_scatter():
        pltpu.make_async_copy(ys.at[0], y_ref.at[pl.ds(0, tm * ts), :], ssem).wait()

    slot = b % 2

    @pl.when(b == 0)
    def _():
        for copy in fetch(be_ref[0]):
            copy.start(priority=1)
        ys[...] = jnp.zeros_like(ys)
        spare = pltpu.make_async_copy(ys.at[0], y_ref.at[pl.ds(y_ref.shape[0] - tm * ts, tm * ts), :], ssem)
        spare.start()
        spare.wait()

    expert = be_ref[b]
    first_of_expert = jnp.logical_or(b == 0, expert != be_ref[jnp.maximum(b - 1, 0)])

    @pl.when(jnp.logical_and(b < nb, first_of_expert))
    def _():
        for copy in fetch(0):
            copy.wait()
        for _, stage, dst in weights:
            dst[...] = stage[...].astype(BF16)
        next_blk = pe_ref[expert] // tm

        @pl.when(next_blk < nb)
        def _():
            for copy in fetch(be_ref[next_blk]):
                copy.start(priority=1)

    @pl.when(b < nb)
    def _():
        scatter(jnp.maximum(b - 1, 0), 1 - slot)
        parts = [_unpack_pair(x_ref[pl.ds(s, tm, stride=ts), :]) for s in range(ts)]
        x = jnp.concatenate([p[0] for p in parts] + [p[1] for p in parts], axis=1).astype(BF16)
        hid = _silu(_dot(x, wg_s[...])) * _dot(x, wu_s[...])
        y = _dot(hid.astype(BF16), wd_s[...])
        for s in range(ts):
            ys[slot, pl.ds(s, tm, stride=ts), :] = _pack_pair(y[:, s * LANES:(s + 1) * LANES],
                                                              y[:, (s + ts) * LANES:(s + ts + 1) * LANES])
        wait_scatter()

        @pl.when(b == last)
        def _():
            scatter(b, slot)
            wait_scatter()

    @pl.when(b == nb)
    def _():
        scatter(b - 1, 1 - slot)
        wait_scatter()


def _moe(row_dst, blk_e, pad_end, nb_real, x_sorted, wg, wu, wd, n_out_rows):
    n_blocks = blk_e.shape[0]
    d, de = wg.shape[1], wg.shape[2]
    tm = MOE_BLOCK
    hbm = pl.BlockSpec(memory_space=pl.ANY)
    return pl.pallas_call(
        _moe_kernel,
        grid_spec=pltpu.PrefetchScalarGridSpec(
            num_scalar_prefetch=4,
            grid=(n_blocks,),
            in_specs=[pl.BlockSpec((tm * TOK_SUBLANES, LANES),
                                   lambda b, rd, be, pe, nb: (jnp.minimum(b, nb[0] - 1), 0)),
                      hbm, hbm, hbm],
            out_specs=hbm,
            scratch_shapes=[pltpu.VMEM((2, tm * TOK_SUBLANES, LANES), jnp.uint32),
                            pltpu.VMEM((d, de), F32), pltpu.VMEM((d, de), F32), pltpu.VMEM((de, d), F32),
                            pltpu.VMEM((d, de), BF16), pltpu.VMEM((d, de), BF16), pltpu.VMEM((de, d), BF16),
                            pltpu.SemaphoreType.DMA(()), pltpu.SemaphoreType.DMA((3,))]),
        out_shape=jax.ShapeDtypeStruct((n_out_rows * TOK_SUBLANES, LANES), jnp.uint32),
        compiler_params=_cparams(("arbitrary",)),
        name="moe",
    )(row_dst, blk_e, pad_end, nb_real, x_sorted, wg, wu, wd)


def _dispatch_kernel(e1_ref, e2_ref, r1_ref, r2_ref, ps_ref, pe_ref, cnt_ref, be_ref, nb_ref,
                     h_ref, x_ref, rd_ref, zbuf, sem, zsem):
    i = pl.program_id(0)
    ts = TOK_SUBLANES
    td = h_ref.shape[0] // ts
    n_tok = e1_ref.shape[0]
    base = i * td
    blk_rows = MOE_BLOCK * ts

    def zero_fill(blk):
        return pltpu.make_async_copy(
            zbuf, x_ref.at[pl.ds(pl.multiple_of(blk * blk_rows, blk_rows), blk_rows), :], zsem)

    @pl.when(i == 0)
    def _():
        zbuf[...] = jnp.zeros_like(zbuf)

        def fill(blk, n_fills):
            e = be_ref[blk]
            partial_last = jnp.logical_and(blk == pe_ref[e] // MOE_BLOCK - 1, cnt_ref[e] % MOE_BLOCK != 0)
            needs_fill = jnp.logical_or(blk >= nb_ref[0], partial_last)

            @pl.when(needs_fill)
            def _():
                zero_fill(blk).start()
            return n_fills + needs_fill.astype(jnp.int32)

        n_fills = lax.fori_loop(0, be_ref.shape[0], fill, 0)

        def init(blk, carry):
            for j in range(MOE_BLOCK):
                rd_ref[blk * MOE_BLOCK + j] = 2 * n_tok + j
            return carry

        lax.fori_loop(0, rd_ref.shape[0] // MOE_BLOCK, init, 0)

        def drain(k, carry):
            zero_fill(0).wait()
            return carry

        lax.fori_loop(0, n_fills, drain, 0)

    def start(t, carry):
        src = h_ref.at[pl.ds(pl.multiple_of(t * ts, ts), ts), :]
        tok = base + t
        for slot, (e_ref, r_ref) in enumerate(((e1_ref, r1_ref), (e2_ref, r2_ref))):
            row = ps_ref[e_ref[tok]] + r_ref[tok]
            pltpu.make_async_copy(src, x_ref.at[pl.ds(pl.multiple_of(row * ts, ts), ts), :], sem).start()
            rd_ref[row] = 2 * tok + slot
        return carry

    lax.fori_loop(0, td, start, 0, unroll=8)
    for _ in range(2):
        pltpu.make_async_copy(h_ref, x_ref.at[pl.ds(0, td * ts), :], sem).wait()


def _dispatch(ri, pad_start, pad_end, counts, blk_e, nb_real, h2p, n_rows):
    t = ri.shape[1]
    td = min(512, t)
    ts = TOK_SUBLANES
    return pl.pallas_call(
        _dispatch_kernel,
        grid_spec=pltpu.PrefetchScalarGridSpec(
            num_scalar_prefetch=9,
            grid=(t // td,),
            in_specs=[pl.BlockSpec((td * ts, LANES), lambda i, *_: (i, 0))],
            out_specs=[pl.BlockSpec(memory_space=pl.ANY), pl.BlockSpec(memory_space=pltpu.SMEM)],
            scratch_shapes=[pltpu.VMEM((MOE_BLOCK * ts, LANES), jnp.uint32),
                            pltpu.SemaphoreType.DMA(()), pltpu.SemaphoreType.DMA(())]),
        out_shape=[jax.ShapeDtypeStruct((n_rows * ts, LANES), jnp.uint32),
                   jax.ShapeDtypeStruct((n_rows,), jnp.int32)],
        compiler_params=_cparams(("arbitrary",)),
        name="dispatch",
    )(ri[0], ri[1], ri[2], ri[3], pad_start, pad_end, counts, blk_e, nb_real, h2p)


def _moe_plan(counts, t):
    tm = MOE_BLOCK
    n_blocks = 2 * t // tm + N_EXPERTS
    padded = ((counts + tm - 1) // tm) * tm
    pad_end = jnp.cumsum(padded)
    pad_start = pad_end - padded
    blk = jnp.arange(n_blocks, dtype=jnp.int32)
    blk_e = jnp.minimum(jnp.sum(pad_end[None, :] <= (blk * tm)[:, None], axis=1), N_EXPERTS - 1).astype(jnp.int32)
    nb_real = (pad_end[-1:] // tm).astype(jnp.int32)
    return pad_start, pad_end, blk_e, nb_real, n_blocks * tm


COMBINE_ROWS = 64


def _combine_kernel(x1_ref, g2_ref, rw_ref, y_ref, o_ref, w_s):
    ts = TOK_SUBLANES
    rows = min(COMBINE_ROWS, x1_ref.shape[0])
    w_s[...] = rw_ref[...].T

    def chunk(c, carry):
        r = pl.ds(pl.multiple_of(c * rows, rows), rows)
        w1 = w_s[r, 0:1]
        w2 = w_s[r, 1:2]
        tile0 = pl.multiple_of(c * rows * 2 * ts, rows * 2 * ts)
        for s in range(ts):
            a_lo, a_hi = _unpack_pair(y_ref[pl.ds(tile0 + s, rows, stride=2 * ts), :])
            b_lo, b_hi = _unpack_pair(y_ref[pl.ds(tile0 + ts + s, rows, stride=2 * ts), :])
            for col, ya, yb in ((s, a_lo, b_lo), (s + ts, a_hi, b_hi)):
                cols = slice(col * LANES, (col + 1) * LANES)
                o_ref[r, cols] = x1_ref[r, cols] + g2_ref[0][:, cols] * (w1 * ya + w2 * yb)
        return carry

    lax.fori_loop(0, x1_ref.shape[0] // rows, chunk, 0)


def _combine(x1, g2, rw, y2, seq):
    t, d = x1.shape
    tc = min(512, seq)
    return pl.pallas_call(
        _combine_kernel,
        grid=(t // tc,),
        in_specs=[pl.BlockSpec((tc, d), lambda i: (i, 0)),
                  pl.BlockSpec((1, 1, d), lambda i: (i * tc // seq, 0, 0)),
                  pl.BlockSpec((rw.shape[0], tc), lambda i: (0, i)),
                  pl.BlockSpec((tc * 2 * TOK_SUBLANES, LANES), lambda i: (i, 0))],
        out_specs=pl.BlockSpec((tc, d), lambda i: (i, 0)),
        out_shape=jax.ShapeDtypeStruct((t, d), F32),
        scratch_shapes=[pltpu.VMEM((tc, rw.shape[0]), F32)],
        compiler_params=_cparams(("arbitrary",)),
        name="combine",
    )(x1, g2, rw, y2)


def _q_up_layout(w_q_up):
    w = w_q_up.reshape(Q_LORA, MLA_HEADS, MLA_QK)
    w = jnp.pad(w, ((0, 0), (0, 0), (0, MLA_QPAD - MLA_QK)))
    return w.reshape(Q_LORA, MLA_HEADS * MLA_QPAD).T.astype(BF16)


def _pad_lanes(g, width):
    return jnp.pad(g, (0, width - g.shape[0])).reshape(1, width)


def kernel(x, c, positions, w_ada, b_ada, norm1_g, w_in, hgrn_lb_logits, hgrn_onorm_g, q_a_norm_g, w_q_up,
           kv_a_norm_g, w_kv_up, q_norm_g, k_norm_g, attn_onorm_g, w_out, norm2_g, w_group, b_group,
           w_router, b_router, w_gate, w_up, w_down):
    bsz, seq, d = x.shape
    t = bsz * seq
    depth = w_ada.shape[0]
    half = MLA_ROPE // 2
    inv_freq = ROPE_BASE ** (-jnp.arange(0, MLA_ROPE, 2, dtype=F32) / MLA_ROPE)
    freq = inv_freq.reshape(half, 1)
    pos3 = positions.reshape(bsz, 1, seq)
    tr = min(1024, t)
    tri = jnp.triu(jnp.ones((tr, tr), BF16), 1)

    x2 = x.reshape(t, d)
    for l in range(depth):
        mod = _ada(c, w_ada[l], b_ada[l]).reshape(bsz, 6, 1, d)
        sh1, sc1, g1, sh2, sc2, g2 = (mod[:, i] for i in range(6))

        proj = _in_proj(x2, norm1_g[l].reshape(1, d), sh1, sc1, w_in[l].T, seq)
        proj3 = proj.reshape(bsz, seq, IN_COLS_PAD)

        o_a = _hgrn(proj3, hgrn_lb_logits, hgrn_onorm_g[l].reshape(1, HGRN_DK), l)

        q, k, v = _mla_up(proj3, pos3, _q_up_layout(w_q_up[l]), w_kv_up[l].T.astype(BF16),
                          q_a_norm_g[l].reshape(1, Q_LORA), kv_a_norm_g[l].reshape(1, KV_LORA),
                          q_norm_g[l].reshape(MLA_QK, 1), k_norm_g[l].reshape(MLA_QK, 1), freq)
        o_b = _attention(q, k, v, attn_onorm_g[l].reshape(MLA_V, 1))

        w_o = w_out[l].astype(BF16)
        wr = jnp.pad(jnp.concatenate([w_router[l], w_group[l]], axis=1),
                     ((0, 0), (0, LANES - N_EXPERTS - N_GROUPS))).astype(BF16)
        br = _pad_lanes(jnp.concatenate([b_router[l], b_group[l]]), LANES)
        x1, h2, lg_t = _out_proj(o_a.reshape(t, HGRN_WIDTH), o_b.reshape(t, HGRN_WIDTH), x2,
                                 w_o[:HGRN_WIDTH], w_o[HGRN_WIDTH:], g1, norm2_g[l].reshape(1, d),
                                 sh2, sc2, wr, br, seq)

        ri, rw, cnt = _route(lg_t, tri)
        counts = cnt[:, 0]
        pad_start, pad_end, blk_e, nb_real, n_rows = _moe_plan(counts, t)
        x_sorted, row_dst = _dispatch(ri, pad_start, pad_end, counts, blk_e, nb_real, h2, n_rows)
        y2 = _moe(row_dst, blk_e, pad_end, nb_real, x_sorted, w_gate[l], w_up[l], w_down[l], 2 * t + MOE_BLOCK)
        x2 = _combine(x1, g2, rw, y2, seq)
    return x2.reshape(bsz, seq, d)
```

```python
import functools

import jax
import jax.numpy as jnp
from jax import lax
from jax.experimental import pallas as pl
from jax.experimental.pallas import tpu as pltpu

F32 = jnp.float32
BF16 = jnp.bfloat16
EPS = 1e-6
LOG2E = 1.4426950408889634

D_MODEL = 2048
HGRN_WIDTH = 1024
HGRN_DK = 128
HGRN_HEADS = 8
HGRN_CHUNK = 64
MLA_HEADS = 8
MLA_NOPE = 128
MLA_ROPE = 64
MLA_QK = MLA_NOPE + MLA_ROPE
MLA_V = 128
MLA_VT_ROWS = MLA_V + 16
MLA_QPAD = 256
Q_LORA = 512
KV_LORA = 256
ROPE_BASE = 10000.0
IN_COLS = 4 * HGRN_WIDTH + Q_LORA + KV_LORA + MLA_ROPE
IN_COLS_PAD = 5120
N_GROUPS = 4
EXPERTS_PER_GROUP = 8
N_EXPERTS = 32
D_EXPERT = 512
ROUTE_ROWS = 40
MOE_BLOCK = 256
LANES = 128
TOK_SUBLANES = 8
VMEM_LIMIT = 56 * 1024 * 1024


def _cparams(sem):
    return pltpu.CompilerParams(dimension_semantics=sem, vmem_limit_bytes=VMEM_LIMIT)


def _dot(a, b):
    return jnp.dot(a, b, preferred_element_type=F32)


def _dot_nt(a, b):
    return lax.dot_general(a, b, (((1,), (1,)), ((), ())), preferred_element_type=F32)


def _rms(x, g):
    return x * lax.rsqrt(jnp.mean(x * x, axis=-1, keepdims=True) + EPS) * g


def _silu(x):
    return x * jax.nn.sigmoid(x)


def _pack_pair(lo, hi):
    lo_b = lax.bitcast_convert_type(lo.astype(BF16).astype(F32), jnp.uint32)
    hi_b = lax.bitcast_convert_type(hi.astype(BF16).astype(F32), jnp.uint32)
    return hi_b | (lo_b >> 16)


def _unpack_pair(w):
    lo = lax.bitcast_convert_type(w << 16, F32)
    hi = lax.bitcast_convert_type(w & jnp.uint32(0xFFFF0000), F32)
    return lo, hi


def _ada_kernel(c_ref, w_ref, b_ref, o_ref):
    ca = _silu(c_ref[...]).astype(BF16)
    o_ref[...] = _dot(ca, w_ref[...].astype(BF16)) + b_ref[...]


def _ada(c, w, b):
    bsz, d = c.shape
    n = w.shape[1]
    tn = 1024
    return pl.pallas_call(
        _ada_kernel,
        grid=(n // tn,),
        in_specs=[pl.BlockSpec((bsz, d), lambda j: (0, 0)),
                  pl.BlockSpec((d, tn), lambda j: (0, j)),
                  pl.BlockSpec((1, tn), lambda j: (0, j))],
        out_specs=pl.BlockSpec((bsz, tn), lambda j: (0, j)),
        out_shape=jax.ShapeDtypeStruct((bsz, n), F32),
        compiler_params=_cparams(("arbitrary",)),
        name="ada",
    )(c, w, b.reshape(1, n))


NORM_ROWS = 32


def _norm_kernel(x_ref, g_ref, sh_ref, sc_ref, h_ref):
    def body(c, carry):
        r = pl.ds(pl.multiple_of(c * NORM_ROWS, NORM_ROWS), NORM_ROWS)
        h = _rms(x_ref[r, :], g_ref[...]) * (1.0 + sc_ref[0]) + sh_ref[0]
        h_ref[r, :] = h.astype(BF16)
        return carry
    lax.fori_loop(0, x_ref.shape[0] // NORM_ROWS, body, 0, unroll=2)


def _norm_mod(x2, g, sh, sc, seq, n_rows):
    t, d = x2.shape
    tm = min(1024, seq, n_rows)
    return pl.pallas_call(
        _norm_kernel,
        grid=(n_rows // tm,),
        in_specs=[pl.BlockSpec((tm, d), lambda i: (i, 0)),
                  pl.BlockSpec((1, d), lambda i: (0, 0)),
                  pl.BlockSpec((1, 1, d), lambda i: (i * tm // seq, 0, 0)),
                  pl.BlockSpec((1, 1, d), lambda i: (i * tm // seq, 0, 0))],
        out_specs=pl.BlockSpec((tm, d), lambda i: (i, 0)),
        out_shape=jax.ShapeDtypeStruct((n_rows, d), BF16),
        compiler_params=_cparams(("arbitrary",)),
        name="norm_mod",
    )(x2, g, sh, sc)


IN_NORM_CHUNKS = 4


def _in_kernel(h0_ref, x_ref, g_ref, sh_ref, sc_ref, w_hbm, o_ref, h_s, w_s, w_sem):
    i = pl.program_id(0)
    j = pl.program_id(1)
    nc = pl.num_programs(1)
    s = i * nc + j
    total = pl.num_programs(0) * nc
    tn = w_s.shape[1]
    slot = i % 2

    def fetch(step):
        col = pl.multiple_of((step % nc) * tn, tn)
        return pltpu.make_async_copy(w_hbm.at[pl.ds(col, tn), :], w_s.at[step % 3], w_sem.at[step % 3])

    @pl.when(s == 0)
    def _():
        h_s[0] = h0_ref[...]
        fetch(s).start()
        fetch(s + 1).start()

    @pl.when(s + 2 < total)
    def _():
        fetch(s + 2).start()

    fetch(s).wait()
    o_ref[...] = _dot_nt(h_s[slot], w_s[s % 3]).astype(BF16)
    rows = x_ref.shape[0] // IN_NORM_CHUNKS
    first = jnp.minimum(j, IN_NORM_CHUNKS - 1) * rows
    gain = g_ref[...]
    scale = 1.0 + sc_ref[0]
    shift = sh_ref[0]
    for c in range(rows // NORM_ROWS):
        r = pl.ds(pl.multiple_of(first + c * NORM_ROWS, NORM_ROWS), NORM_ROWS)
        h_s[1 - slot, r, :] = (_rms(x_ref[r, :], gain) * scale + shift).astype(BF16)


def _in_proj(x2, g, sh, sc, w_t, seq):
    t, d = x2.shape
    tn = 1024
    tm = min(1024, seq)
    n_tiles = t // tm
    w_b = jnp.pad(w_t, ((0, IN_COLS_PAD - IN_COLS), (0, 0))).astype(BF16)
    h0 = _norm_mod(x2, g, sh, sc, seq, tm)

    def nxt(i):
        return jnp.minimum(i + 1, n_tiles - 1)

    return pl.pallas_call(
        _in_kernel,
        grid=(n_tiles, IN_COLS_PAD // tn),
        in_specs=[pl.BlockSpec((tm, d), lambda i, j: (0, 0)),
                  pl.BlockSpec((tm, d), lambda i, j: (nxt(i), 0)),
                  pl.BlockSpec((1, d), lambda i, j: (0, 0)),
                  pl.BlockSpec((1, 1, d), lambda i, j: (nxt(i) * tm // seq, 0, 0)),
                  pl.BlockSpec((1, 1, d), lambda i, j: (nxt(i) * tm // seq, 0, 0)),
                  pl.BlockSpec(memory_space=pl.ANY)],
        out_specs=pl.BlockSpec((tm, tn), lambda i, j: (i, j)),
        out_shape=jax.ShapeDtypeStruct((t, IN_COLS_PAD), BF16),
        scratch_shapes=[pltpu.VMEM((2, tm, d), BF16), pltpu.VMEM((3, tn, d), BF16),
                        pltpu.SemaphoreType.DMA((3,))],
        compiler_params=_cparams(("arbitrary", "arbitrary")),
        name="in_proj",
    )(h0, x2, g, sh, sc, w_b)


HG_ROWS = 256
HGRN_HEADS_PER_STEP = 2


PIPELINE_STATIC_BLOCKS = 8


def _software_pipeline(stages, n_blocks):
    depth = len(stages)

    def step(i, static):
        for k in reversed(range(depth)):
            if static and not 0 <= i - k < n_blocks:
                continue
            stages[k](i - k)

    if n_blocks <= PIPELINE_STATIC_BLOCKS:
        for i in range(n_blocks + depth - 1):
            step(i, True)
        return
    for i in range(depth - 1):
        step(i, True)

    def steady(i, carry):
        step(i, False)
        return carry

    lax.fori_loop(depth - 1, n_blocks, steady, 0)
    for i in range(n_blocks, n_blocks + depth - 1):
        step(i, True)


def _chunk_mask(n):
    row = lax.broadcasted_iota(jnp.int32, (n, n), 0)
    col = lax.broadcasted_iota(jnp.int32, (n, n), 1)
    return jnp.logical_and(row // HGRN_CHUNK == col // HGRN_CHUNK, col <= row)


def _hgrn_kernel(q_ref, f_ref, i_ref, g_ref, lbl_ref, og_ref, o_ref, *scratch, layer):
    for hh in range(q_ref.shape[2] // HGRN_DK):
        ls = pl.ds(hh * HGRN_DK, HGRN_DK)
        _hgrn_head(q_ref.at[:, :, ls], f_ref.at[:, :, ls], i_ref.at[:, :, ls], g_ref.at[:, :, ls],
                   lbl_ref.at[:, ls], og_ref, o_ref.at[:, :, ls], *scratch, layer=layer)


def _hgrn_head(q_ref, f_ref, i_ref, g_ref, lbl_ref, og_ref, o_ref,
               qb_s, u_s, sp_s, dec_s, oi_s, b_s, k_s, qin_s, kin_s, ku_s, a_s, st_s, *, layer):
    seq = q_ref.shape[1]
    cs = HGRN_CHUNK
    rb = min(HG_ROWS, seq)
    nc = rb // cs
    dk = HGRN_DK
    lg = lbl_ref[...]
    ex = jnp.exp(lg - jnp.max(lg, axis=0, keepdims=True))
    sm = ex / jnp.sum(ex, axis=0, keepdims=True)
    lb = jnp.sum(sm[0:layer + 1], axis=0, keepdims=True)
    mask = _chunk_mask(rb)
    tri = mask.astype(BF16)
    row_chunk = lax.broadcasted_iota(jnp.int32, (rb, dk), 0) // cs

    chunk_sel = [(row_chunk == c).astype(BF16) for c in range(nc)]

    def rows(blk):
        return pl.ds(pl.multiple_of(blk * rb, rb), rb)

    def stage1(blk):
        r = rows(blk)
        f = lb + (1.0 - lb) * jax.nn.sigmoid(f_ref[0, r, :].astype(F32))
        lf = jnp.log2(f)
        k_s[r, :] = 1.0 - f
        hi = lf.astype(BF16)
        r1 = lf - hi.astype(F32)
        mid = r1.astype(BF16)
        lo = (r1 - mid.astype(F32)).astype(BF16)
        bhm = _dot(tri, jnp.concatenate([hi, mid], axis=1))
        b_s[r, :] = bhm[:, 0:dk] + bhm[:, dk:] + _dot(tri, lo)

    def stage2(blk):
        r = rows(blk)
        b3 = b_s[r, :].reshape(nc, cs, dk)
        bmid = b3[:, cs // 2 - 1:cs // 2, :]
        blast = b3[:, cs - 1:cs, :]
        q3 = (q_ref[0, r, :].astype(F32) * dk ** -0.5).reshape(nc, cs, dk)
        k3 = k_s[r, :].reshape(nc, cs, dk)
        qin_s[r, :] = (q3 * jnp.exp2(b3 - bmid)).reshape(rb, dk).astype(BF16)
        kin_s[r, :] = (k3 * jnp.exp2(bmid - b3)).reshape(rb, dk).astype(BF16)
        ku_s[r, :] = (k3 * jnp.exp2(blast - b3)).reshape(rb, dk).astype(BF16)
        qb_s[r, :] = (q3 * jnp.exp2(b3)).reshape(rb, dk).astype(BF16)
        dec_s[pl.ds(blk * nc, nc)] = jnp.exp2(blast)

    def stage3(blk):
        r = rows(blk)
        a_s[r, :] = jnp.where(mask, _dot_nt(qin_s[r, :], kin_s[r, :]), 0.0).astype(BF16)
        vt = i_ref[0, r, :].astype(F32).T.astype(BF16)
        ku = ku_s[r, :]
        ut = _dot(vt, jnp.concatenate([ku * sel for sel in chunk_sel], axis=1))
        for c in range(nc):
            u_s[blk * nc + c] = ut[:, c * dk:(c + 1) * dk]

    def stage4(blk):
        r = rows(blk)
        oi_s[r, :] = _dot(a_s[r, :], i_ref[0, r, :])
        st = st_s[...]
        for c in range(nc):
            sp_s[blk * nc + c] = st.astype(BF16)
            st = st * dec_s[blk * nc + c] + u_s[blk * nc + c]
        st_s[...] = st

    def stage5(blk):
        r = rows(blk)
        inter = [_dot_nt(qb_s[pl.ds(pl.multiple_of(blk * rb + c * cs, cs), cs), :], sp_s[blk * nc + c])
                 for c in range(nc)]
        o = oi_s[r, :] + jnp.concatenate(inter, axis=0)
        o = _rms(o, og_ref[...]) * _silu(g_ref[0, r, :].astype(F32))
        o_ref[0, r, :] = o.astype(BF16)

    st_s[...] = jnp.zeros_like(st_s)
    _software_pipeline([stage1, stage2, stage3, stage4, stage5], seq // rb)


def _hgrn(proj3, lb_logits, onorm_g, layer):
    bsz, seq, _ = proj3.shape
    nh = HGRN_HEADS

    hps = HGRN_HEADS_PER_STEP
    width = hps * HGRN_DK

    def col(off):
        return pl.BlockSpec((1, seq, width), lambda b, h: (b, 0, off * (nh // hps) + h))

    nl = lb_logits.shape[0]
    return pl.pallas_call(
        functools.partial(_hgrn_kernel, layer=layer),
        grid=(bsz, nh // hps),
        in_specs=[col(0), col(1), col(2), col(3),
                  pl.BlockSpec((nl, width), lambda b, h: (0, h)),
                  pl.BlockSpec((1, HGRN_DK), lambda b, h: (0, 0))],
        out_specs=pl.BlockSpec((1, seq, width), lambda b, h: (b, 0, h)),
        out_shape=jax.ShapeDtypeStruct((bsz, seq, HGRN_WIDTH), BF16),
        scratch_shapes=[pltpu.VMEM((seq, HGRN_DK), BF16),
                        pltpu.VMEM((seq // HGRN_CHUNK, HGRN_DK, HGRN_DK), F32),
                        pltpu.VMEM((seq // HGRN_CHUNK, HGRN_DK, HGRN_DK), BF16),
                        pltpu.VMEM((seq // HGRN_CHUNK, 1, HGRN_DK), F32),
                        pltpu.VMEM((seq, HGRN_DK), F32),
                        pltpu.VMEM((seq, HGRN_DK), F32), pltpu.VMEM((seq, HGRN_DK), F32),
                        pltpu.VMEM((seq, HGRN_DK), BF16), pltpu.VMEM((seq, HGRN_DK), BF16),
                        pltpu.VMEM((seq, HGRN_DK), BF16), pltpu.VMEM((seq, min(HG_ROWS, seq)), BF16),
                        pltpu.VMEM((HGRN_DK, HGRN_DK), F32)],
        compiler_params=_cparams(("arbitrary", "arbitrary")),
        name="hgrn",
    )(proj3, proj3, proj3, proj3, lb_logits, onorm_g)


UP_ROWS = 1024


def _up_kernel(p_ref, pos_ref, wq_ref, wkv_ref, qag_ref, kvag_ref, qg_ref, kg_ref, freq_ref,
               q_ref, k_ref, v_ref):
    half = MLA_ROPE // 2
    p = p_ref[...].astype(F32)
    qa_n = _rms(p[:, 0:Q_LORA], qag_ref[...]).astype(BF16)
    kva_n = _rms(p[:, Q_LORA:Q_LORA + KV_LORA], kvag_ref[...]).astype(BF16)
    kpe_t = p[:, Q_LORA + KV_LORA:Q_LORA + KV_LORA + LANES].T[0:MLA_ROPE]
    ang = freq_ref[...] * pos_ref[...].astype(F32)
    cos = jnp.cos(ang)
    sin = jnp.sin(ang)

    def rope(x):
        x1, x2 = x[0:half], x[half:]
        return jnp.concatenate([x1 * cos - x2 * sin, x2 * cos + x1 * sin], axis=0)

    kg = kg_ref[...]
    kpe_ss = jnp.sum(kpe_t * kpe_t, axis=0, keepdims=True)
    kpe_rot = rope(kpe_t * kg[MLA_NOPE:])
    qg = qg_ref[...] * (MLA_QK ** -0.5 * LOG2E)
    zpad = jnp.zeros((MLA_QPAD - MLA_QK, p.shape[0]), F32)
    ones_row = (lax.broadcasted_iota(jnp.int32, (MLA_VT_ROWS - MLA_V, p.shape[0]), 0) == 0).astype(BF16)
    for h in range(MLA_HEADS):
        rows = slice(h * MLA_QPAD, (h + 1) * MLA_QPAD)
        qt = _dot_nt(wq_ref[rows, :], qa_n)
        qn = qt[0:MLA_QK] * lax.rsqrt(jnp.sum(qt * qt, axis=0, keepdims=True) / MLA_QK + EPS) * qg
        q_ref[0, h] = jnp.concatenate([qn[0:MLA_NOPE], rope(qn[MLA_NOPE:]), zpad], axis=0).astype(BF16)
        kvt = _dot_nt(wkv_ref[rows, :], kva_n)
        kn = kvt[0:MLA_NOPE]
        rk = lax.rsqrt((jnp.sum(kn * kn, axis=0, keepdims=True) + kpe_ss) / MLA_QK + EPS)
        kt = jnp.concatenate([kn * rk * kg[0:MLA_NOPE], kpe_rot * rk, zpad], axis=0)
        k_ref[0, h] = kt.T.astype(BF16)
        v_ref[0, h, 0:MLA_V] = kvt[MLA_NOPE:].astype(BF16)
        v_ref[0, h, MLA_V:] = ones_row


def _mla_up(proj3, pos3, wq_t, wkv_t, qag, kvag, qg, kg, freq):
    bsz, seq, _ = proj3.shape
    tm = min(UP_ROWS, seq)
    nh = MLA_HEADS
    mla_block = 4 * HGRN_WIDTH // 1024

    def const(shape):
        return pl.BlockSpec(shape, lambda b, i: (0,) * len(shape))

    return pl.pallas_call(
        _up_kernel,
        grid=(bsz, seq // tm),
        in_specs=[pl.BlockSpec((None, tm, 1024), lambda b, i: (b, i, mla_block)),
                  pl.BlockSpec((None, 1, tm), lambda b, i: (b, 0, i)),
                  const(wq_t.shape), const(wkv_t.shape), const(qag.shape), const(kvag.shape),
                  const(qg.shape), const(kg.shape), const(freq.shape)],
        out_specs=[pl.BlockSpec((1, nh, MLA_QPAD, tm), lambda b, i: (b, 0, 0, i)),
                   pl.BlockSpec((1, nh, tm, MLA_QPAD), lambda b, i: (b, 0, i, 0)),
                   pl.BlockSpec((1, nh, MLA_VT_ROWS, tm), lambda b, i: (b, 0, 0, i))],
        out_shape=[jax.ShapeDtypeStruct((bsz, nh, MLA_QPAD, seq), BF16),
                   jax.ShapeDtypeStruct((bsz, nh, seq, MLA_QPAD), BF16),
                   jax.ShapeDtypeStruct((bsz, nh, MLA_VT_ROWS, seq), BF16)],
        compiler_params=_cparams(("arbitrary", "arbitrary")),
        name="mla_up",
    )(proj3, pos3, wq_t, wkv_t, qag, kvag, qg, kg, freq)


ATT_T = 256
ATT_HEADS_PER_STEP = 2


def _attn_kernel(qt_ref, k_ref, vt_ref, g_ref, o_ref):
    for hh in range(k_ref.shape[1]):
        _attn_head(qt_ref.at[:, pl.ds(hh, 1)], k_ref.at[:, pl.ds(hh, 1)], vt_ref.at[:, pl.ds(hh, 1)], g_ref,
                   o_ref.at[:, :, pl.ds(hh * MLA_V, MLA_V)])


def _attn_head(qt_ref, k_ref, vt_ref, g_ref, o_ref):
    seq = k_ref.shape[2]
    t = min(ATT_T, seq)
    key = lax.broadcasted_iota(jnp.int32, (t, t), 0)
    qry = lax.broadcasted_iota(jnp.int32, (t, t), 1)
    causal = key <= qry
    neg = jnp.finfo(F32).min

    def widen(x, off, fill):
        return x if off == 0 else jnp.concatenate([jnp.full((x.shape[0], off), fill, x.dtype), x], axis=1)

    scores = []
    m = None
    for off in range(0, seq, t):
        s = _dot(k_ref[0, 0, off:off + t, :], qt_ref[0, 0, :, off:])
        diag = jnp.where(causal, s[:, 0:t], neg)
        s = diag if off + t == seq else jnp.concatenate([diag, s[:, t:]], axis=1)
        scores.append(s)
        blk_max = widen(jnp.max(s, axis=0, keepdims=True), off, neg)
        m = blk_max if m is None else jnp.maximum(m, blk_max)
    ot = None
    for off, s in zip(range(0, seq, t), scores):
        p = jnp.exp2(s - m[:, off:])
        pv = widen(_dot(vt_ref[0, 0, :, off:off + t], p.astype(BF16)), off, 0.0)
        ot = pv if ot is None else ot + pv
    ot = ot[0:MLA_V] * (1.0 / ot[MLA_V:MLA_V + 1])
    ot = ot * lax.rsqrt(jnp.mean(ot * ot, axis=0, keepdims=True) + EPS) * g_ref[...]
    o_ref[0, :, :] = ot.T.astype(BF16)


def _attention(q, k, v, g):
    bsz, nh, seq, _ = k.shape
    hps = ATT_HEADS_PER_STEP
    return pl.pallas_call(
        _attn_kernel,
        grid=(bsz, nh // hps),
        in_specs=[pl.BlockSpec((1, hps, MLA_QPAD, seq), lambda b, h: (b, h, 0, 0)),
                  pl.BlockSpec((1, hps, seq, MLA_QPAD), lambda b, h: (b, h, 0, 0)),
                  pl.BlockSpec((1, hps, MLA_VT_ROWS, seq), lambda b, h: (b, h, 0, 0)),
                  pl.BlockSpec((MLA_V, 1), lambda b, h: (0, 0))],
        out_specs=pl.BlockSpec((1, seq, hps * MLA_V), lambda b, h: (b, 0, h)),
        out_shape=jax.ShapeDtypeStruct((bsz, seq, nh * MLA_V), BF16),
        compiler_params=_cparams(("arbitrary", "arbitrary")),
        name="attn",
    )(q, k, v, g)


def _out_kernel(oa_ref, ob_ref, x_ref, wa_ref, wb_ref, g1_ref, n2g_ref, sh2_ref, sc2_ref,
                wr_ref, br_ref, x1_ref, h2_ref, lg_ref, mix_s):
    @pl.when(pl.program_id(0) == 0)
    def _():
        mix_s[...] = jnp.zeros_like(mix_s)

    mix_next = _dot(oa_ref[...], wa_ref[...]) + _dot(ob_ref[...], wb_ref[...])
    x1 = x_ref[...] + g1_ref[0] * mix_s[...]
    x1_ref[...] = x1
    h2 = _rms(x1, n2g_ref[...]) * (1.0 + sc2_ref[0]) + sh2_ref[0]
    tm = h2.shape[0]
    for s in range(TOK_SUBLANES):
        h2_ref[pl.ds(s, tm, stride=TOK_SUBLANES), :] = _pack_pair(
            h2[:, s * LANES:(s + 1) * LANES], h2[:, (s + TOK_SUBLANES) * LANES:(s + TOK_SUBLANES + 1) * LANES])
    lg = _dot(h2.astype(BF16), wr_ref[...]) + br_ref[...]
    lg_ref[...] = lg.T[0:ROUTE_ROWS, :]
    mix_s[...] = mix_next


def _out_proj(oa, ob, x2, wa, wb, g1, n2g, sh2, sc2, wr, br, seq):
    t, d = x2.shape
    tm = min(512, seq)
    n_tiles = t // tm

    def const(shape):
        return pl.BlockSpec(shape, lambda i: (0,) * len(shape))

    def prev(i):
        return jnp.maximum(i - 1, 0)

    def per_batch():
        return pl.BlockSpec((1, 1, d), lambda i: (prev(i) * tm // seq, 0, 0))

    return pl.pallas_call(
        _out_kernel,
        grid=(n_tiles + 1,),
        in_specs=[pl.BlockSpec((tm, HGRN_WIDTH), lambda i: (jnp.minimum(i, n_tiles - 1), 0)),
                  pl.BlockSpec((tm, HGRN_WIDTH), lambda i: (jnp.minimum(i, n_tiles - 1), 0)),
                  pl.BlockSpec((tm, d), lambda i: (prev(i), 0)),
                  const(wa.shape), const(wb.shape), per_batch(), const(n2g.shape),
                  per_batch(), per_batch(), const(wr.shape), const(br.shape)],
        out_specs=[pl.BlockSpec((tm, d), lambda i: (prev(i), 0)),
                   pl.BlockSpec((tm * TOK_SUBLANES, LANES), lambda i: (prev(i), 0)),
                   pl.BlockSpec((ROUTE_ROWS, tm), lambda i: (0, prev(i)))],
        out_shape=[jax.ShapeDtypeStruct((t, d), F32),
                   jax.ShapeDtypeStruct((t * TOK_SUBLANES, LANES), jnp.uint32),
                   jax.ShapeDtypeStruct((ROUTE_ROWS, t), F32)],
        scratch_shapes=[pltpu.VMEM((tm, d), F32)],
        compiler_params=_cparams(("arbitrary",)),
        name="out_proj",
    )(oa, ob, x2, wa, wb, g1, n2g, sh2, sc2, wr, br)


def _route_kernel(lg_ref, tri_ref, ri_ref, rw_ref, cnt_ref, carry_s):
    step = pl.program_id(0)

    @pl.when(step == 0)
    def _():
        carry_s[...] = jnp.zeros_like(carry_s)

    lg = lg_ref[...]
    tr = lg.shape[1]
    epg = EXPERTS_PER_GROUP
    gl = lg[N_EXPERTS:N_EXPERTS + N_GROUPS, :]
    row_g = lax.broadcasted_iota(jnp.int32, (N_GROUPS, tr), 0)
    gmax = jnp.max(gl, axis=0, keepdims=True)
    g_sel = jnp.min(jnp.where(gl == gmax, row_g, N_GROUPS), axis=0, keepdims=True)
    p_group = 1.0 / jnp.sum(jnp.exp(gl - gmax), axis=0, keepdims=True)

    e_in = lg[0:epg, :]
    for g in range(1, N_GROUPS):
        e_in = jnp.where(g_sel == g, lg[g * epg:(g + 1) * epg, :], e_in)
    row_e = lax.broadcasted_iota(jnp.int32, (epg, tr), 0)
    top1 = jnp.max(e_in, axis=0, keepdims=True)
    i1 = jnp.min(jnp.where(e_in == top1, row_e, epg), axis=0, keepdims=True)
    rest = jnp.where(row_e == i1, -jnp.inf, e_in)
    top2 = jnp.max(rest, axis=0, keepdims=True)
    i2 = jnp.min(jnp.where(rest == top2, row_e, epg), axis=0, keepdims=True)
    e2w = jnp.exp(top2 - top1)
    w1 = p_group / (1.0 + e2w)
    w2 = p_group * e2w / (1.0 + e2w)
    ex1 = g_sel * epg + i1
    ex2 = g_sel * epg + i2

    row_x = lax.broadcasted_iota(jnp.int32, (N_EXPERTS, tr), 0)
    oh1 = row_x == ex1
    oh2 = row_x == ex2
    oh = jnp.logical_or(oh1, oh2)
    before = _dot(oh.astype(BF16), tri_ref[...]) + carry_s[:, 0:1]
    rank1 = jnp.sum(jnp.where(oh1, before, 0.0), axis=0, keepdims=True)
    rank2 = jnp.sum(jnp.where(oh2, before, 0.0), axis=0, keepdims=True)
    carry_s[...] = carry_s[...] + jnp.sum(oh.astype(F32), axis=1, keepdims=True)

    zi = jnp.zeros((4, tr), jnp.int32)
    ri_ref[...] = jnp.concatenate([ex1, ex2, rank1.astype(jnp.int32), rank2.astype(jnp.int32), zi], axis=0)
    rw_ref[...] = jnp.concatenate([w1, w2, jnp.zeros((6, tr), F32)], axis=0)
    cnt_ref[...] = carry_s[...].astype(jnp.int32)


def _route(lg_t, tri):
    t = lg_t.shape[1]
    tr = tri.shape[0]
    return pl.pallas_call(
        _route_kernel,
        grid=(t // tr,),
        in_specs=[pl.BlockSpec((ROUTE_ROWS, tr), lambda i: (0, i)),
                  pl.BlockSpec((tr, tr), lambda i: (0, 0))],
        out_specs=[pl.BlockSpec((8, tr), lambda i: (0, i)),
                   pl.BlockSpec((8, tr), lambda i: (0, i)),
                   pl.BlockSpec((N_EXPERTS, LANES), lambda i: (0, 0))],
        out_shape=[jax.ShapeDtypeStruct((8, t), jnp.int32),
                   jax.ShapeDtypeStruct((8, t), F32),
                   jax.ShapeDtypeStruct((N_EXPERTS, LANES), jnp.int32)],
        scratch_shapes=[pltpu.VMEM((N_EXPERTS, LANES), F32)],
        compiler_params=_cparams(("arbitrary",)),
        name="route",
    )(lg_t, tri)


def _moe_kernel(rd_ref, be_ref, pe_ref, nb_ref, x_ref, wg_ref, wu_ref, wd_ref, y_ref,
                ys, wg_f, wu_f, wd_f, wg_s, wu_s, wd_s, ssem, wsem):
    b = pl.program_id(0)
    last = pl.num_programs(0) - 1
    nb = nb_ref[0]
    tm = MOE_BLOCK
    ts = TOK_SUBLANES
    weights = ((wg_ref, wg_f, wg_s), (wu_ref, wu_f, wu_s), (wd_ref, wd_f, wd_s))

    def fetch(e):
        return [pltpu.make_async_copy(src.at[e], stage, wsem.at[k]) for k, (src, stage, _) in enumerate(weights)]

    def scatter(blk, src):
        base = blk * tm
        for j in range(tm):
            dst = pl.multiple_of(rd_ref[base + j] * ts, ts)
            pltpu.make_async_copy(ys.at[src, pl.ds(j * ts, ts), :], y_ref.at[pl.ds(dst, ts), :], ssem).start()

    def wait_scatter():
        pltpu.make_async_copy(ys.at[0], y_ref.at[pl.ds(0, tm * ts), :], ssem).wait()

    slot = b % 2

    @pl.when(b == 0)
    def _():
        for copy in fetch(be_ref[0]):
            copy.start(priority=1)
        ys[...] = jnp.zeros_like(ys)
        spare = pltpu.make_async_copy(ys.at[0], y_ref.at[pl.ds(y_ref.shape[0] - tm * ts, tm * ts), :], ssem)
        spare.start()
        spare.wait()

    expert = be_ref[b]
    first_of_expert = jnp.logical_or(b == 0, expert != be_ref[jnp.maximum(b - 1, 0)])

    @pl.when(jnp.logical_and(b < nb, first_of_expert))
    def _():
        for copy in fetch(0):
            copy.wait()
        for _, stage, dst in weights:
            dst[...] = stage[...].astype(BF16)
        next_blk = pe_ref[expert] // tm

        @pl.when(next_blk < nb)
        def _():
            for copy in fetch(be_ref[next_blk]):
                copy.start(priority=1)

    @pl.when(b < nb)
    def _():
        scatter(jnp.maximum(b - 1, 0), 1 - slot)
        parts = [_unpack_pair(x_ref[pl.ds(s, tm, stride=ts), :]) for s in range(ts)]
        x = jnp.concatenate([p[0] for p in parts] + [p[1] for p in parts], axis=1).astype(BF16)
        hid = _silu(_dot(x, wg_s[...])) * _dot(x, wu_s[...])
        y = _dot(hid.astype(BF16), wd_s[...])
        for s in range(ts):
            ys[slot, pl.ds(s, tm, stride=ts), :] = _pack_pair(y[:, s * LANES:(s + 1) * LANES],
                                                              y[:, (s + ts) * LANES:(s + ts + 1) * LANES])
        wait_scatter()

        @pl.when(b == last)
        def _():
            scatter(b, slot)
            wait_scatter()

    @pl.when(b == nb)
    def _():
        scatter(b - 1, 1 - slot)
        wait_scatter()


def _moe(row_dst, blk_e, pad_end, nb_real, x_sorted, wg, wu, wd, n_out_rows):
    n_blocks = blk_e.shape[0]
    d, de = wg.shape[1], wg.shape[2]
    tm = MOE_BLOCK
    hbm = pl.BlockSpec(memory_space=pl.ANY)
    return pl.pallas_call(
        _moe_kernel,
        grid_spec=pltpu.PrefetchScalarGridSpec(
            num_scalar_prefetch=4,
            grid=(n_blocks,),
            in_specs=[pl.BlockSpec((tm * TOK_SUBLANES, LANES),
                                   lambda b, rd, be, pe, nb: (jnp.minimum(b, nb[0] - 1), 0)),
                      hbm, hbm, hbm],
            out_specs=hbm,
            scratch_shapes=[pltpu.VMEM((2, tm * TOK_SUBLANES, LANES), jnp.uint32),
                            pltpu.VMEM((d, de), F32), pltpu.VMEM((d, de), F32), pltpu.VMEM((de, d), F32),
                            pltpu.VMEM((d, de), BF16), pltpu.VMEM((d, de), BF16), pltpu.VMEM((de, d), BF16),
                            pltpu.SemaphoreType.DMA(()), pltpu.SemaphoreType.DMA((3,))]),
        out_shape=jax.ShapeDtypeStruct((n_out_rows * TOK_SUBLANES, LANES), jnp.uint32),
        compiler_params=_cparams(("arbitrary",)),
        name="moe",
    )(row_dst, blk_e, pad_end, nb_real, x_sorted, wg, wu, wd)


def _dispatch_kernel(e1_ref, e2_ref, r1_ref, r2_ref, ps_ref, pe_ref, cnt_ref, be_ref, nb_ref,
                     h_ref, x_ref, rd_ref, zbuf, sem, zsem):
    i = pl.program_id(0)
    ts = TOK_SUBLANES
    td = h_ref.shape[0] // ts
    n_tok = e1_ref.shape[0]
    base = i * td
    blk_rows = MOE_BLOCK * ts

    def zero_fill(blk):
        return pltpu.make_async_copy(
            zbuf, x_ref.at[pl.ds(pl.multiple_of(blk * blk_rows, blk_rows), blk_rows), :], zsem)

    @pl.when(i == 0)
    def _():
        zbuf[...] = jnp.zeros_like(zbuf)

        def fill(blk, n_fills):
            e = be_ref[blk]
            partial_last = jnp.logical_and(blk == pe_ref[e] // MOE_BLOCK - 1, cnt_ref[e] % MOE_BLOCK != 0)
            needs_fill = jnp.logical_or(blk >= nb_ref[0], partial_last)

            @pl.when(needs_fill)
            def _():
                zero_fill(blk).start()
            return n_fills + needs_fill.astype(jnp.int32)

        n_fills = lax.fori_loop(0, be_ref.shape[0], fill, 0)

        def init(blk, carry):
            for j in range(MOE_BLOCK):
                rd_ref[blk * MOE_BLOCK + j] = 2 * n_tok + j
            return carry

        lax.fori_loop(0, rd_ref.shape[0] // MOE_BLOCK, init, 0)

        def drain(k, carry):
            zero_fill(0).wait()
            return carry

        lax.fori_loop(0, n_fills, drain, 0)

    def start(t, carry):
        src = h_ref.at[pl.ds(pl.multiple_of(t * ts, ts), ts), :]
        tok = base + t
        for slot, (e_ref, r_ref) in enumerate(((e1_ref, r1_ref), (e2_ref, r2_ref))):
            row = ps_ref[e_ref[tok]] + r_ref[tok]
            pltpu.make_async_copy(src, x_ref.at[pl.ds(pl.multiple_of(row * ts, ts), ts), :], sem).start()
            rd_ref[row] = 2 * tok + slot
        return carry

    lax.fori_loop(0, td, start, 0, unroll=8)
    for _ in range(2):
        pltpu.make_async_copy(h_ref, x_ref.at[pl.ds(0, td * ts), :], sem).wait()


def _dispatch(ri, pad_start, pad_end, counts, blk_e, nb_real, h2p, n_rows):
    t = ri.shape[1]
    td = min(512, t)
    ts = TOK_SUBLANES
    return pl.pallas_call(
        _dispatch_kernel,
        grid_spec=pltpu.PrefetchScalarGridSpec(
            num_scalar_prefetch=9,
            grid=(t // td,),
            in_specs=[pl.BlockSpec((td * ts, LANES), lambda i, *_: (i, 0))],
            out_specs=[pl.BlockSpec(memory_space=pl.ANY), pl.BlockSpec(memory_space=pltpu.SMEM)],
            scratch_shapes=[pltpu.VMEM((MOE_BLOCK * ts, LANES), jnp.uint32),
                            pltpu.SemaphoreType.DMA(()), pltpu.SemaphoreType.DMA(())]),
        out_shape=[jax.ShapeDtypeStruct((n_rows * ts, LANES), jnp.uint32),
                   jax.ShapeDtypeStruct((n_rows,), jnp.int32)],
        compiler_params=_cparams(("arbitrary",)),
        name="dispatch",
    )(ri[0], ri[1], ri[2], ri[3], pad_start, pad_end, counts, blk_e, nb_real, h2p)


def _moe_plan(counts, t):
    tm = MOE_BLOCK
    n_blocks = 2 * t // tm + N_EXPERTS
    padded = ((counts + tm - 1) // tm) * tm
    pad_end = jnp.cumsum(padded)
    pad_start = pad_end - padded
    blk = jnp.arange(n_blocks, dtype=jnp.int32)
    blk_e = jnp.minimum(jnp.sum(pad_end[None, :] <= (blk * tm)[:, None], axis=1), N_EXPERTS - 1).astype(jnp.int32)
    nb_real = (pad_end[-1:] // tm).astype(jnp.int32)
    return pad_start, pad_end, blk_e, nb_real, n_blocks * tm


COMBINE_ROWS = 64


def _combine_kernel(x1_ref, g2_ref, rw_ref, y_ref, o_ref, w_s):
    ts = TOK_SUBLANES
    rows = min(COMBINE_ROWS, x1_ref.shape[0])
    w_s[...] = rw_ref[...].T

    def chunk(c, carry):
        r = pl.ds(pl.multiple_of(c * rows, rows), rows)
        w1 = w_s[r, 0:1]
        w2 = w_s[r, 1:2]
        tile0 = pl.multiple_of(c * rows * 2 * ts, rows * 2 * ts)
        for s in range(ts):
            a_lo, a_hi = _unpack_pair(y_ref[pl.ds(tile0 + s, rows, stride=2 * ts), :])
            b_lo, b_hi = _unpack_pair(y_ref[pl.ds(tile0 + ts + s, rows, stride=2 * ts), :])
            for col, ya, yb in ((s, a_lo, b_lo), (s + ts, a_hi, b_hi)):
                cols = slice(col * LANES, (col + 1) * LANES)
                o_ref[r, cols] = x1_ref[r, cols] + g2_ref[0][:, cols] * (w1 * ya + w2 * yb)
        return carry

    lax.fori_loop(0, x1_ref.shape[0] // rows, chunk, 0)


def _combine(x1, g2, rw, y2, seq):
    t, d = x1.shape
    tc = min(512, seq)
    return pl.pallas_call(
        _combine_kernel,
        grid=(t // tc,),
        in_specs=[pl.BlockSpec((tc, d), lambda i: (i, 0)),
                  pl.BlockSpec((1, 1, d), lambda i: (i * tc // seq, 0, 0)),
                  pl.BlockSpec((rw.shape[0], tc), lambda i: (0, i)),
                  pl.BlockSpec((tc * 2 * TOK_SUBLANES, LANES), lambda i: (i, 0))],
        out_specs=pl.BlockSpec((tc, d), lambda i: (i, 0)),
        out_shape=jax.ShapeDtypeStruct((t, d), F32),
        scratch_shapes=[pltpu.VMEM((tc, rw.shape[0]), F32)],
        compiler_params=_cparams(("arbitrary",)),
        name="combine",
    )(x1, g2, rw, y2)


def _q_up_layout(w_q_up):
    w = w_q_up.reshape(Q_LORA, MLA_HEADS, MLA_QK)
    w = jnp.pad(w, ((0, 0), (0, 0), (0, MLA_QPAD - MLA_QK)))
    return w.reshape(Q_LORA, MLA_HEADS * MLA_QPAD).T.astype(BF16)


def _pad_lanes(g, width):
    return jnp.pad(g, (0, width - g.shape[0])).reshape(1, width)


def kernel(x, c, positions, w_ada, b_ada, norm1_g, w_in, hgrn_lb_logits, hgrn_onorm_g, q_a_norm_g, w_q_up,
           kv_a_norm_g, w_kv_up, q_norm_g, k_norm_g, attn_onorm_g, w_out, norm2_g, w_group, b_group,
           w_router, b_router, w_gate, w_up, w_down):
    bsz, seq, d = x.shape
    t = bsz * seq
    depth = w_ada.shape[0]
    half = MLA_ROPE // 2
    inv_freq = ROPE_BASE ** (-jnp.arange(0, MLA_ROPE, 2, dtype=F32) / MLA_ROPE)
    freq = inv_freq.reshape(half, 1)
    pos3 = positions.reshape(bsz, 1, seq)
    tr = min(1024, t)
    tri = jnp.triu(jnp.ones((tr, tr), BF16), 1)

    x2 = x.reshape(t, d)
    for l in range(depth):
        mod = _ada(c, w_ada[l], b_ada[l]).reshape(bsz, 6, 1, d)
        sh1, sc1, g1, sh2, sc2, g2 = (mod[:, i] for i in range(6))

        proj = _in_proj(x2, norm1_g[l].reshape(1, d), sh1, sc1, w_in[l].T, seq)
        proj3 = proj.reshape(bsz, seq, IN_COLS_PAD)

        o_a = _hgrn(proj3, hgrn_lb_logits, hgrn_onorm_g[l].reshape(1, HGRN_DK), l)

        q, k, v = _mla_up(proj3, pos3, _q_up_layout(w_q_up[l]), w_kv_up[l].T.astype(BF16),
                          q_a_norm_g[l].reshape(1, Q_LORA), kv_a_norm_g[l].reshape(1, KV_LORA),
                          q_norm_g[l].reshape(MLA_QK, 1), k_norm_g[l].reshape(MLA_QK, 1), freq)
        o_b = _attention(q, k, v, attn_onorm_g[l].reshape(MLA_V, 1))

        w_o = w_out[l].astype(BF16)
        wr = jnp.pad(jnp.concatenate([w_router[l], w_group[l]], axis=1),
                     ((0, 0), (0, LANES - N_EXPERTS - N_GROUPS))).astype(BF16)
        br = _pad_lanes(jnp.concatenate([b_router[l], b_group[l]]), LANES)
        x1, h2, lg_t = _out_proj(o_a.reshape(t, HGRN_WIDTH), o_b.reshape(t, HGRN_WIDTH), x2,
                                 w_o[:HGRN_WIDTH], w_o[HGRN_WIDTH:], g1, norm2_g[l].reshape(1, d),
                                 sh2, sc2, wr, br, seq)

        ri, rw, cnt = _route(lg_t, tri)
        counts = cnt[:, 0]
        pad_start, pad_end, blk_e, nb_real, n_rows = _moe_plan(counts, t)
        x_sorted, row_dst = _dispatch(ri, pad_start, pad_end, counts, blk_e, nb_real, h2, n_rows)
        y2 = _moe(row_dst, blk_e, pad_end, nb_real, x_sorted, w_gate[l], w_up[l], w_down[l], 2 * t + MOE_BLOCK)
        x2 = _combine(x1, g2, rw, y2, seq)
    return x2.reshape(bsz, seq, d)
```
